```python
import math
import jax, jax.numpy as jnp
from jax import lax
import numpy as np

D_MODEL = 1024
BATCH = 16
SEQ = 2048
DEPTH = 2

GRID_W = 64
CTX_LEN = 256
HEAD_DIM = 64
CHUNK = 64
Q_BLOCK = 128
EPS = 1e-6
GLA_HEADS = 4
GLA_DK = 64
GLA_DV = 64
GLA_GATE_RANK = 16
GLA_TAU = 16.0
GDN_HEADS = 4
GDN_DK = 64
GDN_DV = 64
SHORT_CONV = 5
ATT_Q_HEADS = 8
ATT_KV_HEADS = 2
ATT_GROUP = ATT_Q_HEADS // ATT_KV_HEADS
ROPE_THETA = 10000.0
D_MIX = GLA_HEADS * GLA_DV + GDN_HEADS * GDN_DV + ATT_Q_HEADS * HEAD_DIM
D_FF = 2816
N_EXPERTS = 8
TOP_K = 2
N_DENSE = (DEPTH + 1) // 2
N_MOE = DEPTH // 2
IN_SIZES = (GLA_HEADS * GLA_DK, GLA_HEADS * GLA_DK, GLA_HEADS * GLA_DV, GLA_HEADS * GLA_DV, 2 * GLA_GATE_RANK,
            GDN_HEADS * GDN_DK, GDN_HEADS * GDN_DK, GDN_HEADS * GDN_DV, GDN_HEADS * GDN_DV, 2 * GDN_HEADS, 2 * GDN_HEADS,
            ATT_Q_HEADS * HEAD_DIM, ATT_KV_HEADS * HEAD_DIM, ATT_KV_HEADS * HEAD_DIM)
D_IN = sum(IN_SIZES)

kernel_name = "hybrid_gla_gdn_gqa_moe_diffusion_block"

F32 = jnp.float32


def rmsnorm(x, gain):
    xf = x.astype(F32)
    y = xf * lax.rsqrt(jnp.mean(xf * xf, axis=-1, keepdims=True) + EPS)
    return (y * gain.astype(F32)).astype(x.dtype)


def l2norm(x):
    return x * lax.rsqrt(jnp.sum(x * x, axis=-1, keepdims=True) + EPS)


def modulate(h, shift, scale):
    return h * (1 + scale) + shift


def split_cols(p):
    idx = [int(i) for i in np.cumsum(IN_SIZES)[:-1]]
    return jnp.split(p, idx, axis=-1)


def to_heads(t, n_heads):
    B, L, _ = t.shape
    return t.reshape(B, L, n_heads, -1).transpose(0, 2, 1, 3)


def _chunks(t):
    B, H, L = t.shape[:3]
    t = t.reshape((B, H, L // CHUNK, CHUNK) + t.shape[3:])
    return jnp.moveaxis(t, 2, 0)


def _unchunk(t):
    n, B, H, C = t.shape[:4]
    return jnp.moveaxis(t, 0, 2).reshape((B, H, n * C) + t.shape[4:])


def gla_chunk_scan(q, k, v, log_a, s0):
    tri = jnp.tril(jnp.ones((CHUNK, CHUNK), bool))

    def step(S, inp):
        qc, kc, vc, gc = inp
        b = jnp.cumsum(gc, axis=2)
        diff = b[:, :, :, None, :] - b[:, :, None, :, :]
        decay = jnp.exp(jnp.where(tri[:, :, None], diff, -jnp.inf))
        scores = jnp.einsum('bhtd,bhsd,bhtsd->bhts', qc, kc, decay)
        o = jnp.einsum('bhtd,bhde->bhte', qc * jnp.exp(b), S) + jnp.einsum('bhts,bhse->bhte', scores, vc)
        b_end = b[:, :, -1:, :]
        S = jnp.exp(b_end[:, :, 0, :, None]) * S + jnp.einsum('bhsd,bhse->bhde', kc * jnp.exp(b_end - b), vc)
        return S, o

    S, o = lax.scan(step, s0, tuple(_chunks(t) for t in (q, k, v, log_a)))
    return _unchunk(o), S


def gdn_chunk_scan(q, k, v, log_a, beta, s0):
    tri = jnp.tril(jnp.ones((CHUNK, CHUNK), bool))
    strict = jnp.tril(jnp.ones((CHUNK, CHUNK), bool), -1)
    eye = jnp.eye(CHUNK, dtype=F32)

    def step(S, inp):
        qc, kc, vc, gc, bc = inp
        gam = jnp.cumsum(gc, axis=-1)
        dec = jnp.exp(jnp.where(tri, gam[..., :, None] - gam[..., None, :], -jnp.inf))
        a_mat = jnp.where(strict, bc[..., :, None] * dec * jnp.einsum('bhtd,bhsd->bhts', kc, kc), 0.0)
        rhs = jnp.concatenate([bc[..., None] * vc, (bc * jnp.exp(gam))[..., None] * kc], axis=-1)
        sol = lax.linalg.triangular_solve(a_mat + eye, rhs, left_side=True, lower=True, unit_diagonal=True)
        dv = vc.shape[-1]
        u = sol[..., :dv] - jnp.einsum('bhsd,bhde->bhse', sol[..., dv:], S)
        qk = jnp.einsum('bhtd,bhsd->bhts', qc, kc) * dec
        o = jnp.exp(gam)[..., None] * jnp.einsum('bhtd,bhde->bhte', qc, S) + jnp.einsum('bhts,bhse->bhte', qk, u)
        g_end = gam[..., -1:]
        S = jnp.exp(g_end)[..., None] * S + jnp.einsum('bhsd,bhse->bhde', kc * jnp.exp(g_end - gam)[..., None], u)
        return S, o

    S, o = lax.scan(step, s0, tuple(_chunks(t) for t in (q, k, v, log_a, beta)))
    return _unchunk(o), S


def bidirectional_scan(scan_fn, shared_ctx, shared_lat, dir_ctx, dir_lat, s0):
    o_ctx, o_lat = None, None
    for d in range(2):
        f = (lambda t: jnp.flip(t, axis=2)) if d == 1 else (lambda t: t)
        oc, sc = scan_fn(*[f(t) for t in shared_ctx + dir_ctx[d]], s0)
        ol, _ = scan_fn(*[f(t) for t in shared_lat + dir_lat[d]], sc)
        o_ctx = f(oc) if o_ctx is None else o_ctx + f(oc)
        o_lat = f(ol) if o_lat is None else o_lat + f(ol)
    return o_ctx, o_lat


def centred_dwconv(x, w):
    K, C = w.shape
    pad = K // 2
    return lax.conv_general_dilated(x, w[:, None, :].astype(x.dtype), window_strides=(1,),
                                    padding=[(pad, K - 1 - pad)], dimension_numbers=('NWC', 'WIO', 'NWC'),
                                    feature_group_count=C)


def gla_mixer(p_ctx, p_lat, gate_up, gate_bias, out_gain, need_ctx):
    def prep(parts):
        q, k, v, _, g_low = [t.astype(F32) for t in parts]
        q = to_heads(q, GLA_HEADS) * GLA_DK ** -0.5
        k = to_heads(k, GLA_HEADS)
        v = to_heads(v, GLA_HEADS)
        g_dirs = jnp.split(g_low, 2, axis=-1)
        per_dir = [(to_heads(jax.nn.log_sigmoid(g_dirs[d] @ gate_up[d].astype(F32) + gate_bias[d].astype(F32))
                             / GLA_TAU, GLA_HEADS),) for d in range(2)]
        return (q, k, v), per_dir

    sh_c, dir_c = prep(p_ctx)
    sh_l, dir_l = prep(p_lat)
    s0 = jnp.zeros((p_lat[0].shape[0], GLA_HEADS, GLA_DK, GLA_DV), F32)
    o_c, o_l = bidirectional_scan(gla_chunk_scan, sh_c, sh_l, dir_c, dir_l, s0)

    def finish(o, r):
        B, _, L, _ = o.shape
        o = rmsnorm(jnp.swapaxes(o, 1, 2), out_gain).reshape(B, L, GLA_HEADS * GLA_DV)
        return o * jax.nn.silu(r.astype(F32))

    return (finish(o_c, p_ctx[3]) if need_ctx else None), finish(o_l, p_lat[3])


def gdn_mixer(p_ctx, p_lat, conv_w, a_log, dt_bias, out_gain, need_ctx):
    qk_w = GDN_HEADS * GDN_DK

    def prep(parts):
        q, k, v, _, a, bt = parts
        qkv = jax.nn.silu(centred_dwconv(jnp.concatenate([q, k, v], axis=-1), conv_w).astype(F32))
        q, k, v = jnp.split(qkv, [qk_w, 2 * qk_w], axis=-1)
        q = l2norm(to_heads(q, GDN_HEADS)) * GDN_DK ** -0.5
        k = l2norm(to_heads(k, GDN_HEADS))
        v = to_heads(v, GDN_HEADS)
        a = a.astype(F32)
        bt = bt.astype(F32)
        per_dir = []
        for d in range(2):
            ad = a[..., d * GDN_HEADS:(d + 1) * GDN_HEADS]
            bd = bt[..., d * GDN_HEADS:(d + 1) * GDN_HEADS]
            g = -jnp.exp(a_log[d].astype(F32)) * jax.nn.softplus(ad + dt_bias[d].astype(F32))
            per_dir.append((jnp.swapaxes(g, 1, 2), jnp.swapaxes(jax.nn.sigmoid(bd), 1, 2)))
        return (q, k, v), per_dir

    sh_c, dir_c = prep(p_ctx)
    sh_l, dir_l = prep(p_lat)
    s0 = jnp.zeros((p_lat[0].shape[0], GDN_HEADS, GDN_DK, GDN_DV), F32)
    o_c, o_l = bidirectional_scan(gdn_chunk_scan, sh_c, sh_l, dir_c, dir_l, s0)

    def finish(o, z):
        B, _, L, _ = o.shape
        z = z.astype(F32).reshape(B, L, GDN_HEADS, GDN_DV)
        return (rmsnorm(jnp.swapaxes(o, 1, 2), out_gain) * jax.nn.silu(z)).reshape(B, L, GDN_HEADS * GDN_DV)

    return (finish(o_c, p_ctx[3]) if need_ctx else None), finish(o_l, p_lat[3])


def _rotate(xs, ang):
    x1, x2 = jnp.split(xs, 2, axis=-1)
    cos = jnp.cos(ang).astype(xs.dtype)
    sin = jnp.sin(ang).astype(xs.dtype)
    return jnp.concatenate([x1 * cos - x2 * sin, x2 * cos + x1 * sin], axis=-1)


def axial_rope(x, ang_row, ang_col):
    shape = (1, x.shape[1]) + (1,) * (x.ndim - 3) + (ang_row.shape[-1],)
    xr, xc = jnp.split(x, 2, axis=-1)
    return jnp.concatenate([_rotate(xr, ang_row.reshape(shape)), _rotate(xc, ang_col.reshape(shape))], axis=-1)


def blocked_attention(q, k, v):
    B, Lq = q.shape[:2]
    nb = Lq // Q_BLOCK
    kf = k.astype(F32)
    scale = HEAD_DIM ** -0.5
    qb = jnp.moveaxis(q.reshape((B, nb, Q_BLOCK) + q.shape[2:]), 1, 0)

    def one(qblk):
        s = jnp.einsum('bqhgd,bkhd->bhgqk', qblk.astype(F32), kf) * scale
        p = jax.nn.softmax(s, axis=-1)
        return jnp.einsum('bhgqk,bkhd->bqhgd', p.astype(v.dtype), v)

    o = lax.map(one, qb)
    return jnp.moveaxis(o, 0, 1).reshape(q.shape)


def attention_mixer(p_ctx, p_lat, q_gain, k_gain, ang_row, ang_col, need_ctx):
    def prep(parts, rope):
        q, k, v = parts
        B, L, _ = q.shape
        q = rmsnorm(q.reshape(B, L, ATT_KV_HEADS, ATT_GROUP, HEAD_DIM), q_gain)
        k = rmsnorm(k.reshape(B, L, ATT_KV_HEADS, HEAD_DIM), k_gain)
        v = v.reshape(B, L, ATT_KV_HEADS, HEAD_DIM)
        if rope:
            q = axial_rope(q, ang_row, ang_col)
            k = axial_rope(k, ang_row, ang_col)
        return q, k, v

    qc, kc, vc = prep(p_ctx, False)
    ql, kl, vl = prep(p_lat, True)
    k_all = jnp.concatenate([kl, kc], axis=1)
    v_all = jnp.concatenate([vl, vc], axis=1)
    B, L = ql.shape[:2]
    o_lat = blocked_attention(ql, k_all, v_all).reshape(B, L, ATT_Q_HEADS * HEAD_DIM)
    o_ctx = blocked_attention(qc, kc, vc).reshape(B, qc.shape[1], ATT_Q_HEADS * HEAD_DIM) if need_ctx else None
    return o_ctx, o_lat


def swiglu(h, w_gu, w_down):
    g, u = jnp.split(h @ w_gu, 2, axis=-1)
    return (jax.nn.silu(g) * u) @ w_down


def moe_swiglu(h, w_router, b_router, w_gu, w_down):
    logits = h.astype(F32) @ w_router.astype(F32) + b_router.astype(F32)
    top_v, top_i = lax.top_k(logits, TOP_K)
    top_w = jax.nn.softmax(top_v, axis=-1)
    combine = jnp.sum(jax.nn.one_hot(top_i, N_EXPERTS, dtype=F32) * top_w[..., None], axis=-2)
    y = jnp.zeros(h.shape, F32)
    for e in range(N_EXPERTS):
        y = y + combine[..., e:e + 1] * swiglu(h, w_gu[e], w_down[e]).astype(F32)
    return y.astype(h.dtype)


def setup_inputs(seed: int = 0) -> dict:
    key = jax.random.key(seed)
    ks = iter(jax.random.split(key, 32))

    def nrm(shape, scale):
        return jax.random.normal(next(ks), shape, F32) * scale

    x = nrm((BATCH, SEQ, D_MODEL), 1.0)
    c = nrm((BATCH, D_MODEL), 1.0)
    ctx = nrm((BATCH, CTX_LEN, D_MODEL), 1.0)
    c_ctx = nrm((D_MODEL,), 1.0)
    w_mod = nrm((DEPTH, D_MODEL, 6 * D_MODEL), 0.5 * D_MODEL ** -0.5)
    b_mod = nrm((DEPTH, 6 * D_MODEL), 0.02)
    norm_mix = 1.0 + nrm((DEPTH, D_MODEL), 0.02)
    norm_ffn = 1.0 + nrm((DEPTH, D_MODEL), 0.02)
    w_in = nrm((DEPTH, D_MODEL, D_IN), D_MODEL ** -0.5)
    gla_gate_up = nrm((DEPTH, 2, GLA_GATE_RANK, GLA_HEADS * GLA_DK), GLA_GATE_RANK ** -0.5)
    gla_gate_bias = nrm((DEPTH, 2, GLA_HEADS * GLA_DK), 0.1)
    gla_out_gain = 1.0 + nrm((DEPTH, GLA_DV), 0.02)
    gdn_conv = nrm((DEPTH, SHORT_CONV, 2 * GDN_HEADS * GDN_DK + GDN_HEADS * GDN_DV), SHORT_CONV ** -0.5)
    gdn_a_log = jnp.log(jax.random.uniform(next(ks), (DEPTH, 2, GDN_HEADS), F32, 1.0, 16.0))
    dt = jnp.exp(jax.random.uniform(next(ks), (DEPTH, 2, GDN_HEADS), F32, math.log(1e-3), math.log(0.1)))
    gdn_dt_bias = dt + jnp.log(-jnp.expm1(-dt))
    gdn_out_gain = 1.0 + nrm((DEPTH, GDN_DV), 0.02)
    att_q_gain = 1.0 + nrm((DEPTH, HEAD_DIM), 0.02)
    att_k_gain = 1.0 + nrm((DEPTH, HEAD_DIM), 0.02)
    w_out = nrm((DEPTH, D_MIX, D_MODEL), D_MIX ** -0.5)
    ffn_gate_up = nrm((N_DENSE, D_MODEL, 2 * D_FF), D_MODEL ** -0.5)
    ffn_down = nrm((N_DENSE, D_FF, D_MODEL), D_FF ** -0.5)
    moe_router = nrm((N_MOE, D_MODEL, N_EXPERTS), D_MODEL ** -0.5)
    moe_router_bias = nrm((N_MOE, N_EXPERTS), 0.01)
    moe_gate_up = nrm((N_MOE, N_EXPERTS, D_MODEL, 2 * D_FF), D_MODEL ** -0.5)
    moe_down = nrm((N_MOE, N_EXPERTS, D_FF, D_MODEL), D_FF ** -0.5)
    return {"x": x, "c": c, "ctx": ctx, "c_ctx": c_ctx, "w_mod": w_mod, "b_mod": b_mod,
            "norm_mix": norm_mix, "norm_ffn": norm_ffn, "w_in": w_in, "gla_gate_up": gla_gate_up,
            "gla_gate_bias": gla_gate_bias, "gla_out_gain": gla_out_gain, "gdn_conv": gdn_conv,
            "gdn_a_log": gdn_a_log, "gdn_dt_bias": gdn_dt_bias, "gdn_out_gain": gdn_out_gain,
            "att_q_gain": att_q_gain, "att_k_gain": att_k_gain, "w_out": w_out,
            "ffn_gate_up": ffn_gate_up, "ffn_down": ffn_down, "moe_router": moe_router,
            "moe_router_bias": moe_router_bias, "moe_gate_up": moe_gate_up, "moe_down": moe_down}


def reference(x, c, ctx, c_ctx, w_mod, b_mod, norm_mix, norm_ffn, w_in, gla_gate_up, gla_gate_bias,
              gla_out_gain, gdn_conv, gdn_a_log, gdn_dt_bias, gdn_out_gain, att_q_gain, att_k_gain, w_out,
              ffn_gate_up, ffn_down, moe_router, moe_router_bias, moe_gate_up, moe_down):
    L = x.shape[1]
    rows = L // GRID_W
    row = jnp.repeat(jnp.arange(rows), GRID_W).astype(F32)
    col = jnp.tile(jnp.arange(GRID_W), rows).astype(F32)
    inv_freq = ROPE_THETA ** (-jnp.arange(0, HEAD_DIM // 2, 2, dtype=F32) / (HEAD_DIM // 2))
    ang_row = row[:, None] * inv_freq
    ang_col = col[:, None] * inv_freq

    h_lat, h_ctx = x, ctx
    for layer in range(DEPTH):
        need_ctx = layer < DEPTH - 1
        m_lat = jax.nn.silu(c) @ w_mod[layer] + b_mod[layer]
        m_ctx = jax.nn.silu(c_ctx) @ w_mod[layer] + b_mod[layer]
        sh1, sc1, g1, sh2, sc2, g2 = jnp.split(m_lat[:, None, :], 6, axis=-1)
        sh1c, sc1c, g1c, sh2c, sc2c, g2c = jnp.split(m_ctx, 6, axis=-1)

        a_lat = modulate(rmsnorm(h_lat, norm_mix[layer]), sh1, sc1)
        a_ctx = modulate(rmsnorm(h_ctx, norm_mix[layer]), sh1c, sc1c)
        p_lat = split_cols(a_lat @ w_in[layer])
        p_ctx = split_cols(a_ctx @ w_in[layer])
        gla_c, gla_l = gla_mixer(p_ctx[0:5], p_lat[0:5], gla_gate_up[layer], gla_gate_bias[layer],
                                 gla_out_gain[layer], need_ctx)
        gdn_c, gdn_l = gdn_mixer(p_ctx[5:11], p_lat[5:11], gdn_conv[layer], gdn_a_log[layer], gdn_dt_bias[layer],
                                 gdn_out_gain[layer], need_ctx)
        att_c, att_l = attention_mixer(p_ctx[11:14], p_lat[11:14], att_q_gain[layer], att_k_gain[layer],
                                       ang_row, ang_col, need_ctx)
        y_lat = jnp.concatenate([gla_l, gdn_l, att_l], axis=-1).astype(h_lat.dtype) @ w_out[layer]
        h_lat = h_lat + g1 * y_lat
        if need_ctx:
            y_ctx = jnp.concatenate([gla_c, gdn_c, att_c], axis=-1).astype(h_ctx.dtype) @ w_out[layer]
            h_ctx = h_ctx + g1c * y_ctx

        b_lat = modulate(rmsnorm(h_lat, norm_ffn[layer]), sh2, sc2)
        j = layer // 2
        if layer % 2 == 0:
            h_lat = h_lat + g2 * swiglu(b_lat, ffn_gate_up[j], ffn_down[j])
        else:
            h_lat = h_lat + g2 * moe_swiglu(b_lat, moe_router[j], moe_router_bias[j], moe_gate_up[j], moe_down[j])
        if need_ctx:
            b_ctx = modulate(rmsnorm(h_ctx, norm_ffn[layer]), sh2c, sc2c)
            if layer % 2 == 0:
                h_ctx = h_ctx + g2c * swiglu(b_ctx, ffn_gate_up[j], ffn_down[j])
            else:
                h_ctx = h_ctx + g2c * moe_swiglu(b_ctx, moe_router[j], moe_router_bias[j], moe_gate_up[j],
                                                 moe_down[j])
    return h_lat
```

```python
import functools

import numpy as np
import jax
import jax.numpy as jnp
from jax import lax
from jax.experimental import pallas as pl
from jax.experimental.pallas import tpu as pltpu

F32 = jnp.float32
BF16 = jnp.bfloat16

GRID_W = 64
HEAD_DIM = 64
CHUNK = 64
SUB = 16
EPS = 1e-6
GLA_HEADS = 4
GLA_GATE_RANK = 16
GLA_TAU = 16.0
GDN_HEADS = 4
SHORT_CONV = 5
ATT_Q_HEADS = 8
ATT_KV_HEADS = 2
ROPE_THETA = 10000.0
N_EXPERTS = 8
MIX_W = GLA_HEADS * HEAD_DIM
EXP_CLAMP = 80.0

LANES = 128
V7X_VMEM_BYTES = 64 * 1024 * 1024
VMEM_LIMIT = 56 * 1024 * 1024


def _cparams(sem):
    return pltpu.CompilerParams(dimension_semantics=sem, vmem_limit_bytes=VMEM_LIMIT)


def _silu(x):
    return x / (1.0 + jnp.exp(-x))


def _sigmoid(x):
    return 1.0 / (1.0 + jnp.exp(-x))


def _softplus(x):
    return jnp.maximum(x, 0.0) + jnp.log(1.0 + jnp.exp(-jnp.abs(x)))


def _dot(a, b):
    return jnp.dot(a.astype(BF16), b.astype(BF16), preferred_element_type=F32)


def _dot_nt(a, b):
    return lax.dot_general(a.astype(BF16), b.astype(BF16), (((1,), (1,)), ((), ())),
                           preferred_element_type=F32)


def _split(x):
    hi = x.astype(BF16)
    lo = (x - hi.astype(F32)).astype(BF16)
    return hi, lo


def _dot_xhl(x, w):
    hi, lo = _split(x)
    w = w.astype(BF16)
    return (jnp.dot(hi, w, preferred_element_type=F32) + jnp.dot(lo, w, preferred_element_type=F32))


def _dot_whl(w, x):
    hi, lo = _split(x)
    w = w.astype(BF16)
    return (jnp.dot(w, hi, preferred_element_type=F32) + jnp.dot(w, lo, preferred_element_type=F32))


def _dot3(a, b):
    ah, al = _split(a)
    bh, bl = _split(b)
    return (jnp.dot(ah, bh, preferred_element_type=F32) + jnp.dot(ah, bl, preferred_element_type=F32)
            + jnp.dot(al, bh, preferred_element_type=F32))


def _seg_sum64(sq, seg):
    outs = []
    for j in range(sq.shape[1] // LANES):
        outs.append(_dot_xhl(sq[:, LANES * j:LANES * (j + 1)], seg))
    return outs[0] if len(outs) == 1 else jnp.concatenate(outs, axis=1)


def _iota(shape, dim):
    return lax.broadcasted_iota(jnp.int32, shape, dim)


def _mod_kernel(c_ref, w_ref, b_ref, o_ref):
    s = _silu(c_ref[...])
    o_ref[0] = _dot(s, w_ref[0]) + b_ref[0]


def _modulation(cvec, w_mod, b_mod):
    depth, d, n = w_mod.shape
    rows = cvec.shape[0]
    tn = 1536
    return pl.pallas_call(
        _mod_kernel,
        grid=(depth, n // tn),
        in_specs=[pl.BlockSpec((rows, d), lambda l, j: (0, 0)),
                  pl.BlockSpec((1, d, tn), lambda l, j: (l, 0, j)),
                  pl.BlockSpec((1, 1, tn), lambda l, j: (l, 0, j))],
        out_specs=pl.BlockSpec((1, rows, tn), lambda l, j: (l, 0, j)),
        out_shape=jax.ShapeDtypeStruct((depth, rows, n), F32),
        compiler_params=_cparams(("arbitrary", "arbitrary")),
        name="modulation",
    )(cvec, w_mod, b_mod.reshape(depth, 1, n))


def _swap16(n, lane):
    fwd = pltpu.roll(n, LANES - 16, 1)
    bwd = pltpu.roll(n, 16, 1)
    return jnp.where((lane % 32) < 16, fwd, bwd)


def _inproj_kernel(*refs, rope):
    if rope:
        (h_ref, sh_ref, sc_ref, gain_ref, w_ref, hg_ref, seg_ref, cos_ref, sin_ref,
         gla_ref, gdn_ref, q_ref, kv_ref, small_ref) = refs
    else:
        (h_ref, sh_ref, sc_ref, gain_ref, w_ref, hg_ref, seg_ref,
         gla_ref, gdn_ref, q_ref, kv_ref, small_ref) = refs
    x = h_ref[...]
    ms = jnp.mean(x * x, axis=-1, keepdims=True)
    a = x * lax.rsqrt(ms + EPS) * (gain_ref[...] * (1.0 + sc_ref[0])) + sh_ref[0]
    p = jnp.dot(a.astype(BF16), w_ref[...], preferred_element_type=F32)
    gla_ref[...] = p[:, 0:1024]
    gdn_ref[...] = p[:, 1024:2048]
    small_ref[...] = p[:, 2816:2944]
    seg = seg_ref[...]
    lane = _iota((1, LANES), 1)
    outs = []
    for j in range(5):
        t = p[:, 2048 + LANES * j:2048 + LANES * (j + 1)]
        ss = _dot_xhl(t * t, seg)
        n = t * lax.rsqrt(ss * (1.0 / HEAD_DIM) + EPS) * hg_ref[:, LANES * j:LANES * (j + 1)]
        if rope:
            n = n * cos_ref[...] + _swap16(n, lane) * sin_ref[...]
        outs.append(n)
    q_ref[...] = jnp.concatenate(outs[:4], axis=1).astype(BF16)
    kv_ref[...] = jnp.concatenate([outs[4], p[:, 2688:2816]], axis=1).astype(BF16)


def _inproj(h2d, mod144, mod_row_fn, gain, w_p, hg, seg, tables, seq_len, tm):
    rows, d = h2d.shape
    n_all = w_p.shape[1]
    rope = tables is not None
    tiles_per_seq = seq_len // tm
    in_specs = [pl.BlockSpec((tm, d), lambda i: (i, 0)),
                pl.BlockSpec((1, 1, d), lambda i: (mod_row_fn(i // tiles_per_seq) * 6 + 0, 0, 0)),
                pl.BlockSpec((1, 1, d), lambda i: (mod_row_fn(i // tiles_per_seq) * 6 + 1, 0, 0)),
                pl.BlockSpec((1, d), lambda i: (0, 0)),
                pl.BlockSpec((d, n_all), lambda i: (0, 0)),
                pl.BlockSpec((1, 640), lambda i: (0, 0)),
                pl.BlockSpec((LANES, LANES), lambda i: (0, 0))]
    args = [h2d, mod144, mod144, gain, w_p, hg, seg]
    if rope:
        in_specs += [pl.BlockSpec((tm, LANES), lambda i: (i % tiles_per_seq, 0)),
                     pl.BlockSpec((tm, LANES), lambda i: (i % tiles_per_seq, 0))]
        args += list(tables)
    out_shape = (jax.ShapeDtypeStruct((rows, 1024), F32), jax.ShapeDtypeStruct((rows, 1024), F32),
                 jax.ShapeDtypeStruct((rows, 512), BF16), jax.ShapeDtypeStruct((rows, 256), BF16),
                 jax.ShapeDtypeStruct((rows, LANES), F32))
    out_specs = (pl.BlockSpec((tm, 1024), lambda i: (i, 0)), pl.BlockSpec((tm, 1024), lambda i: (i, 0)),
                 pl.BlockSpec((tm, 512), lambda i: (i, 0)), pl.BlockSpec((tm, 256), lambda i: (i, 0)),
                 pl.BlockSpec((tm, LANES), lambda i: (i, 0)))
    return pl.pallas_call(
        functools.partial(_inproj_kernel, rope=rope),
        grid=(rows // tm,), in_specs=in_specs, out_specs=out_specs, out_shape=out_shape,
        compiler_params=_cparams(("parallel",)),
        name="inproj_rope" if rope else "inproj",
    )(*args)


def _attn_kernel(*refs, nkv, tq):
    q_ref = refs[0]
    kv_refs = refs[1:1 + nkv]
    o_ref = refs[1 + nkv]
    q = q_ref[0]
    lane = _iota((1, LANES), 1)
    mlo = (lane < HEAD_DIM).astype(BF16)
    mhi = (lane >= HEAD_DIM).astype(BF16)
    kvs = [r[0] for r in kv_refs]
    outs = []
    for j in range(4):
        qc = q[:, LANES * j:LANES * (j + 1)]
        q2 = jnp.concatenate([qc * mlo, qc * mhi], axis=0)
        ss = [lax.dot_general(q2, kv[:, 0:LANES], (((1,), (1,)), ((), ())), preferred_element_type=F32)
              for kv in kvs]
        m = functools.reduce(jnp.maximum, [jnp.max(s, axis=-1, keepdims=True) for s in ss])
        ps = [jnp.exp(s - m) for s in ss]
        l = functools.reduce(lambda a, b: a + b, [jnp.sum(p, axis=-1, keepdims=True) for p in ps])
        o = functools.reduce(lambda a, b: a + b,
                             [jnp.dot(p.astype(BF16), kv[:, LANES:2 * LANES], preferred_element_type=F32)
                              for p, kv in zip(ps, kvs)])
        o = o / l
        outs.append(jnp.where(lane < HEAD_DIM, o[:tq], o[tq:]))
    o_ref[0] = jnp.concatenate(outs, axis=1).astype(BF16)


def _attention(q, kvs, tq):
    b, lq, _ = q.shape
    in_specs = [pl.BlockSpec((1, tq, 512), lambda i, j: (i, j, 0))]
    for kv in kvs:
        in_specs.append(pl.BlockSpec((1, kv.shape[1], 256), lambda i, j: (i, 0, 0)))
    return pl.pallas_call(
        functools.partial(_attn_kernel, nkv=len(kvs), tq=tq),
        grid=(b, lq // tq), in_specs=in_specs,
        out_specs=pl.BlockSpec((1, tq, 512), lambda i, j: (i, j, 0)),
        out_shape=jax.ShapeDtypeStruct((b, lq, 512), BF16),
        compiler_params=_cparams(("parallel", "arbitrary")),
        name="attention",
    )(q, *kvs)


def _head_masks():
    lane = _iota((1, MIX_W), 1)
    return [(lane // HEAD_DIM == h).astype(F32) for h in range(GLA_HEADS)]


def _blockdiag_mask():
    r = _iota((MIX_W, MIX_W), 0) // HEAD_DIM
    c = _iota((MIX_W, MIX_W), 1) // HEAD_DIM
    return r == c


def _finish_rows(o, gate, gain, seg):
    ss = _seg_sum64(o * o, seg)
    return o * lax.rsqrt(ss * (1.0 / HEAD_DIM) + EPS) * gain * _silu(gate)


def _scan_segments(xc_ref, xl_ref, step):
    for seg_i, x_ref in enumerate((xc_ref, xl_ref)):
        nch = x_ref.shape[1] // CHUNK

        def body(i, carry, seg_i=seg_i, nch=nch):
            step(seg_i, i, 0)
            step(seg_i, nch - 1 - i, 1)
            return carry

        lax.fori_loop(0, nch, body, 0)


def _gla_kernel(*refs, need_ctx):
    if need_ctx:
        (xc_ref, xl_ref, sc_ref, sl_ref, wg_ref, bg_ref, gain_ref, seg_ref,
         oc_ref, ol_ref, of_ref, ob_ref, st_ref) = refs
    else:
        (xc_ref, xl_ref, sc_ref, sl_ref, wg_ref, bg_ref, gain_ref, seg_ref,
         ol_ref, of_ref, ob_ref, st_ref) = refs
        oc_ref = None
    lc = xc_ref.shape[1]
    x_refs = (xc_ref, xl_ref)
    s_refs = (sc_ref, sl_ref)
    row_off = (0, lc)
    hmask = _head_masks()
    bd = _blockdiag_mask()
    ti = _iota((CHUNK, CHUNK), 0)
    si = _iota((CHUNK, CHUNK), 1)
    tri = [(si <= ti).astype(BF16), (si >= ti).astype(BF16)]
    rr = _iota((4 * CHUNK, CHUNK), 0)
    cc = _iota((4 * CHUNK, CHUNK), 1)
    t_of_row = (rr // (GLA_HEADS * SUB)) * SUB + rr % SUB
    causal = [cc <= t_of_row, cc >= t_of_row]
    st_ref[...] = jnp.zeros(st_ref.shape, F32)

    def step(seg_i, c, d):
        x_ref, s_ref = x_refs[seg_i], s_refs[seg_i]
        r0 = pl.multiple_of(c * CHUNK, CHUNK)
        q = x_ref[0, pl.ds(r0, CHUNK), 0:MIX_W] * (HEAD_DIM ** -0.5)
        k = x_ref[0, pl.ds(r0, CHUNK), MIX_W:2 * MIX_W]
        v = x_ref[0, pl.ds(r0, CHUNK), 2 * MIX_W:3 * MIX_W]
        sm = s_ref[0, pl.ds(r0, CHUNK), :]
        xg = _dot3(sm, wg_ref[d]) + bg_ref[d]
        g = (jnp.minimum(xg, 0.0) - jnp.log(1.0 + jnp.exp(-jnp.abs(xg)))) * (1.0 / GLA_TAU)
        b = _dot_whl(tri[d], g)
        pieces = []
        for i in range(CHUNK // SUB):
            e = SUB * i if d == 0 else SUB * i + SUB - 1
            bref = b[e:e + 1, :] - g[e:e + 1, :]
            kt = k * jnp.exp(jnp.minimum(bref - b, EXP_CLAMP))
            qt = q[SUB * i:SUB * (i + 1), :] * jnp.exp(b[SUB * i:SUB * (i + 1), :] - bref)
            qs = jnp.concatenate([qt * hmask[h] for h in range(GLA_HEADS)], axis=0)
            pieces.append(_dot_nt(qs, kt))
        scores = jnp.where(causal[d], jnp.concatenate(pieces, axis=0), 0.0)
        r = _dot(scores, v)
        intra = []
        for i in range(CHUNK // SUB):
            acc = None
            for h in range(GLA_HEADS):
                lo = (i * GLA_HEADS + h) * SUB
                term = r[lo:lo + SUB, :] * hmask[h]
                acc = term if acc is None else acc + term
            intra.append(acc)
        o_intra = jnp.concatenate(intra, axis=0)
        st = st_ref[d]
        o = _dot_nt(q * jnp.exp(b), st) + o_intra
        dst = of_ref if d == 0 else ob_ref
        dst[pl.ds(row_off[seg_i] + r0, CHUNK), :] = o
        e = CHUNK - 1 if d == 0 else 0
        b_end = b[e:e + 1, :]
        kh = k * jnp.exp(b_end - b)
        st_ref[d] = st * jnp.exp(b_end) + jnp.where(bd, _dot(v.T, kh), 0.0)

    _scan_segments(xc_ref, xl_ref, step)

    gain = gain_ref[...]
    seg = seg_ref[...]
    blk = 256
    outs = ((oc_ref, xc_ref, 0), (ol_ref, xl_ref, lc))
    for o_ref, x_ref, off in outs:
        if o_ref is None:
            continue

        def fin(i, carry, o_ref=o_ref, x_ref=x_ref, off=off):
            r0 = pl.multiple_of(i * blk, blk)
            o = of_ref[pl.ds(off + r0, blk), :] + ob_ref[pl.ds(off + r0, blk), :]
            gate = x_ref[0, pl.ds(r0, blk), 3 * MIX_W:4 * MIX_W]
            o_ref[0, pl.ds(r0, blk), :] = _finish_rows(o, gate, gain, seg).astype(BF16)
            return carry

        lax.fori_loop(0, x_ref.shape[1] // blk, fin, 0)


def _gla(x_ctx, x_lat, s_ctx, s_lat, wg, bg, gain, seg, need_ctx):
    b, lc, _ = x_ctx.shape
    ll = x_lat.shape[1]
    in_specs = [pl.BlockSpec((1, lc, 1024), lambda i: (i, 0, 0)),
                pl.BlockSpec((1, ll, 1024), lambda i: (i, 0, 0)),
                pl.BlockSpec((1, lc, LANES), lambda i: (i, 0, 0)),
                pl.BlockSpec((1, ll, LANES), lambda i: (i, 0, 0)),
                pl.BlockSpec((2, LANES, MIX_W), lambda i: (0, 0, 0)),
                pl.BlockSpec((2, 1, MIX_W), lambda i: (0, 0, 0)),
                pl.BlockSpec((1, MIX_W), lambda i: (0, 0)),
                pl.BlockSpec((LANES, LANES), lambda i: (0, 0))]
    out_shape = [jax.ShapeDtypeStruct((b, ll, MIX_W), BF16)]
    out_specs = [pl.BlockSpec((1, ll, MIX_W), lambda i: (i, 0, 0))]
    if need_ctx:
        out_shape.insert(0, jax.ShapeDtypeStruct((b, lc, MIX_W), BF16))
        out_specs.insert(0, pl.BlockSpec((1, lc, MIX_W), lambda i: (i, 0, 0)))
    res = pl.pallas_call(
        functools.partial(_gla_kernel, need_ctx=need_ctx),
        grid=(b,), in_specs=in_specs, out_specs=out_specs, out_shape=out_shape,
        scratch_shapes=[pltpu.VMEM((lc + ll, MIX_W), F32), pltpu.VMEM((lc + ll, MIX_W), F32),
                        pltpu.VMEM((2, MIX_W, MIX_W), F32)],
        compiler_params=_cparams(("parallel",)),
        name="gla_scan",
    )(x_ctx, x_lat, s_ctx, s_lat, wg, bg, gain, seg)
    return (res[0], res[1]) if need_ctx else (None, res[0])


def _gdn_kernel(*refs, need_ctx):
    if need_ctx:
        (xc_ref, xl_ref, sc_ref, sl_ref, cw_ref, alog_ref, dtb_ref, sel_ref, selb_ref, gain_ref, seg_ref,
         oc_ref, ol_ref, qkv_ref, of_ref, ob_ref, st_ref) = refs
    else:
        (xc_ref, xl_ref, sc_ref, sl_ref, cw_ref, alog_ref, dtb_ref, sel_ref, selb_ref, gain_ref, seg_ref,
         ol_ref, qkv_ref, of_ref, ob_ref, st_ref) = refs
        oc_ref = None
    lc = xc_ref.shape[1]
    x_refs = (xc_ref, xl_ref)
    s_refs = (sc_ref, sl_ref)
    row_off = (0, lc)
    seg = seg_ref[...]
    hmask = _head_masks()
    bd = _blockdiag_mask()
    ti = _iota((CHUNK, CHUNK), 0)
    si = _iota((CHUNK, CHUNK), 1)
    tri = [(si <= ti).astype(BF16), (si >= ti).astype(BF16)]
    ones = jnp.ones((CHUNK, CHUNK), BF16)
    tp = _iota((CHUNK, MIX_W), 0)
    sp = _iota((CHUNK, MIX_W), 1) % CHUNK
    le = sp <= tp
    ge = sp >= tp
    m_tri = [le, ge]
    m_strict = [sp < tp, sp > tp]
    m_sum = [ge.astype(F32), le.astype(F32)]
    blk16 = (tp // SUB) == (sp // SUB)
    eye = (tp == sp).astype(F32)

    cw = cw_ref[...]
    for seg_i, x_ref in enumerate(x_refs):
        ln = x_ref.shape[1]
        nch = ln // CHUNK

        def conv(c, carry, x_ref=x_ref, ln=ln, nch=nch, off=row_off[seg_i]):
            r0 = pl.multiple_of(c * CHUNK, CHUNK)
            center = x_ref[0, pl.ds(r0, CHUNK), 0:3 * MIX_W]
            p0 = pl.multiple_of(jnp.maximum(r0 - 8, 0), 8)
            n0 = pl.multiple_of(jnp.minimum(r0 + CHUNK, ln - 8), 8)
            prev = x_ref[0, pl.ds(p0, 8), 0:3 * MIX_W] * jnp.where(c > 0, 1.0, 0.0)
            nxt = x_ref[0, pl.ds(n0, 8), 0:3 * MIX_W] * jnp.where(c < nch - 1, 1.0, 0.0)
            ext = jnp.concatenate([prev, center, nxt], axis=0)
            pad = SHORT_CONV // 2
            acc = None
            for j in range(SHORT_CONV):
                term = ext[8 - pad + j:8 - pad + j + CHUNK, :] * cw[j:j + 1, :]
                acc = term if acc is None else acc + term
            y = _silu(acc)
            qk = y[:, 0:2 * MIX_W]
            ss = _seg_sum64(qk * qk, seg)
            qk = qk * lax.rsqrt(ss + EPS)
            qkv_ref[pl.ds(off + r0, CHUNK), 0:MIX_W] = qk[:, 0:MIX_W] * (HEAD_DIM ** -0.5)
            qkv_ref[pl.ds(off + r0, CHUNK), MIX_W:2 * MIX_W] = qk[:, MIX_W:2 * MIX_W]
            qkv_ref[pl.ds(off + r0, CHUNK), 2 * MIX_W:3 * MIX_W] = y[:, 2 * MIX_W:3 * MIX_W]
            return carry

        lax.fori_loop(0, nch, conv, 0)

    st_ref[...] = jnp.zeros(st_ref.shape, F32)

    def pk(y):
        return jnp.where(bd, jnp.concatenate([y] * GDN_HEADS, axis=0), 0.0)

    def mm(x, y):
        return _dot3(x, pk(y))

    def step(seg_i, c, d):
        s_ref = s_refs[seg_i]
        r0 = pl.multiple_of(c * CHUNK, CHUNK)
        row = row_off[seg_i] + r0
        qn = qkv_ref[pl.ds(row, CHUNK), 0:MIX_W]
        kn = qkv_ref[pl.ds(row, CHUNK), MIX_W:2 * MIX_W]
        v = qkv_ref[pl.ds(row, CHUNK), 2 * MIX_W:3 * MIX_W]
        sm = s_ref[0, pl.ds(r0, CHUNK), :]
        g = -jnp.exp(alog_ref[d]) * _softplus(sm + dtb_ref[d])
        beta = _sigmoid(sm)
        gexp = _dot_xhl(g, sel_ref[d])
        bexp = _dot_xhl(beta, selb_ref[d])
        gam_t = _dot_whl(tri[d], gexp)
        gam_s = _dot_whl(ones, gexp * m_sum[d])
        dec = jnp.where(m_tri[d], jnp.exp(jnp.minimum(gam_t - gam_s, 0.0)), 0.0)
        kstack = jnp.concatenate([kn * hmask[h] for h in range(GDN_HEADS)], axis=0)
        kk = _dot_nt(kn, kstack)
        qk = _dot_nt(qn, kstack)
        a = jnp.where(m_strict[d], bexp * dec * kk, 0.0)
        dg = jnp.where(blk16, a, 0.0)
        lo = a - dg
        d2 = mm(dg, dg)
        d4 = mm(d2, d2)
        d8 = mm(d4, d4)
        t_inv = mm(mm(mm(eye - dg, eye + d2), eye + d4), eye + d8)
        m = mm(t_inv, lo)
        egam = jnp.exp(gam_t)
        sols = []
        for rhs in (bexp * v, bexp * egam * kn):
            y = mm(t_inv, rhs)
            z = y + mm(m, mm(m, y))
            sols.append(z - mm(m, z))
        sol_v, sol_k = sols
        st = st_ref[d]
        u = sol_v - _dot(sol_k, st)
        p = jnp.where(m_tri[d], qk * dec, 0.0)
        o = egam * _dot(qn, st) + _dot(p, pk(u))
        dst = of_ref if d == 0 else ob_ref
        dst[pl.ds(row, CHUNK), :] = o
        e = CHUNK - 1 if d == 0 else 0
        g_end = gam_t[e:e + 1, :]
        kh = kn * jnp.exp(g_end - gam_t)
        st_ref[d] = st * jnp.exp(g_end) + jnp.where(bd, _dot(kh.T, u), 0.0)

    _scan_segments(xc_ref, xl_ref, step)

    gain = gain_ref[...]
    blk = 256
    outs = ((oc_ref, xc_ref, 0), (ol_ref, xl_ref, lc))
    for o_ref, x_ref, off in outs:
        if o_ref is None:
            continue

        def fin(i, carry, o_ref=o_ref, x_ref=x_ref, off=off):
            r0 = pl.multiple_of(i * blk, blk)
            o = of_ref[pl.ds(off + r0, blk), :] + ob_ref[pl.ds(off + r0, blk), :]
            gate = x_ref[0, pl.ds(r0, blk), 3 * MIX_W:4 * MIX_W]
            o_ref[0, pl.ds(r0, blk), :] = _finish_rows(o, gate, gain, seg).astype(BF16)
            return carry

        lax.fori_loop(0, x_ref.shape[1] // blk, fin, 0)


def _gdn(x_ctx, x_lat, s_ctx, s_lat, cw, alog, dtb, sel, selb, gain, seg, need_ctx):
    b, lc, _ = x_ctx.shape
    ll = x_lat.shape[1]
    in_specs = [pl.BlockSpec((1, lc, 1024), lambda i: (i, 0, 0)),
                pl.BlockSpec((1, ll, 1024), lambda i: (i, 0, 0)),
                pl.BlockSpec((1, lc, LANES), lambda i: (i, 0, 0)),
                pl.BlockSpec((1, ll, LANES), lambda i: (i, 0, 0)),
                pl.BlockSpec((8, 3 * MIX_W), lambda i: (0, 0)),
                pl.BlockSpec((2, 1, LANES), lambda i: (0, 0, 0)),
                pl.BlockSpec((2, 1, LANES), lambda i: (0, 0, 0)),
                pl.BlockSpec((2, LANES, MIX_W), lambda i: (0, 0, 0)),
                pl.BlockSpec((2, LANES, MIX_W), lambda i: (0, 0, 0)),
                pl.BlockSpec((1, MIX_W), lambda i: (0, 0)),
                pl.BlockSpec((LANES, LANES), lambda i: (0, 0))]
    out_shape = [jax.ShapeDtypeStruct((b, ll, MIX_W), BF16)]
    out_specs = [pl.BlockSpec((1, ll, MIX_W), lambda i: (i, 0, 0))]
    if need_ctx:
        out_shape.insert(0, jax.ShapeDtypeStruct((b, lc, MIX_W), BF16))
        out_specs.insert(0, pl.BlockSpec((1, lc, MIX_W), lambda i: (i, 0, 0)))
    res = pl.pallas_call(
        functools.partial(_gdn_kernel, need_ctx=need_ctx),
        grid=(b,), in_specs=in_specs, out_specs=out_specs, out_shape=out_shape,
        scratch_shapes=[pltpu.VMEM((lc + ll, 3 * MIX_W), F32),
                        pltpu.VMEM((lc + ll, MIX_W), F32), pltpu.VMEM((lc + ll, MIX_W), F32),
                        pltpu.VMEM((2, MIX_W, MIX_W), F32)],
        compiler_params=_cparams(("parallel",)),
        name="gdn_scan",
    )(x_ctx, x_lat, s_ctx, s_lat, cw, alog, dtb, sel, selb, gain, seg)
    return (res[0], res[1]) if need_ctx else (None, res[0])


def _outproj_kernel(gla_ref, gdn_ref, att_ref, h_ref, g_ref, w_ref, o_ref):
    y = (jnp.dot(gla_ref[...], w_ref[0:256, :], preferred_element_type=F32)
         + jnp.dot(gdn_ref[...], w_ref[256:512, :], preferred_element_type=F32)
         + jnp.dot(att_ref[...], w_ref[512:1024, :], preferred_element_type=F32))
    o_ref[...] = h_ref[...] + g_ref[0] * y


def _outproj(gla, gdn, att, h2d, mod144, mod_row_fn, w, seq_len, tm):
    rows, d = h2d.shape
    tiles_per_seq = seq_len // tm
    return pl.pallas_call(
        _outproj_kernel,
        grid=(rows // tm,),
        in_specs=[pl.BlockSpec((tm, 256), lambda i: (i, 0)),
                  pl.BlockSpec((tm, 256), lambda i: (i, 0)),
                  pl.BlockSpec((tm, 512), lambda i: (i, 0)),
                  pl.BlockSpec((tm, d), lambda i: (i, 0)),
                  pl.BlockSpec((1, 1, d), lambda i: (mod_row_fn(i // tiles_per_seq) * 6 + 2, 0, 0)),
                  pl.BlockSpec((d, d), lambda i: (0, 0))],
        out_specs=pl.BlockSpec((tm, d), lambda i: (i, 0)),
        out_shape=jax.ShapeDtypeStruct((rows, d), F32),
        compiler_params=_cparams(("parallel",)),
        name="outproj",
    )(gla, gdn, att, h2d, mod144, w)


def _norm_mod(h_ref, sh_ref, sc_ref, gain_ref):
    x = h_ref[...]
    ms = jnp.mean(x * x, axis=-1, keepdims=True)
    return x * lax.rsqrt(ms + EPS) * (gain_ref[...] * (1.0 + sc_ref[0])) + sh_ref[0]


def _ffn_kernel(h_ref, sh_ref, sc_ref, g_ref, gain_ref, wg_ref, wu_ref, wd_ref, o_ref, b_scr, acc_scr, *, nf):
    f = pl.program_id(1)

    @pl.when(f == 0)
    def _():
        b_scr[...] = _norm_mod(h_ref, sh_ref, sc_ref, gain_ref).astype(BF16)
        acc_scr[...] = jnp.zeros(acc_scr.shape, F32)

    b = b_scr[...]
    gg = jnp.dot(b, wg_ref[...], preferred_element_type=F32)
    uu = jnp.dot(b, wu_ref[...], preferred_element_type=F32)
    hid = (_silu(gg) * uu).astype(BF16)
    acc_scr[...] += jnp.dot(hid, wd_ref[...], preferred_element_type=F32)

    @pl.when(f == nf - 1)
    def _():
        o_ref[...] = h_ref[...] + g_ref[0] * acc_scr[...]


def _ffn(h2d, mod144, mod_row_fn, gain, w_gu, w_down, seq_len, tm, tf):
    rows, d = h2d.shape
    dff = w_down.shape[0]
    nf = dff // tf
    tiles_per_seq = seq_len // tm

    def mod_spec(k):
        return pl.BlockSpec((1, 1, d), lambda i, f: (mod_row_fn(i // tiles_per_seq) * 6 + k, 0, 0))

    return pl.pallas_call(
        functools.partial(_ffn_kernel, nf=nf),
        grid=(rows // tm, nf),
        in_specs=[pl.BlockSpec((tm, d), lambda i, f: (i, 0)),
                  mod_spec(3), mod_spec(4), mod_spec(5),
                  pl.BlockSpec((1, d), lambda i, f: (0, 0)),
                  pl.BlockSpec((d, tf), lambda i, f: (0, f)),
                  pl.BlockSpec((d, tf), lambda i, f: (0, nf + f)),
                  pl.BlockSpec((tf, d), lambda i, f: (f, 0))],
        out_specs=pl.BlockSpec((tm, d), lambda i, f: (i, 0)),
        out_shape=jax.ShapeDtypeStruct((rows, d), F32),
        scratch_shapes=[pltpu.VMEM((tm, d), BF16), pltpu.VMEM((tm, d), F32)],
        compiler_params=_cparams(("parallel", "arbitrary")),
        name="ffn",
    )(h2d, mod144, mod144, mod144, gain, w_gu, w_gu, w_down)


def _moe_kernel(h_ref, sh_ref, sc_ref, g_ref, gain_ref, wr_ref, br_ref, wg_ref, wu_ref, wd_ref, o_ref,
                b_scr, comb_scr, acc_scr, *, nf):
    e = pl.program_id(1)
    f = pl.program_id(2)
    lane = _iota((1, LANES), 1)
    lane_f = lane.astype(F32)

    @pl.when((e == 0) & (f == 0))
    def _():
        b = _norm_mod(h_ref, sh_ref, sc_ref, gain_ref)
        b_scr[...] = b.astype(BF16)
        acc_scr[...] = jnp.zeros(acc_scr.shape, F32)
        logits = _dot3(b, wr_ref[...]) + br_ref[...]
        logits = jnp.where(lane < N_EXPERTS, logits, -jnp.inf)
        m1 = jnp.max(logits, axis=-1, keepdims=True)
        i1 = jnp.min(jnp.where(logits == m1, lane_f, float(LANES)), axis=-1, keepdims=True)
        rest = jnp.where(lane_f == i1, -jnp.inf, logits)
        m2 = jnp.max(rest, axis=-1, keepdims=True)
        i2 = jnp.min(jnp.where(rest == m2, lane_f, float(LANES)), axis=-1, keepdims=True)
        t = jnp.exp(m2 - m1)
        w1 = 1.0 / (1.0 + t)
        comb_scr[...] = jnp.where(lane_f == i1, w1, 0.0) + jnp.where(lane_f == i2, t * w1, 0.0)

    b = b_scr[...]
    gg = jnp.dot(b, wg_ref[0], preferred_element_type=F32)
    uu = jnp.dot(b, wu_ref[0], preferred_element_type=F32)
    hid = (_silu(gg) * uu).astype(BF16)
    w_e = jnp.sum(jnp.where(lane == e, comb_scr[...], 0.0), axis=-1, keepdims=True)
    acc_scr[...] += w_e * jnp.dot(hid, wd_ref[0], preferred_element_type=F32)

    @pl.when((e == N_EXPERTS - 1) & (f == nf - 1))
    def _():
        o_ref[...] = h_ref[...] + g_ref[0] * acc_scr[...]


def _moe(h2d, mod144, mod_row_fn, gain, w_router, b_router, w_gu, w_down, seq_len, tm, tf):
    rows, d = h2d.shape
    ne, dff, _ = w_down.shape
    nf = dff // tf
    tiles_per_seq = seq_len // tm

    def mod_spec(k):
        return pl.BlockSpec((1, 1, d), lambda i, e, f: (mod_row_fn(i // tiles_per_seq) * 6 + k, 0, 0))

    return pl.pallas_call(
        functools.partial(_moe_kernel, nf=nf),
        grid=(rows // tm, ne, nf),
        in_specs=[pl.BlockSpec((tm, d), lambda i, e, f: (i, 0)),
                  mod_spec(3), mod_spec(4), mod_spec(5),
                  pl.BlockSpec((1, d), lambda i, e, f: (0, 0)),
                  pl.BlockSpec((d, LANES), lambda i, e, f: (0, 0)),
                  pl.BlockSpec((1, LANES), lambda i, e, f: (0, 0)),
                  pl.BlockSpec((1, d, tf), lambda i, e, f: (e, 0, f)),
                  pl.BlockSpec((1, d, tf), lambda i, e, f: (e, 0, nf + f)),
                  pl.BlockSpec((1, tf, d), lambda i, e, f: (e, f, 0))],
        out_specs=pl.BlockSpec((tm, d), lambda i, e, f: (i, 0)),
        out_shape=jax.ShapeDtypeStruct((rows, d), F32),
        scratch_shapes=[pltpu.VMEM((tm, d), BF16), pltpu.VMEM((tm, LANES), F32), pltpu.VMEM((tm, d), F32)],
        compiler_params=_cparams(("parallel", "arbitrary", "arbitrary")),
        name="moe",
    )(h2d, mod144, mod144, mod144, gain, w_router, b_router, w_gu, w_gu, w_down)


ATT_HEAD_ORDER = (0, 4, 1, 5, 2, 6, 3, 7)


def _layout_w_in(w):
    gla = w[:, 0:1024]
    glow = w[:, 1024:1056]
    gdn = w[:, 1056:2080]
    ab = w[:, 2080:2096]
    q = jnp.concatenate([w[:, 2096 + HEAD_DIM * h:2096 + HEAD_DIM * (h + 1)] for h in ATT_HEAD_ORDER], axis=1)
    kv = w[:, 2608:2864]
    pad = jnp.zeros((w.shape[0], LANES - 48), w.dtype)
    return jnp.concatenate([gla, gdn, q, kv, glow, ab, pad], axis=1).astype(BF16)


def _layout_w_out(w):
    att = [w[512 + HEAD_DIM * h:512 + HEAD_DIM * (h + 1)] for h in ATT_HEAD_ORDER]
    return jnp.concatenate([w[0:512]] + att, axis=0).astype(BF16)


def _rope_tables(seq_len):
    rows = seq_len // GRID_W
    row = jnp.repeat(jnp.arange(rows), GRID_W).astype(F32)
    col = jnp.tile(jnp.arange(GRID_W), rows).astype(F32)
    inv_freq = ROPE_THETA ** (-jnp.arange(0, HEAD_DIM // 2, 2, dtype=F32) / (HEAD_DIM // 2))
    ar = row[:, None] * inv_freq
    ac = col[:, None] * inv_freq
    cos = jnp.concatenate([jnp.cos(ar), jnp.cos(ar), jnp.cos(ac), jnp.cos(ac)], axis=-1)
    sin = jnp.concatenate([-jnp.sin(ar), jnp.sin(ar), -jnp.sin(ac), jnp.sin(ac)], axis=-1)
    return jnp.tile(cos, (1, 2)), jnp.tile(sin, (1, 2))


def _seg_matrix():
    i = np.arange(LANES)
    return jnp.asarray((i[:, None] // HEAD_DIM) == (i[None, :] // HEAD_DIM), dtype=BF16)


def _gdn_select():
    sel = np.zeros((2, LANES, MIX_W), np.float32)
    selb = np.zeros((2, LANES, MIX_W), np.float32)
    for d in range(2):
        for h in range(GDN_HEADS):
            sel[d, 32 + GDN_HEADS * d + h, HEAD_DIM * h:HEAD_DIM * (h + 1)] = 1.0
            selb[d, 40 + GDN_HEADS * d + h, HEAD_DIM * h:HEAD_DIM * (h + 1)] = 1.0
    return jnp.asarray(sel, BF16), jnp.asarray(selb, BF16)


def _lane_rows(vals, base):
    out = jnp.zeros((2, 1, LANES), F32)
    for d in range(2):
        out = out.at[d, 0, base + GDN_HEADS * d:base + GDN_HEADS * (d + 1)].set(vals[d].astype(F32))
    return out


def kernel(x, c, ctx, c_ctx, w_mod, b_mod, norm_mix, norm_ffn, w_in, gla_gate_up, gla_gate_bias, gla_out_gain,
           gdn_conv, gdn_a_log, gdn_dt_bias, gdn_out_gain, att_q_gain, att_k_gain, w_out, ffn_gate_up, ffn_down,
           moe_router, moe_router_bias, moe_gate_up, moe_down):
    bsz, seq, d = x.shape
    lctx = ctx.shape[1]
    depth = w_mod.shape[0]
    ctx_row = bsz

    mod_rows = ((bsz + 1 + 7) // 8) * 8
    cvec = jnp.concatenate([c, c_ctx[None, :], jnp.zeros((mod_rows - bsz - 1, d), F32)], axis=0)
    mods = _modulation(cvec, w_mod, b_mod)

    seg = _seg_matrix()
    tables = _rope_tables(seq)
    sel, selb = _gdn_select()
    lat_row = lambda b: b
    ctx_row_fn = lambda b: ctx_row

    h_lat = x.reshape(bsz * seq, d)
    h_ctx = ctx.reshape(bsz * lctx, d)
    for layer in range(depth):
        need_ctx = layer < depth - 1
        mod144 = mods[layer].reshape(mod_rows * 6, 1, d)
        w_p = _layout_w_in(w_in[layer])
        w_o = _layout_w_out(w_out[layer])
        hg = jnp.concatenate([jnp.tile(att_q_gain[layer], ATT_Q_HEADS) * (HEAD_DIM ** -0.5),
                              jnp.tile(att_k_gain[layer], ATT_KV_HEADS)])[None, :].astype(F32)
        gain_mix = norm_mix[layer][None, :]
        gain_ffn = norm_ffn[layer][None, :]

        gla_l, gdn_l, q_l, kv_l, sm_l = _inproj(h_lat, mod144, lat_row, gain_mix, w_p, hg, seg, tables, seq, 256)
        gla_c, gdn_c, q_c, kv_c, sm_c = _inproj(h_ctx, mod144, ctx_row_fn, gain_mix, w_p, hg, seg, None, lctx, 256)

        r3 = lambda t, n: t.reshape(bsz, n, t.shape[-1])
        wg = jnp.zeros((2, LANES, MIX_W), F32)
        for dd in range(2):
            wg = wg.at[dd, GLA_GATE_RANK * dd:GLA_GATE_RANK * (dd + 1), :].set(gla_gate_up[layer, dd].astype(F32))
        bg = gla_gate_bias[layer].reshape(2, 1, MIX_W).astype(F32)
        gla_gain = jnp.tile(gla_out_gain[layer], GLA_HEADS)[None, :].astype(F32)
        o_gla_c, o_gla_l = _gla(r3(gla_c, lctx), r3(gla_l, seq), r3(sm_c, lctx), r3(sm_l, seq),
                                wg, bg, gla_gain, seg, need_ctx)

        cw = jnp.concatenate([gdn_conv[layer].astype(F32), jnp.zeros((8 - SHORT_CONV, 3 * MIX_W), F32)], axis=0)
        alog = _lane_rows(gdn_a_log[layer], 32)
        dtb = _lane_rows(gdn_dt_bias[layer], 32)
        gdn_gain = jnp.tile(gdn_out_gain[layer], GDN_HEADS)[None, :].astype(F32)
        o_gdn_c, o_gdn_l = _gdn(r3(gdn_c, lctx), r3(gdn_l, seq), r3(sm_c, lctx), r3(sm_l, seq),
                                cw, alog, dtb, sel, selb, gdn_gain, seg, need_ctx)

        o_att_l = _attention(r3(q_l, seq), [r3(kv_l, seq), r3(kv_c, lctx)], 128)
        h_lat = _outproj(o_gla_l.reshape(-1, MIX_W), o_gdn_l.reshape(-1, MIX_W), o_att_l.reshape(-1, 512),
                         h_lat, mod144, lat_row, w_o, seq, 512)
        if need_ctx:
            o_att_c = _attention(r3(q_c, lctx), [r3(kv_c, lctx)], 128)
            h_ctx = _outproj(o_gla_c.reshape(-1, MIX_W), o_gdn_c.reshape(-1, MIX_W), o_att_c.reshape(-1, 512),
                             h_ctx, mod144, ctx_row_fn, w_o, lctx, 256)

        j = layer // 2
        if layer % 2 == 0:
            w_gu = ffn_gate_up[j].astype(BF16)
            w_dn = ffn_down[j].astype(BF16)
            h_lat = _ffn(h_lat, mod144, lat_row, gain_ffn, w_gu, w_dn, seq, 512, 1408)
            if need_ctx:
                h_ctx = _ffn(h_ctx, mod144, ctx_row_fn, gain_ffn, w_gu, w_dn, lctx, 256, 1408)
        else:
            w_gu = moe_gate_up[j].astype(BF16)
            w_dn = moe_down[j].astype(BF16)
            w_r = jnp.concatenate([moe_router[j].astype(F32), jnp.zeros((d, LANES - N_EXPERTS), F32)], axis=1)
            b_r = jnp.concatenate([moe_router_bias[j].astype(F32), jnp.zeros((LANES - N_EXPERTS,), F32)])[None, :]
            h_lat = _moe(h_lat, mod144, lat_row, gain_ffn, w_r, b_r, w_gu, w_dn, seq, 512, 1408)
            if need_ctx:
                h_ctx = _moe(h_ctx, mod144, ctx_row_fn, gain_ffn, w_r, b_r, w_gu, w_dn, lctx, 256, 1408)
    return h_lat.reshape(bsz, seq, d)
```

```python
import functools

import numpy as np
import jax
import jax.numpy as jnp
from jax import lax
from jax.experimental import pallas as pl
from jax.experimental.pallas import tpu as pltpu

F32 = jnp.float32
BF16 = jnp.bfloat16

GRID_W = 64
HEAD_DIM = 64
CHUNK = 64
SUB = 16
EPS = 1e-6
GLA_HEADS = 4
GLA_GATE_RANK = 16
GLA_TAU = 16.0
GDN_HEADS = 4
SHORT_CONV = 5
ATT_Q_HEADS = 8
ATT_KV_HEADS = 2
ROPE_THETA = 10000.0
N_EXPERTS = 8
MIX_W = GLA_HEADS * HEAD_DIM
EXP_CLAMP = 80.0

LANES = 128
V7X_VMEM_BYTES = 64 * 1024 * 1024
VMEM_LIMIT = 56 * 1024 * 1024


def _cparams(sem):
    return pltpu.CompilerParams(dimension_semantics=sem, vmem_limit_bytes=VMEM_LIMIT)


def _silu(x):
    return x / (1.0 + jnp.exp(-x))


def _sigmoid(x):
    return 1.0 / (1.0 + jnp.exp(-x))


def _softplus(x):
    return jnp.maximum(x, 0.0) + jnp.log(1.0 + jnp.exp(-jnp.abs(x)))


def _dot(a, b):
    return jnp.dot(a.astype(BF16), b.astype(BF16), preferred_element_type=F32)


def _dot_nt(a, b):
    return lax.dot_general(a.astype(BF16), b.astype(BF16), (((1,), (1,)), ((), ())),
                           preferred_element_type=F32)


def _split(x):
    hi = x.astype(BF16)
    lo = (x - hi.astype(F32)).astype(BF16)
    return hi, lo


def _dot_xhl(x, w):
    hi, lo = _split(x)
    w = w.astype(BF16)
    return (jnp.dot(hi, w, preferred_element_type=F32) + jnp.dot(lo, w, preferred_element_type=F32))


def _dot_whl(w, x):
    hi, lo = _split(x)
    w = w.astype(BF16)
    return (jnp.dot(w, hi, preferred_element_type=F32) + jnp.dot(w, lo, preferred_element_type=F32))


def _dot3(a, b):
    ah, al = _split(a)
    bh, bl = _split(b)
    return (jnp.dot(ah, bh, preferred_element_type=F32) + jnp.dot(ah, bl, preferred_element_type=F32)
            + jnp.dot(al, bh, preferred_element_type=F32))


def _seg_sum64(sq, seg):
    outs = []
    for j in range(sq.shape[1] // LANES):
        outs.append(_dot_xhl(sq[:, LANES * j:LANES * (j + 1)], seg))
    return outs[0] if len(outs) == 1 else jnp.concatenate(outs, axis=1)


def _iota(shape, dim):
    return lax.broadcasted_iota(jnp.int32, shape, dim)


def _mod_kernel(c_ref, w_ref, b_ref, o_ref):
    s = _silu(c_ref[...])
    o_ref[0] = _dot(s, w_ref[0]) + b_ref[0]


def _modulation(cvec, w_mod, b_mod):
    depth, d, n = w_mod.shape
    rows = cvec.shape[0]
    tn = 1536
    return pl.pallas_call(
        _mod_kernel,
        grid=(depth, n // tn),
        in_specs=[pl.BlockSpec((rows, d), lambda l, j: (0, 0)),
                  pl.BlockSpec((1, d, tn), lambda l, j: (l, 0, j)),
                  pl.BlockSpec((1, 1, tn), lambda l, j: (l, 0, j))],
        out_specs=pl.BlockSpec((1, rows, tn), lambda l, j: (l, 0, j)),
        out_shape=jax.ShapeDtypeStruct((depth, rows, n), F32),
        compiler_params=_cparams(("arbitrary", "arbitrary")),
        name="modulation",
    )(cvec, w_mod, b_mod.reshape(depth, 1, n))


def _swap16(n, lane):
    fwd = pltpu.roll(n, LANES - 16, 1)
    bwd = pltpu.roll(n, 16, 1)
    return jnp.where((lane % 32) < 16, fwd, bwd)


def _inproj_kernel(*refs, rope):
    if rope:
        (h_ref, sh_ref, sc_ref, gain_ref, w_ref, hg_ref, seg_ref, cos_ref, sin_ref,
         gla_ref, gdn_ref, q_ref, kv_ref, small_ref) = refs
    else:
        (h_ref, sh_ref, sc_ref, gain_ref, w_ref, hg_ref, seg_ref,
         gla_ref, gdn_ref, q_ref, kv_ref, small_ref) = refs
    x = h_ref[...]
    ms = jnp.mean(x * x, axis=-1, keepdims=True)
    a = x * lax.rsqrt(ms + EPS) * (gain_ref[...] * (1.0 + sc_ref[0])) + sh_ref[0]
    p = jnp.dot(a.astype(BF16), w_ref[...], preferred_element_type=F32)
    gla_ref[...] = p[:, 0:1024]
    gdn_ref[...] = p[:, 1024:2048]
    small_ref[...] = p[:, 2816:2944]
    seg = seg_ref[...]
    lane = _iota((1, LANES), 1)
    outs = []
    for j in range(5):
        t = p[:, 2048 + LANES * j:2048 + LANES * (j + 1)]
        ss = _dot_xhl(t * t, seg)
        n = t * lax.rsqrt(ss * (1.0 / HEAD_DIM) + EPS) * hg_ref[:, LANES * j:LANES * (j + 1)]
        if rope:
            n = n * cos_ref[...] + _swap16(n, lane) * sin_ref[...]
        outs.append(n)
    q_ref[...] = jnp.concatenate(outs[:4], axis=1).astype(BF16)
    kv_ref[...] = jnp.concatenate([outs[4], p[:, 2688:2816]], axis=1).astype(BF16)


def _inproj(h2d, mod144, mod_row_fn, gain, w_p, hg, seg, tables, seq_len, tm):
    rows, d = h2d.shape
    n_all = w_p.shape[1]
    rope = tables is not None
    tiles_per_seq = seq_len // tm
    in_specs = [pl.BlockSpec((tm, d), lambda i: (i, 0)),
                pl.BlockSpec((1, 1, d), lambda i: (mod_row_fn(i // tiles_per_seq) * 6 + 0, 0, 0)),
                pl.BlockSpec((1, 1, d), lambda i: (mod_row_fn(i // tiles_per_seq) * 6 + 1, 0, 0)),
                pl.BlockSpec((1, d), lambda i: (0, 0)),
                pl.BlockSpec((d, n_all), lambda i: (0, 0)),
                pl.BlockSpec((1, 640), lambda i: (0, 0)),
                pl.BlockSpec((LANES, LANES), lambda i: (0, 0))]
    args = [h2d, mod144, mod144, gain, w_p, hg, seg]
    if rope:
        in_specs += [pl.BlockSpec((tm, LANES), lambda i: (i % tiles_per_seq, 0)),
                     pl.BlockSpec((tm, LANES), lambda i: (i % tiles_per_seq, 0))]
        args += list(tables)
    out_shape = (jax.ShapeDtypeStruct((rows, 1024), F32), jax.ShapeDtypeStruct((rows, 1024), F32),
                 jax.ShapeDtypeStruct((rows, 512), BF16), jax.ShapeDtypeStruct((rows, 256), BF16),
                 jax.ShapeDtypeStruct((rows, LANES), F32))
    out_specs = (pl.BlockSpec((tm, 1024), lambda i: (i, 0)), pl.BlockSpec((tm, 1024), lambda i: (i, 0)),
                 pl.BlockSpec((tm, 512), lambda i: (i, 0)), pl.BlockSpec((tm, 256), lambda i: (i, 0)),
                 pl.BlockSpec((tm, LANES), lambda i: (i, 0)))
    return pl.pallas_call(
        functools.partial(_inproj_kernel, rope=rope),
        grid=(rows // tm,), in_specs=in_specs, out_specs=out_specs, out_shape=out_shape,
        compiler_params=_cparams(("parallel",)),
        name="inproj_rope" if rope else "inproj",
    )(*args)


def _attn_kernel(*refs, nkv, tq):
    q_ref = refs[0]
    kv_refs = refs[1:1 + nkv]
    o_ref = refs[1 + nkv]
    q = q_ref[0]
    lane = _iota((1, LANES), 1)
    mlo = (lane < HEAD_DIM).astype(BF16)
    mhi = (lane >= HEAD_DIM).astype(BF16)
    kvs = [r[0] for r in kv_refs]
    outs = []
    for j in range(4):
        qc = q[:, LANES * j:LANES * (j + 1)]
        q2 = jnp.concatenate([qc * mlo, qc * mhi], axis=0)
        ss = [lax.dot_general(q2, kv[:, 0:LANES], (((1,), (1,)), ((), ())), preferred_element_type=F32)
              for kv in kvs]
        m = functools.reduce(jnp.maximum, [jnp.max(s, axis=-1, keepdims=True) for s in ss])
        ps = [jnp.exp(s - m) for s in ss]
        l = functools.reduce(lambda a, b: a + b, [jnp.sum(p, axis=-1, keepdims=True) for p in ps])
        o = functools.reduce(lambda a, b: a + b,
                             [jnp.dot(p.astype(BF16), kv[:, LANES:2 * LANES], preferred_element_type=F32)
                              for p, kv in zip(ps, kvs)])
        o = o / l
        outs.append(jnp.where(lane < HEAD_DIM, o[:tq], o[tq:]))
    o_ref[0] = jnp.concatenate(outs, axis=1).astype(BF16)


def _attention(q, kvs, tq):
    b, lq, _ = q.shape
    in_specs = [pl.BlockSpec((1, tq, 512), lambda i, j: (i, j, 0))]
    for kv in kvs:
        in_specs.append(pl.BlockSpec((1, kv.shape[1], 256), lambda i, j: (i, 0, 0)))
    return pl.pallas_call(
        functools.partial(_attn_kernel, nkv=len(kvs), tq=tq),
        grid=(b, lq // tq), in_specs=in_specs,
        out_specs=pl.BlockSpec((1, tq, 512), lambda i, j: (i, j, 0)),
        out_shape=jax.ShapeDtypeStruct((b, lq, 512), BF16),
        compiler_params=_cparams(("parallel", "arbitrary")),
        name="attention",
    )(q, *kvs)


def _head_masks():
    lane = _iota((1, MIX_W), 1)
    return [(lane // HEAD_DIM == h).astype(F32) for h in range(GLA_HEADS)]


def _blockdiag_mask():
    r = _iota((MIX_W, MIX_W), 0) // HEAD_DIM
    c = _iota((MIX_W, MIX_W), 1) // HEAD_DIM
    return r == c


def _finish_rows(o, gate, gain, seg):
    ss = _seg_sum64(o * o, seg)
    return o * lax.rsqrt(ss * (1.0 / HEAD_DIM) + EPS) * gain * _silu(gate)


def _scan_segments(xc_ref, xl_ref, step):
    for seg_i, x_ref in enumerate((xc_ref, xl_ref)):
        nch = x_ref.shape[1] // CHUNK

        def body(i, carry, seg_i=seg_i, nch=nch):
            step(seg_i, i, 0)
            step(seg_i, nch - 1 - i, 1)
            return carry

        lax.fori_loop(0, nch, body, 0)


def _gla_kernel(*refs, need_ctx):
    if need_ctx:
        (xc_ref, xl_ref, sc_ref, sl_ref, wg_ref, bg_ref, gain_ref, seg_ref,
         oc_ref, ol_ref, of_ref, ob_ref, st_ref) = refs
    else:
        (xc_ref, xl_ref, sc_ref, sl_ref, wg_ref, bg_ref, gain_ref, seg_ref,
         ol_ref, of_ref, ob_ref, st_ref) = refs
        oc_ref = None
    lc = xc_ref.shape[1]
    x_refs = (xc_ref, xl_ref)
    s_refs = (sc_ref, sl_ref)
    row_off = (0, lc)
    hmask = _head_masks()
    bd = _blockdiag_mask()
    ti = _iota((CHUNK, CHUNK), 0)
    si = _iota((CHUNK, CHUNK), 1)
    tri = [(si <= ti).astype(BF16), (si >= ti).astype(BF16)]
    rr = _iota((4 * CHUNK, CHUNK), 0)
    cc = _iota((4 * CHUNK, CHUNK), 1)
    t_of_row = (rr // (GLA_HEADS * SUB)) * SUB + rr % SUB
    causal = [cc <= t_of_row, cc >= t_of_row]
    st_ref[...] = jnp.zeros(st_ref.shape, F32)

    def step(seg_i, c, d):
        x_ref, s_ref = x_refs[seg_i], s_refs[seg_i]
        r0 = pl.multiple_of(c * CHUNK, CHUNK)
        q = x_ref[0, pl.ds(r0, CHUNK), 0:MIX_W] * (HEAD_DIM ** -0.5)
        k = x_ref[0, pl.ds(r0, CHUNK), MIX_W:2 * MIX_W]
        v = x_ref[0, pl.ds(r0, CHUNK), 2 * MIX_W:3 * MIX_W]
        sm = s_ref[0, pl.ds(r0, CHUNK), :]
        xg = _dot3(sm, wg_ref[d]) + bg_ref[d]
        g = (jnp.minimum(xg, 0.0) - jnp.log(1.0 + jnp.exp(-jnp.abs(xg)))) * (1.0 / GLA_TAU)
        b = _dot_whl(tri[d], g)
        pieces = []
        for i in range(CHUNK // SUB):
            e = SUB * i if d == 0 else SUB * i + SUB - 1
            bref = b[e:e + 1, :] - g[e:e + 1, :]
            kt = k * jnp.exp(jnp.minimum(bref - b, EXP_CLAMP))
            qt = q[SUB * i:SUB * (i + 1), :] * jnp.exp(b[SUB * i:SUB * (i + 1), :] - bref)
            qs = jnp.concatenate([qt * hmask[h] for h in range(GLA_HEADS)], axis=0)
            pieces.append(_dot_nt(qs, kt))
        scores = jnp.where(causal[d], jnp.concatenate(pieces, axis=0), 0.0)
        r = _dot(scores, v)
        intra = []
        for i in range(CHUNK // SUB):
            acc = None
            for h in range(GLA_HEADS):
                lo = (i * GLA_HEADS + h) * SUB
                term = r[lo:lo + SUB, :] * hmask[h]
                acc = term if acc is None else acc + term
            intra.append(acc)
        o_intra = jnp.concatenate(intra, axis=0)
        st = st_ref[d]
        o = _dot_nt(q * jnp.exp(b), st) + o_intra
        dst = of_ref if d == 0 else ob_ref
        dst[pl.ds(row_off[seg_i] + r0, CHUNK), :] = o
        e = CHUNK - 1 if d == 0 else 0
        b_end = b[e:e + 1, :]
        kh = k * jnp.exp(b_end - b)
        st_ref[d] = st * jnp.exp(b_end) + jnp.where(bd, _dot(v.T, kh), 0.0)

    _scan_segments(xc_ref, xl_ref, step)

    gain = gain_ref[...]
    seg = seg_ref[...]
    blk = 256
    outs = ((oc_ref, xc_ref, 0), (ol_ref, xl_ref, lc))
    for o_ref, x_ref, off in outs:
        if o_ref is None:
            continue

        def fin(i, carry, o_ref=o_ref, x_ref=x_ref, off=off):
            r0 = pl.multiple_of(i * blk, blk)
            o = of_ref[pl.ds(off + r0, blk), :] + ob_ref[pl.ds(off + r0, blk), :]
            gate = x_ref[0, pl.ds(r0, blk), 3 * MIX_W:4 * MIX_W]
            o_ref[0, pl.ds(r0, blk), :] = _finish_rows(o, gate, gain, seg).astype(BF16)
            return carry

        lax.fori_loop(0, x_ref.shape[1] // blk, fin, 0)


def _gla(x_ctx, x_lat, s_ctx, s_lat, wg, bg, gain, seg, need_ctx):
    b, lc, _ = x_ctx.shape
    ll = x_lat.shape[1]
    in_specs = [pl.BlockSpec((1, lc, 1024), lambda i: (i, 0, 0)),
                pl.BlockSpec((1, ll, 1024), lambda i: (i, 0, 0)),
                pl.BlockSpec((1, lc, LANES), lambda i: (i, 0, 0)),
                pl.BlockSpec((1, ll, LANES), lambda i: (i, 0, 0)),
                pl.BlockSpec((2, LANES, MIX_W), lambda i: (0, 0, 0)),
                pl.BlockSpec((2, 1, MIX_W), lambda i: (0, 0, 0)),
                pl.BlockSpec((1, MIX_W), lambda i: (0, 0)),
                pl.BlockSpec((LANES, LANES), lambda i: (0, 0))]
    out_shape = [jax.ShapeDtypeStruct((b, ll, MIX_W), BF16)]
    out_specs = [pl.BlockSpec((1, ll, MIX_W), lambda i: (i, 0, 0))]
    if need_ctx:
        out_shape.insert(0, jax.ShapeDtypeStruct((b, lc, MIX_W), BF16))
        out_specs.insert(0, pl.BlockSpec((1, lc, MIX_W), lambda i: (i, 0, 0)))
    res = pl.pallas_call(
        functools.partial(_gla_kernel, need_ctx=need_ctx),
        grid=(b,), in_specs=in_specs, out_specs=out_specs, out_shape=out_shape,
        scratch_shapes=[pltpu.VMEM((lc + ll, MIX_W), F32), pltpu.VMEM((lc + ll, MIX_W), F32),
                        pltpu.VMEM((2, MIX_W, MIX_W), F32)],
        compiler_params=_cparams(("parallel",)),
        name="gla_scan",
    )(x_ctx, x_lat, s_ctx, s_lat, wg, bg, gain, seg)
    return (res[0], res[1]) if need_ctx else (None, res[0])


def _gdn_kernel(*refs, need_ctx):
    if need_ctx:
        (xc_ref, xl_ref, sc_ref, sl_ref, cw_ref, alog_ref, dtb_ref, sel_ref, selb_ref, gain_ref, seg_ref,
         oc_ref, ol_ref, qkv_ref, of_ref, ob_ref, st_ref) = refs
    else:
        (xc_ref, xl_ref, sc_ref, sl_ref, cw_ref, alog_ref, dtb_ref, sel_ref, selb_ref, gain_ref, seg_ref,
         ol_ref, qkv_ref, of_ref, ob_ref, st_ref) = refs
        oc_ref = None
    lc = xc_ref.shape[1]
    x_refs = (xc_ref, xl_ref)
    s_refs = (sc_ref, sl_ref)
    row_off = (0, lc)
    seg = seg_ref[...]
    hmask = _head_masks()
    bd = _blockdiag_mask()
    ti = _iota((CHUNK, CHUNK), 0)
    si = _iota((CHUNK, CHUNK), 1)
    tri = [(si <= ti).astype(BF16), (si >= ti).astype(BF16)]
    ones = jnp.ones((CHUNK, CHUNK), BF16)
    tp = _iota((CHUNK, MIX_W), 0)
    sp = _iota((CHUNK, MIX_W), 1) % CHUNK
    le = sp <= tp
    ge = sp >= tp
    m_tri = [le, ge]
    m_strict = [sp < tp, sp > tp]
    m_sum = [ge.astype(F32), le.astype(F32)]
    blk16 = (tp // SUB) == (sp // SUB)
    eye = (tp == sp).astype(F32)

    cw = cw_ref[...]
    for seg_i, x_ref in enumerate(x_refs):
        ln = x_ref.shape[1]
        nch = ln // CHUNK

        def conv(c, carry, x_ref=x_ref, ln=ln, nch=nch, off=row_off[seg_i]):
            r0 = pl.multiple_of(c * CHUNK, CHUNK)
            center = x_ref[0, pl.ds(r0, CHUNK), 0:3 * MIX_W]
            p0 = pl.multiple_of(jnp.maximum(r0 - 8, 0), 8)
            n0 = pl.multiple_of(jnp.minimum(r0 + CHUNK, ln - 8), 8)
            prev = x_ref[0, pl.ds(p0, 8), 0:3 * MIX_W] * jnp.where(c > 0, 1.0, 0.0)
            nxt = x_ref[0, pl.ds(n0, 8), 0:3 * MIX_W] * jnp.where(c < nch - 1, 1.0, 0.0)
            ext = jnp.concatenate([prev, center, nxt], axis=0)
            pad = SHORT_CONV // 2
            acc = None
            for j in range(SHORT_CONV):
                term = ext[8 - pad + j:8 - pad + j + CHUNK, :] * cw[j:j + 1, :]
                acc = term if acc is None else acc + term
            y = _silu(acc)
            qk = y[:, 0:2 * MIX_W]
            ss = _seg_sum64(qk * qk, seg)
            qk = qk * lax.rsqrt(ss + EPS)
            qkv_ref[pl.ds(off + r0, CHUNK), 0:MIX_W] = qk[:, 0:MIX_W] * (HEAD_DIM ** -0.5)
            qkv_ref[pl.ds(off + r0, CHUNK), MIX_W:2 * MIX_W] = qk[:, MIX_W:2 * MIX_W]
            qkv_ref[pl.ds(off + r0, CHUNK), 2 * MIX_W:3 * MIX_W] = y[:, 2 * MIX_W:3 * MIX_W]
            return carry

        lax.fori_loop(0, nch, conv, 0)

    st_ref[...] = jnp.zeros(st_ref.shape, F32)

    def pk(y):
        return jnp.where(bd, jnp.concatenate([y] * GDN_HEADS, axis=0), 0.0)

    def mm(x, y):
        return _dot(x, pk(y))

    def step(seg_i, c, d):
        s_ref = s_refs[seg_i]
        r0 = pl.multiple_of(c * CHUNK, CHUNK)
        row = row_off[seg_i] + r0
        qn = qkv_ref[pl.ds(row, CHUNK), 0:MIX_W]
        kn = qkv_ref[pl.ds(row, CHUNK), MIX_W:2 * MIX_W]
        v = qkv_ref[pl.ds(row, CHUNK), 2 * MIX_W:3 * MIX_W]
        sm = s_ref[0, pl.ds(r0, CHUNK), :]
        g = -jnp.exp(alog_ref[d]) * _softplus(sm + dtb_ref[d])
        beta = _sigmoid(sm)
        gexp = _dot_xhl(g, sel_ref[d])
        bexp = _dot_xhl(beta, selb_ref[d])
        gam_t = _dot_whl(tri[d], gexp)
        gam_s = _dot_whl(ones, gexp * m_sum[d])
        dec = jnp.where(m_tri[d], jnp.exp(jnp.minimum(gam_t - gam_s, 0.0)), 0.0)
        kstack = jnp.concatenate([kn * hmask[h] for h in range(GDN_HEADS)], axis=0)
        kk = _dot_nt(kn, kstack)
        qk = _dot_nt(qn, kstack)
        a = jnp.where(m_strict[d], bexp * dec * kk, 0.0)
        dg = jnp.where(blk16, a, 0.0)
        lo = a - dg
        d2 = mm(dg, dg)
        d4 = mm(d2, d2)
        d8 = mm(d4, d4)
        t_inv = mm(mm(mm(eye - dg, eye + d2), eye + d4), eye + d8)
        m = mm(t_inv, lo)
        egam = jnp.exp(gam_t)
        sols = []
        for rhs in (bexp * v, bexp * egam * kn):
            y = mm(t_inv, rhs)
            z = y + mm(m, mm(m, y))
            sols.append(z - mm(m, z))
        sol_v, sol_k = sols
        st = st_ref[d]
        u = sol_v - _dot(sol_k, st)
        p = jnp.where(m_tri[d], qk * dec, 0.0)
        o = egam * _dot(qn, st) + _dot(p, pk(u))
        dst = of_ref if d == 0 else ob_ref
        dst[pl.ds(row, CHUNK), :] = o
        e = CHUNK - 1 if d == 0 else 0
        g_end = gam_t[e:e + 1, :]
        kh = kn * jnp.exp(g_end - gam_t)
        st_ref[d] = st * jnp.exp(g_end) + jnp.where(bd, _dot(kh.T, u), 0.0)

    _scan_segments(xc_ref, xl_ref, step)

    gain = gain_ref[...]
    blk = 256
    outs = ((oc_ref, xc_ref, 0), (ol_ref, xl_ref, lc))
    for o_ref, x_ref, off in outs:
        if o_ref is None:
            continue

        def fin(i, carry, o_ref=o_ref, x_ref=x_ref, off=off):
            r0 = pl.multiple_of(i * blk, blk)
            o = of_ref[pl.ds(off + r0, blk), :] + ob_ref[pl.ds(off + r0, blk), :]
            gate = x_ref[0, pl.ds(r0, blk), 3 * MIX_W:4 * MIX_W]
            o_ref[0, pl.ds(r0, blk), :] = _finish_rows(o, gate, gain, seg).astype(BF16)
            return carry

        lax.fori_loop(0, x_ref.shape[1] // blk, fin, 0)


def _gdn(x_ctx, x_lat, s_ctx, s_lat, cw, alog, dtb, sel, selb, gain, seg, need_ctx):
    b, lc, _ = x_ctx.shape
    ll = x_lat.shape[1]
    in_specs = [pl.BlockSpec((1, lc, 1024), lambda i: (i, 0, 0)),
                pl.BlockSpec((1, ll, 1024), lambda i: (i, 0, 0)),
                pl.BlockSpec((1, lc, LANES), lambda i: (i, 0, 0)),
                pl.BlockSpec((1, ll, LANES), lambda i: (i, 0, 0)),
                pl.BlockSpec((8, 3 * MIX_W), lambda i: (0, 0)),
                pl.BlockSpec((2, 1, LANES), lambda i: (0, 0, 0)),
                pl.BlockSpec((2, 1, LANES), lambda i: (0, 0, 0)),
                pl.BlockSpec((2, LANES, MIX_W), lambda i: (0, 0, 0)),
                pl.BlockSpec((2, LANES, MIX_W), lambda i: (0, 0, 0)),
                pl.BlockSpec((1, MIX_W), lambda i: (0, 0)),
                pl.BlockSpec((LANES, LANES), lambda i: (0, 0))]
    out_shape = [jax.ShapeDtypeStruct((b, ll, MIX_W), BF16)]
    out_specs = [pl.BlockSpec((1, ll, MIX_W), lambda i: (i, 0, 0))]
    if need_ctx:
        out_shape.insert(0, jax.ShapeDtypeStruct((b, lc, MIX_W), BF16))
        out_specs.insert(0, pl.BlockSpec((1, lc, MIX_W), lambda i: (i, 0, 0)))
    res = pl.pallas_call(
        functools.partial(_gdn_kernel, need_ctx=need_ctx),
        grid=(b,), in_specs=in_specs, out_specs=out_specs, out_shape=out_shape,
        scratch_shapes=[pltpu.VMEM((lc + ll, 3 * MIX_W), F32),
                        pltpu.VMEM((lc + ll, MIX_W), F32), pltpu.VMEM((lc + ll, MIX_W), F32),
                        pltpu.VMEM((2, MIX_W, MIX_W), F32)],
        compiler_params=_cparams(("parallel",)),
        name="gdn_scan",
    )(x_ctx, x_lat, s_ctx, s_lat, cw, alog, dtb, sel, selb, gain, seg)
    return (res[0], res[1]) if need_ctx else (None, res[0])


PAIR_W = 2 * HEAD_DIM
PREP_CHUNKS = 4


def _gdn2_kernel(*refs, need_ctx):
    if need_ctx:
        (xc_ref, xl_ref, sc_ref, sl_ref, cw_ref, alog_ref, dtb_ref, sel_ref, selb_ref, gain_ref, seg_ref,
         oc_ref, ol_ref, sk_ref, p_ref, qe_ref, sv_ref, kh_ref, dg_ref, of_ref, ob_ref, st_ref) = refs
    else:
        (xc_ref, xl_ref, sc_ref, sl_ref, cw_ref, alog_ref, dtb_ref, sel_ref, selb_ref, gain_ref, seg_ref,
         ol_ref, sk_ref, p_ref, qe_ref, sv_ref, kh_ref, dg_ref, of_ref, ob_ref, st_ref) = refs
        oc_ref = None
    lc = xc_ref.shape[1]
    x_refs = (xc_ref, xl_ref)
    s_refs = (sc_ref, sl_ref)
    row_off = (0, lc)
    npair = MIX_W // PAIR_W
    seg = seg_ref[...]
    ti = _iota((CHUNK, CHUNK), 0)
    si = _iota((CHUNK, CHUNK), 1)
    tri = [(si <= ti).astype(BF16), (si >= ti).astype(BF16)]
    ones = jnp.ones((CHUNK, CHUNK), BF16)
    tp = _iota((CHUNK, PAIR_W), 0)
    sp = _iota((CHUNK, PAIR_W), 1) % CHUNK
    le = sp <= tp
    ge = sp >= tp
    m_tri = [le, ge]
    m_strict = [sp < tp, sp > tp]
    m_sum = [ge.astype(F32), le.astype(F32)]
    blk16 = (tp // SUB) == (sp // SUB)
    eye = (tp == sp).astype(F32)
    bd2 = (_iota((PAIR_W, PAIR_W), 0) // HEAD_DIM) == (_iota((PAIR_W, PAIR_W), 1) // HEAD_DIM)
    lane_p = _iota((1, PAIR_W), 1)
    hm2 = [(lane_p // HEAD_DIM == h).astype(F32) for h in range(2)]
    cw = cw_ref[...]

    def pk2(y):
        return jnp.where(bd2, jnp.concatenate([y, y], axis=0), 0.0).astype(BF16)

    def mm(x, y):
        return jnp.dot(x.astype(BF16), pk2(y), preferred_element_type=F32)

    def front(seg_i, c):
        x_ref, s_ref = x_refs[seg_i], s_refs[seg_i]
        ln = x_ref.shape[1]
        nch = ln // CHUNK
        r0 = pl.multiple_of(c * CHUNK, CHUNK)
        center = x_ref[0, pl.ds(r0, CHUNK), 0:3 * MIX_W]
        p0 = pl.multiple_of(jnp.maximum(r0 - 8, 0), 8)
        n0 = pl.multiple_of(jnp.minimum(r0 + CHUNK, ln - 8), 8)
        prev = x_ref[0, pl.ds(p0, 8), 0:3 * MIX_W] * jnp.where(c > 0, 1.0, 0.0)
        nxt = x_ref[0, pl.ds(n0, 8), 0:3 * MIX_W] * jnp.where(c < nch - 1, 1.0, 0.0)
        ext = jnp.concatenate([prev, center, nxt], axis=0)
        pad = SHORT_CONV // 2
        acc = None
        for j in range(SHORT_CONV):
            term = ext[8 - pad + j:8 - pad + j + CHUNK, :] * cw[j:j + 1, :]
            acc = term if acc is None else acc + term
        y = _silu(acc)
        qk = y[:, 0:2 * MIX_W]
        qk = qk * lax.rsqrt(_seg_sum64(qk * qk, seg) + EPS)
        f = dict(row=row_off[seg_i] + r0, qn=qk[:, 0:MIX_W] * (HEAD_DIM ** -0.5), kn=qk[:, MIX_W:2 * MIX_W],
                 v=y[:, 2 * MIX_W:3 * MIX_W])
        sm = s_ref[0, pl.ds(r0, CHUNK), :]
        beta = _sigmoid(sm)
        f["kk"], f["qk"] = [], []
        for p in range(npair):
            ls = slice(PAIR_W * p, PAIR_W * (p + 1))
            kstack = jnp.concatenate([f["kn"][:, ls] * hm2[0], f["kn"][:, ls] * hm2[1]], axis=0)
            f["kk"].append(_dot_nt(f["kn"][:, ls], kstack))
            f["qk"].append(_dot_nt(f["qn"][:, ls], kstack))
        f["gexp"] = [_dot_xhl(-jnp.exp(alog_ref[d]) * _softplus(sm + dtb_ref[d]), sel_ref[d]) for d in range(2)]
        f["bexp"] = [_dot_xhl(beta, selb_ref[d]) for d in range(2)]
        f["gam"] = [_dot_whl(tri[d], f["gexp"][d]) for d in range(2)]
        return f

    def prep(seg_i, c2):
        fs = [front(seg_i, c2 * PREP_CHUNKS + k) for k in range(PREP_CHUNKS)]
        chains = [(f, d, p) for f in fs for d in range(2) for p in range(npair)]
        lss = [slice(PAIR_W * p, PAIR_W * (p + 1)) for _, _, p in chains]
        gam_t = [f["gam"][d][:, ls] for (f, d, _), ls in zip(chains, lss)]
        gam_s = [_dot_whl(ones, f["gexp"][d][:, ls] * m_sum[d]) for (f, d, _), ls in zip(chains, lss)]
        bx = [f["bexp"][d][:, ls] for (f, d, _), ls in zip(chains, lss)]
        dec = [jnp.where(m_tri[d], jnp.exp(jnp.minimum(gt - gs, 0.0)), 0.0)
               for (_, d, _), gt, gs in zip(chains, gam_t, gam_s)]
        a = [jnp.where(m_strict[d], b_ * dc * f["kk"][p], 0.0) for (f, d, p), b_, dc in zip(chains, bx, dec)]
        dgn = [jnp.where(blk16, a_, 0.0) for a_ in a]
        lo = [a_ - g_ for a_, g_ in zip(a, dgn)]
        d2 = [mm(g_, g_) for g_ in dgn]
        t1 = [mm(eye - g_, eye + s_) for g_, s_ in zip(dgn, d2)]
        d4 = [mm(s_, s_) for s_ in d2]
        t2 = [mm(t_, eye + s_) for t_, s_ in zip(t1, d4)]
        d8 = [mm(s_, s_) for s_ in d4]
        t_inv = [mm(t_, eye + s_) for t_, s_ in zip(t2, d8)]
        m = [mm(t_, l_) for t_, l_ in zip(t_inv, lo)]
        m2 = [mm(m_, m_) for m_ in m]
        w1 = [mm(eye - m_, eye + s_) for m_, s_ in zip(m, m2)]
        w = [mm(w_, t_) for w_, t_ in zip(w1, t_inv)]
        egam = [jnp.exp(gt) for gt in gam_t]
        solv = [mm(w_, b_ * f["v"][:, ls]) for (f, _, _), w_, b_, ls in zip(chains, w, bx, lss)]
        solk = [mm(w_, b_ * eg * f["kn"][:, ls]) for (f, _, _), w_, b_, eg, ls in zip(chains, w, bx, egam, lss)]
        for i, (f, d, p) in enumerate(chains):
            ls, row = lss[i], f["row"]
            e = CHUNK - 1 if d == 0 else 0
            g_end = gam_t[i][e:e + 1, :]
            sv_ref[d, pl.ds(row, CHUNK), ls] = solv[i]
            sk_ref[d, pl.ds(row, CHUNK), ls] = solk[i].astype(BF16)
            p_ref[d, pl.ds(row, CHUNK), ls] = jnp.where(m_tri[d], f["qk"][p] * dec[i], 0.0).astype(BF16)
            qe_ref[d, pl.ds(row, CHUNK), ls] = (egam[i] * f["qn"][:, ls]).astype(BF16)
            kh_ref[d, pl.ds(row, CHUNK), ls] = f["kn"][:, ls] * jnp.exp(g_end - gam_t[i])
            dg_ref[d, pl.ds(pl.multiple_of(row // 8, 8), 8), ls] = jnp.broadcast_to(jnp.exp(g_end), (8, PAIR_W))

    for seg_i, x_ref in enumerate(x_refs):
        def prep_body(c, carry, seg_i=seg_i):
            prep(seg_i, c)
            return carry

        lax.fori_loop(0, x_ref.shape[1] // (CHUNK * PREP_CHUNKS), prep_body, 0)

    st_ref[...] = jnp.zeros(st_ref.shape, F32)

    def scan_body(seg_i, i, nch):
        chains = [(d, p) for d in range(2) for p in range(npair)]
        rows = [row_off[seg_i] + pl.multiple_of((i if d == 0 else nch - 1 - i) * CHUNK, CHUNK) for d, _ in chains]
        lss = [slice(PAIR_W * p, PAIR_W * (p + 1)) for _, p in chains]
        st = [st_ref[d, p] for d, p in chains]
        stb = [s_.astype(BF16) for s_ in st]
        u = [sv_ref[d, pl.ds(r, CHUNK), ls] - jnp.dot(sk_ref[d, pl.ds(r, CHUNK), ls], sb, preferred_element_type=F32)
             for (d, _), r, ls, sb in zip(chains, rows, lss, stb)]
        oq = [jnp.dot(qe_ref[d, pl.ds(r, CHUNK), ls], sb, preferred_element_type=F32)
              for (d, _), r, ls, sb in zip(chains, rows, lss, stb)]
        ou = [jnp.dot(p_ref[d, pl.ds(r, CHUNK), ls], pk2(u_), preferred_element_type=F32)
              for (d, _), r, ls, u_ in zip(chains, rows, lss, u)]
        ku = [_dot(kh_ref[d, pl.ds(r, CHUNK), ls].T, u_) for (d, _), r, ls, u_ in zip(chains, rows, lss, u)]
        for j, (d, p) in enumerate(chains):
            dst = of_ref if d == 0 else ob_ref
            dst[pl.ds(rows[j], CHUNK), lss[j]] = oq[j] + ou[j]
            dgr = dg_ref[d, pl.ds(pl.multiple_of(rows[j] // 8, 8), 8), lss[j]][0:1, :]
            st_ref[d, p] = st[j] * dgr + jnp.where(bd2, ku[j], 0.0)

    for seg_i, x_ref in enumerate(x_refs):
        nch = x_ref.shape[1] // CHUNK

        def scan_iter(i, carry, seg_i=seg_i, nch=nch):
            scan_body(seg_i, i, nch)
            return carry

        lax.fori_loop(0, nch, scan_iter, 0)

    gain = gain_ref[...]
    blk = 256
    outs = ((oc_ref, xc_ref, 0), (ol_ref, xl_ref, lc))
    for o_ref, x_ref, off in outs:
        if o_ref is None:
            continue

        def fin(i, carry, o_ref=o_ref, x_ref=x_ref, off=off):
            r0 = pl.multiple_of(i * blk, blk)
            o = of_ref[pl.ds(off + r0, blk), :] + ob_ref[pl.ds(off + r0, blk), :]
            gate = x_ref[0, pl.ds(r0, blk), 3 * MIX_W:4 * MIX_W]
            o_ref[0, pl.ds(r0, blk), :] = _finish_rows(o, gate, gain, seg).astype(BF16)
            return carry

        lax.fori_loop(0, x_ref.shape[1] // blk, fin, 0)


def _gdn2(x_ctx, x_lat, s_ctx, s_lat, cw, alog, dtb, sel, selb, gain, seg, need_ctx):
    b, lc, _ = x_ctx.shape
    ll = x_lat.shape[1]
    lt = lc + ll
    in_specs = [pl.BlockSpec((1, lc, 1024), lambda i: (i, 0, 0)),
                pl.BlockSpec((1, ll, 1024), lambda i: (i, 0, 0)),
                pl.BlockSpec((1, lc, LANES), lambda i: (i, 0, 0)),
                pl.BlockSpec((1, ll, LANES), lambda i: (i, 0, 0)),
                pl.BlockSpec((8, 3 * MIX_W), lambda i: (0, 0)),
                pl.BlockSpec((2, 1, LANES), lambda i: (0, 0, 0)),
                pl.BlockSpec((2, 1, LANES), lambda i: (0, 0, 0)),
                pl.BlockSpec((2, LANES, MIX_W), lambda i: (0, 0, 0)),
                pl.BlockSpec((2, LANES, MIX_W), lambda i: (0, 0, 0)),
                pl.BlockSpec((1, MIX_W), lambda i: (0, 0)),
                pl.BlockSpec((LANES, LANES), lambda i: (0, 0))]
    out_shape = [jax.ShapeDtypeStruct((b, ll, MIX_W), BF16)]
    out_specs = [pl.BlockSpec((1, ll, MIX_W), lambda i: (i, 0, 0))]
    if need_ctx:
        out_shape.insert(0, jax.ShapeDtypeStruct((b, lc, MIX_W), BF16))
        out_specs.insert(0, pl.BlockSpec((1, lc, MIX_W), lambda i: (i, 0, 0)))
    res = pl.pallas_call(
        functools.partial(_gdn2_kernel, need_ctx=need_ctx),
        grid=(b,), in_specs=in_specs, out_specs=out_specs, out_shape=out_shape,
        scratch_shapes=[pltpu.VMEM((2, lt, MIX_W), BF16), pltpu.VMEM((2, lt, MIX_W), BF16),
                        pltpu.VMEM((2, lt, MIX_W), BF16), pltpu.VMEM((2, lt, MIX_W), F32),
                        pltpu.VMEM((2, lt, MIX_W), F32), pltpu.VMEM((2, lt // 8, MIX_W), F32),
                        pltpu.VMEM((lt, MIX_W), F32), pltpu.VMEM((lt, MIX_W), F32),
                        pltpu.VMEM((2, MIX_W // PAIR_W, PAIR_W, PAIR_W), F32)],
        compiler_params=_cparams(("parallel",)),
        name="gdn_scan",
    )(x_ctx, x_lat, s_ctx, s_lat, cw, alog, dtb, sel, selb, gain, seg)
    return (res[0], res[1]) if need_ctx else (None, res[0])


def _outproj_kernel(gla_ref, gdn_ref, att_ref, h_ref, g_ref, w_ref, o_ref):
    y = (jnp.dot(gla_ref[...], w_ref[0:256, :], preferred_element_type=F32)
         + jnp.dot(gdn_ref[...], w_ref[256:512, :], preferred_element_type=F32)
         + jnp.dot(att_ref[...], w_ref[512:1024, :], preferred_element_type=F32))
    o_ref[...] = h_ref[...] + g_ref[0] * y


def _outproj(gla, gdn, att, h2d, mod144, mod_row_fn, w, seq_len, tm):
    rows, d = h2d.shape
    tiles_per_seq = seq_len // tm
    return pl.pallas_call(
        _outproj_kernel,
        grid=(rows // tm,),
        in_specs=[pl.BlockSpec((tm, 256), lambda i: (i, 0)),
                  pl.BlockSpec((tm, 256), lambda i: (i, 0)),
                  pl.BlockSpec((tm, 512), lambda i: (i, 0)),
                  pl.BlockSpec((tm, d), lambda i: (i, 0)),
                  pl.BlockSpec((1, 1, d), lambda i: (mod_row_fn(i // tiles_per_seq) * 6 + 2, 0, 0)),
                  pl.BlockSpec((d, d), lambda i: (0, 0))],
        out_specs=pl.BlockSpec((tm, d), lambda i: (i, 0)),
        out_shape=jax.ShapeDtypeStruct((rows, d), F32),
        compiler_params=_cparams(("parallel",)),
        name="outproj",
    )(gla, gdn, att, h2d, mod144, w)


def _norm_mod(h_ref, sh_ref, sc_ref, gain_ref):
    x = h_ref[...]
    ms = jnp.mean(x * x, axis=-1, keepdims=True)
    return x * lax.rsqrt(ms + EPS) * (gain_ref[...] * (1.0 + sc_ref[0])) + sh_ref[0]


def _ffn_kernel(h_ref, sh_ref, sc_ref, g_ref, gain_ref, wg_ref, wu_ref, wd_ref, o_ref, b_scr, acc_scr, *, nf):
    f = pl.program_id(1)

    @pl.when(f == 0)
    def _():
        b_scr[...] = _norm_mod(h_ref, sh_ref, sc_ref, gain_ref).astype(BF16)
        acc_scr[...] = jnp.zeros(acc_scr.shape, F32)

    b = b_scr[...]
    gg = jnp.dot(b, wg_ref[...], preferred_element_type=F32)
    uu = jnp.dot(b, wu_ref[...], preferred_element_type=F32)
    hid = (_silu(gg) * uu).astype(BF16)
    acc_scr[...] += jnp.dot(hid, wd_ref[...], preferred_element_type=F32)

    @pl.when(f == nf - 1)
    def _():
        o_ref[...] = h_ref[...] + g_ref[0] * acc_scr[...]


def _ffn(h2d, mod144, mod_row_fn, gain, w_gu, w_down, seq_len, tm, tf):
    rows, d = h2d.shape
    dff = w_down.shape[0]
    nf = dff // tf
    tiles_per_seq = seq_len // tm

    def mod_spec(k):
        return pl.BlockSpec((1, 1, d), lambda i, f: (mod_row_fn(i // tiles_per_seq) * 6 + k, 0, 0))

    return pl.pallas_call(
        functools.partial(_ffn_kernel, nf=nf),
        grid=(rows // tm, nf),
        in_specs=[pl.BlockSpec((tm, d), lambda i, f: (i, 0)),
                  mod_spec(3), mod_spec(4), mod_spec(5),
                  pl.BlockSpec((1, d), lambda i, f: (0, 0)),
                  pl.BlockSpec((d, tf), lambda i, f: (0, f)),
                  pl.BlockSpec((d, tf), lambda i, f: (0, nf + f)),
                  pl.BlockSpec((tf, d), lambda i, f: (f, 0))],
        out_specs=pl.BlockSpec((tm, d), lambda i, f: (i, 0)),
        out_shape=jax.ShapeDtypeStruct((rows, d), F32),
        scratch_shapes=[pltpu.VMEM((tm, d), BF16), pltpu.VMEM((tm, d), F32)],
        compiler_params=_cparams(("parallel", "arbitrary")),
        name="ffn",
    )(h2d, mod144, mod144, mod144, gain, w_gu, w_gu, w_down)


def _moe_kernel(h_ref, sh_ref, sc_ref, g_ref, gain_ref, wr_ref, br_ref, wg_ref, wu_ref, wd_ref, o_ref,
                b_scr, comb_scr, acc_scr, *, nf):
    e = pl.program_id(1)
    f = pl.program_id(2)
    lane = _iota((1, LANES), 1)
    lane_f = lane.astype(F32)

    @pl.when((e == 0) & (f == 0))
    def _():
        b = _norm_mod(h_ref, sh_ref, sc_ref, gain_ref)
        b_scr[...] = b.astype(BF16)
        acc_scr[...] = jnp.zeros(acc_scr.shape, F32)
        logits = _dot3(b, wr_ref[...]) + br_ref[...]
        logits = jnp.where(lane < N_EXPERTS, logits, -jnp.inf)
        m1 = jnp.max(logits, axis=-1, keepdims=True)
        i1 = jnp.min(jnp.where(logits == m1, lane_f, float(LANES)), axis=-1, keepdims=True)
        rest = jnp.where(lane_f == i1, -jnp.inf, logits)
        m2 = jnp.max(rest, axis=-1, keepdims=True)
        i2 = jnp.min(jnp.where(rest == m2, lane_f, float(LANES)), axis=-1, keepdims=True)
        t = jnp.exp(m2 - m1)
        w1 = 1.0 / (1.0 + t)
        comb_scr[...] = jnp.where(lane_f == i1, w1, 0.0) + jnp.where(lane_f == i2, t * w1, 0.0)

    b = b_scr[...]
    gg = jnp.dot(b, wg_ref[0], preferred_element_type=F32)
    uu = jnp.dot(b, wu_ref[0], preferred_element_type=F32)
    hid = (_silu(gg) * uu).astype(BF16)
    w_e = jnp.sum(jnp.where(lane == e, comb_scr[...], 0.0), axis=-1, keepdims=True)
    acc_scr[...] += w_e * jnp.dot(hid, wd_ref[0], preferred_element_type=F32)

    @pl.when((e == N_EXPERTS - 1) & (f == nf - 1))
    def _():
        o_ref[...] = h_ref[...] + g_ref[0] * acc_scr[...]


def _moe(h2d, mod144, mod_row_fn, gain, w_router, b_router, w_gu, w_down, seq_len, tm, tf):
    rows, d = h2d.shape
    ne, dff, _ = w_down.shape
    nf = dff // tf
    tiles_per_seq = seq_len // tm

    def mod_spec(k):
        return pl.BlockSpec((1, 1, d), lambda i, e, f: (mod_row_fn(i // tiles_per_seq) * 6 + k, 0, 0))

    return pl.pallas_call(
        functools.partial(_moe_kernel, nf=nf),
        grid=(rows // tm, ne, nf),
        in_specs=[pl.BlockSpec((tm, d), lambda i, e, f: (i, 0)),
                  mod_spec(3), mod_spec(4), mod_spec(5),
                  pl.BlockSpec((1, d), lambda i, e, f: (0, 0)),
                  pl.BlockSpec((d, LANES), lambda i, e, f: (0, 0)),
                  pl.BlockSpec((1, LANES), lambda i, e, f: (0, 0)),
                  pl.BlockSpec((1, d, tf), lambda i, e, f: (e, 0, f)),
                  pl.BlockSpec((1, d, tf), lambda i, e, f: (e, 0, nf + f)),
                  pl.BlockSpec((1, tf, d), lambda i, e, f: (e, f, 0))],
        out_specs=pl.BlockSpec((tm, d), lambda i, e, f: (i, 0)),
        out_shape=jax.ShapeDtypeStruct((rows, d), F32),
        scratch_shapes=[pltpu.VMEM((tm, d), BF16), pltpu.VMEM((tm, LANES), F32), pltpu.VMEM((tm, d), F32)],
        compiler_params=_cparams(("parallel", "arbitrary", "arbitrary")),
        name="moe",
    )(h2d, mod144, mod144, mod144, gain, w_router, b_router, w_gu, w_gu, w_down)


ATT_HEAD_ORDER = (0, 4, 1, 5, 2, 6, 3, 7)


def _layout_w_in(w):
    gla = w[:, 0:1024]
    glow = w[:, 1024:1056]
    gdn = w[:, 1056:2080]
    ab = w[:, 2080:2096]
    q = jnp.concatenate([w[:, 2096 + HEAD_DIM * h:2096 + HEAD_DIM * (h + 1)] for h in ATT_HEAD_ORDER], axis=1)
    kv = w[:, 2608:2864]
    pad = jnp.zeros((w.shape[0], LANES - 48), w.dtype)
    return jnp.concatenate([gla, gdn, q, kv, glow, ab, pad], axis=1).astype(BF16)


def _layout_w_out(w):
    att = [w[512 + HEAD_DIM * h:512 + HEAD_DIM * (h + 1)] for h in ATT_HEAD_ORDER]
    return jnp.concatenate([w[0:512]] + att, axis=0).astype(BF16)


def _rope_tables(seq_len):
    rows = seq_len // GRID_W
    row = jnp.repeat(jnp.arange(rows), GRID_W).astype(F32)
    col = jnp.tile(jnp.arange(GRID_W), rows).astype(F32)
    inv_freq = ROPE_THETA ** (-jnp.arange(0, HEAD_DIM // 2, 2, dtype=F32) / (HEAD_DIM // 2))
    ar = row[:, None] * inv_freq
    ac = col[:, None] * inv_freq
    cos = jnp.concatenate([jnp.cos(ar), jnp.cos(ar), jnp.cos(ac), jnp.cos(ac)], axis=-1)
    sin = jnp.concatenate([-jnp.sin(ar), jnp.sin(ar), -jnp.sin(ac), jnp.sin(ac)], axis=-1)
    return jnp.tile(cos, (1, 2)), jnp.tile(sin, (1, 2))


def _seg_matrix():
    i = np.arange(LANES)
    return jnp.asarray((i[:, None] // HEAD_DIM) == (i[None, :] // HEAD_DIM), dtype=BF16)


def _gdn_select():
    sel = np.zeros((2, LANES, MIX_W), np.float32)
    selb = np.zeros((2, LANES, MIX_W), np.float32)
    for d in range(2):
        for h in range(GDN_HEADS):
            sel[d, 32 + GDN_HEADS * d + h, HEAD_DIM * h:HEAD_DIM * (h + 1)] = 1.0
            selb[d, 40 + GDN_HEADS * d + h, HEAD_DIM * h:HEAD_DIM * (h + 1)] = 1.0
    return jnp.asarray(sel, BF16), jnp.asarray(selb, BF16)


def _lane_rows(vals, base):
    out = jnp.zeros((2, 1, LANES), F32)
    for d in range(2):
        out = out.at[d, 0, base + GDN_HEADS * d:base + GDN_HEADS * (d + 1)].set(vals[d].astype(F32))
    return out


def kernel(x, c, ctx, c_ctx, w_mod, b_mod, norm_mix, norm_ffn, w_in, gla_gate_up, gla_gate_bias, gla_out_gain,
           gdn_conv, gdn_a_log, gdn_dt_bias, gdn_out_gain, att_q_gain, att_k_gain, w_out, ffn_gate_up, ffn_down,
           moe_router, moe_router_bias, moe_gate_up, moe_down):
    bsz, seq, d = x.shape
    lctx = ctx.shape[1]
    depth = w_mod.shape[0]
    ctx_row = bsz

    mod_rows = ((bsz + 1 + 7) // 8) * 8
    cvec = jnp.concatenate([c, c_ctx[None, :], jnp.zeros((mod_rows - bsz - 1, d), F32)], axis=0)
    mods = _modulation(cvec, w_mod, b_mod)

    seg = _seg_matrix()
    tables = _rope_tables(seq)
    sel, selb = _gdn_select()
    lat_row = lambda b: b
    ctx_row_fn = lambda b: ctx_row

    h_lat = x.reshape(bsz * seq, d)
    h_ctx = ctx.reshape(bsz * lctx, d)
    for layer in range(depth):
        need_ctx = layer < depth - 1
        mod144 = mods[layer].reshape(mod_rows * 6, 1, d)
        w_p = _layout_w_in(w_in[layer])
        w_o = _layout_w_out(w_out[layer])
        hg = jnp.concatenate([jnp.tile(att_q_gain[layer], ATT_Q_HEADS) * (HEAD_DIM ** -0.5),
                              jnp.tile(att_k_gain[layer], ATT_KV_HEADS)])[None, :].astype(F32)
        gain_mix = norm_mix[layer][None, :]
        gain_ffn = norm_ffn[layer][None, :]

        gla_l, gdn_l, q_l, kv_l, sm_l = _inproj(h_lat, mod144, lat_row, gain_mix, w_p, hg, seg, tables, seq, 256)
        gla_c, gdn_c, q_c, kv_c, sm_c = _inproj(h_ctx, mod144, ctx_row_fn, gain_mix, w_p, hg, seg, None, lctx, 256)

        r3 = lambda t, n: t.reshape(bsz, n, t.shape[-1])
        wg = jnp.zeros((2, LANES, MIX_W), F32)
        for dd in range(2):
            wg = wg.at[dd, GLA_GATE_RANK * dd:GLA_GATE_RANK * (dd + 1), :].set(gla_gate_up[layer, dd].astype(F32))
        bg = gla_gate_bias[layer].reshape(2, 1, MIX_W).astype(F32)
        gla_gain = jnp.tile(gla_out_gain[layer], GLA_HEADS)[None, :].astype(F32)
        o_gla_c, o_gla_l = _gla(r3(gla_c, lctx), r3(gla_l, seq), r3(sm_c, lctx), r3(sm_l, seq),
                                wg, bg, gla_gain, seg, need_ctx)

        cw = jnp.concatenate([gdn_conv[layer].astype(F32), jnp.zeros((8 - SHORT_CONV, 3 * MIX_W), F32)], axis=0)
        alog = _lane_rows(gdn_a_log[layer], 32)
        dtb = _lane_rows(gdn_dt_bias[layer], 32)
        gdn_gain = jnp.tile(gdn_out_gain[layer], GDN_HEADS)[None, :].astype(F32)
        o_gdn_c, o_gdn_l = _gdn2(r3(gdn_c, lctx), r3(gdn_l, seq), r3(sm_c, lctx), r3(sm_l, seq),
                                cw, alog, dtb, sel, selb, gdn_gain, seg, need_ctx)

        o_att_l = _attention(r3(q_l, seq), [r3(kv_l, seq), r3(kv_c, lctx)], 128)
        h_lat = _outproj(o_gla_l.reshape(-1, MIX_W), o_gdn_l.reshape(-1, MIX_W), o_att_l.reshape(-1, 512),
                         h_lat, mod144, lat_row, w_o, seq, 512)
        if need_ctx:
            o_att_c = _attention(r3(q_c, lctx), [r3(kv_c, lctx)], 128)
            h_ctx = _outproj(o_gla_c.reshape(-1, MIX_W), o_gdn_c.reshape(-1, MIX_W), o_att_c.reshape(-1, 512),
                             h_ctx, mod144, ctx_row_fn, w_o, lctx, 256)

        j = layer // 2
        if layer % 2 == 0:
            w_gu = ffn_gate_up[j].astype(BF16)
            w_dn = ffn_down[j].astype(BF16)
            h_lat = _ffn(h_lat, mod144, lat_row, gain_ffn, w_gu, w_dn, seq, 512, 1408)
            if need_ctx:
                h_ctx = _ffn(h_ctx, mod144, ctx_row_fn, gain_ffn, w_gu, w_dn, lctx, 256, 1408)
        else:
            w_gu = moe_gate_up[j].astype(BF16)
            w_dn = moe_down[j].astype(BF16)
            w_r = jnp.concatenate([moe_router[j].astype(F32), jnp.zeros((d, LANES - N_EXPERTS), F32)], axis=1)
            b_r = jnp.concatenate([moe_router_bias[j].astype(F32), jnp.zeros((LANES - N_EXPERTS,), F32)])[None, :]
            h_lat = _moe(h_lat, mod144, lat_row, gain_ffn, w_r, b_r, w_gu, w_dn, seq, 512, 1408)
            if need_ctx:
                h_ctx = _moe(h_ctx, mod144, ctx_row_fn, gain_ffn, w_r, b_r, w_gu, w_dn, lctx, 256, 1408)
    return h_lat.reshape(bsz, seq, d)
```

```python
import functools

import numpy as np
import jax
import jax.numpy as jnp
from jax import lax
from jax.experimental import pallas as pl
from jax.experimental.pallas import tpu as pltpu

F32 = jnp.float32
BF16 = jnp.bfloat16

GRID_W = 64
HEAD_DIM = 64
CHUNK = 64
SUB = 16
EPS = 1e-6
GLA_HEADS = 4
GLA_GATE_RANK = 16
GLA_TAU = 16.0
GDN_HEADS = 4
SHORT_CONV = 5
ATT_Q_HEADS = 8
ATT_KV_HEADS = 2
ROPE_THETA = 10000.0
N_EXPERTS = 8
MIX_W = GLA_HEADS * HEAD_DIM
EXP_CLAMP = 80.0

LANES = 128
V7X_VMEM_BYTES = 64 * 1024 * 1024
VMEM_LIMIT = 56 * 1024 * 1024


def _cparams(sem):
    return pltpu.CompilerParams(dimension_semantics=sem, vmem_limit_bytes=VMEM_LIMIT)


def _silu(x):
    return x / (1.0 + jnp.exp(-x))


def _sigmoid(x):
    return 1.0 / (1.0 + jnp.exp(-x))


def _softplus(x):
    return jnp.maximum(x, 0.0) + jnp.log(1.0 + jnp.exp(-jnp.abs(x)))


def _dot(a, b):
    return jnp.dot(a.astype(BF16), b.astype(BF16), preferred_element_type=F32)


def _dot_nt(a, b):
    return lax.dot_general(a.astype(BF16), b.astype(BF16), (((1,), (1,)), ((), ())),
                           preferred_element_type=F32)


def _split(x):
    hi = x.astype(BF16)
    lo = (x - hi.astype(F32)).astype(BF16)
    return hi, lo


def _dot_xhl(x, w):
    hi, lo = _split(x)
    w = w.astype(BF16)
    return (jnp.dot(hi, w, preferred_element_type=F32) + jnp.dot(lo, w, preferred_element_type=F32))


def _dot_whl(w, x):
    hi, lo = _split(x)
    w = w.astype(BF16)
    return (jnp.dot(w, hi, preferred_element_type=F32) + jnp.dot(w, lo, preferred_element_type=F32))


def _dot3(a, b):
    ah, al = _split(a)
    bh, bl = _split(b)
    return (jnp.dot(ah, bh, preferred_element_type=F32) + jnp.dot(ah, bl, preferred_element_type=F32)
            + jnp.dot(al, bh, preferred_element_type=F32))


def _seg_sum64(sq, seg):
    outs = []
    for j in range(sq.shape[1] // LANES):
        outs.append(_dot_xhl(sq[:, LANES * j:LANES * (j + 1)], seg))
    return outs[0] if len(outs) == 1 else jnp.concatenate(outs, axis=1)


def _iota(shape, dim):
    return lax.broadcasted_iota(jnp.int32, shape, dim)


def _mod_kernel(c_ref, w_ref, b_ref, o_ref):
    s = _silu(c_ref[...])
    o_ref[0] = _dot(s, w_ref[0]) + b_ref[0]


def _modulation(cvec, w_mod, b_mod):
    depth, d, n = w_mod.shape
    rows = cvec.shape[0]
    tn = 1536
    return pl.pallas_call(
        _mod_kernel,
        grid=(depth, n // tn),
        in_specs=[pl.BlockSpec((rows, d), lambda l, j: (0, 0)),
                  pl.BlockSpec((1, d, tn), lambda l, j: (l, 0, j)),
                  pl.BlockSpec((1, 1, tn), lambda l, j: (l, 0, j))],
        out_specs=pl.BlockSpec((1, rows, tn), lambda l, j: (l, 0, j)),
        out_shape=jax.ShapeDtypeStruct((depth, rows, n), F32),
        compiler_params=_cparams(("arbitrary", "arbitrary")),
        name="modulation",
    )(cvec, w_mod, b_mod.reshape(depth, 1, n))


def _swap16(n, lane):
    fwd = pltpu.roll(n, LANES - 16, 1)
    bwd = pltpu.roll(n, 16, 1)
    return jnp.where((lane % 32) < 16, fwd, bwd)


def _inproj_kernel(*refs, rope):
    if rope:
        (h_ref, sh_ref, sc_ref, gain_ref, w_ref, hg_ref, seg_ref, cos_ref, sin_ref,
         gla_ref, gdn_ref, q_ref, kv_ref, small_ref) = refs
    else:
        (h_ref, sh_ref, sc_ref, gain_ref, w_ref, hg_ref, seg_ref,
         gla_ref, gdn_ref, q_ref, kv_ref, small_ref) = refs
    x = h_ref[...]
    ms = jnp.mean(x * x, axis=-1, keepdims=True)
    a = x * lax.rsqrt(ms + EPS) * (gain_ref[...] * (1.0 + sc_ref[0])) + sh_ref[0]
    p = jnp.dot(a.astype(BF16), w_ref[...], preferred_element_type=F32)
    gla_ref[...] = p[:, 0:1024]
    gdn_ref[...] = p[:, 1024:2048]
    small_ref[...] = p[:, 2816:2944]
    seg = seg_ref[...]
    lane = _iota((1, LANES), 1)
    outs = []
    for j in range(5):
        t = p[:, 2048 + LANES * j:2048 + LANES * (j + 1)]
        ss = _dot_xhl(t * t, seg)
        n = t * lax.rsqrt(ss * (1.0 / HEAD_DIM) + EPS) * hg_ref[:, LANES * j:LANES * (j + 1)]
        if rope:
            n = n * cos_ref[...] + _swap16(n, lane) * sin_ref[...]
        outs.append(n)
    q_ref[...] = jnp.concatenate(outs[:4], axis=1).astype(BF16)
    kv_ref[...] = jnp.concatenate([outs[4], p[:, 2688:2816]], axis=1).astype(BF16)


def _inproj(h2d, mod144, mod_row_fn, gain, w_p, hg, seg, tables, seq_len, tm):
    rows, d = h2d.shape
    n_all = w_p.shape[1]
    rope = tables is not None
    tiles_per_seq = seq_len // tm
    in_specs = [pl.BlockSpec((tm, d), lambda i: (i, 0)),
                pl.BlockSpec((1, 1, d), lambda i: (mod_row_fn(i // tiles_per_seq) * 6 + 0, 0, 0)),
                pl.BlockSpec((1, 1, d), lambda i: (mod_row_fn(i // tiles_per_seq) * 6 + 1, 0, 0)),
                pl.BlockSpec((1, d), lambda i: (0, 0)),
                pl.BlockSpec((d, n_all), lambda i: (0, 0)),
                pl.BlockSpec((1, 640), lambda i: (0, 0)),
                pl.BlockSpec((LANES, LANES), lambda i: (0, 0))]
    args = [h2d, mod144, mod144, gain, w_p, hg, seg]
    if rope:
        in_specs += [pl.BlockSpec((tm, LANES), lambda i: (i % tiles_per_seq, 0)),
                     pl.BlockSpec((tm, LANES), lambda i: (i % tiles_per_seq, 0))]
        args += list(tables)
    out_shape = (jax.ShapeDtypeStruct((rows, 1024), F32), jax.ShapeDtypeStruct((rows, 1024), F32),
                 jax.ShapeDtypeStruct((rows, 512), BF16), jax.ShapeDtypeStruct((rows, 256), BF16),
                 jax.ShapeDtypeStruct((rows, LANES), F32))
    out_specs = (pl.BlockSpec((tm, 1024), lambda i: (i, 0)), pl.BlockSpec((tm, 1024), lambda i: (i, 0)),
                 pl.BlockSpec((tm, 512), lambda i: (i, 0)), pl.BlockSpec((tm, 256), lambda i: (i, 0)),
                 pl.BlockSpec((tm, LANES), lambda i: (i, 0)))
    return pl.pallas_call(
        functools.partial(_inproj_kernel, rope=rope),
        grid=(rows // tm,), in_specs=in_specs, out_specs=out_specs, out_shape=out_shape,
        compiler_params=_cparams(("parallel",)),
        name="inproj_rope" if rope else "inproj",
    )(*args)


def _attn_kernel(*refs, nkv, tq):
    q_ref = refs[0]
    kv_refs = refs[1:1 + nkv]
    o_ref = refs[1 + nkv]
    q = q_ref[0]
    lane = _iota((1, LANES), 1)
    mlo = (lane < HEAD_DIM).astype(BF16)
    mhi = (lane >= HEAD_DIM).astype(BF16)
    kvs = [r[0] for r in kv_refs]
    outs = []
    for j in range(4):
        qc = q[:, LANES * j:LANES * (j + 1)]
        q2 = jnp.concatenate([qc * mlo, qc * mhi], axis=0)
        ss = [lax.dot_general(q2, kv[:, 0:LANES], (((1,), (1,)), ((), ())), preferred_element_type=F32)
              for kv in kvs]
        m = functools.reduce(jnp.maximum, [jnp.max(s, axis=-1, keepdims=True) for s in ss])
        ps = [jnp.exp(s - m) for s in ss]
        l = functools.reduce(lambda a, b: a + b, [jnp.sum(p, axis=-1, keepdims=True) for p in ps])
        o = functools.reduce(lambda a, b: a + b,
                             [jnp.dot(p.astype(BF16), kv[:, LANES:2 * LANES], preferred_element_type=F32)
                              for p, kv in zip(ps, kvs)])
        o = o / l
        outs.append(jnp.where(lane < HEAD_DIM, o[:tq], o[tq:]))
    o_ref[0] = jnp.concatenate(outs, axis=1).astype(BF16)


def _attention(q, kvs, tq):
    b, lq, _ = q.shape
    in_specs = [pl.BlockSpec((1, tq, 512), lambda i, j: (i, j, 0))]
    for kv in kvs:
        in_specs.append(pl.BlockSpec((1, kv.shape[1], 256), lambda i, j: (i, 0, 0)))
    return pl.pallas_call(
        functools.partial(_attn_kernel, nkv=len(kvs), tq=tq),
        grid=(b, lq // tq), in_specs=in_specs,
        out_specs=pl.BlockSpec((1, tq, 512), lambda i, j: (i, j, 0)),
        out_shape=jax.ShapeDtypeStruct((b, lq, 512), BF16),
        compiler_params=_cparams(("parallel", "arbitrary")),
        name="attention",
    )(q, *kvs)


def _head_masks():
    lane = _iota((1, MIX_W), 1)
    return [(lane // HEAD_DIM == h).astype(F32) for h in range(GLA_HEADS)]


def _blockdiag_mask():
    r = _iota((MIX_W, MIX_W), 0) // HEAD_DIM
    c = _iota((MIX_W, MIX_W), 1) // HEAD_DIM
    return r == c


def _finish_rows(o, gate, gain, seg):
    ss = _seg_sum64(o * o, seg)
    return o * lax.rsqrt(ss * (1.0 / HEAD_DIM) + EPS) * gain * _silu(gate)


def _scan_segments(xc_ref, xl_ref, step):
    for seg_i, x_ref in enumerate((xc_ref, xl_ref)):
        nch = x_ref.shape[1] // CHUNK

        def body(i, carry, seg_i=seg_i, nch=nch):
            step(seg_i, i, 0)
            step(seg_i, nch - 1 - i, 1)
            return carry

        lax.fori_loop(0, nch, body, 0)


def _gla_kernel(*refs, need_ctx):
    if need_ctx:
        (xc_ref, xl_ref, sc_ref, sl_ref, wg_ref, bg_ref, gain_ref, seg_ref,
         oc_ref, ol_ref, of_ref, ob_ref, st_ref) = refs
    else:
        (xc_ref, xl_ref, sc_ref, sl_ref, wg_ref, bg_ref, gain_ref, seg_ref,
         ol_ref, of_ref, ob_ref, st_ref) = refs
        oc_ref = None
    lc = xc_ref.shape[1]
    x_refs = (xc_ref, xl_ref)
    s_refs = (sc_ref, sl_ref)
    row_off = (0, lc)
    hmask = _head_masks()
    bd = _blockdiag_mask()
    ti = _iota((CHUNK, CHUNK), 0)
    si = _iota((CHUNK, CHUNK), 1)
    tri = [(si <= ti).astype(BF16), (si >= ti).astype(BF16)]
    rr = _iota((4 * CHUNK, CHUNK), 0)
    cc = _iota((4 * CHUNK, CHUNK), 1)
    t_of_row = (rr // (GLA_HEADS * SUB)) * SUB + rr % SUB
    causal = [cc <= t_of_row, cc >= t_of_row]
    st_ref[...] = jnp.zeros(st_ref.shape, F32)

    def step(seg_i, c, d):
        x_ref, s_ref = x_refs[seg_i], s_refs[seg_i]
        r0 = pl.multiple_of(c * CHUNK, CHUNK)
        q = x_ref[0, pl.ds(r0, CHUNK), 0:MIX_W] * (HEAD_DIM ** -0.5)
        k = x_ref[0, pl.ds(r0, CHUNK), MIX_W:2 * MIX_W]
        v = x_ref[0, pl.ds(r0, CHUNK), 2 * MIX_W:3 * MIX_W]
        sm = s_ref[0, pl.ds(r0, CHUNK), :]
        xg = _dot3(sm, wg_ref[d]) + bg_ref[d]
        g = (jnp.minimum(xg, 0.0) - jnp.log(1.0 + jnp.exp(-jnp.abs(xg)))) * (1.0 / GLA_TAU)
        b = _dot_whl(tri[d], g)
        pieces = []
        for i in range(CHUNK // SUB):
            e = SUB * i if d == 0 else SUB * i + SUB - 1
            bref = b[e:e + 1, :] - g[e:e + 1, :]
            kt = k * jnp.exp(jnp.minimum(bref - b, EXP_CLAMP))
            qt = q[SUB * i:SUB * (i + 1), :] * jnp.exp(b[SUB * i:SUB * (i + 1), :] - bref)
            qs = jnp.concatenate([qt * hmask[h] for h in range(GLA_HEADS)], axis=0)
            pieces.append(_dot_nt(qs, kt))
        scores = jnp.where(causal[d], jnp.concatenate(pieces, axis=0), 0.0)
        r = _dot(scores, v)
        intra = []
        for i in range(CHUNK // SUB):
            acc = None
            for h in range(GLA_HEADS):
                lo = (i * GLA_HEADS + h) * SUB
                term = r[lo:lo + SUB, :] * hmask[h]
                acc = term if acc is None else acc + term
            intra.append(acc)
        o_intra = jnp.concatenate(intra, axis=0)
        st = st_ref[d]
        o = _dot_nt(q * jnp.exp(b), st) + o_intra
        dst = of_ref if d == 0 else ob_ref
        dst[pl.ds(row_off[seg_i] + r0, CHUNK), :] = o
        e = CHUNK - 1 if d == 0 else 0
        b_end = b[e:e + 1, :]
        kh = k * jnp.exp(b_end - b)
        st_ref[d] = st * jnp.exp(b_end) + jnp.where(bd, _dot(v.T, kh), 0.0)

    _scan_segments(xc_ref, xl_ref, step)

    gain = gain_ref[...]
    seg = seg_ref[...]
    blk = 256
    outs = ((oc_ref, xc_ref, 0), (ol_ref, xl_ref, lc))
    for o_ref, x_ref, off in outs:
        if o_ref is None:
            continue

        def fin(i, carry, o_ref=o_ref, x_ref=x_ref, off=off):
            r0 = pl.multiple_of(i * blk, blk)
            o = of_ref[pl.ds(off + r0, blk), :] + ob_ref[pl.ds(off + r0, blk), :]
            gate = x_ref[0, pl.ds(r0, blk), 3 * MIX_W:4 * MIX_W]
            o_ref[0, pl.ds(r0, blk), :] = _finish_rows(o, gate, gain, seg).astype(BF16)
            return carry

        lax.fori_loop(0, x_ref.shape[1] // blk, fin, 0)


def _gla(x_ctx, x_lat, s_ctx, s_lat, wg, bg, gain, seg, need_ctx):
    b, lc, _ = x_ctx.shape
    ll = x_lat.shape[1]
    in_specs = [pl.BlockSpec((1, lc, 1024), lambda i: (i, 0, 0)),
                pl.BlockSpec((1, ll, 1024), lambda i: (i, 0, 0)),
                pl.BlockSpec((1, lc, LANES), lambda i: (i, 0, 0)),
                pl.BlockSpec((1, ll, LANES), lambda i: (i, 0, 0)),
                pl.BlockSpec((2, LANES, MIX_W), lambda i: (0, 0, 0)),
                pl.BlockSpec((2, 1, MIX_W), lambda i: (0, 0, 0)),
                pl.BlockSpec((1, MIX_W), lambda i: (0, 0)),
                pl.BlockSpec((LANES, LANES), lambda i: (0, 0))]
    out_shape = [jax.ShapeDtypeStruct((b, ll, MIX_W), BF16)]
    out_specs = [pl.BlockSpec((1, ll, MIX_W), lambda i: (i, 0, 0))]
    if need_ctx:
        out_shape.insert(0, jax.ShapeDtypeStruct((b, lc, MIX_W), BF16))
        out_specs.insert(0, pl.BlockSpec((1, lc, MIX_W), lambda i: (i, 0, 0)))
    res = pl.pallas_call(
        functools.partial(_gla_kernel, need_ctx=need_ctx),
        grid=(b,), in_specs=in_specs, out_specs=out_specs, out_shape=out_shape,
        scratch_shapes=[pltpu.VMEM((lc + ll, MIX_W), F32), pltpu.VMEM((lc + ll, MIX_W), F32),
                        pltpu.VMEM((2, MIX_W, MIX_W), F32)],
        compiler_params=_cparams(("parallel",)),
        name="gla_scan",
    )(x_ctx, x_lat, s_ctx, s_lat, wg, bg, gain, seg)
    return (res[0], res[1]) if need_ctx else (None, res[0])


def _gdn_kernel(*refs, need_ctx):
    if need_ctx:
        (xc_ref, xl_ref, sc_ref, sl_ref, cw_ref, alog_ref, dtb_ref, sel_ref, selb_ref, gain_ref, seg_ref,
         oc_ref, ol_ref, qkv_ref, of_ref, ob_ref, st_ref) = refs
    else:
        (xc_ref, xl_ref, sc_ref, sl_ref, cw_ref, alog_ref, dtb_ref, sel_ref, selb_ref, gain_ref, seg_ref,
         ol_ref, qkv_ref, of_ref, ob_ref, st_ref) = refs
        oc_ref = None
    lc = xc_ref.shape[1]
    x_refs = (xc_ref, xl_ref)
    s_refs = (sc_ref, sl_ref)
    row_off = (0, lc)
    seg = seg_ref[...]
    hmask = _head_masks()
    bd = _blockdiag_mask()
    ti = _iota((CHUNK, CHUNK), 0)
    si = _iota((CHUNK, CHUNK), 1)
    tri = [(si <= ti).astype(BF16), (si >= ti).astype(BF16)]
    ones = jnp.ones((CHUNK, CHUNK), BF16)
    tp = _iota((CHUNK, MIX_W), 0)
    sp = _iota((CHUNK, MIX_W), 1) % CHUNK
    le = sp <= tp
    ge = sp >= tp
    m_tri = [le, ge]
    m_strict = [sp < tp, sp > tp]
    m_sum = [ge.astype(F32), le.astype(F32)]
    blk16 = (tp // SUB) == (sp // SUB)
    eye = (tp == sp).astype(F32)

    cw = cw_ref[...]
    for seg_i, x_ref in enumerate(x_refs):
        ln = x_ref.shape[1]
        nch = ln // CHUNK

        def conv(c, carry, x_ref=x_ref, ln=ln, nch=nch, off=row_off[seg_i]):
            r0 = pl.multiple_of(c * CHUNK, CHUNK)
            center = x_ref[0, pl.ds(r0, CHUNK), 0:3 * MIX_W]
            p0 = pl.multiple_of(jnp.maximum(r0 - 8, 0), 8)
            n0 = pl.multiple_of(jnp.minimum(r0 + CHUNK, ln - 8), 8)
            prev = x_ref[0, pl.ds(p0, 8), 0:3 * MIX_W] * jnp.where(c > 0, 1.0, 0.0)
            nxt = x_ref[0, pl.ds(n0, 8), 0:3 * MIX_W] * jnp.where(c < nch - 1, 1.0, 0.0)
            ext = jnp.concatenate([prev, center, nxt], axis=0)
            pad = SHORT_CONV // 2
            acc = None
            for j in range(SHORT_CONV):
                term = ext[8 - pad + j:8 - pad + j + CHUNK, :] * cw[j:j + 1, :]
                acc = term if acc is None else acc + term
            y = _silu(acc)
            qk = y[:, 0:2 * MIX_W]
            ss = _seg_sum64(qk * qk, seg)
            qk = qk * lax.rsqrt(ss + EPS)
            qkv_ref[pl.ds(off + r0, CHUNK), 0:MIX_W] = qk[:, 0:MIX_W] * (HEAD_DIM ** -0.5)
            qkv_ref[pl.ds(off + r0, CHUNK), MIX_W:2 * MIX_W] = qk[:, MIX_W:2 * MIX_W]
            qkv_ref[pl.ds(off + r0, CHUNK), 2 * MIX_W:3 * MIX_W] = y[:, 2 * MIX_W:3 * MIX_W]
            return carry

        lax.fori_loop(0, nch, conv, 0)

    st_ref[...] = jnp.zeros(st_ref.shape, F32)

    def pk(y):
        return jnp.where(bd, jnp.concatenate([y] * GDN_HEADS, axis=0), 0.0)

    def mm(x, y):
        return _dot(x, pk(y))

    def step(seg_i, c, d):
        s_ref = s_refs[seg_i]
        r0 = pl.multiple_of(c * CHUNK, CHUNK)
        row = row_off[seg_i] + r0
        qn = qkv_ref[pl.ds(row, CHUNK), 0:MIX_W]
        kn = qkv_ref[pl.ds(row, CHUNK), MIX_W:2 * MIX_W]
        v = qkv_ref[pl.ds(row, CHUNK), 2 * MIX_W:3 * MIX_W]
        sm = s_ref[0, pl.ds(r0, CHUNK), :]
        g = -jnp.exp(alog_ref[d]) * _softplus(sm + dtb_ref[d])
        beta = _sigmoid(sm)
        gexp = _dot_xhl(g, sel_ref[d])
        bexp = _dot_xhl(beta, selb_ref[d])
        gam_t = _dot_whl(tri[d], gexp)
        gam_s = _dot_whl(ones, gexp * m_sum[d])
        dec = jnp.where(m_tri[d], jnp.exp(jnp.minimum(gam_t - gam_s, 0.0)), 0.0)
        kstack = jnp.concatenate([kn * hmask[h] for h in range(GDN_HEADS)], axis=0)
        kk = _dot_nt(kn, kstack)
        qk = _dot_nt(qn, kstack)
        a = jnp.where(m_strict[d], bexp * dec * kk, 0.0)
        dg = jnp.where(blk16, a, 0.0)
        lo = a - dg
        d2 = mm(dg, dg)
        d4 = mm(d2, d2)
        d8 = mm(d4, d4)
        t_inv = mm(mm(mm(eye - dg, eye + d2), eye + d4), eye + d8)
        m = mm(t_inv, lo)
        egam = jnp.exp(gam_t)
        sols = []
        for rhs in (bexp * v, bexp * egam * kn):
            y = mm(t_inv, rhs)
            z = y + mm(m, mm(m, y))
            sols.append(z - mm(m, z))
        sol_v, sol_k = sols
        st = st_ref[d]
        u = sol_v - _dot(sol_k, st)
        p = jnp.where(m_tri[d], qk * dec, 0.0)
        o = egam * _dot(qn, st) + _dot(p, pk(u))
        dst = of_ref if d == 0 else ob_ref
        dst[pl.ds(row, CHUNK), :] = o
        e = CHUNK - 1 if d == 0 else 0
        g_end = gam_t[e:e + 1, :]
        kh = kn * jnp.exp(g_end - gam_t)
        st_ref[d] = st * jnp.exp(g_end) + jnp.where(bd, _dot(kh.T, u), 0.0)

    _scan_segments(xc_ref, xl_ref, step)

    gain = gain_ref[...]
    blk = 256
    outs = ((oc_ref, xc_ref, 0), (ol_ref, xl_ref, lc))
    for o_ref, x_ref, off in outs:
        if o_ref is None:
            continue

        def fin(i, carry, o_ref=o_ref, x_ref=x_ref, off=off):
            r0 = pl.multiple_of(i * blk, blk)
            o = of_ref[pl.ds(off + r0, blk), :] + ob_ref[pl.ds(off + r0, blk), :]
            gate = x_ref[0, pl.ds(r0, blk), 3 * MIX_W:4 * MIX_W]
            o_ref[0, pl.ds(r0, blk), :] = _finish_rows(o, gate, gain, seg).astype(BF16)
            return carry

        lax.fori_loop(0, x_ref.shape[1] // blk, fin, 0)


def _gdn(x_ctx, x_lat, s_ctx, s_lat, cw, alog, dtb, sel, selb, gain, seg, need_ctx):
    b, lc, _ = x_ctx.shape
    ll = x_lat.shape[1]
    in_specs = [pl.BlockSpec((1, lc, 1024), lambda i: (i, 0, 0)),
                pl.BlockSpec((1, ll, 1024), lambda i: (i, 0, 0)),
                pl.BlockSpec((1, lc, LANES), lambda i: (i, 0, 0)),
                pl.BlockSpec((1, ll, LANES), lambda i: (i, 0, 0)),
                pl.BlockSpec((8, 3 * MIX_W), lambda i: (0, 0)),
                pl.BlockSpec((2, 1, LANES), lambda i: (0, 0, 0)),
                pl.BlockSpec((2, 1, LANES), lambda i: (0, 0, 0)),
                pl.BlockSpec((2, LANES, MIX_W), lambda i: (0, 0, 0)),
                pl.BlockSpec((2, LANES, MIX_W), lambda i: (0, 0, 0)),
                pl.BlockSpec((1, MIX_W), lambda i: (0, 0)),
                pl.BlockSpec((LANES, LANES), lambda i: (0, 0))]
    out_shape = [jax.ShapeDtypeStruct((b, ll, MIX_W), BF16)]
    out_specs = [pl.BlockSpec((1, ll, MIX_W), lambda i: (i, 0, 0))]
    if need_ctx:
        out_shape.insert(0, jax.ShapeDtypeStruct((b, lc, MIX_W), BF16))
        out_specs.insert(0, pl.BlockSpec((1, lc, MIX_W), lambda i: (i, 0, 0)))
    res = pl.pallas_call(
        functools.partial(_gdn_kernel, need_ctx=need_ctx),
        grid=(b,), in_specs=in_specs, out_specs=out_specs, out_shape=out_shape,
        scratch_shapes=[pltpu.VMEM((lc + ll, 3 * MIX_W), F32),
                        pltpu.VMEM((lc + ll, MIX_W), F32), pltpu.VMEM((lc + ll, MIX_W), F32),
                        pltpu.VMEM((2, MIX_W, MIX_W), F32)],
        compiler_params=_cparams(("parallel",)),
        name="gdn_scan",
    )(x_ctx, x_lat, s_ctx, s_lat, cw, alog, dtb, sel, selb, gain, seg)
    return (res[0], res[1]) if need_ctx else (None, res[0])


PAIR_W = 2 * HEAD_DIM
PREP_CHUNKS = 4


def _gdn2_kernel(*refs, need_ctx):
    if need_ctx:
        (xc_ref, xl_ref, sc_ref, sl_ref, cw_ref, alog_ref, dtb_ref, sel_ref, selb_ref, gain_ref, seg_ref,
         oc_ref, ol_ref, sk_ref, p_ref, qe_ref, sv_ref, kh_ref, dg_ref, of_ref, ob_ref, st_ref) = refs
    else:
        (xc_ref, xl_ref, sc_ref, sl_ref, cw_ref, alog_ref, dtb_ref, sel_ref, selb_ref, gain_ref, seg_ref,
         ol_ref, sk_ref, p_ref, qe_ref, sv_ref, kh_ref, dg_ref, of_ref, ob_ref, st_ref) = refs
        oc_ref = None
    lc = xc_ref.shape[1]
    x_refs = (xc_ref, xl_ref)
    s_refs = (sc_ref, sl_ref)
    row_off = (0, lc)
    npair = MIX_W // PAIR_W
    seg = seg_ref[...]
    ti = _iota((CHUNK, CHUNK), 0)
    si = _iota((CHUNK, CHUNK), 1)
    tri = [(si <= ti).astype(BF16), (si >= ti).astype(BF16)]
    ones = jnp.ones((CHUNK, CHUNK), BF16)
    tp = _iota((CHUNK, PAIR_W), 0)
    sp = _iota((CHUNK, PAIR_W), 1) % CHUNK
    le = sp <= tp
    ge = sp >= tp
    m_tri = [le, ge]
    m_strict = [sp < tp, sp > tp]
    m_sum = [ge.astype(F32), le.astype(F32)]
    blk16 = (tp // SUB) == (sp // SUB)
    eye = (tp == sp).astype(F32)
    bd2 = (_iota((PAIR_W, PAIR_W), 0) // HEAD_DIM) == (_iota((PAIR_W, PAIR_W), 1) // HEAD_DIM)
    lane_p = _iota((1, PAIR_W), 1)
    hm2 = [(lane_p // HEAD_DIM == h).astype(F32) for h in range(2)]
    cw = cw_ref[...]

    def pk2(y):
        return jnp.where(bd2, jnp.concatenate([y, y], axis=0), 0.0).astype(BF16)

    def mm(x, y):
        return jnp.dot(x.astype(BF16), pk2(y), preferred_element_type=F32)

    def front(seg_i, c):
        x_ref, s_ref = x_refs[seg_i], s_refs[seg_i]
        ln = x_ref.shape[1]
        nch = ln // CHUNK
        r0 = pl.multiple_of(c * CHUNK, CHUNK)
        center = x_ref[0, pl.ds(r0, CHUNK), 0:3 * MIX_W]
        p0 = pl.multiple_of(jnp.maximum(r0 - 8, 0), 8)
        n0 = pl.multiple_of(jnp.minimum(r0 + CHUNK, ln - 8), 8)
        prev = x_ref[0, pl.ds(p0, 8), 0:3 * MIX_W] * jnp.where(c > 0, 1.0, 0.0)
        nxt = x_ref[0, pl.ds(n0, 8), 0:3 * MIX_W] * jnp.where(c < nch - 1, 1.0, 0.0)
        ext = jnp.concatenate([prev, center, nxt], axis=0)
        pad = SHORT_CONV // 2
        acc = None
        for j in range(SHORT_CONV):
            term = ext[8 - pad + j:8 - pad + j + CHUNK, :] * cw[j:j + 1, :]
            acc = term if acc is None else acc + term
        y = _silu(acc)
        qk = y[:, 0:2 * MIX_W]
        qk = qk * lax.rsqrt(_seg_sum64(qk * qk, seg) + EPS)
        f = dict(row=row_off[seg_i] + r0, qn=qk[:, 0:MIX_W] * (HEAD_DIM ** -0.5), kn=qk[:, MIX_W:2 * MIX_W],
                 v=y[:, 2 * MIX_W:3 * MIX_W])
        sm = s_ref[0, pl.ds(r0, CHUNK), :]
        beta = _sigmoid(sm)
        f["kk"], f["qk"] = [], []
        for p in range(npair):
            ls = slice(PAIR_W * p, PAIR_W * (p + 1))
            kstack = jnp.concatenate([f["kn"][:, ls] * hm2[0], f["kn"][:, ls] * hm2[1]], axis=0)
            f["kk"].append(_dot_nt(f["kn"][:, ls], kstack))
            f["qk"].append(_dot_nt(f["qn"][:, ls], kstack))
        f["gexp"] = [_dot_xhl(-jnp.exp(alog_ref[d]) * _softplus(sm + dtb_ref[d]), sel_ref[d]) for d in range(2)]
        f["bexp"] = [_dot_xhl(beta, selb_ref[d]) for d in range(2)]
        f["gam"] = [_dot_whl(tri[d], f["gexp"][d]) for d in range(2)]
        return f

    def prep(seg_i, c2):
        fs = [front(seg_i, c2 * PREP_CHUNKS + k) for k in range(PREP_CHUNKS)]
        chains = [(f, d, p) for f in fs for d in range(2) for p in range(npair)]
        lss = [slice(PAIR_W * p, PAIR_W * (p + 1)) for _, _, p in chains]
        gam_t = [f["gam"][d][:, ls] for (f, d, _), ls in zip(chains, lss)]
        gam_s = [_dot_whl(ones, f["gexp"][d][:, ls] * m_sum[d]) for (f, d, _), ls in zip(chains, lss)]
        bx = [f["bexp"][d][:, ls] for (f, d, _), ls in zip(chains, lss)]
        dec = [jnp.where(m_tri[d], jnp.exp(jnp.minimum(gt - gs, 0.0)), 0.0)
               for (_, d, _), gt, gs in zip(chains, gam_t, gam_s)]
        a = [jnp.where(m_strict[d], b_ * dc * f["kk"][p], 0.0) for (f, d, p), b_, dc in zip(chains, bx, dec)]
        dgn = [jnp.where(blk16, a_, 0.0) for a_ in a]
        lo = [a_ - g_ for a_, g_ in zip(a, dgn)]
        d2 = [mm(g_, g_) for g_ in dgn]
        t1 = [mm(eye - g_, eye + s_) for g_, s_ in zip(dgn, d2)]
        d4 = [mm(s_, s_) for s_ in d2]
        t2 = [mm(t_, eye + s_) for t_, s_ in zip(t1, d4)]
        d8 = [mm(s_, s_) for s_ in d4]
        t_inv = [mm(t_, eye + s_) for t_, s_ in zip(t2, d8)]
        m = [mm(t_, l_) for t_, l_ in zip(t_inv, lo)]
        m2 = [mm(m_, m_) for m_ in m]
        w1 = [mm(eye - m_, eye + s_) for m_, s_ in zip(m, m2)]
        w = [mm(w_, t_) for w_, t_ in zip(w1, t_inv)]
        egam = [jnp.exp(gt) for gt in gam_t]
        solv = [mm(w_, b_ * f["v"][:, ls]) for (f, _, _), w_, b_, ls in zip(chains, w, bx, lss)]
        solk = [mm(w_, b_ * eg * f["kn"][:, ls]) for (f, _, _), w_, b_, eg, ls in zip(chains, w, bx, egam, lss)]
        for i, (f, d, p) in enumerate(chains):
            ls, row = lss[i], f["row"]
            e = CHUNK - 1 if d == 0 else 0
            g_end = gam_t[i][e:e + 1, :]
            sv_ref[d, pl.ds(row, CHUNK), ls] = solv[i]
            sk_ref[d, pl.ds(row, CHUNK), ls] = solk[i].astype(BF16)
            p_ref[d, pl.ds(row, CHUNK), ls] = jnp.where(m_tri[d], f["qk"][p] * dec[i], 0.0).astype(BF16)
            qe_ref[d, pl.ds(row, CHUNK), ls] = (egam[i] * f["qn"][:, ls]).astype(BF16)
            kh_ref[d, pl.ds(row, CHUNK), ls] = f["kn"][:, ls] * jnp.exp(g_end - gam_t[i])
            dg_ref[d, pl.ds(pl.multiple_of(row // 8, 8), 8), ls] = jnp.broadcast_to(jnp.exp(g_end), (8, PAIR_W))

    for seg_i, x_ref in enumerate(x_refs):
        def prep_body(c, carry, seg_i=seg_i):
            prep(seg_i, c)
            return carry

        lax.fori_loop(0, x_ref.shape[1] // (CHUNK * PREP_CHUNKS), prep_body, 0)

    st_ref[...] = jnp.zeros(st_ref.shape, F32)

    def scan_body(seg_i, i, nch):
        chains = [(d, p) for d in range(2) for p in range(npair)]
        rows = [row_off[seg_i] + pl.multiple_of((i if d == 0 else nch - 1 - i) * CHUNK, CHUNK) for d, _ in chains]
        lss = [slice(PAIR_W * p, PAIR_W * (p + 1)) for _, p in chains]
        st = [st_ref[d, p] for d, p in chains]
        stb = [s_.astype(BF16) for s_ in st]
        u = [sv_ref[d, pl.ds(r, CHUNK), ls] - jnp.dot(sk_ref[d, pl.ds(r, CHUNK), ls], sb, preferred_element_type=F32)
             for (d, _), r, ls, sb in zip(chains, rows, lss, stb)]
        oq = [jnp.dot(qe_ref[d, pl.ds(r, CHUNK), ls], sb, preferred_element_type=F32)
              for (d, _), r, ls, sb in zip(chains, rows, lss, stb)]
        ou = [jnp.dot(p_ref[d, pl.ds(r, CHUNK), ls], pk2(u_), preferred_element_type=F32)
              for (d, _), r, ls, u_ in zip(chains, rows, lss, u)]
        ku = [_dot(kh_ref[d, pl.ds(r, CHUNK), ls].T, u_) for (d, _), r, ls, u_ in zip(chains, rows, lss, u)]
        for j, (d, p) in enumerate(chains):
            dst = of_ref if d == 0 else ob_ref
            dst[pl.ds(rows[j], CHUNK), lss[j]] = oq[j] + ou[j]
            dgr = dg_ref[d, pl.ds(pl.multiple_of(rows[j] // 8, 8), 8), lss[j]][0:1, :]
            st_ref[d, p] = st[j] * dgr + jnp.where(bd2, ku[j], 0.0)

    for seg_i, x_ref in enumerate(x_refs):
        nch = x_ref.shape[1] // CHUNK

        def scan_iter(i, carry, seg_i=seg_i, nch=nch):
            scan_body(seg_i, i, nch)
            return carry

        lax.fori_loop(0, nch, scan_iter, 0)

    gain = gain_ref[...]
    blk = 256
    outs = ((oc_ref, xc_ref, 0), (ol_ref, xl_ref, lc))
    for o_ref, x_ref, off in outs:
        if o_ref is None:
            continue

        def fin(i, carry, o_ref=o_ref, x_ref=x_ref, off=off):
            r0 = pl.multiple_of(i * blk, blk)
            o = of_ref[pl.ds(off + r0, blk), :] + ob_ref[pl.ds(off + r0, blk), :]
            gate = x_ref[0, pl.ds(r0, blk), 3 * MIX_W:4 * MIX_W]
            o_ref[0, pl.ds(r0, blk), :] = _finish_rows(o, gate, gain, seg).astype(BF16)
            return carry

        lax.fori_loop(0, x_ref.shape[1] // blk, fin, 0)


def _gdn2(x_ctx, x_lat, s_ctx, s_lat, cw, alog, dtb, sel, selb, gain, seg, need_ctx):
    b, lc, _ = x_ctx.shape
    ll = x_lat.shape[1]
    lt = lc + ll
    in_specs = [pl.BlockSpec((1, lc, 1024), lambda i: (i, 0, 0)),
                pl.BlockSpec((1, ll, 1024), lambda i: (i, 0, 0)),
                pl.BlockSpec((1, lc, LANES), lambda i: (i, 0, 0)),
                pl.BlockSpec((1, ll, LANES), lambda i: (i, 0, 0)),
                pl.BlockSpec((8, 3 * MIX_W), lambda i: (0, 0)),
                pl.BlockSpec((2, 1, LANES), lambda i: (0, 0, 0)),
                pl.BlockSpec((2, 1, LANES), lambda i: (0, 0, 0)),
                pl.BlockSpec((2, LANES, MIX_W), lambda i: (0, 0, 0)),
                pl.BlockSpec((2, LANES, MIX_W), lambda i: (0, 0, 0)),
                pl.BlockSpec((1, MIX_W), lambda i: (0, 0)),
                pl.BlockSpec((LANES, LANES), lambda i: (0, 0))]
    out_shape = [jax.ShapeDtypeStruct((b, ll, MIX_W), BF16)]
    out_specs = [pl.BlockSpec((1, ll, MIX_W), lambda i: (i, 0, 0))]
    if need_ctx:
        out_shape.insert(0, jax.ShapeDtypeStruct((b, lc, MIX_W), BF16))
        out_specs.insert(0, pl.BlockSpec((1, lc, MIX_W), lambda i: (i, 0, 0)))
    res = pl.pallas_call(
        functools.partial(_gdn2_kernel, need_ctx=need_ctx),
        grid=(b,), in_specs=in_specs, out_specs=out_specs, out_shape=out_shape,
        scratch_shapes=[pltpu.VMEM((2, lt, MIX_W), BF16), pltpu.VMEM((2, lt, MIX_W), BF16),
                        pltpu.VMEM((2, lt, MIX_W), BF16), pltpu.VMEM((2, lt, MIX_W), F32),
                        pltpu.VMEM((2, lt, MIX_W), F32), pltpu.VMEM((2, lt // 8, MIX_W), F32),
                        pltpu.VMEM((lt, MIX_W), F32), pltpu.VMEM((lt, MIX_W), F32),
                        pltpu.VMEM((2, MIX_W // PAIR_W, PAIR_W, PAIR_W), F32)],
        compiler_params=_cparams(("parallel",)),
        name="gdn_scan",
    )(x_ctx, x_lat, s_ctx, s_lat, cw, alog, dtb, sel, selb, gain, seg)
    return (res[0], res[1]) if need_ctx else (None, res[0])


def _outproj_kernel(gla_ref, gdn_ref, att_ref, h_ref, g_ref, w_ref, o_ref):
    y = (jnp.dot(gla_ref[...], w_ref[0:256, :], preferred_element_type=F32)
         + jnp.dot(gdn_ref[...], w_ref[256:512, :], preferred_element_type=F32)
         + jnp.dot(att_ref[...], w_ref[512:1024, :], preferred_element_type=F32))
    o_ref[...] = h_ref[...] + g_ref[0] * y


def _outproj(gla, gdn, att, h2d, mod144, mod_row_fn, w, seq_len, tm):
    rows, d = h2d.shape
    tiles_per_seq = seq_len // tm
    return pl.pallas_call(
        _outproj_kernel,
        grid=(rows // tm,),
        in_specs=[pl.BlockSpec((tm, 256), lambda i: (i, 0)),
                  pl.BlockSpec((tm, 256), lambda i: (i, 0)),
                  pl.BlockSpec((tm, 512), lambda i: (i, 0)),
                  pl.BlockSpec((tm, d), lambda i: (i, 0)),
                  pl.BlockSpec((1, 1, d), lambda i: (mod_row_fn(i // tiles_per_seq) * 6 + 2, 0, 0)),
                  pl.BlockSpec((d, d), lambda i: (0, 0))],
        out_specs=pl.BlockSpec((tm, d), lambda i: (i, 0)),
        out_shape=jax.ShapeDtypeStruct((rows, d), F32),
        compiler_params=_cparams(("parallel",)),
        name="outproj",
    )(gla, gdn, att, h2d, mod144, w)


def _norm_mod(h_ref, sh_ref, sc_ref, gain_ref):
    x = h_ref[...]
    ms = jnp.mean(x * x, axis=-1, keepdims=True)
    return x * lax.rsqrt(ms + EPS) * (gain_ref[...] * (1.0 + sc_ref[0])) + sh_ref[0]


def _ffn_kernel(h_ref, sh_ref, sc_ref, g_ref, gain_ref, wg_ref, wu_ref, wd_ref, o_ref, b_scr, acc_scr, *, nf):
    f = pl.program_id(1)

    @pl.when(f == 0)
    def _():
        b_scr[...] = _norm_mod(h_ref, sh_ref, sc_ref, gain_ref).astype(BF16)
        acc_scr[...] = jnp.zeros(acc_scr.shape, F32)

    b = b_scr[...]
    gg = jnp.dot(b, wg_ref[...], preferred_element_type=F32)
    uu = jnp.dot(b, wu_ref[...], preferred_element_type=F32)
    hid = (_silu(gg) * uu).astype(BF16)
    acc_scr[...] += jnp.dot(hid, wd_ref[...], preferred_element_type=F32)

    @pl.when(f == nf - 1)
    def _():
        o_ref[...] = h_ref[...] + g_ref[0] * acc_scr[...]


def _ffn(h2d, mod144, mod_row_fn, gain, w_gu, w_down, seq_len, tm, tf):
    rows, d = h2d.shape
    dff = w_down.shape[0]
    nf = dff // tf
    tiles_per_seq = seq_len // tm

    def mod_spec(k):
        return pl.BlockSpec((1, 1, d), lambda i, f: (mod_row_fn(i // tiles_per_seq) * 6 + k, 0, 0))

    return pl.pallas_call(
        functools.partial(_ffn_kernel, nf=nf),
        grid=(rows // tm, nf),
        in_specs=[pl.BlockSpec((tm, d), lambda i, f: (i, 0)),
                  mod_spec(3), mod_spec(4), mod_spec(5),
                  pl.BlockSpec((1, d), lambda i, f: (0, 0)),
                  pl.BlockSpec((d, tf), lambda i, f: (0, f)),
                  pl.BlockSpec((d, tf), lambda i, f: (0, nf + f)),
                  pl.BlockSpec((tf, d), lambda i, f: (f, 0))],
        out_specs=pl.BlockSpec((tm, d), lambda i, f: (i, 0)),
        out_shape=jax.ShapeDtypeStruct((rows, d), F32),
        scratch_shapes=[pltpu.VMEM((tm, d), BF16), pltpu.VMEM((tm, d), F32)],
        compiler_params=_cparams(("parallel", "arbitrary")),
        name="ffn",
    )(h2d, mod144, mod144, mod144, gain, w_gu, w_gu, w_down)


def _moe_kernel(h_ref, sh_ref, sc_ref, g_ref, gain_ref, wr_ref, br_ref, wg_ref, wu_ref, wd_ref, o_ref,
                b_scr, comb_scr, acc_scr, *, nf):
    e = pl.program_id(1)
    f = pl.program_id(2)
    lane = _iota((1, LANES), 1)
    lane_f = lane.astype(F32)

    @pl.when((e == 0) & (f == 0))
    def _():
        b = _norm_mod(h_ref, sh_ref, sc_ref, gain_ref)
        b_scr[...] = b.astype(BF16)
        acc_scr[...] = jnp.zeros(acc_scr.shape, F32)
        logits = _dot3(b, wr_ref[...]) + br_ref[...]
        logits = jnp.where(lane < N_EXPERTS, logits, -jnp.inf)
        m1 = jnp.max(logits, axis=-1, keepdims=True)
        i1 = jnp.min(jnp.where(logits == m1, lane_f, float(LANES)), axis=-1, keepdims=True)
        rest = jnp.where(lane_f == i1, -jnp.inf, logits)
        m2 = jnp.max(rest, axis=-1, keepdims=True)
        i2 = jnp.min(jnp.where(rest == m2, lane_f, float(LANES)), axis=-1, keepdims=True)
        t = jnp.exp(m2 - m1)
        w1 = 1.0 / (1.0 + t)
        comb_scr[...] = jnp.where(lane_f == i1, w1, 0.0) + jnp.where(lane_f == i2, t * w1, 0.0)

    b = b_scr[...]
    gg = jnp.dot(b, wg_ref[0], preferred_element_type=F32)
    uu = jnp.dot(b, wu_ref[0], preferred_element_type=F32)
    hid = (_silu(gg) * uu).astype(BF16)
    w_e = jnp.sum(jnp.where(lane == e, comb_scr[...], 0.0), axis=-1, keepdims=True)
    acc_scr[...] += w_e * jnp.dot(hid, wd_ref[0], preferred_element_type=F32)

    @pl.when((e == N_EXPERTS - 1) & (f == nf - 1))
    def _():
        o_ref[...] = h_ref[...] + g_ref[0] * acc_scr[...]


def _moe(h2d, mod144, mod_row_fn, gain, w_router, b_router, w_gu, w_down, seq_len, tm, tf):
    rows, d = h2d.shape
    ne, dff, _ = w_down.shape
    nf = dff // tf
    tiles_per_seq = seq_len // tm

    def mod_spec(k):
        return pl.BlockSpec((1, 1, d), lambda i, e, f: (mod_row_fn(i // tiles_per_seq) * 6 + k, 0, 0))

    return pl.pallas_call(
        functools.partial(_moe_kernel, nf=nf),
        grid=(rows // tm, ne, nf),
        in_specs=[pl.BlockSpec((tm, d), lambda i, e, f: (i, 0)),
                  mod_spec(3), mod_spec(4), mod_spec(5),
                  pl.BlockSpec((1, d), lambda i, e, f: (0, 0)),
                  pl.BlockSpec((d, LANES), lambda i, e, f: (0, 0)),
                  pl.BlockSpec((1, LANES), lambda i, e, f: (0, 0)),
                  pl.BlockSpec((1, d, tf), lambda i, e, f: (e, 0, f)),
                  pl.BlockSpec((1, d, tf), lambda i, e, f: (e, 0, nf + f)),
                  pl.BlockSpec((1, tf, d), lambda i, e, f: (e, f, 0))],
        out_specs=pl.BlockSpec((tm, d), lambda i, e, f: (i, 0)),
        out_shape=jax.ShapeDtypeStruct((rows, d), F32),
        scratch_shapes=[pltpu.VMEM((tm, d), BF16), pltpu.VMEM((tm, LANES), F32), pltpu.VMEM((tm, d), F32)],
        compiler_params=_cparams(("parallel", "arbitrary", "arbitrary")),
        name="moe",
    )(h2d, mod144, mod144, mod144, gain, w_router, b_router, w_gu, w_gu, w_down)


MOE_TILE = 512
GATHER_ROWS = 2048
COMBINE_TOKENS = 256


def _router_kernel(h_ref, sh_ref, sc_ref, gain_ref, wr_ref, br_ref, b_ref, route_ref):
    lane = _iota((1, LANES), 1)
    lane_f = lane.astype(F32)
    b = _norm_mod(h_ref, sh_ref, sc_ref, gain_ref)
    b_ref[...] = b
    logits = _dot3(b, wr_ref[...]) + br_ref[...]
    logits = jnp.where(lane < N_EXPERTS, logits, -jnp.inf)
    m1 = jnp.max(logits, axis=-1, keepdims=True)
    i1 = jnp.min(jnp.where(logits == m1, lane_f, float(LANES)), axis=-1, keepdims=True)
    rest = jnp.where(lane_f == i1, -jnp.inf, logits)
    m2 = jnp.max(rest, axis=-1, keepdims=True)
    i2 = jnp.min(jnp.where(rest == m2, lane_f, float(LANES)), axis=-1, keepdims=True)
    t = jnp.exp(m2 - m1)
    w1 = 1.0 / (1.0 + t)
    route_ref[...] = (jnp.where(lane == 0, i1, 0.0) + jnp.where(lane == 1, i2, 0.0)
                      + jnp.where(lane == 2, w1, 0.0) + jnp.where(lane == 3, t * w1, 0.0))


def _router(h2d, mod144, mod_row_fn, gain, w_router, b_router, seq_len, tm):
    rows, d = h2d.shape
    tiles_per_seq = seq_len // tm

    def mod_spec(k):
        return pl.BlockSpec((1, 1, d), lambda i: (mod_row_fn(i // tiles_per_seq) * 6 + k, 0, 0))

    return pl.pallas_call(
        _router_kernel,
        grid=(rows // tm,),
        in_specs=[pl.BlockSpec((tm, d), lambda i: (i, 0)), mod_spec(3), mod_spec(4),
                  pl.BlockSpec((1, d), lambda i: (0, 0)),
                  pl.BlockSpec((d, LANES), lambda i: (0, 0)),
                  pl.BlockSpec((1, LANES), lambda i: (0, 0))],
        out_specs=(pl.BlockSpec((tm, d), lambda i: (i, 0)), pl.BlockSpec((tm, LANES), lambda i: (i, 0))),
        out_shape=(jax.ShapeDtypeStruct((rows, d), F32), jax.ShapeDtypeStruct((rows, LANES), F32)),
        compiler_params=_cparams(("parallel",)),
        name="moe_router",
    )(h2d, mod144, mod144, gain, w_router, b_router)


def _row_gather_kernel(idx_ref, src_ref, dst_ref, sem):
    base = pl.program_id(0) * GATHER_ROWS

    def issue(r, carry):
        pltpu.make_async_copy(src_ref.at[pl.ds(idx_ref[0, 0, r], 1)], dst_ref.at[pl.ds(base + r, 1)], sem).start()
        return carry

    lax.fori_loop(0, GATHER_ROWS, issue, 0, unroll=8)
    pltpu.make_async_copy(src_ref.at[pl.ds(0, GATHER_ROWS)], dst_ref.at[pl.ds(base, GATHER_ROWS)], sem).wait()


def _row_gather(src, idx):
    n = idx.shape[0]
    steps = n // GATHER_ROWS
    return pl.pallas_call(
        _row_gather_kernel,
        grid=(steps,),
        in_specs=[pl.BlockSpec((1, 1, GATHER_ROWS), lambda i: (i, 0, 0), memory_space=pltpu.SMEM),
                  pl.BlockSpec(memory_space=pl.ANY)],
        out_specs=pl.BlockSpec(memory_space=pl.ANY),
        out_shape=jax.ShapeDtypeStruct((n, src.shape[1]), src.dtype),
        scratch_shapes=[pltpu.SemaphoreType.DMA(())],
        compiler_params=_cparams(("arbitrary",)),
        name="moe_row_gather",
    )(idx.reshape(steps, 1, GATHER_ROWS), src)


def _moe_group_kernel(te_ref, nt_ref, xs_ref, wrow_ref, wg_ref, wu_ref, wd_ref, ys_ref, xb_scr, acc_scr, *, nf):
    i = pl.program_id(0)
    f = pl.program_id(1)
    used = i < nt_ref[0]

    @pl.when(used & (f == 0))
    def _():
        xb_scr[...] = xs_ref[...].astype(BF16)

    @pl.when(used)
    def _():
        x = xb_scr[...]
        gg = jnp.dot(x, wg_ref[0], preferred_element_type=F32)
        uu = jnp.dot(x, wu_ref[0], preferred_element_type=F32)
        hid = (_silu(gg) * uu * wrow_ref[...]).astype(BF16)
        part = jnp.dot(hid, wd_ref[0], preferred_element_type=F32)

        @pl.when(f == 0)
        def _():
            acc_scr[...] = part

        @pl.when(f > 0)
        def _():
            acc_scr[...] += part

    @pl.when(used & (f == nf - 1))
    def _():
        ys_ref[...] = acc_scr[...]

    @pl.when(jnp.logical_not(used) & (f == nf - 1))
    def _():
        ys_ref[...] = jnp.zeros(ys_ref.shape, F32)


def _moe_group(tile_expert, n_tiles_used, xs, wrow, w_gu, w_down, tf):
    p, d = xs.shape
    dff = w_down.shape[1]
    nf = dff // tf
    grid_spec = pltpu.PrefetchScalarGridSpec(
        num_scalar_prefetch=2,
        grid=(p // MOE_TILE, nf),
        in_specs=[pl.BlockSpec((MOE_TILE, d), lambda i, f, te, nt: (i, 0)),
                  pl.BlockSpec((MOE_TILE, 1), lambda i, f, te, nt: (i, 0)),
                  pl.BlockSpec((1, d, tf), lambda i, f, te, nt: (te[i], 0, f)),
                  pl.BlockSpec((1, d, tf), lambda i, f, te, nt: (te[i], 0, nf + f)),
                  pl.BlockSpec((1, tf, d), lambda i, f, te, nt: (te[i], f, 0))],
        out_specs=pl.BlockSpec((MOE_TILE, d), lambda i, f, te, nt: (i, 0)),
        scratch_shapes=[pltpu.VMEM((MOE_TILE, d), BF16), pltpu.VMEM((MOE_TILE, d), F32)])
    return pl.pallas_call(
        functools.partial(_moe_group_kernel, nf=nf),
        grid_spec=grid_spec,
        out_shape=jax.ShapeDtypeStruct((p, d), F32),
        compiler_params=_cparams(("arbitrary", "arbitrary")),
        name="moe_experts",
    )(tile_expert, n_tiles_used, xs, wrow, w_gu, w_gu, w_down)


def _moe_combine_kernel(pos_ref, ys_ref, h_ref, g_ref, o_ref, buf, sem):
    n = 2 * COMBINE_TOKENS

    def issue(r, carry):
        pltpu.make_async_copy(ys_ref.at[pl.ds(pos_ref[0, 0, r], 1)], buf.at[pl.ds(r, 1)], sem).start()
        return carry

    lax.fori_loop(0, n, issue, 0, unroll=8)
    pltpu.make_async_copy(ys_ref.at[pl.ds(0, n)], buf, sem).wait()
    o_ref[...] = h_ref[...] + g_ref[0] * (buf[0:COMBINE_TOKENS, :] + buf[COMBINE_TOKENS:n, :])


def _moe_combine(pos, ys, h2d, mod144, mod_row_fn, seq_len):
    rows, d = h2d.shape
    tm = COMBINE_TOKENS
    tiles_per_seq = seq_len // tm
    steps = rows // tm
    return pl.pallas_call(
        _moe_combine_kernel,
        grid=(steps,),
        in_specs=[pl.BlockSpec((1, 1, 2 * tm), lambda i: (i, 0, 0), memory_space=pltpu.SMEM),
                  pl.BlockSpec(memory_space=pl.ANY),
                  pl.BlockSpec((tm, d), lambda i: (i, 0)),
                  pl.BlockSpec((1, 1, d), lambda i: (mod_row_fn(i // tiles_per_seq) * 6 + 5, 0, 0))],
        out_specs=pl.BlockSpec((tm, d), lambda i: (i, 0)),
        out_shape=jax.ShapeDtypeStruct((rows, d), F32),
        scratch_shapes=[pltpu.VMEM((2 * tm, d), F32), pltpu.SemaphoreType.DMA(())],
        compiler_params=_cparams(("arbitrary",)),
        name="moe_combine",
    )(pos, ys, h2d, mod144)


def _moe_routed(h2d, mod144, mod_row_fn, gain, w_router, b_router, w_gu, w_down, seq_len):
    rows, d = h2d.shape
    b, route = _router(h2d, mod144, mod_row_fn, gain, w_router, b_router, seq_len, min(512, seq_len))
    ex = jnp.concatenate([route[:, 0], route[:, 1]]).astype(jnp.int32)
    wt = jnp.concatenate([route[:, 2], route[:, 3]])
    tok = jnp.concatenate([jnp.arange(rows, dtype=jnp.int32)] * 2)
    onehot = (ex[:, None] == jnp.arange(N_EXPERTS, dtype=jnp.int32)[None, :]).astype(jnp.int32)
    rank = jnp.sum((jnp.cumsum(onehot, axis=0) - onehot) * onehot, axis=1)
    counts = jnp.sum(onehot, axis=0)
    padded = ((counts + MOE_TILE - 1) // MOE_TILE) * MOE_TILE
    ends = jnp.cumsum(padded)
    starts = ends - padded
    pos = jnp.sum(onehot * starts[None, :], axis=1) + rank
    p_rows = 2 * rows + N_EXPERTS * MOE_TILE
    p_rows = ((p_rows + GATHER_ROWS - 1) // GATHER_ROWS) * GATHER_ROWS
    tok_sorted = jnp.zeros((p_rows,), jnp.int32).at[pos].set(tok)
    w_sorted = jnp.zeros((p_rows,), F32).at[pos].set(wt)
    tile_first = jnp.arange(p_rows // MOE_TILE, dtype=jnp.int32) * MOE_TILE
    tile_expert = jnp.minimum(jnp.sum((tile_first[:, None] >= ends[None, :]).astype(jnp.int32), axis=1),
                              N_EXPERTS - 1).astype(jnp.int32)
    n_tiles_used = (ends[-1:] // MOE_TILE).astype(jnp.int32)
    xs = _row_gather(b, tok_sorted)
    ys = _moe_group(tile_expert, n_tiles_used, xs, w_sorted[:, None], w_gu, w_down, 1408)
    steps = rows // COMBINE_TOKENS
    pos2 = jnp.concatenate([pos[:rows].reshape(steps, 1, COMBINE_TOKENS),
                            pos[rows:].reshape(steps, 1, COMBINE_TOKENS)], axis=2).astype(jnp.int32)
    return _moe_combine(pos2, ys, h2d, mod144, mod_row_fn, seq_len)


ATT_HEAD_ORDER = (0, 4, 1, 5, 2, 6, 3, 7)


def _layout_w_in(w):
    gla = w[:, 0:1024]
    glow = w[:, 1024:1056]
    gdn = w[:, 1056:2080]
    ab = w[:, 2080:2096]
    q = jnp.concatenate([w[:, 2096 + HEAD_DIM * h:2096 + HEAD_DIM * (h + 1)] for h in ATT_HEAD_ORDER], axis=1)
    kv = w[:, 2608:2864]
    pad = jnp.zeros((w.shape[0], LANES - 48), w.dtype)
    return jnp.concatenate([gla, gdn, q, kv, glow, ab, pad], axis=1).astype(BF16)


def _layout_w_out(w):
    att = [w[512 + HEAD_DIM * h:512 + HEAD_DIM * (h + 1)] for h in ATT_HEAD_ORDER]
    return jnp.concatenate([w[0:512]] + att, axis=0).astype(BF16)


def _rope_tables(seq_len):
    rows = seq_len // GRID_W
    row = jnp.repeat(jnp.arange(rows), GRID_W).astype(F32)
    col = jnp.tile(jnp.arange(GRID_W), rows).astype(F32)
    inv_freq = ROPE_THETA ** (-jnp.arange(0, HEAD_DIM // 2, 2, dtype=F32) / (HEAD_DIM // 2))
    ar = row[:, None] * inv_freq
    ac = col[:, None] * inv_freq
    cos = jnp.concatenate([jnp.cos(ar), jnp.cos(ar), jnp.cos(ac), jnp.cos(ac)], axis=-1)
    sin = jnp.concatenate([-jnp.sin(ar), jnp.sin(ar), -jnp.sin(ac), jnp.sin(ac)], axis=-1)
    return jnp.tile(cos, (1, 2)), jnp.tile(sin, (1, 2))


def _seg_matrix():
    i = np.arange(LANES)
    return jnp.asarray((i[:, None] // HEAD_DIM) == (i[None, :] // HEAD_DIM), dtype=BF16)


def _gdn_select():
    sel = np.zeros((2, LANES, MIX_W), np.float32)
    selb = np.zeros((2, LANES, MIX_W), np.float32)
    for d in range(2):
        for h in range(GDN_HEADS):
            sel[d, 32 + GDN_HEADS * d + h, HEAD_DIM * h:HEAD_DIM * (h + 1)] = 1.0
            selb[d, 40 + GDN_HEADS * d + h, HEAD_DIM * h:HEAD_DIM * (h + 1)] = 1.0
    return jnp.asarray(sel, BF16), jnp.asarray(selb, BF16)


def _lane_rows(vals, base):
    out = jnp.zeros((2, 1, LANES), F32)
    for d in range(2):
        out = out.at[d, 0, base + GDN_HEADS * d:base + GDN_HEADS * (d + 1)].set(vals[d].astype(F32))
    return out


def kernel(x, c, ctx, c_ctx, w_mod, b_mod, norm_mix, norm_ffn, w_in, gla_gate_up, gla_gate_bias, gla_out_gain,
           gdn_conv, gdn_a_log, gdn_dt_bias, gdn_out_gain, att_q_gain, att_k_gain, w_out, ffn_gate_up, ffn_down,
           moe_router, moe_router_bias, moe_gate_up, moe_down):
    bsz, seq, d = x.shape
    lctx = ctx.shape[1]
    depth = w_mod.shape[0]
    ctx_row = bsz

    mod_rows = ((bsz + 1 + 7) // 8) * 8
    cvec = jnp.concatenate([c, c_ctx[None, :], jnp.zeros((mod_rows - bsz - 1, d), F32)], axis=0)
    mods = _modulation(cvec, w_mod, b_mod)

    seg = _seg_matrix()
    tables = _rope_tables(seq)
    sel, selb = _gdn_select()
    lat_row = lambda b: b
    ctx_row_fn = lambda b: ctx_row

    h_lat = x.reshape(bsz * seq, d)
    h_ctx = ctx.reshape(bsz * lctx, d)
    for layer in range(depth):
        need_ctx = layer < depth - 1
        mod144 = mods[layer].reshape(mod_rows * 6, 1, d)
        w_p = _layout_w_in(w_in[layer])
        w_o = _layout_w_out(w_out[layer])
        hg = jnp.concatenate([jnp.tile(att_q_gain[layer], ATT_Q_HEADS) * (HEAD_DIM ** -0.5),
                              jnp.tile(att_k_gain[layer], ATT_KV_HEADS)])[None, :].astype(F32)
        gain_mix = norm_mix[layer][None, :]
        gain_ffn = norm_ffn[layer][None, :]

        gla_l, gdn_l, q_l, kv_l, sm_l = _inproj(h_lat, mod144, lat_row, gain_mix, w_p, hg, seg, tables, seq, 256)
        gla_c, gdn_c, q_c, kv_c, sm_c = _inproj(h_ctx, mod144, ctx_row_fn, gain_mix, w_p, hg, seg, None, lctx, 256)

        r3 = lambda t, n: t.reshape(bsz, n, t.shape[-1])
        wg = jnp.zeros((2, LANES, MIX_W), F32)
        for dd in range(2):
            wg = wg.at[dd, GLA_GATE_RANK * dd:GLA_GATE_RANK * (dd + 1), :].set(gla_gate_up[layer, dd].astype(F32))
        bg = gla_gate_bias[layer].reshape(2, 1, MIX_W).astype(F32)
        gla_gain = jnp.tile(gla_out_gain[layer], GLA_HEADS)[None, :].astype(F32)
        o_gla_c, o_gla_l = _gla(r3(gla_c, lctx), r3(gla_l, seq), r3(sm_c, lctx), r3(sm_l, seq),
                                wg, bg, gla_gain, seg, need_ctx)

        cw = jnp.concatenate([gdn_conv[layer].astype(F32), jnp.zeros((8 - SHORT_CONV, 3 * MIX_W), F32)], axis=0)
        alog = _lane_rows(gdn_a_log[layer], 32)
        dtb = _lane_rows(gdn_dt_bias[layer], 32)
        gdn_gain = jnp.tile(gdn_out_gain[layer], GDN_HEADS)[None, :].astype(F32)
        o_gdn_c, o_gdn_l = _gdn2(r3(gdn_c, lctx), r3(gdn_l, seq), r3(sm_c, lctx), r3(sm_l, seq),
                                cw, alog, dtb, sel, selb, gdn_gain, seg, need_ctx)

        o_att_l = _attention(r3(q_l, seq), [r3(kv_l, seq), r3(kv_c, lctx)], 128)
        h_lat = _outproj(o_gla_l.reshape(-1, MIX_W), o_gdn_l.reshape(-1, MIX_W), o_att_l.reshape(-1, 512),
                         h_lat, mod144, lat_row, w_o, seq, 512)
        if need_ctx:
            o_att_c = _attention(r3(q_c, lctx), [r3(kv_c, lctx)], 128)
            h_ctx = _outproj(o_gla_c.reshape(-1, MIX_W), o_gdn_c.reshape(-1, MIX_W), o_att_c.reshape(-1, 512),
                             h_ctx, mod144, ctx_row_fn, w_o, lctx, 256)

        j = layer // 2
        if layer % 2 == 0:
            w_gu = ffn_gate_up[j].astype(BF16)
            w_dn = ffn_down[j].astype(BF16)
            h_lat = _ffn(h_lat, mod144, lat_row, gain_ffn, w_gu, w_dn, seq, 512, 1408)
            if need_ctx:
                h_ctx = _ffn(h_ctx, mod144, ctx_row_fn, gain_ffn, w_gu, w_dn, lctx, 256, 1408)
        else:
            w_gu = moe_gate_up[j].astype(BF16)
            w_dn = moe_down[j].astype(BF16)
            w_r = jnp.concatenate([moe_router[j].astype(F32), jnp.zeros((d, LANES - N_EXPERTS), F32)], axis=1)
            b_r = jnp.concatenate([moe_router_bias[j].astype(F32), jnp.zeros((LANES - N_EXPERTS,), F32)])[None, :]
            h_lat = _moe_routed(h_lat, mod144, lat_row, gain_ffn, w_r, b_r, w_gu, w_dn, seq)
            if need_ctx:
                h_ctx = _moe_routed(h_ctx, mod144, ctx_row_fn, gain_ffn, w_r, b_r, w_gu, w_dn, lctx)
    return h_lat.reshape(bsz, seq, d)
```

```python
import functools

import numpy as np
import jax
import jax.numpy as jnp
from jax import lax
from jax.experimental import pallas as pl
from jax.experimental.pallas import tpu as pltpu

F32 = jnp.float32
BF16 = jnp.bfloat16

GRID_W = 64
HEAD_DIM = 64
CHUNK = 64
SUB = 16
EPS = 1e-6
GLA_HEADS = 4
GLA_GATE_RANK = 16
GLA_TAU = 16.0
GDN_HEADS = 4
SHORT_CONV = 5
ATT_Q_HEADS = 8
ATT_KV_HEADS = 2
ROPE_THETA = 10000.0
N_EXPERTS = 8
MIX_W = GLA_HEADS * HEAD_DIM
EXP_CLAMP = 80.0

LANES = 128
V7X_VMEM_BYTES = 64 * 1024 * 1024
VMEM_LIMIT = 56 * 1024 * 1024


def _cparams(sem):
    return pltpu.CompilerParams(dimension_semantics=sem, vmem_limit_bytes=VMEM_LIMIT)


def _silu(x):
    return x / (1.0 + jnp.exp(-x))


def _sigmoid(x):
    return 1.0 / (1.0 + jnp.exp(-x))


def _softplus(x):
    return jnp.maximum(x, 0.0) + jnp.log(1.0 + jnp.exp(-jnp.abs(x)))


def _dot(a, b):
    return jnp.dot(a.astype(BF16), b.astype(BF16), preferred_element_type=F32)


def _dot_nt(a, b):
    return lax.dot_general(a.astype(BF16), b.astype(BF16), (((1,), (1,)), ((), ())),
                           preferred_element_type=F32)


def _split(x):
    hi = x.astype(BF16)
    lo = (x - hi.astype(F32)).astype(BF16)
    return hi, lo


def _dot_xhl(x, w):
    hi, lo = _split(x)
    w = w.astype(BF16)
    return (jnp.dot(hi, w, preferred_element_type=F32) + jnp.dot(lo, w, preferred_element_type=F32))


def _dot_whl(w, x):
    hi, lo = _split(x)
    w = w.astype(BF16)
    return (jnp.dot(w, hi, preferred_element_type=F32) + jnp.dot(w, lo, preferred_element_type=F32))


def _dot3(a, b):
    ah, al = _split(a)
    bh, bl = _split(b)
    return (jnp.dot(ah, bh, preferred_element_type=F32) + jnp.dot(ah, bl, preferred_element_type=F32)
            + jnp.dot(al, bh, preferred_element_type=F32))


def _seg_sum64(sq, seg):
    outs = []
    for j in range(sq.shape[1] // LANES):
        outs.append(_dot_xhl(sq[:, LANES * j:LANES * (j + 1)], seg))
    return outs[0] if len(outs) == 1 else jnp.concatenate(outs, axis=1)


def _iota(shape, dim):
    return lax.broadcasted_iota(jnp.int32, shape, dim)


def _mod_kernel(c_ref, w_ref, b_ref, o_ref):
    s = _silu(c_ref[...])
    o_ref[0] = _dot(s, w_ref[0]) + b_ref[0]


def _modulation(cvec, w_mod, b_mod):
    depth, d, n = w_mod.shape
    rows = cvec.shape[0]
    tn = 1536
    return pl.pallas_call(
        _mod_kernel,
        grid=(depth, n // tn),
        in_specs=[pl.BlockSpec((rows, d), lambda l, j: (0, 0)),
                  pl.BlockSpec((1, d, tn), lambda l, j: (l, 0, j)),
                  pl.BlockSpec((1, 1, tn), lambda l, j: (l, 0, j))],
        out_specs=pl.BlockSpec((1, rows, tn), lambda l, j: (l, 0, j)),
        out_shape=jax.ShapeDtypeStruct((depth, rows, n), F32),
        compiler_params=_cparams(("arbitrary", "arbitrary")),
        name="modulation",
    )(cvec, w_mod, b_mod.reshape(depth, 1, n))


def _swap16(n, lane):
    fwd = pltpu.roll(n, LANES - 16, 1)
    bwd = pltpu.roll(n, 16, 1)
    return jnp.where((lane % 32) < 16, fwd, bwd)


def _inproj_kernel(*refs, rope):
    if rope:
        (h_ref, sh_ref, sc_ref, gain_ref, w_ref, hg_ref, seg_ref, cos_ref, sin_ref,
         gla_ref, gdn_ref, q_ref, kv_ref, small_ref) = refs
    else:
        (h_ref, sh_ref, sc_ref, gain_ref, w_ref, hg_ref, seg_ref,
         gla_ref, gdn_ref, q_ref, kv_ref, small_ref) = refs
    x = h_ref[...]
    ms = jnp.mean(x * x, axis=-1, keepdims=True)
    a = x * lax.rsqrt(ms + EPS) * (gain_ref[...] * (1.0 + sc_ref[0])) + sh_ref[0]
    p = jnp.dot(a.astype(BF16), w_ref[...], preferred_element_type=F32)
    gla_ref[...] = p[:, 0:1024]
    gdn_ref[...] = p[:, 1024:2048]
    small_ref[...] = p[:, 2816:2944]
    seg = seg_ref[...]
    lane = _iota((1, LANES), 1)
    outs = []
    for j in range(5):
        t = p[:, 2048 + LANES * j:2048 + LANES * (j + 1)]
        ss = _dot_xhl(t * t, seg)
        n = t * lax.rsqrt(ss * (1.0 / HEAD_DIM) + EPS) * hg_ref[:, LANES * j:LANES * (j + 1)]
        if rope:
            n = n * cos_ref[...] + _swap16(n, lane) * sin_ref[...]
        outs.append(n)
    q_ref[...] = jnp.concatenate(outs[:4], axis=1).astype(BF16)
    kv_ref[...] = jnp.concatenate([outs[4], p[:, 2688:2816]], axis=1).astype(BF16)


def _inproj(h2d, mod144, mod_row_fn, gain, w_p, hg, seg, tables, seq_len, tm):
    rows, d = h2d.shape
    n_all = w_p.shape[1]
    rope = tables is not None
    tiles_per_seq = seq_len // tm
    in_specs = [pl.BlockSpec((tm, d), lambda i: (i, 0)),
                pl.BlockSpec((1, 1, d), lambda i: (mod_row_fn(i // tiles_per_seq) * 6 + 0, 0, 0)),
                pl.BlockSpec((1, 1, d), lambda i: (mod_row_fn(i // tiles_per_seq) * 6 + 1, 0, 0)),
                pl.BlockSpec((1, d), lambda i: (0, 0)),
                pl.BlockSpec((d, n_all), lambda i: (0, 0)),
                pl.BlockSpec((1, 640), lambda i: (0, 0)),
                pl.BlockSpec((LANES, LANES), lambda i: (0, 0))]
    args = [h2d, mod144, mod144, gain, w_p, hg, seg]
    if rope:
        in_specs += [pl.BlockSpec((tm, LANES), lambda i: (i % tiles_per_seq, 0)),
                     pl.BlockSpec((tm, LANES), lambda i: (i % tiles_per_seq, 0))]
        args += list(tables)
    out_shape = (jax.ShapeDtypeStruct((rows, 1024), F32), jax.ShapeDtypeStruct((rows, 1024), F32),
                 jax.ShapeDtypeStruct((rows, 512), BF16), jax.ShapeDtypeStruct((rows, 256), BF16),
                 jax.ShapeDtypeStruct((rows, LANES), F32))
    out_specs = (pl.BlockSpec((tm, 1024), lambda i: (i, 0)), pl.BlockSpec((tm, 1024), lambda i: (i, 0)),
                 pl.BlockSpec((tm, 512), lambda i: (i, 0)), pl.BlockSpec((tm, 256), lambda i: (i, 0)),
                 pl.BlockSpec((tm, LANES), lambda i: (i, 0)))
    return pl.pallas_call(
        functools.partial(_inproj_kernel, rope=rope),
        grid=(rows // tm,), in_specs=in_specs, out_specs=out_specs, out_shape=out_shape,
        compiler_params=_cparams(("parallel",)),
        name="inproj_rope" if rope else "inproj",
    )(*args)


def _attn_kernel(*refs, nkv, tq):
    q_ref = refs[0]
    kv_refs = refs[1:1 + nkv]
    o_ref = refs[1 + nkv]
    q = q_ref[0]
    lane = _iota((1, LANES), 1)
    mlo = (lane < HEAD_DIM).astype(BF16)
    mhi = (lane >= HEAD_DIM).astype(BF16)
    kvs = [r[0] for r in kv_refs]
    outs = []
    for j in range(4):
        qc = q[:, LANES * j:LANES * (j + 1)]
        q2 = jnp.concatenate([qc * mlo, qc * mhi], axis=0)
        ss = [lax.dot_general(q2, kv[:, 0:LANES], (((1,), (1,)), ((), ())), preferred_element_type=F32)
              for kv in kvs]
        m = functools.reduce(jnp.maximum, [jnp.max(s, axis=-1, keepdims=True) for s in ss])
        ps = [jnp.exp(s - m) for s in ss]
        l = functools.reduce(lambda a, b: a + b, [jnp.sum(p, axis=-1, keepdims=True) for p in ps])
        o = functools.reduce(lambda a, b: a + b,
                             [jnp.dot(p.astype(BF16), kv[:, LANES:2 * LANES], preferred_element_type=F32)
                              for p, kv in zip(ps, kvs)])
        o = o / l
        outs.append(jnp.where(lane < HEAD_DIM, o[:tq], o[tq:]))
    o_ref[0] = jnp.concatenate(outs, axis=1).astype(BF16)


def _attention(q, kvs, tq):
    b, lq, _ = q.shape
    in_specs = [pl.BlockSpec((1, tq, 512), lambda i, j: (i, j, 0))]
    for kv in kvs:
        in_specs.append(pl.BlockSpec((1, kv.shape[1], 256), lambda i, j: (i, 0, 0)))
    return pl.pallas_call(
        functools.partial(_attn_kernel, nkv=len(kvs), tq=tq),
        grid=(b, lq // tq), in_specs=in_specs,
        out_specs=pl.BlockSpec((1, tq, 512), lambda i, j: (i, j, 0)),
        out_shape=jax.ShapeDtypeStruct((b, lq, 512), BF16),
        compiler_params=_cparams(("parallel", "arbitrary")),
        name="attention",
    )(q, *kvs)


def _head_masks():
    lane = _iota((1, MIX_W), 1)
    return [(lane // HEAD_DIM == h).astype(F32) for h in range(GLA_HEADS)]


def _blockdiag_mask():
    r = _iota((MIX_W, MIX_W), 0) // HEAD_DIM
    c = _iota((MIX_W, MIX_W), 1) // HEAD_DIM
    return r == c


def _finish_rows(o, gate, gain, seg):
    ss = _seg_sum64(o * o, seg)
    return o * lax.rsqrt(ss * (1.0 / HEAD_DIM) + EPS) * gain * _silu(gate)


def _scan_segments(xc_ref, xl_ref, step):
    for seg_i, x_ref in enumerate((xc_ref, xl_ref)):
        nch = x_ref.shape[1] // CHUNK

        def body(i, carry, seg_i=seg_i, nch=nch):
            step(seg_i, i, 0)
            step(seg_i, nch - 1 - i, 1)
            return carry

        lax.fori_loop(0, nch, body, 0)


def _gla_kernel(*refs, need_ctx):
    if need_ctx:
        (xc_ref, xl_ref, sc_ref, sl_ref, wg_ref, bg_ref, gain_ref, seg_ref,
         oc_ref, ol_ref, of_ref, ob_ref, st_ref) = refs
    else:
        (xc_ref, xl_ref, sc_ref, sl_ref, wg_ref, bg_ref, gain_ref, seg_ref,
         ol_ref, of_ref, ob_ref, st_ref) = refs
        oc_ref = None
    lc = xc_ref.shape[1]
    x_refs = (xc_ref, xl_ref)
    s_refs = (sc_ref, sl_ref)
    row_off = (0, lc)
    hmask = _head_masks()
    bd = _blockdiag_mask()
    ti = _iota((CHUNK, CHUNK), 0)
    si = _iota((CHUNK, CHUNK), 1)
    tri = [(si <= ti).astype(BF16), (si >= ti).astype(BF16)]
    rr = _iota((4 * CHUNK, CHUNK), 0)
    cc = _iota((4 * CHUNK, CHUNK), 1)
    t_of_row = (rr // (GLA_HEADS * SUB)) * SUB + rr % SUB
    causal = [cc <= t_of_row, cc >= t_of_row]
    st_ref[...] = jnp.zeros(st_ref.shape, F32)

    def step(seg_i, c, d):
        x_ref, s_ref = x_refs[seg_i], s_refs[seg_i]
        r0 = pl.multiple_of(c * CHUNK, CHUNK)
        q = x_ref[0, pl.ds(r0, CHUNK), 0:MIX_W] * (HEAD_DIM ** -0.5)
        k = x_ref[0, pl.ds(r0, CHUNK), MIX_W:2 * MIX_W]
        v = x_ref[0, pl.ds(r0, CHUNK), 2 * MIX_W:3 * MIX_W]
        sm = s_ref[0, pl.ds(r0, CHUNK), :]
        xg = _dot3(sm, wg_ref[d]) + bg_ref[d]
        g = (jnp.minimum(xg, 0.0) - jnp.log(1.0 + jnp.exp(-jnp.abs(xg)))) * (1.0 / GLA_TAU)
        b = _dot_whl(tri[d], g)
        pieces = []
        for i in range(CHUNK // SUB):
            e = SUB * i if d == 0 else SUB * i + SUB - 1
            bref = b[e:e + 1, :] - g[e:e + 1, :]
            kt = k * jnp.exp(jnp.minimum(bref - b, EXP_CLAMP))
            qt = q[SUB * i:SUB * (i + 1), :] * jnp.exp(b[SUB * i:SUB * (i + 1), :] - bref)
            qs = jnp.concatenate([qt * hmask[h] for h in range(GLA_HEADS)], axis=0)
            pieces.append(_dot_nt(qs, kt))
        scores = jnp.where(causal[d], jnp.concatenate(pieces, axis=0), 0.0)
        r = _dot(scores, v)
        intra = []
        for i in range(CHUNK // SUB):
            acc = None
            for h in range(GLA_HEADS):
                lo = (i * GLA_HEADS + h) * SUB
                term = r[lo:lo + SUB, :] * hmask[h]
                acc = term if acc is None else acc + term
            intra.append(acc)
        o_intra = jnp.concatenate(intra, axis=0)
        st = st_ref[d]
        o = _dot_nt(q * jnp.exp(b), st) + o_intra
        dst = of_ref if d == 0 else ob_ref
        dst[pl.ds(row_off[seg_i] + r0, CHUNK), :] = o
        e = CHUNK - 1 if d == 0 else 0
        b_end = b[e:e + 1, :]
        kh = k * jnp.exp(b_end - b)
        st_ref[d] = st * jnp.exp(b_end) + jnp.where(bd, _dot(v.T, kh), 0.0)

    _scan_segments(xc_ref, xl_ref, step)

    gain = gain_ref[...]
    seg = seg_ref[...]
    blk = 256
    outs = ((oc_ref, xc_ref, 0), (ol_ref, xl_ref, lc))
    for o_ref, x_ref, off in outs:
        if o_ref is None:
            continue

        def fin(i, carry, o_ref=o_ref, x_ref=x_ref, off=off):
            r0 = pl.multiple_of(i * blk, blk)
            o = of_ref[pl.ds(off + r0, blk), :] + ob_ref[pl.ds(off + r0, blk), :]
            gate = x_ref[0, pl.ds(r0, blk), 3 * MIX_W:4 * MIX_W]
            o_ref[0, pl.ds(r0, blk), :] = _finish_rows(o, gate, gain, seg).astype(BF16)
            return carry

        lax.fori_loop(0, x_ref.shape[1] // blk, fin, 0)


def _gla(x_ctx, x_lat, s_ctx, s_lat, wg, bg, gain, seg, need_ctx):
    b, lc, _ = x_ctx.shape
    ll = x_lat.shape[1]
    in_specs = [pl.BlockSpec((1, lc, 1024), lambda i: (i, 0, 0)),
                pl.BlockSpec((1, ll, 1024), lambda i: (i, 0, 0)),
                pl.BlockSpec((1, lc, LANES), lambda i: (i, 0, 0)),
                pl.BlockSpec((1, ll, LANES), lambda i: (i, 0, 0)),
                pl.BlockSpec((2, LANES, MIX_W), lambda i: (0, 0, 0)),
                pl.BlockSpec((2, 1, MIX_W), lambda i: (0, 0, 0)),
                pl.BlockSpec((1, MIX_W), lambda i: (0, 0)),
                pl.BlockSpec((LANES, LANES), lambda i: (0, 0))]
    out_shape = [jax.ShapeDtypeStruct((b, ll, MIX_W), BF16)]
    out_specs = [pl.BlockSpec((1, ll, MIX_W), lambda i: (i, 0, 0))]
    if need_ctx:
        out_shape.insert(0, jax.ShapeDtypeStruct((b, lc, MIX_W), BF16))
        out_specs.insert(0, pl.BlockSpec((1, lc, MIX_W), lambda i: (i, 0, 0)))
    res = pl.pallas_call(
        functools.partial(_gla_kernel, need_ctx=need_ctx),
        grid=(b,), in_specs=in_specs, out_specs=out_specs, out_shape=out_shape,
        scratch_shapes=[pltpu.VMEM((lc + ll, MIX_W), F32), pltpu.VMEM((lc + ll, MIX_W), F32),
                        pltpu.VMEM((2, MIX_W, MIX_W), F32)],
        compiler_params=_cparams(("parallel",)),
        name="gla_scan",
    )(x_ctx, x_lat, s_ctx, s_lat, wg, bg, gain, seg)
    return (res[0], res[1]) if need_ctx else (None, res[0])


def _gdn_kernel(*refs, need_ctx):
    if need_ctx:
        (xc_ref, xl_ref, sc_ref, sl_ref, cw_ref, alog_ref, dtb_ref, sel_ref, selb_ref, gain_ref, seg_ref,
         oc_ref, ol_ref, qkv_ref, of_ref, ob_ref, st_ref) = refs
    else:
        (xc_ref, xl_ref, sc_ref, sl_ref, cw_ref, alog_ref, dtb_ref, sel_ref, selb_ref, gain_ref, seg_ref,
         ol_ref, qkv_ref, of_ref, ob_ref, st_ref) = refs
        oc_ref = None
    lc = xc_ref.shape[1]
    x_refs = (xc_ref, xl_ref)
    s_refs = (sc_ref, sl_ref)
    row_off = (0, lc)
    seg = seg_ref[...]
    hmask = _head_masks()
    bd = _blockdiag_mask()
    ti = _iota((CHUNK, CHUNK), 0)
    si = _iota((CHUNK, CHUNK), 1)
    tri = [(si <= ti).astype(BF16), (si >= ti).astype(BF16)]
    ones = jnp.ones((CHUNK, CHUNK), BF16)
    tp = _iota((CHUNK, MIX_W), 0)
    sp = _iota((CHUNK, MIX_W), 1) % CHUNK
    le = sp <= tp
    ge = sp >= tp
    m_tri = [le, ge]
    m_strict = [sp < tp, sp > tp]
    m_sum = [ge.astype(F32), le.astype(F32)]
    blk16 = (tp // SUB) == (sp // SUB)
    eye = (tp == sp).astype(F32)

    cw = cw_ref[...]
    for seg_i, x_ref in enumerate(x_refs):
        ln = x_ref.shape[1]
        nch = ln // CHUNK

        def conv(c, carry, x_ref=x_ref, ln=ln, nch=nch, off=row_off[seg_i]):
            r0 = pl.multiple_of(c * CHUNK, CHUNK)
            center = x_ref[0, pl.ds(r0, CHUNK), 0:3 * MIX_W]
            p0 = pl.multiple_of(jnp.maximum(r0 - 8, 0), 8)
            n0 = pl.multiple_of(jnp.minimum(r0 + CHUNK, ln - 8), 8)
            prev = x_ref[0, pl.ds(p0, 8), 0:3 * MIX_W] * jnp.where(c > 0, 1.0, 0.0)
            nxt = x_ref[0, pl.ds(n0, 8), 0:3 * MIX_W] * jnp.where(c < nch - 1, 1.0, 0.0)
            ext = jnp.concatenate([prev, center, nxt], axis=0)
            pad = SHORT_CONV // 2
            acc = None
            for j in range(SHORT_CONV):
                term = ext[8 - pad + j:8 - pad + j + CHUNK, :] * cw[j:j + 1, :]
                acc = term if acc is None else acc + term
            y = _silu(acc)
            qk = y[:, 0:2 * MIX_W]
            ss = _seg_sum64(qk * qk, seg)
            qk = qk * lax.rsqrt(ss + EPS)
            qkv_ref[pl.ds(off + r0, CHUNK), 0:MIX_W] = qk[:, 0:MIX_W] * (HEAD_DIM ** -0.5)
            qkv_ref[pl.ds(off + r0, CHUNK), MIX_W:2 * MIX_W] = qk[:, MIX_W:2 * MIX_W]
            qkv_ref[pl.ds(off + r0, CHUNK), 2 * MIX_W:3 * MIX_W] = y[:, 2 * MIX_W:3 * MIX_W]
            return carry

        lax.fori_loop(0, nch, conv, 0)

    st_ref[...] = jnp.zeros(st_ref.shape, F32)

    def pk(y):
        return jnp.where(bd, jnp.concatenate([y] * GDN_HEADS, axis=0), 0.0)

    def mm(x, y):
        return _dot(x, pk(y))

    def step(seg_i, c, d):
        s_ref = s_refs[seg_i]
        r0 = pl.multiple_of(c * CHUNK, CHUNK)
        row = row_off[seg_i] + r0
        qn = qkv_ref[pl.ds(row, CHUNK), 0:MIX_W]
        kn = qkv_ref[pl.ds(row, CHUNK), MIX_W:2 * MIX_W]
        v = qkv_ref[pl.ds(row, CHUNK), 2 * MIX_W:3 * MIX_W]
        sm = s_ref[0, pl.ds(r0, CHUNK), :]
        g = -jnp.exp(alog_ref[d]) * _softplus(sm + dtb_ref[d])
        beta = _sigmoid(sm)
        gexp = _dot_xhl(g, sel_ref[d])
        bexp = _dot_xhl(beta, selb_ref[d])
        gam_t = _dot_whl(tri[d], gexp)
        gam_s = _dot_whl(ones, gexp * m_sum[d])
        dec = jnp.where(m_tri[d], jnp.exp(jnp.minimum(gam_t - gam_s, 0.0)), 0.0)
        kstack = jnp.concatenate([kn * hmask[h] for h in range(GDN_HEADS)], axis=0)
        kk = _dot_nt(kn, kstack)
        qk = _dot_nt(qn, kstack)
        a = jnp.where(m_strict[d], bexp * dec * kk, 0.0)
        dg = jnp.where(blk16, a, 0.0)
        lo = a - dg
        d2 = mm(dg, dg)
        d4 = mm(d2, d2)
        d8 = mm(d4, d4)
        t_inv = mm(mm(mm(eye - dg, eye + d2), eye + d4), eye + d8)
        m = mm(t_inv, lo)
        egam = jnp.exp(gam_t)
        sols = []
        for rhs in (bexp * v, bexp * egam * kn):
            y = mm(t_inv, rhs)
            z = y + mm(m, mm(m, y))
            sols.append(z - mm(m, z))
        sol_v, sol_k = sols
        st = st_ref[d]
        u = sol_v - _dot(sol_k, st)
        p = jnp.where(m_tri[d], qk * dec, 0.0)
        o = egam * _dot(qn, st) + _dot(p, pk(u))
        dst = of_ref if d == 0 else ob_ref
        dst[pl.ds(row, CHUNK), :] = o
        e = CHUNK - 1 if d == 0 else 0
        g_end = gam_t[e:e + 1, :]
        kh = kn * jnp.exp(g_end - gam_t)
        st_ref[d] = st * jnp.exp(g_end) + jnp.where(bd, _dot(kh.T, u), 0.0)

    _scan_segments(xc_ref, xl_ref, step)

    gain = gain_ref[...]
    blk = 256
    outs = ((oc_ref, xc_ref, 0), (ol_ref, xl_ref, lc))
    for o_ref, x_ref, off in outs:
        if o_ref is None:
            continue

        def fin(i, carry, o_ref=o_ref, x_ref=x_ref, off=off):
            r0 = pl.multiple_of(i * blk, blk)
            o = of_ref[pl.ds(off + r0, blk), :] + ob_ref[pl.ds(off + r0, blk), :]
            gate = x_ref[0, pl.ds(r0, blk), 3 * MIX_W:4 * MIX_W]
            o_ref[0, pl.ds(r0, blk), :] = _finish_rows(o, gate, gain, seg).astype(BF16)
            return carry

        lax.fori_loop(0, x_ref.shape[1] // blk, fin, 0)


def _gdn(x_ctx, x_lat, s_ctx, s_lat, cw, alog, dtb, sel, selb, gain, seg, need_ctx):
    b, lc, _ = x_ctx.shape
    ll = x_lat.shape[1]
    in_specs = [pl.BlockSpec((1, lc, 1024), lambda i: (i, 0, 0)),
                pl.BlockSpec((1, ll, 1024), lambda i: (i, 0, 0)),
                pl.BlockSpec((1, lc, LANES), lambda i: (i, 0, 0)),
                pl.BlockSpec((1, ll, LANES), lambda i: (i, 0, 0)),
                pl.BlockSpec((8, 3 * MIX_W), lambda i: (0, 0)),
                pl.BlockSpec((2, 1, LANES), lambda i: (0, 0, 0)),
                pl.BlockSpec((2, 1, LANES), lambda i: (0, 0, 0)),
                pl.BlockSpec((2, LANES, MIX_W), lambda i: (0, 0, 0)),
                pl.BlockSpec((2, LANES, MIX_W), lambda i: (0, 0, 0)),
                pl.BlockSpec((1, MIX_W), lambda i: (0, 0)),
                pl.BlockSpec((LANES, LANES), lambda i: (0, 0))]
    out_shape = [jax.ShapeDtypeStruct((b, ll, MIX_W), BF16)]
    out_specs = [pl.BlockSpec((1, ll, MIX_W), lambda i: (i, 0, 0))]
    if need_ctx:
        out_shape.insert(0, jax.ShapeDtypeStruct((b, lc, MIX_W), BF16))
        out_specs.insert(0, pl.BlockSpec((1, lc, MIX_W), lambda i: (i, 0, 0)))
    res = pl.pallas_call(
        functools.partial(_gdn_kernel, need_ctx=need_ctx),
        grid=(b,), in_specs=in_specs, out_specs=out_specs, out_shape=out_shape,
        scratch_shapes=[pltpu.VMEM((lc + ll, 3 * MIX_W), F32),
                        pltpu.VMEM((lc + ll, MIX_W), F32), pltpu.VMEM((lc + ll, MIX_W), F32),
                        pltpu.VMEM((2, MIX_W, MIX_W), F32)],
        compiler_params=_cparams(("parallel",)),
        name="gdn_scan",
    )(x_ctx, x_lat, s_ctx, s_lat, cw, alog, dtb, sel, selb, gain, seg)
    return (res[0], res[1]) if need_ctx else (None, res[0])


PAIR_W = 2 * HEAD_DIM
PREP_CHUNKS = 4


def _gdn2_kernel(*refs, need_ctx):
    if need_ctx:
        (xc_ref, xl_ref, sc_ref, sl_ref, cw_ref, alog_ref, dtb_ref, sel_ref, selb_ref, gain_ref, seg_ref,
         oc_ref, ol_ref, sk_ref, p_ref, qe_ref, sv_ref, kh_ref, dg_ref, of_ref, ob_ref, st_ref) = refs
    else:
        (xc_ref, xl_ref, sc_ref, sl_ref, cw_ref, alog_ref, dtb_ref, sel_ref, selb_ref, gain_ref, seg_ref,
         ol_ref, sk_ref, p_ref, qe_ref, sv_ref, kh_ref, dg_ref, of_ref, ob_ref, st_ref) = refs
        oc_ref = None
    lc = xc_ref.shape[1]
    x_refs = (xc_ref, xl_ref)
    s_refs = (sc_ref, sl_ref)
    row_off = (0, lc)
    npair = MIX_W // PAIR_W
    seg = seg_ref[...]
    ti = _iota((CHUNK, CHUNK), 0)
    si = _iota((CHUNK, CHUNK), 1)
    tri = [(si <= ti).astype(BF16), (si >= ti).astype(BF16)]
    ones = jnp.ones((CHUNK, CHUNK), BF16)
    tp = _iota((CHUNK, PAIR_W), 0)
    sp = _iota((CHUNK, PAIR_W), 1) % CHUNK
    le = sp <= tp
    ge = sp >= tp
    m_tri = [le, ge]
    m_strict = [sp < tp, sp > tp]
    m_sum = [ge.astype(F32), le.astype(F32)]
    blk16 = (tp // SUB) == (sp // SUB)
    eye = (tp == sp).astype(F32)
    bd2 = (_iota((PAIR_W, PAIR_W), 0) // HEAD_DIM) == (_iota((PAIR_W, PAIR_W), 1) // HEAD_DIM)
    lane_p = _iota((1, PAIR_W), 1)
    hm2 = [(lane_p // HEAD_DIM == h).astype(F32) for h in range(2)]
    cw = cw_ref[...]

    def pk2(y):
        return jnp.where(bd2, jnp.concatenate([y, y], axis=0), 0.0).astype(BF16)

    def mm(x, y):
        return jnp.dot(x.astype(BF16), pk2(y), preferred_element_type=F32)

    def front(seg_i, c):
        x_ref, s_ref = x_refs[seg_i], s_refs[seg_i]
        ln = x_ref.shape[1]
        nch = ln // CHUNK
        r0 = pl.multiple_of(c * CHUNK, CHUNK)
        center = x_ref[0, pl.ds(r0, CHUNK), 0:3 * MIX_W]
        p0 = pl.multiple_of(jnp.maximum(r0 - 8, 0), 8)
        n0 = pl.multiple_of(jnp.minimum(r0 + CHUNK, ln - 8), 8)
        prev = x_ref[0, pl.ds(p0, 8), 0:3 * MIX_W] * jnp.where(c > 0, 1.0, 0.0)
        nxt = x_ref[0, pl.ds(n0, 8), 0:3 * MIX_W] * jnp.where(c < nch - 1, 1.0, 0.0)
        ext = jnp.concatenate([prev, center, nxt], axis=0)
        pad = SHORT_CONV // 2
        acc = None
        for j in range(SHORT_CONV):
            term = ext[8 - pad + j:8 - pad + j + CHUNK, :] * cw[j:j + 1, :]
            acc = term if acc is None else acc + term
        y = _silu(acc)
        qk = y[:, 0:2 * MIX_W]
        qk = qk * lax.rsqrt(_seg_sum64(qk * qk, seg) + EPS)
        f = dict(row=row_off[seg_i] + r0, qn=qk[:, 0:MIX_W] * (HEAD_DIM ** -0.5), kn=qk[:, MIX_W:2 * MIX_W],
                 v=y[:, 2 * MIX_W:3 * MIX_W])
        sm = s_ref[0, pl.ds(r0, CHUNK), :]
        beta = _sigmoid(sm)
        f["kk"], f["qk"] = [], []
        for p in range(npair):
            ls = slice(PAIR_W * p, PAIR_W * (p + 1))
            kstack = jnp.concatenate([f["kn"][:, ls] * hm2[0], f["kn"][:, ls] * hm2[1]], axis=0)
            f["kk"].append(_dot_nt(f["kn"][:, ls], kstack))
            f["qk"].append(_dot_nt(f["qn"][:, ls], kstack))
        f["gexp"] = [_dot_xhl(-jnp.exp(alog_ref[d]) * _softplus(sm + dtb_ref[d]), sel_ref[d]) for d in range(2)]
        f["bexp"] = [_dot_xhl(beta, selb_ref[d]) for d in range(2)]
        f["gam"] = [_dot_whl(tri[d], f["gexp"][d]) for d in range(2)]
        return f

    def prep(seg_i, c2):
        fs = [front(seg_i, c2 * PREP_CHUNKS + k) for k in range(PREP_CHUNKS)]
        chains = [(f, d, p) for f in fs for d in range(2) for p in range(npair)]
        lss = [slice(PAIR_W * p, PAIR_W * (p + 1)) for _, _, p in chains]
        gam_t = [f["gam"][d][:, ls] for (f, d, _), ls in zip(chains, lss)]
        gam_s = [_dot_whl(ones, f["gexp"][d][:, ls] * m_sum[d]) for (f, d, _), ls in zip(chains, lss)]
        bx = [f["bexp"][d][:, ls] for (f, d, _), ls in zip(chains, lss)]
        dec = [jnp.where(m_tri[d], jnp.exp(jnp.minimum(gt - gs, 0.0)), 0.0)
               for (_, d, _), gt, gs in zip(chains, gam_t, gam_s)]
        a = [jnp.where(m_strict[d], b_ * dc * f["kk"][p], 0.0) for (f, d, p), b_, dc in zip(chains, bx, dec)]
        dgn = [jnp.where(blk16, a_, 0.0) for a_ in a]
        lo = [a_ - g_ for a_, g_ in zip(a, dgn)]
        d2 = [mm(g_, g_) for g_ in dgn]
        t1 = [mm(eye - g_, eye + s_) for g_, s_ in zip(dgn, d2)]
        d4 = [mm(s_, s_) for s_ in d2]
        t2 = [mm(t_, eye + s_) for t_, s_ in zip(t1, d4)]
        d8 = [mm(s_, s_) for s_ in d4]
        t_inv = [mm(t_, eye + s_) for t_, s_ in zip(t2, d8)]
        m = [mm(t_, l_) for t_, l_ in zip(t_inv, lo)]
        m2 = [mm(m_, m_) for m_ in m]
        w1 = [mm(eye - m_, eye + s_) for m_, s_ in zip(m, m2)]
        w = [mm(w_, t_) for w_, t_ in zip(w1, t_inv)]
        egam = [jnp.exp(gt) for gt in gam_t]
        solv = [mm(w_, b_ * f["v"][:, ls]) for (f, _, _), w_, b_, ls in zip(chains, w, bx, lss)]
        solk = [mm(w_, b_ * eg * f["kn"][:, ls]) for (f, _, _), w_, b_, eg, ls in zip(chains, w, bx, egam, lss)]
        for i, (f, d, p) in enumerate(chains):
            ls, row = lss[i], f["row"]
            e = CHUNK - 1 if d == 0 else 0
            g_end = gam_t[i][e:e + 1, :]
            sv_ref[d, pl.ds(row, CHUNK), ls] = solv[i]
            sk_ref[d, pl.ds(row, CHUNK), ls] = solk[i].astype(BF16)
            p_ref[d, pl.ds(row, CHUNK), ls] = jnp.where(m_tri[d], f["qk"][p] * dec[i], 0.0).astype(BF16)
            qe_ref[d, pl.ds(row, CHUNK), ls] = (egam[i] * f["qn"][:, ls]).astype(BF16)
            kh_ref[d, pl.ds(row, CHUNK), ls] = f["kn"][:, ls] * jnp.exp(g_end - gam_t[i])
            dg_ref[d, pl.ds(pl.multiple_of(row // 8, 8), 8), ls] = jnp.broadcast_to(jnp.exp(g_end), (8, PAIR_W))

    for seg_i, x_ref in enumerate(x_refs):
        def prep_body(c, carry, seg_i=seg_i):
            prep(seg_i, c)
            return carry

        lax.fori_loop(0, x_ref.shape[1] // (CHUNK * PREP_CHUNKS), prep_body, 0)

    st_ref[...] = jnp.zeros(st_ref.shape, F32)

    def scan_body(seg_i, i, nch):
        chains = [(d, p) for d in range(2) for p in range(npair)]
        rows = [row_off[seg_i] + pl.multiple_of((i if d == 0 else nch - 1 - i) * CHUNK, CHUNK) for d, _ in chains]
        lss = [slice(PAIR_W * p, PAIR_W * (p + 1)) for _, p in chains]
        st = [st_ref[d, p] for d, p in chains]
        stb = [s_.astype(BF16) for s_ in st]
        u = [sv_ref[d, pl.ds(r, CHUNK), ls] - jnp.dot(sk_ref[d, pl.ds(r, CHUNK), ls], sb, preferred_element_type=F32)
             for (d, _), r, ls, sb in zip(chains, rows, lss, stb)]
        oq = [jnp.dot(qe_ref[d, pl.ds(r, CHUNK), ls], sb, preferred_element_type=F32)
              for (d, _), r, ls, sb in zip(chains, rows, lss, stb)]
        ou = [jnp.dot(p_ref[d, pl.ds(r, CHUNK), ls], pk2(u_), preferred_element_type=F32)
              for (d, _), r, ls, u_ in zip(chains, rows, lss, u)]
        ku = [_dot(kh_ref[d, pl.ds(r, CHUNK), ls].T, u_) for (d, _), r, ls, u_ in zip(chains, rows, lss, u)]
        for j, (d, p) in enumerate(chains):
            dst = of_ref if d == 0 else ob_ref
            dst[pl.ds(rows[j], CHUNK), lss[j]] = oq[j] + ou[j]
            dgr = dg_ref[d, pl.ds(pl.multiple_of(rows[j] // 8, 8), 8), lss[j]][0:1, :]
            st_ref[d, p] = st[j] * dgr + jnp.where(bd2, ku[j], 0.0)

    for seg_i, x_ref in enumerate(x_refs):
        nch = x_ref.shape[1] // CHUNK

        def scan_iter(i, carry, seg_i=seg_i, nch=nch):
            scan_body(seg_i, i, nch)
            return carry

        lax.fori_loop(0, nch, scan_iter, 0)

    gain = gain_ref[...]
    blk = 256
    outs = ((oc_ref, xc_ref, 0), (ol_ref, xl_ref, lc))
    for o_ref, x_ref, off in outs:
        if o_ref is None:
            continue

        def fin(i, carry, o_ref=o_ref, x_ref=x_ref, off=off):
            r0 = pl.multiple_of(i * blk, blk)
            o = of_ref[pl.ds(off + r0, blk), :] + ob_ref[pl.ds(off + r0, blk), :]
            gate = x_ref[0, pl.ds(r0, blk), 3 * MIX_W:4 * MIX_W]
            o_ref[0, pl.ds(r0, blk), :] = _finish_rows(o, gate, gain, seg).astype(BF16)
            return carry

        lax.fori_loop(0, x_ref.shape[1] // blk, fin, 0)


def _gdn2(x_ctx, x_lat, s_ctx, s_lat, cw, alog, dtb, sel, selb, gain, seg, need_ctx):
    b, lc, _ = x_ctx.shape
    ll = x_lat.shape[1]
    lt = lc + ll
    in_specs = [pl.BlockSpec((1, lc, 1024), lambda i: (i, 0, 0)),
                pl.BlockSpec((1, ll, 1024), lambda i: (i, 0, 0)),
                pl.BlockSpec((1, lc, LANES), lambda i: (i, 0, 0)),
                pl.BlockSpec((1, ll, LANES), lambda i: (i, 0, 0)),
                pl.BlockSpec((8, 3 * MIX_W), lambda i: (0, 0)),
                pl.BlockSpec((2, 1, LANES), lambda i: (0, 0, 0)),
                pl.BlockSpec((2, 1, LANES), lambda i: (0, 0, 0)),
                pl.BlockSpec((2, LANES, MIX_W), lambda i: (0, 0, 0)),
                pl.BlockSpec((2, LANES, MIX_W), lambda i: (0, 0, 0)),
                pl.BlockSpec((1, MIX_W), lambda i: (0, 0)),
                pl.BlockSpec((LANES, LANES), lambda i: (0, 0))]
    out_shape = [jax.ShapeDtypeStruct((b, ll, MIX_W), BF16)]
    out_specs = [pl.BlockSpec((1, ll, MIX_W), lambda i: (i, 0, 0))]
    if need_ctx:
        out_shape.insert(0, jax.ShapeDtypeStruct((b, lc, MIX_W), BF16))
        out_specs.insert(0, pl.BlockSpec((1, lc, MIX_W), lambda i: (i, 0, 0)))
    res = pl.pallas_call(
        functools.partial(_gdn2_kernel, need_ctx=need_ctx),
        grid=(b,), in_specs=in_specs, out_specs=out_specs, out_shape=out_shape,
        scratch_shapes=[pltpu.VMEM((2, lt, MIX_W), BF16), pltpu.VMEM((2, lt, MIX_W), BF16),
                        pltpu.VMEM((2, lt, MIX_W), BF16), pltpu.VMEM((2, lt, MIX_W), F32),
                        pltpu.VMEM((2, lt, MIX_W), F32), pltpu.VMEM((2, lt // 8, MIX_W), F32),
                        pltpu.VMEM((lt, MIX_W), F32), pltpu.VMEM((lt, MIX_W), F32),
                        pltpu.VMEM((2, MIX_W // PAIR_W, PAIR_W, PAIR_W), F32)],
        compiler_params=_cparams(("parallel",)),
        name="gdn_scan",
    )(x_ctx, x_lat, s_ctx, s_lat, cw, alog, dtb, sel, selb, gain, seg)
    return (res[0], res[1]) if need_ctx else (None, res[0])


def _outproj_kernel(gla_ref, gdn_ref, att_ref, h_ref, g_ref, w_ref, o_ref):
    y = (jnp.dot(gla_ref[...], w_ref[0:256, :], preferred_element_type=F32)
         + jnp.dot(gdn_ref[...], w_ref[256:512, :], preferred_element_type=F32)
         + jnp.dot(att_ref[...], w_ref[512:1024, :], preferred_element_type=F32))
    o_ref[...] = h_ref[...] + g_ref[0] * y


def _outproj(gla, gdn, att, h2d, mod144, mod_row_fn, w, seq_len, tm):
    rows, d = h2d.shape
    tiles_per_seq = seq_len // tm
    return pl.pallas_call(
        _outproj_kernel,
        grid=(rows // tm,),
        in_specs=[pl.BlockSpec((tm, 256), lambda i: (i, 0)),
                  pl.BlockSpec((tm, 256), lambda i: (i, 0)),
                  pl.BlockSpec((tm, 512), lambda i: (i, 0)),
                  pl.BlockSpec((tm, d), lambda i: (i, 0)),
                  pl.BlockSpec((1, 1, d), lambda i: (mod_row_fn(i // tiles_per_seq) * 6 + 2, 0, 0)),
                  pl.BlockSpec((d, d), lambda i: (0, 0))],
        out_specs=pl.BlockSpec((tm, d), lambda i: (i, 0)),
        out_shape=jax.ShapeDtypeStruct((rows, d), F32),
        compiler_params=_cparams(("parallel",)),
        name="outproj",
    )(gla, gdn, att, h2d, mod144, w)


def _norm_mod(h_ref, sh_ref, sc_ref, gain_ref):
    x = h_ref[...]
    ms = jnp.mean(x * x, axis=-1, keepdims=True)
    return x * lax.rsqrt(ms + EPS) * (gain_ref[...] * (1.0 + sc_ref[0])) + sh_ref[0]


def _ffn_kernel(h_ref, sh_ref, sc_ref, g_ref, gain_ref, wg_ref, wu_ref, wd_ref, o_ref, b_scr, acc_scr, *, nf):
    f = pl.program_id(1)

    @pl.when(f == 0)
    def _():
        b_scr[...] = _norm_mod(h_ref, sh_ref, sc_ref, gain_ref).astype(BF16)
        acc_scr[...] = jnp.zeros(acc_scr.shape, F32)

    b = b_scr[...]
    gg = jnp.dot(b, wg_ref[...], preferred_element_type=F32)
    uu = jnp.dot(b, wu_ref[...], preferred_element_type=F32)
    hid = (_silu(gg) * uu).astype(BF16)
    acc_scr[...] += jnp.dot(hid, wd_ref[...], preferred_element_type=F32)

    @pl.when(f == nf - 1)
    def _():
        o_ref[...] = h_ref[...] + g_ref[0] * acc_scr[...]


def _ffn(h2d, mod144, mod_row_fn, gain, w_gu, w_down, seq_len, tm, tf):
    rows, d = h2d.shape
    dff = w_down.shape[0]
    nf = dff // tf
    tiles_per_seq = seq_len // tm

    def mod_spec(k):
        return pl.BlockSpec((1, 1, d), lambda i, f: (mod_row_fn(i // tiles_per_seq) * 6 + k, 0, 0))

    return pl.pallas_call(
        functools.partial(_ffn_kernel, nf=nf),
        grid=(rows // tm, nf),
        in_specs=[pl.BlockSpec((tm, d), lambda i, f: (i, 0)),
                  mod_spec(3), mod_spec(4), mod_spec(5),
                  pl.BlockSpec((1, d), lambda i, f: (0, 0)),
                  pl.BlockSpec((d, tf), lambda i, f: (0, f)),
                  pl.BlockSpec((d, tf), lambda i, f: (0, nf + f)),
                  pl.BlockSpec((tf, d), lambda i, f: (f, 0))],
        out_specs=pl.BlockSpec((tm, d), lambda i, f: (i, 0)),
        out_shape=jax.ShapeDtypeStruct((rows, d), F32),
        scratch_shapes=[pltpu.VMEM((tm, d), BF16), pltpu.VMEM((tm, d), F32)],
        compiler_params=_cparams(("parallel", "arbitrary")),
        name="ffn",
    )(h2d, mod144, mod144, mod144, gain, w_gu, w_gu, w_down)


def _moe_kernel(h_ref, sh_ref, sc_ref, g_ref, gain_ref, wr_ref, br_ref, wg_ref, wu_ref, wd_ref, o_ref,
                b_scr, comb_scr, acc_scr, *, nf):
    e = pl.program_id(1)
    f = pl.program_id(2)
    lane = _iota((1, LANES), 1)
    lane_f = lane.astype(F32)

    @pl.when((e == 0) & (f == 0))
    def _():
        b = _norm_mod(h_ref, sh_ref, sc_ref, gain_ref)
        b_scr[...] = b.astype(BF16)
        acc_scr[...] = jnp.zeros(acc_scr.shape, F32)
        logits = _dot3(b, wr_ref[...]) + br_ref[...]
        logits = jnp.where(lane < N_EXPERTS, logits, -jnp.inf)
        m1 = jnp.max(logits, axis=-1, keepdims=True)
        i1 = jnp.min(jnp.where(logits == m1, lane_f, float(LANES)), axis=-1, keepdims=True)
        rest = jnp.where(lane_f == i1, -jnp.inf, logits)
        m2 = jnp.max(rest, axis=-1, keepdims=True)
        i2 = jnp.min(jnp.where(rest == m2, lane_f, float(LANES)), axis=-1, keepdims=True)
        t = jnp.exp(m2 - m1)
        w1 = 1.0 / (1.0 + t)
        comb_scr[...] = jnp.where(lane_f == i1, w1, 0.0) + jnp.where(lane_f == i2, t * w1, 0.0)

    b = b_scr[...]
    gg = jnp.dot(b, wg_ref[0], preferred_element_type=F32)
    uu = jnp.dot(b, wu_ref[0], preferred_element_type=F32)
    hid = (_silu(gg) * uu).astype(BF16)
    w_e = jnp.sum(jnp.where(lane == e, comb_scr[...], 0.0), axis=-1, keepdims=True)
    acc_scr[...] += w_e * jnp.dot(hid, wd_ref[0], preferred_element_type=F32)

    @pl.when((e == N_EXPERTS - 1) & (f == nf - 1))
    def _():
        o_ref[...] = h_ref[...] + g_ref[0] * acc_scr[...]


def _moe(h2d, mod144, mod_row_fn, gain, w_router, b_router, w_gu, w_down, seq_len, tm, tf):
    rows, d = h2d.shape
    ne, dff, _ = w_down.shape
    nf = dff // tf
    tiles_per_seq = seq_len // tm

    def mod_spec(k):
        return pl.BlockSpec((1, 1, d), lambda i, e, f: (mod_row_fn(i // tiles_per_seq) * 6 + k, 0, 0))

    return pl.pallas_call(
        functools.partial(_moe_kernel, nf=nf),
        grid=(rows // tm, ne, nf),
        in_specs=[pl.BlockSpec((tm, d), lambda i, e, f: (i, 0)),
                  mod_spec(3), mod_spec(4), mod_spec(5),
                  pl.BlockSpec((1, d), lambda i, e, f: (0, 0)),
                  pl.BlockSpec((d, LANES), lambda i, e, f: (0, 0)),
                  pl.BlockSpec((1, LANES), lambda i, e, f: (0, 0)),
                  pl.BlockSpec((1, d, tf), lambda i, e, f: (e, 0, f)),
                  pl.BlockSpec((1, d, tf), lambda i, e, f: (e, 0, nf + f)),
                  pl.BlockSpec((1, tf, d), lambda i, e, f: (e, f, 0))],
        out_specs=pl.BlockSpec((tm, d), lambda i, e, f: (i, 0)),
        out_shape=jax.ShapeDtypeStruct((rows, d), F32),
        scratch_shapes=[pltpu.VMEM((tm, d), BF16), pltpu.VMEM((tm, LANES), F32), pltpu.VMEM((tm, d), F32)],
        compiler_params=_cparams(("parallel", "arbitrary", "arbitrary")),
        name="moe",
    )(h2d, mod144, mod144, mod144, gain, w_router, b_router, w_gu, w_gu, w_down)


MOE_TILE = 512
COMBINE_TOKENS = 256


def _router_kernel(h_ref, sh_ref, sc_ref, gain_ref, wr_ref, br_ref, b_ref, route_ref):
    lane = _iota((1, LANES), 1)
    lane_f = lane.astype(F32)
    b = _norm_mod(h_ref, sh_ref, sc_ref, gain_ref)
    b_ref[...] = b
    logits = _dot3(b, wr_ref[...]) + br_ref[...]
    logits = jnp.where(lane < N_EXPERTS, logits, -jnp.inf)
    m1 = jnp.max(logits, axis=-1, keepdims=True)
    i1 = jnp.min(jnp.where(logits == m1, lane_f, float(LANES)), axis=-1, keepdims=True)
    rest = jnp.where(lane_f == i1, -jnp.inf, logits)
    m2 = jnp.max(rest, axis=-1, keepdims=True)
    i2 = jnp.min(jnp.where(rest == m2, lane_f, float(LANES)), axis=-1, keepdims=True)
    t = jnp.exp(m2 - m1)
    w1 = 1.0 / (1.0 + t)
    route_ref[...] = (jnp.where(lane == 0, i1, 0.0) + jnp.where(lane == 1, i2, 0.0)
                      + jnp.where(lane == 2, w1, 0.0) + jnp.where(lane == 3, t * w1, 0.0))


def _router(h2d, mod144, mod_row_fn, gain, w_router, b_router, seq_len, tm):
    rows, d = h2d.shape
    tiles_per_seq = seq_len // tm

    def mod_spec(k):
        return pl.BlockSpec((1, 1, d), lambda i: (mod_row_fn(i // tiles_per_seq) * 6 + k, 0, 0))

    return pl.pallas_call(
        _router_kernel,
        grid=(rows // tm,),
        in_specs=[pl.BlockSpec((tm, d), lambda i: (i, 0)), mod_spec(3), mod_spec(4),
                  pl.BlockSpec((1, d), lambda i: (0, 0)),
                  pl.BlockSpec((d, LANES), lambda i: (0, 0)),
                  pl.BlockSpec((1, LANES), lambda i: (0, 0))],
        out_specs=(pl.BlockSpec((tm, d), lambda i: (i, 0)), pl.BlockSpec((tm, LANES), lambda i: (i, 0))),
        out_shape=(jax.ShapeDtypeStruct((rows, d), F32), jax.ShapeDtypeStruct((rows, LANES), F32)),
        compiler_params=_cparams(("parallel",)),
        name="moe_router",
    )(h2d, mod144, mod144, gain, w_router, b_router)


def _moe_group_kernel(te_ref, nt_ref, idx_ref, idx_next_ref, b_ref, wg_ref, wu_ref, wd_ref, ys_ref,
                      xbuf, sems, xb_scr, acc_scr, *, nf):
    i = pl.program_id(0)
    f = pl.program_id(1)
    n_used = nt_ref[0]
    used = i < n_used
    slot = i % 2

    def gather(rows_ref, dst_slot):
        def issue(r, carry):
            pltpu.make_async_copy(b_ref.at[pl.ds(rows_ref[0, 0, r], 1)], xbuf.at[dst_slot, pl.ds(r, 1)],
                                  sems.at[dst_slot]).start()
            return carry

        lax.fori_loop(0, MOE_TILE, issue, 0, unroll=8)

    @pl.when((f == 0) & (i == 0) & used)
    def _():
        gather(idx_ref, 0)

    @pl.when((f == 0) & (i + 1 < n_used))
    def _():
        gather(idx_next_ref, 1 - slot)

    @pl.when(used & (f == 0))
    def _():
        pltpu.make_async_copy(b_ref.at[pl.ds(0, MOE_TILE)], xbuf.at[slot], sems.at[slot]).wait()
        xb_scr[...] = xbuf[slot].astype(BF16)

    @pl.when(used)
    def _():
        x = xb_scr[...]
        gg = jnp.dot(x, wg_ref[0], preferred_element_type=F32)
        uu = jnp.dot(x, wu_ref[0], preferred_element_type=F32)
        hid = (_silu(gg) * uu).astype(BF16)
        part = jnp.dot(hid, wd_ref[0], preferred_element_type=F32)

        @pl.when(f == 0)
        def _():
            acc_scr[...] = part

        @pl.when(f > 0)
        def _():
            acc_scr[...] += part

    @pl.when(used & (f == nf - 1))
    def _():
        ys_ref[...] = acc_scr[...]

    @pl.when(jnp.logical_not(used) & (f == nf - 1))
    def _():
        ys_ref[...] = jnp.zeros(ys_ref.shape, F32)


def _moe_group(tile_expert, n_tiles_used, tok_sorted, b, w_gu, w_down, tf):
    p = tok_sorted.shape[0]
    d = b.shape[1]
    dff = w_down.shape[1]
    nf = dff // tf
    n_tiles = p // MOE_TILE
    idx3 = tok_sorted.reshape(n_tiles, 1, MOE_TILE)
    grid_spec = pltpu.PrefetchScalarGridSpec(
        num_scalar_prefetch=2,
        grid=(n_tiles, nf),
        in_specs=[pl.BlockSpec((1, 1, MOE_TILE), lambda i, f, te, nt: (i, 0, 0), memory_space=pltpu.SMEM),
                  pl.BlockSpec((1, 1, MOE_TILE), lambda i, f, te, nt: (jnp.minimum(i + 1, n_tiles - 1), 0, 0),
                               memory_space=pltpu.SMEM),
                  pl.BlockSpec(memory_space=pl.ANY),
                  pl.BlockSpec((1, d, tf), lambda i, f, te, nt: (te[i], 0, f)),
                  pl.BlockSpec((1, d, tf), lambda i, f, te, nt: (te[i], 0, nf + f)),
                  pl.BlockSpec((1, tf, d), lambda i, f, te, nt: (te[i], f, 0))],
        out_specs=pl.BlockSpec((MOE_TILE, d), lambda i, f, te, nt: (i, 0)),
        scratch_shapes=[pltpu.VMEM((2, MOE_TILE, d), F32), pltpu.SemaphoreType.DMA((2,)),
                        pltpu.VMEM((MOE_TILE, d), BF16), pltpu.VMEM((MOE_TILE, d), F32)])
    return pl.pallas_call(
        functools.partial(_moe_group_kernel, nf=nf),
        grid_spec=grid_spec,
        out_shape=jax.ShapeDtypeStruct((p, d), F32),
        compiler_params=_cparams(("arbitrary", "arbitrary")),
        name="moe_experts",
    )(tile_expert, n_tiles_used, idx3, idx3, b, w_gu, w_gu, w_down)


def _moe_combine_kernel(pos_ref, ys_ref, h_ref, route_ref, g_ref, o_ref, buf, sem):
    n = 2 * COMBINE_TOKENS

    def issue(r, carry):
        pltpu.make_async_copy(ys_ref.at[pl.ds(pos_ref[0, 0, r], 1)], buf.at[pl.ds(r, 1)], sem).start()
        return carry

    lax.fori_loop(0, n, issue, 0, unroll=8)
    pltpu.make_async_copy(ys_ref.at[pl.ds(0, n)], buf, sem).wait()
    w1 = route_ref[:, 2:3]
    w2 = route_ref[:, 3:4]
    o_ref[...] = h_ref[...] + g_ref[0] * (w1 * buf[0:COMBINE_TOKENS, :] + w2 * buf[COMBINE_TOKENS:n, :])


def _moe_combine(pos, ys, h2d, route, mod144, mod_row_fn, seq_len):
    rows, d = h2d.shape
    tm = COMBINE_TOKENS
    tiles_per_seq = seq_len // tm
    steps = rows // tm
    return pl.pallas_call(
        _moe_combine_kernel,
        grid=(steps,),
        in_specs=[pl.BlockSpec((1, 1, 2 * tm), lambda i: (i, 0, 0), memory_space=pltpu.SMEM),
                  pl.BlockSpec(memory_space=pl.ANY),
                  pl.BlockSpec((tm, d), lambda i: (i, 0)),
                  pl.BlockSpec((tm, LANES), lambda i: (i, 0)),
                  pl.BlockSpec((1, 1, d), lambda i: (mod_row_fn(i // tiles_per_seq) * 6 + 5, 0, 0))],
        out_specs=pl.BlockSpec((tm, d), lambda i: (i, 0)),
        out_shape=jax.ShapeDtypeStruct((rows, d), F32),
        scratch_shapes=[pltpu.VMEM((2 * tm, d), F32), pltpu.SemaphoreType.DMA(())],
        compiler_params=_cparams(("arbitrary",)),
        name="moe_combine",
    )(pos, ys, h2d, route, mod144)


def _moe_routed(h2d, mod144, mod_row_fn, gain, w_router, b_router, w_gu, w_down, seq_len):
    rows, d = h2d.shape
    b, route = _router(h2d, mod144, mod_row_fn, gain, w_router, b_router, seq_len, min(512, seq_len))
    ex = jnp.concatenate([route[:, 0], route[:, 1]]).astype(jnp.int32)
    tok = jnp.concatenate([jnp.arange(rows, dtype=jnp.int32)] * 2)
    onehot = (ex[:, None] == jnp.arange(N_EXPERTS, dtype=jnp.int32)[None, :]).astype(jnp.int32)
    rank = jnp.sum((jnp.cumsum(onehot, axis=0) - onehot) * onehot, axis=1)
    counts = jnp.sum(onehot, axis=0)
    padded = ((counts + MOE_TILE - 1) // MOE_TILE) * MOE_TILE
    ends = jnp.cumsum(padded)
    starts = ends - padded
    pos = jnp.sum(onehot * starts[None, :], axis=1) + rank
    p_rows = 2 * rows + N_EXPERTS * MOE_TILE
    tok_sorted = jnp.zeros((p_rows,), jnp.int32).at[pos].set(tok)
    tile_first = jnp.arange(p_rows // MOE_TILE, dtype=jnp.int32) * MOE_TILE
    tile_expert = jnp.minimum(jnp.sum((tile_first[:, None] >= ends[None, :]).astype(jnp.int32), axis=1),
                              N_EXPERTS - 1).astype(jnp.int32)
    n_tiles_used = (ends[-1:] // MOE_TILE).astype(jnp.int32)
    ys = _moe_group(tile_expert, n_tiles_used, tok_sorted, b, w_gu, w_down, 1408)
    steps = rows // COMBINE_TOKENS
    pos2 = jnp.concatenate([pos[:rows].reshape(steps, 1, COMBINE_TOKENS),
                            pos[rows:].reshape(steps, 1, COMBINE_TOKENS)], axis=2).astype(jnp.int32)
    return _moe_combine(pos2, ys, h2d, route, mod144, mod_row_fn, seq_len)


ATT_HEAD_ORDER = (0, 4, 1, 5, 2, 6, 3, 7)


def _layout_w_in(w):
    gla = w[:, 0:1024]
    glow = w[:, 1024:1056]
    gdn = w[:, 1056:2080]
    ab = w[:, 2080:2096]
    q = jnp.concatenate([w[:, 2096 + HEAD_DIM * h:2096 + HEAD_DIM * (h + 1)] for h in ATT_HEAD_ORDER], axis=1)
    kv = w[:, 2608:2864]
    pad = jnp.zeros((w.shape[0], LANES - 48), w.dtype)
    return jnp.concatenate([gla, gdn, q, kv, glow, ab, pad], axis=1).astype(BF16)


def _layout_w_out(w):
    att = [w[512 + HEAD_DIM * h:512 + HEAD_DIM * (h + 1)] for h in ATT_HEAD_ORDER]
    return jnp.concatenate([w[0:512]] + att, axis=0).astype(BF16)


def _rope_tables(seq_len):
    rows = seq_len // GRID_W
    row = jnp.repeat(jnp.arange(rows), GRID_W).astype(F32)
    col = jnp.tile(jnp.arange(GRID_W), rows).astype(F32)
    inv_freq = ROPE_THETA ** (-jnp.arange(0, HEAD_DIM // 2, 2, dtype=F32) / (HEAD_DIM // 2))
    ar = row[:, None] * inv_freq
    ac = col[:, None] * inv_freq
    cos = jnp.concatenate([jnp.cos(ar), jnp.cos(ar), jnp.cos(ac), jnp.cos(ac)], axis=-1)
    sin = jnp.concatenate([-jnp.sin(ar), jnp.sin(ar), -jnp.sin(ac), jnp.sin(ac)], axis=-1)
    return jnp.tile(cos, (1, 2)), jnp.tile(sin, (1, 2))


def _seg_matrix():
    i = np.arange(LANES)
    return jnp.asarray((i[:, None] // HEAD_DIM) == (i[None, :] // HEAD_DIM), dtype=BF16)


def _gdn_select():
    sel = np.zeros((2, LANES, MIX_W), np.float32)
    selb = np.zeros((2, LANES, MIX_W), np.float32)
    for d in range(2):
        for h in range(GDN_HEADS):
            sel[d, 32 + GDN_HEADS * d + h, HEAD_DIM * h:HEAD_DIM * (h + 1)] = 1.0
            selb[d, 40 + GDN_HEADS * d + h, HEAD_DIM * h:HEAD_DIM * (h + 1)] = 1.0
    return jnp.asarray(sel, BF16), jnp.asarray(selb, BF16)


def _lane_rows(vals, base):
    out = jnp.zeros((2, 1, LANES), F32)
    for d in range(2):
        out = out.at[d, 0, base + GDN_HEADS * d:base + GDN_HEADS * (d + 1)].set(vals[d].astype(F32))
    return out


def kernel(x, c, ctx, c_ctx, w_mod, b_mod, norm_mix, norm_ffn, w_in, gla_gate_up, gla_gate_bias, gla_out_gain,
           gdn_conv, gdn_a_log, gdn_dt_bias, gdn_out_gain, att_q_gain, att_k_gain, w_out, ffn_gate_up, ffn_down,
           moe_router, moe_router_bias, moe_gate_up, moe_down):
    bsz, seq, d = x.shape
    lctx = ctx.shape[1]
    depth = w_mod.shape[0]
    ctx_row = bsz

    mod_rows = ((bsz + 1 + 7) // 8) * 8
    cvec = jnp.concatenate([c, c_ctx[None, :], jnp.zeros((mod_rows - bsz - 1, d), F32)], axis=0)
    mods = _modulation(cvec, w_mod, b_mod)

    seg = _seg_matrix()
    tables = _rope_tables(seq)
    sel, selb = _gdn_select()
    lat_row = lambda b: b
    ctx_row_fn = lambda b: ctx_row

    h_lat = x.reshape(bsz * seq, d)
    h_ctx = ctx.reshape(bsz * lctx, d)
    for layer in range(depth):
        need_ctx = layer < depth - 1
        mod144 = mods[layer].reshape(mod_rows * 6, 1, d)
        w_p = _layout_w_in(w_in[layer])
        w_o = _layout_w_out(w_out[layer])
        hg = jnp.concatenate([jnp.tile(att_q_gain[layer], ATT_Q_HEADS) * (HEAD_DIM ** -0.5),
                              jnp.tile(att_k_gain[layer], ATT_KV_HEADS)])[None, :].astype(F32)
        gain_mix = norm_mix[layer][None, :]
        gain_ffn = norm_ffn[layer][None, :]

        gla_l, gdn_l, q_l, kv_l, sm_l = _inproj(h_lat, mod144, lat_row, gain_mix, w_p, hg, seg, tables, seq, 256)
        gla_c, gdn_c, q_c, kv_c, sm_c = _inproj(h_ctx, mod144, ctx_row_fn, gain_mix, w_p, hg, seg, None, lctx, 256)

        r3 = lambda t, n: t.reshape(bsz, n, t.shape[-1])
        wg = jnp.zeros((2, LANES, MIX_W), F32)
        for dd in range(2):
            wg = wg.at[dd, GLA_GATE_RANK * dd:GLA_GATE_RANK * (dd + 1), :].set(gla_gate_up[layer, dd].astype(F32))
        bg = gla_gate_bias[layer].reshape(2, 1, MIX_W).astype(F32)
        gla_gain = jnp.tile(gla_out_gain[layer], GLA_HEADS)[None, :].astype(F32)
        o_gla_c, o_gla_l = _gla(r3(gla_c, lctx), r3(gla_l, seq), r3(sm_c, lctx), r3(sm_l, seq),
                                wg, bg, gla_gain, seg, need_ctx)

        cw = jnp.concatenate([gdn_conv[layer].astype(F32), jnp.zeros((8 - SHORT_CONV, 3 * MIX_W), F32)], axis=0)
        alog = _lane_rows(gdn_a_log[layer], 32)
        dtb = _lane_rows(gdn_dt_bias[layer], 32)
        gdn_gain = jnp.tile(gdn_out_gain[layer], GDN_HEADS)[None, :].astype(F32)
        o_gdn_c, o_gdn_l = _gdn2(r3(gdn_c, lctx), r3(gdn_l, seq), r3(sm_c, lctx), r3(sm_l, seq),
                                cw, alog, dtb, sel, selb, gdn_gain, seg, need_ctx)

        o_att_l = _attention(r3(q_l, seq), [r3(kv_l, seq), r3(kv_c, lctx)], 128)
        h_lat = _outproj(o_gla_l.reshape(-1, MIX_W), o_gdn_l.reshape(-1, MIX_W), o_att_l.reshape(-1, 512),
                         h_lat, mod144, lat_row, w_o, seq, 512)
        if need_ctx:
            o_att_c = _attention(r3(q_c, lctx), [r3(kv_c, lctx)], 128)
            h_ctx = _outproj(o_gla_c.reshape(-1, MIX_W), o_gdn_c.reshape(-1, MIX_W), o_att_c.reshape(-1, 512),
                             h_ctx, mod144, ctx_row_fn, w_o, lctx, 256)

        j = layer // 2
        if layer % 2 == 0:
            w_gu = ffn_gate_up[j].astype(BF16)
            w_dn = ffn_down[j].astype(BF16)
            h_lat = _ffn(h_lat, mod144, lat_row, gain_ffn, w_gu, w_dn, seq, 512, 1408)
            if need_ctx:
                h_ctx = _ffn(h_ctx, mod144, ctx_row_fn, gain_ffn, w_gu, w_dn, lctx, 256, 1408)
        else:
            w_gu = moe_gate_up[j].astype(BF16)
            w_dn = moe_down[j].astype(BF16)
            w_r = jnp.concatenate([moe_router[j].astype(F32), jnp.zeros((d, LANES - N_EXPERTS), F32)], axis=1)
            b_r = jnp.concatenate([moe_router_bias[j].astype(F32), jnp.zeros((LANES - N_EXPERTS,), F32)])[None, :]
            h_lat = _moe_routed(h_lat, mod144, lat_row, gain_ffn, w_r, b_r, w_gu, w_dn, seq)
            if need_ctx:
                h_ctx = _moe_routed(h_ctx, mod144, ctx_row_fn, gain_ffn, w_r, b_r, w_gu, w_dn, lctx)
    return h_lat.reshape(bsz, seq, d)
```

```python
import functools

import numpy as np
import jax
import jax.numpy as jnp
from jax import lax
from jax.experimental import pallas as pl
from jax.experimental.pallas import tpu as pltpu

F32 = jnp.float32
BF16 = jnp.bfloat16

GRID_W = 64
HEAD_DIM = 64
CHUNK = 64
SUB = 16
EPS = 1e-6
GLA_HEADS = 4
GLA_GATE_RANK = 16
GLA_TAU = 16.0
GDN_HEADS = 4
SHORT_CONV = 5
ATT_Q_HEADS = 8
ATT_KV_HEADS = 2
ROPE_THETA = 10000.0
N_EXPERTS = 8
MIX_W = GLA_HEADS * HEAD_DIM
LOG2E = 1.4426950408889634
EXP_CLAMP = 80.0

LANES = 128
V7X_VMEM_BYTES = 64 * 1024 * 1024
VMEM_LIMIT = 56 * 1024 * 1024


def _cparams(sem):
    return pltpu.CompilerParams(dimension_semantics=sem, vmem_limit_bytes=VMEM_LIMIT)


def _silu(x):
    return x / (1.0 + jnp.exp(-x))


def _sigmoid(x):
    return 1.0 / (1.0 + jnp.exp(-x))


def _softplus(x):
    return jnp.maximum(x, 0.0) + jnp.log(1.0 + jnp.exp(-jnp.abs(x)))


def _dot(a, b):
    return jnp.dot(a.astype(BF16), b.astype(BF16), preferred_element_type=F32)


def _dot_nt(a, b):
    return lax.dot_general(a.astype(BF16), b.astype(BF16), (((1,), (1,)), ((), ())),
                           preferred_element_type=F32)


def _split(x):
    hi = x.astype(BF16)
    lo = (x - hi.astype(F32)).astype(BF16)
    return hi, lo


def _dot_xhl(x, w):
    hi, lo = _split(x)
    w = w.astype(BF16)
    return (jnp.dot(hi, w, preferred_element_type=F32) + jnp.dot(lo, w, preferred_element_type=F32))


def _dot_whl(w, x):
    hi, lo = _split(x)
    w = w.astype(BF16)
    return (jnp.dot(w, hi, preferred_element_type=F32) + jnp.dot(w, lo, preferred_element_type=F32))


def _dot3(a, b):
    ah, al = _split(a)
    bh, bl = _split(b)
    return (jnp.dot(ah, bh, preferred_element_type=F32) + jnp.dot(ah, bl, preferred_element_type=F32)
            + jnp.dot(al, bh, preferred_element_type=F32))


def _seg_sum64(sq, seg):
    outs = []
    for j in range(sq.shape[1] // LANES):
        outs.append(_dot_xhl(sq[:, LANES * j:LANES * (j + 1)], seg))
    return outs[0] if len(outs) == 1 else jnp.concatenate(outs, axis=1)


def _iota(shape, dim):
    return lax.broadcasted_iota(jnp.int32, shape, dim)


def _mod_kernel(c_ref, w_ref, b_ref, o_ref):
    s = _silu(c_ref[...])
    o_ref[0] = _dot(s, w_ref[0]) + b_ref[0]


def _modulation(cvec, w_mod, b_mod):
    depth, d, n = w_mod.shape
    rows = cvec.shape[0]
    tn = 1536
    return pl.pallas_call(
        _mod_kernel,
        grid=(depth, n // tn),
        in_specs=[pl.BlockSpec((rows, d), lambda l, j: (0, 0)),
                  pl.BlockSpec((1, d, tn), lambda l, j: (l, 0, j)),
                  pl.BlockSpec((1, 1, tn), lambda l, j: (l, 0, j))],
        out_specs=pl.BlockSpec((1, rows, tn), lambda l, j: (l, 0, j)),
        out_shape=jax.ShapeDtypeStruct((depth, rows, n), F32),
        compiler_params=_cparams(("arbitrary", "arbitrary")),
        name="modulation",
    )(cvec, w_mod, b_mod.reshape(depth, 1, n))


def _swap16(n, lane):
    fwd = pltpu.roll(n, LANES - 16, 1)
    bwd = pltpu.roll(n, 16, 1)
    return jnp.where((lane % 32) < 16, fwd, bwd)


def _inproj_kernel(*refs, rope):
    if rope:
        (h_ref, sh_ref, sc_ref, gain_ref, w_ref, hg_ref, seg_ref, cos_ref, sin_ref,
         gla_ref, gdn_ref, q_ref, kv_ref, small_ref) = refs
    else:
        (h_ref, sh_ref, sc_ref, gain_ref, w_ref, hg_ref, seg_ref,
         gla_ref, gdn_ref, q_ref, kv_ref, small_ref) = refs
    x = h_ref[...]
    ms = jnp.mean(x * x, axis=-1, keepdims=True)
    a = x * lax.rsqrt(ms + EPS) * (gain_ref[...] * (1.0 + sc_ref[0])) + sh_ref[0]
    p = jnp.dot(a.astype(BF16), w_ref[...], preferred_element_type=F32)
    gla_ref[...] = p[:, 0:1024]
    gdn_ref[...] = p[:, 1024:2048]
    small_ref[...] = p[:, 2816:2944]
    seg = seg_ref[...]
    lane = _iota((1, LANES), 1)
    outs = []
    for j in range(5):
        t = p[:, 2048 + LANES * j:2048 + LANES * (j + 1)]
        ss = _dot_xhl(t * t, seg)
        n = t * lax.rsqrt(ss * (1.0 / HEAD_DIM) + EPS) * hg_ref[:, LANES * j:LANES * (j + 1)]
        if rope:
            n = n * cos_ref[...] + _swap16(n, lane) * sin_ref[...]
        outs.append(n)
    q_ref[...] = jnp.concatenate(outs[:4], axis=1).astype(BF16)
    kv_ref[...] = jnp.concatenate([outs[4], p[:, 2688:2816]], axis=1).astype(BF16)


def _inproj(h2d, mod144, mod_row_fn, gain, w_p, hg, seg, tables, seq_len, tm):
    rows, d = h2d.shape
    n_all = w_p.shape[1]
    rope = tables is not None
    tiles_per_seq = seq_len // tm
    in_specs = [pl.BlockSpec((tm, d), lambda i: (i, 0)),
                pl.BlockSpec((1, 1, d), lambda i: (mod_row_fn(i // tiles_per_seq) * 6 + 0, 0, 0)),
                pl.BlockSpec((1, 1, d), lambda i: (mod_row_fn(i // tiles_per_seq) * 6 + 1, 0, 0)),
                pl.BlockSpec((1, d), lambda i: (0, 0)),
                pl.BlockSpec((d, n_all), lambda i: (0, 0)),
                pl.BlockSpec((1, 640), lambda i: (0, 0)),
                pl.BlockSpec((LANES, LANES), lambda i: (0, 0))]
    args = [h2d, mod144, mod144, gain, w_p, hg, seg]
    if rope:
        in_specs += [pl.BlockSpec((tm, LANES), lambda i: (i % tiles_per_seq, 0)),
                     pl.BlockSpec((tm, LANES), lambda i: (i % tiles_per_seq, 0))]
        args += list(tables)
    out_shape = (jax.ShapeDtypeStruct((rows, 1024), F32), jax.ShapeDtypeStruct((rows, 1024), F32),
                 jax.ShapeDtypeStruct((rows, 512), BF16), jax.ShapeDtypeStruct((rows, 256), BF16),
                 jax.ShapeDtypeStruct((rows, LANES), F32))
    out_specs = (pl.BlockSpec((tm, 1024), lambda i: (i, 0)), pl.BlockSpec((tm, 1024), lambda i: (i, 0)),
                 pl.BlockSpec((tm, 512), lambda i: (i, 0)), pl.BlockSpec((tm, 256), lambda i: (i, 0)),
                 pl.BlockSpec((tm, LANES), lambda i: (i, 0)))
    return pl.pallas_call(
        functools.partial(_inproj_kernel, rope=rope),
        grid=(rows // tm,), in_specs=in_specs, out_specs=out_specs, out_shape=out_shape,
        compiler_params=_cparams(("parallel",)),
        name="inproj_rope" if rope else "inproj",
    )(*args)


ATT_CHUNKS_PER_DOT = 1


def _attn_kernel(*refs, nkv, tq):
    q_ref = refs[0]
    kv_refs = refs[1:1 + nkv]
    o_ref = refs[1 + nkv]
    q = q_ref[0]
    lane = _iota((1, LANES), 1)
    mlo = (lane < HEAD_DIM).astype(BF16)
    mhi = (lane >= HEAD_DIM).astype(BF16)
    kvs = [r[0] for r in kv_refs]
    outs = []
    for jj in range(0, 4, ATT_CHUNKS_PER_DOT):
        pieces = []
        for j in range(jj, jj + ATT_CHUNKS_PER_DOT):
            qc = q[:, LANES * j:LANES * (j + 1)]
            pieces += [qc * mlo, qc * mhi]
        q_all = jnp.concatenate(pieces, axis=0)
        ss = [lax.dot_general(q_all, kv[:, 0:LANES], (((1,), (1,)), ((), ())), preferred_element_type=F32)
              for kv in kvs]
        m = functools.reduce(jnp.maximum, [jnp.max(s, axis=-1, keepdims=True) for s in ss])
        ps = [jnp.exp2(s - m) for s in ss]
        l = functools.reduce(lambda a, b: a + b, [jnp.sum(p, axis=-1, keepdims=True) for p in ps])
        o = functools.reduce(lambda a, b: a + b,
                             [jnp.dot(p.astype(BF16), kv[:, LANES:2 * LANES], preferred_element_type=F32)
                              for p, kv in zip(ps, kvs)])
        o = o / l
        for j in range(ATT_CHUNKS_PER_DOT):
            outs.append(jnp.where(lane < HEAD_DIM, o[2 * j * tq:(2 * j + 1) * tq],
                                  o[(2 * j + 1) * tq:(2 * j + 2) * tq]))
    o_ref[0] = jnp.concatenate(outs, axis=1).astype(BF16)


def _attention(q, kvs, tq):
    b, lq, _ = q.shape
    in_specs = [pl.BlockSpec((1, tq, 512), lambda i, j: (i, j, 0))]
    for kv in kvs:
        in_specs.append(pl.BlockSpec((1, kv.shape[1], 256), lambda i, j: (i, 0, 0)))
    return pl.pallas_call(
        functools.partial(_attn_kernel, nkv=len(kvs), tq=tq),
        grid=(b, lq // tq), in_specs=in_specs,
        out_specs=pl.BlockSpec((1, tq, 512), lambda i, j: (i, j, 0)),
        out_shape=jax.ShapeDtypeStruct((b, lq, 512), BF16),
        compiler_params=_cparams(("parallel", "arbitrary")),
        name="attention",
    )(q, *kvs)


def _head_masks():
    lane = _iota((1, MIX_W), 1)
    return [(lane // HEAD_DIM == h).astype(F32) for h in range(GLA_HEADS)]


def _blockdiag_mask():
    r = _iota((MIX_W, MIX_W), 0) // HEAD_DIM
    c = _iota((MIX_W, MIX_W), 1) // HEAD_DIM
    return r == c


def _finish_rows(o, gate, gain, seg):
    ss = _seg_sum64(o * o, seg)
    return o * lax.rsqrt(ss * (1.0 / HEAD_DIM) + EPS) * gain * _silu(gate)


def _scan_segments(xc_ref, xl_ref, step):
    for seg_i, x_ref in enumerate((xc_ref, xl_ref)):
        nch = x_ref.shape[1] // CHUNK

        def body(i, carry, seg_i=seg_i, nch=nch):
            step(seg_i, i, 0)
            step(seg_i, nch - 1 - i, 1)
            return carry

        lax.fori_loop(0, nch, body, 0)


def _gla_kernel(*refs, need_ctx):
    if need_ctx:
        (xc_ref, xl_ref, sc_ref, sl_ref, wg_ref, bg_ref, gain_ref, seg_ref,
         oc_ref, ol_ref, of_ref, ob_ref, qh_ref, kh_ref, dg_ref, st_ref) = refs
    else:
        (xc_ref, xl_ref, sc_ref, sl_ref, wg_ref, bg_ref, gain_ref, seg_ref,
         ol_ref, of_ref, ob_ref, qh_ref, kh_ref, dg_ref, st_ref) = refs
        oc_ref = None
    lc = xc_ref.shape[1]
    x_refs = (xc_ref, xl_ref)
    s_refs = (sc_ref, sl_ref)
    row_off = (0, lc)
    o_refs = (of_ref, ob_ref)
    hmask = _head_masks()
    bd = _blockdiag_mask()
    ti = _iota((CHUNK, CHUNK), 0)
    si = _iota((CHUNK, CHUNK), 1)
    tri = [(si <= ti).astype(BF16), (si >= ti).astype(BF16)]
    rr = _iota((4 * CHUNK, CHUNK), 0)
    cc = _iota((4 * CHUNK, CHUNK), 1)
    t_of_row = (rr // (GLA_HEADS * SUB)) * SUB + rr % SUB
    causal = [cc <= t_of_row, cc >= t_of_row]
    nblk = CHUNK // SUB

    def prep(seg_i, c2):
        x_ref, s_ref = x_refs[seg_i], s_refs[seg_i]
        fs = []
        for kk in range(PREP_CHUNKS):
            r0 = pl.multiple_of((c2 * PREP_CHUNKS + kk) * CHUNK, CHUNK)
            fs.append(dict(row=row_off[seg_i] + r0,
                           q=x_ref[0, pl.ds(r0, CHUNK), 0:MIX_W] * (HEAD_DIM ** -0.5),
                           k=x_ref[0, pl.ds(r0, CHUNK), MIX_W:2 * MIX_W],
                           v=x_ref[0, pl.ds(r0, CHUNK), 2 * MIX_W:3 * MIX_W],
                           sm=s_ref[0, pl.ds(r0, CHUNK), :]))
        chains = [(f, d) for f in fs for d in range(2)]
        xg = [_dot3(f["sm"], wg_ref[d]) + bg_ref[d] for f, d in chains]
        g = [(jnp.minimum(x_, 0.0) - jnp.log(1.0 + jnp.exp(-jnp.abs(x_)))) * (1.0 / GLA_TAU) for x_ in xg]
        b = [_dot_whl(tri[d], g_) for (_, d), g_ in zip(chains, g)]
        pieces = []
        for i in range(nblk):
            row_pieces = []
            for (f, d), g_, b_ in zip(chains, g, b):
                e = SUB * i if d == 0 else SUB * i + SUB - 1
                bref = b_[e:e + 1, :] - g_[e:e + 1, :]
                kt = f["k"] * jnp.exp(jnp.minimum(bref - b_, EXP_CLAMP))
                qt = f["q"][SUB * i:SUB * (i + 1), :] * jnp.exp(b_[SUB * i:SUB * (i + 1), :] - bref)
                qs = jnp.concatenate([qt * hmask[h] for h in range(GLA_HEADS)], axis=0)
                row_pieces.append(_dot_nt(qs, kt))
            pieces.append(row_pieces)
        scores = [jnp.where(causal[d], jnp.concatenate([pieces[i][j] for i in range(nblk)], axis=0), 0.0)
                  for j, (_, d) in enumerate(chains)]
        r = [_dot(s_, f["v"]) for (f, _), s_ in zip(chains, scores)]
        for j, (f, d) in enumerate(chains):
            intra = []
            for i in range(nblk):
                acc = None
                for h in range(GLA_HEADS):
                    lo = (i * GLA_HEADS + h) * SUB
                    term = r[j][lo:lo + SUB, :] * hmask[h]
                    acc = term if acc is None else acc + term
                intra.append(acc)
            row = f["row"]
            e = CHUNK - 1 if d == 0 else 0
            b_end = b[j][e:e + 1, :]
            o_refs[d][pl.ds(row, CHUNK), :] = jnp.concatenate(intra, axis=0)
            qh_ref[d, pl.ds(row, CHUNK), :] = (f["q"] * jnp.exp(b[j])).astype(BF16)
            kh_ref[d, pl.ds(row, CHUNK), :] = (f["k"] * jnp.exp(b_end - b[j])).astype(BF16)
            dg_ref[d, pl.ds(pl.multiple_of(row // 8, 8), 8), :] = jnp.broadcast_to(jnp.exp(b_end), (8, MIX_W))

    for seg_i, x_ref in enumerate(x_refs):
        def prep_body(c, carry, seg_i=seg_i):
            prep(seg_i, c)
            return carry

        lax.fori_loop(0, x_ref.shape[1] // (CHUNK * PREP_CHUNKS), prep_body, 0)

    st_ref[...] = jnp.zeros(st_ref.shape, F32)

    def scan_body(seg_i, i, nch):
        x_ref = x_refs[seg_i]
        r0s = [pl.multiple_of((i if d == 0 else nch - 1 - i) * CHUNK, CHUNK) for d in range(2)]
        rows = [row_off[seg_i] + r0 for r0 in r0s]
        st = [st_ref[d] for d in range(2)]
        inter = [_dot_nt(qh_ref[d, pl.ds(rows[d], CHUNK), :], st[d]) for d in range(2)]
        upd = [jnp.dot(x_ref[0, pl.ds(r0s[d], CHUNK), 2 * MIX_W:3 * MIX_W].T.astype(BF16),
                       kh_ref[d, pl.ds(rows[d], CHUNK), :], preferred_element_type=F32) for d in range(2)]
        for d in range(2):
            o_refs[d][pl.ds(rows[d], CHUNK), :] += inter[d]
            dgr = dg_ref[d, pl.ds(pl.multiple_of(rows[d] // 8, 8), 8), :][0:1, :]
            st_ref[d] = st[d] * dgr + jnp.where(bd, upd[d], 0.0)

    for seg_i, x_ref in enumerate(x_refs):
        nch = x_ref.shape[1] // CHUNK

        def scan_iter(i, carry, seg_i=seg_i, nch=nch):
            scan_body(seg_i, i, nch)
            return carry

        lax.fori_loop(0, nch, scan_iter, 0)

    gain = gain_ref[...]
    seg = seg_ref[...]
    blk = 256
    outs = ((oc_ref, xc_ref, 0), (ol_ref, xl_ref, lc))
    for o_ref, x_ref, off in outs:
        if o_ref is None:
            continue

        def fin(i, carry, o_ref=o_ref, x_ref=x_ref, off=off):
            r0 = pl.multiple_of(i * blk, blk)
            o = of_ref[pl.ds(off + r0, blk), :] + ob_ref[pl.ds(off + r0, blk), :]
            gate = x_ref[0, pl.ds(r0, blk), 3 * MIX_W:4 * MIX_W]
            o_ref[0, pl.ds(r0, blk), :] = _finish_rows(o, gate, gain, seg).astype(BF16)
            return carry

        lax.fori_loop(0, x_ref.shape[1] // blk, fin, 0)


def _gla(x_ctx, x_lat, s_ctx, s_lat, wg, bg, gain, seg, need_ctx):
    b, lc, _ = x_ctx.shape
    ll = x_lat.shape[1]
    in_specs = [pl.BlockSpec((1, lc, 1024), lambda i: (i, 0, 0)),
                pl.BlockSpec((1, ll, 1024), lambda i: (i, 0, 0)),
                pl.BlockSpec((1, lc, LANES), lambda i: (i, 0, 0)),
                pl.BlockSpec((1, ll, LANES), lambda i: (i, 0, 0)),
                pl.BlockSpec((2, LANES, MIX_W), lambda i: (0, 0, 0)),
                pl.BlockSpec((2, 1, MIX_W), lambda i: (0, 0, 0)),
                pl.BlockSpec((1, MIX_W), lambda i: (0, 0)),
                pl.BlockSpec((LANES, LANES), lambda i: (0, 0))]
    out_shape = [jax.ShapeDtypeStruct((b, ll, MIX_W), BF16)]
    out_specs = [pl.BlockSpec((1, ll, MIX_W), lambda i: (i, 0, 0))]
    if need_ctx:
        out_shape.insert(0, jax.ShapeDtypeStruct((b, lc, MIX_W), BF16))
        out_specs.insert(0, pl.BlockSpec((1, lc, MIX_W), lambda i: (i, 0, 0)))
    res = pl.pallas_call(
        functools.partial(_gla_kernel, need_ctx=need_ctx),
        grid=(b,), in_specs=in_specs, out_specs=out_specs, out_shape=out_shape,
        scratch_shapes=[pltpu.VMEM((lc + ll, MIX_W), F32), pltpu.VMEM((lc + ll, MIX_W), F32),
                        pltpu.VMEM((2, lc + ll, MIX_W), BF16), pltpu.VMEM((2, lc + ll, MIX_W), BF16),
                        pltpu.VMEM((2, (lc + ll) // 8, MIX_W), F32),
                        pltpu.VMEM((2, MIX_W, MIX_W), F32)],
        compiler_params=_cparams(("parallel",)),
        name="gla_scan",
    )(x_ctx, x_lat, s_ctx, s_lat, wg, bg, gain, seg)
    return (res[0], res[1]) if need_ctx else (None, res[0])


def _gdn_kernel(*refs, need_ctx):
    if need_ctx:
        (xc_ref, xl_ref, sc_ref, sl_ref, cw_ref, alog_ref, dtb_ref, sel_ref, selb_ref, gain_ref, seg_ref,
         oc_ref, ol_ref, qkv_ref, of_ref, ob_ref, st_ref) = refs
    else:
        (xc_ref, xl_ref, sc_ref, sl_ref, cw_ref, alog_ref, dtb_ref, sel_ref, selb_ref, gain_ref, seg_ref,
         ol_ref, qkv_ref, of_ref, ob_ref, st_ref) = refs
        oc_ref = None
    lc = xc_ref.shape[1]
    x_refs = (xc_ref, xl_ref)
    s_refs = (sc_ref, sl_ref)
    row_off = (0, lc)
    seg = seg_ref[...]
    hmask = _head_masks()
    bd = _blockdiag_mask()
    ti = _iota((CHUNK, CHUNK), 0)
    si = _iota((CHUNK, CHUNK), 1)
    tri = [(si <= ti).astype(BF16), (si >= ti).astype(BF16)]
    ones = jnp.ones((CHUNK, CHUNK), BF16)
    tp = _iota((CHUNK, MIX_W), 0)
    sp = _iota((CHUNK, MIX_W), 1) % CHUNK
    le = sp <= tp
    ge = sp >= tp
    m_tri = [le, ge]
    m_strict = [sp < tp, sp > tp]
    m_sum = [ge.astype(F32), le.astype(F32)]
    blk16 = (tp // SUB) == (sp // SUB)
    eye = (tp == sp).astype(F32)

    cw = cw_ref[...]
    for seg_i, x_ref in enumerate(x_refs):
        ln = x_ref.shape[1]
        nch = ln // CHUNK

        def conv(c, carry, x_ref=x_ref, ln=ln, nch=nch, off=row_off[seg_i]):
            r0 = pl.multiple_of(c * CHUNK, CHUNK)
            center = x_ref[0, pl.ds(r0, CHUNK), 0:3 * MIX_W]
            p0 = pl.multiple_of(jnp.maximum(r0 - 8, 0), 8)
            n0 = pl.multiple_of(jnp.minimum(r0 + CHUNK, ln - 8), 8)
            prev = x_ref[0, pl.ds(p0, 8), 0:3 * MIX_W] * jnp.where(c > 0, 1.0, 0.0)
            nxt = x_ref[0, pl.ds(n0, 8), 0:3 * MIX_W] * jnp.where(c < nch - 1, 1.0, 0.0)
            ext = jnp.concatenate([prev, center, nxt], axis=0)
            pad = SHORT_CONV // 2
            acc = None
            for j in range(SHORT_CONV):
                term = ext[8 - pad + j:8 - pad + j + CHUNK, :] * cw[j:j + 1, :]
                acc = term if acc is None else acc + term
            y = _silu(acc)
            qk = y[:, 0:2 * MIX_W]
            ss = _seg_sum64(qk * qk, seg)
            qk = qk * lax.rsqrt(ss + EPS)
            qkv_ref[pl.ds(off + r0, CHUNK), 0:MIX_W] = qk[:, 0:MIX_W] * (HEAD_DIM ** -0.5)
            qkv_ref[pl.ds(off + r0, CHUNK), MIX_W:2 * MIX_W] = qk[:, MIX_W:2 * MIX_W]
            qkv_ref[pl.ds(off + r0, CHUNK), 2 * MIX_W:3 * MIX_W] = y[:, 2 * MIX_W:3 * MIX_W]
            return carry

        lax.fori_loop(0, nch, conv, 0)

    st_ref[...] = jnp.zeros(st_ref.shape, F32)

    def pk(y):
        return jnp.where(bd, jnp.concatenate([y] * GDN_HEADS, axis=0), 0.0)

    def mm(x, y):
        return _dot(x, pk(y))

    def step(seg_i, c, d):
        s_ref = s_refs[seg_i]
        r0 = pl.multiple_of(c * CHUNK, CHUNK)
        row = row_off[seg_i] + r0
        qn = qkv_ref[pl.ds(row, CHUNK), 0:MIX_W]
        kn = qkv_ref[pl.ds(row, CHUNK), MIX_W:2 * MIX_W]
        v = qkv_ref[pl.ds(row, CHUNK), 2 * MIX_W:3 * MIX_W]
        sm = s_ref[0, pl.ds(r0, CHUNK), :]
        g = -jnp.exp(alog_ref[d]) * _softplus(sm + dtb_ref[d])
        beta = _sigmoid(sm)
        gexp = _dot_xhl(g, sel_ref[d])
        bexp = _dot_xhl(beta, selb_ref[d])
        gam_t = _dot_whl(tri[d], gexp)
        gam_s = _dot_whl(ones, gexp * m_sum[d])
        dec = jnp.where(m_tri[d], jnp.exp(jnp.minimum(gam_t - gam_s, 0.0)), 0.0)
        kstack = jnp.concatenate([kn * hmask[h] for h in range(GDN_HEADS)], axis=0)
        kk = _dot_nt(kn, kstack)
        qk = _dot_nt(qn, kstack)
        a = jnp.where(m_strict[d], bexp * dec * kk, 0.0)
        dg = jnp.where(blk16, a, 0.0)
        lo = a - dg
        d2 = mm(dg, dg)
        d4 = mm(d2, d2)
        d8 = mm(d4, d4)
        t_inv = mm(mm(mm(eye - dg, eye + d2), eye + d4), eye + d8)
        m = mm(t_inv, lo)
        egam = jnp.exp(gam_t)
        sols = []
        for rhs in (bexp * v, bexp * egam * kn):
            y = mm(t_inv, rhs)
            z = y + mm(m, mm(m, y))
            sols.append(z - mm(m, z))
        sol_v, sol_k = sols
        st = st_ref[d]
        u = sol_v - _dot(sol_k, st)
        p = jnp.where(m_tri[d], qk * dec, 0.0)
        o = egam * _dot(qn, st) + _dot(p, pk(u))
        dst = of_ref if d == 0 else ob_ref
        dst[pl.ds(row, CHUNK), :] = o
        e = CHUNK - 1 if d == 0 else 0
        g_end = gam_t[e:e + 1, :]
        kh = kn * jnp.exp(g_end - gam_t)
        st_ref[d] = st * jnp.exp(g_end) + jnp.where(bd, _dot(kh.T, u), 0.0)

    _scan_segments(xc_ref, xl_ref, step)

    gain = gain_ref[...]
    blk = 256
    outs = ((oc_ref, xc_ref, 0), (ol_ref, xl_ref, lc))
    for o_ref, x_ref, off in outs:
        if o_ref is None:
            continue

        def fin(i, carry, o_ref=o_ref, x_ref=x_ref, off=off):
            r0 = pl.multiple_of(i * blk, blk)
            o = of_ref[pl.ds(off + r0, blk), :] + ob_ref[pl.ds(off + r0, blk), :]
            gate = x_ref[0, pl.ds(r0, blk), 3 * MIX_W:4 * MIX_W]
            o_ref[0, pl.ds(r0, blk), :] = _finish_rows(o, gate, gain, seg).astype(BF16)
            return carry

        lax.fori_loop(0, x_ref.shape[1] // blk, fin, 0)


def _gdn(x_ctx, x_lat, s_ctx, s_lat, cw, alog, dtb, sel, selb, gain, seg, need_ctx):
    b, lc, _ = x_ctx.shape
    ll = x_lat.shape[1]
    in_specs = [pl.BlockSpec((1, lc, 1024), lambda i: (i, 0, 0)),
                pl.BlockSpec((1, ll, 1024), lambda i: (i, 0, 0)),
                pl.BlockSpec((1, lc, LANES), lambda i: (i, 0, 0)),
                pl.BlockSpec((1, ll, LANES), lambda i: (i, 0, 0)),
                pl.BlockSpec((8, 3 * MIX_W), lambda i: (0, 0)),
                pl.BlockSpec((2, 1, LANES), lambda i: (0, 0, 0)),
                pl.BlockSpec((2, 1, LANES), lambda i: (0, 0, 0)),
                pl.BlockSpec((2, LANES, MIX_W), lambda i: (0, 0, 0)),
                pl.BlockSpec((2, LANES, MIX_W), lambda i: (0, 0, 0)),
                pl.BlockSpec((1, MIX_W), lambda i: (0, 0)),
                pl.BlockSpec((LANES, LANES), lambda i: (0, 0))]
    out_shape = [jax.ShapeDtypeStruct((b, ll, MIX_W), BF16)]
    out_specs = [pl.BlockSpec((1, ll, MIX_W), lambda i: (i, 0, 0))]
    if need_ctx:
        out_shape.insert(0, jax.ShapeDtypeStruct((b, lc, MIX_W), BF16))
        out_specs.insert(0, pl.BlockSpec((1, lc, MIX_W), lambda i: (i, 0, 0)))
    res = pl.pallas_call(
        functools.partial(_gdn_kernel, need_ctx=need_ctx),
        grid=(b,), in_specs=in_specs, out_specs=out_specs, out_shape=out_shape,
        scratch_shapes=[pltpu.VMEM((lc + ll, 3 * MIX_W), F32),
                        pltpu.VMEM((lc + ll, MIX_W), F32), pltpu.VMEM((lc + ll, MIX_W), F32),
                        pltpu.VMEM((2, MIX_W, MIX_W), F32)],
        compiler_params=_cparams(("parallel",)),
        name="gdn_scan",
    )(x_ctx, x_lat, s_ctx, s_lat, cw, alog, dtb, sel, selb, gain, seg)
    return (res[0], res[1]) if need_ctx else (None, res[0])


PAIR_W = 2 * HEAD_DIM
PREP_CHUNKS = 4


def _gdn2_kernel(*refs, need_ctx):
    if need_ctx:
        (xc_ref, xl_ref, sc_ref, sl_ref, cw_ref, alog_ref, dtb_ref, sel_ref, selb_ref, gain_ref, seg_ref,
         oc_ref, ol_ref, sk_ref, p_ref, qe_ref, sv_ref, kh_ref, dg_ref, of_ref, ob_ref, st_ref) = refs
    else:
        (xc_ref, xl_ref, sc_ref, sl_ref, cw_ref, alog_ref, dtb_ref, sel_ref, selb_ref, gain_ref, seg_ref,
         ol_ref, sk_ref, p_ref, qe_ref, sv_ref, kh_ref, dg_ref, of_ref, ob_ref, st_ref) = refs
        oc_ref = None
    lc = xc_ref.shape[1]
    x_refs = (xc_ref, xl_ref)
    s_refs = (sc_ref, sl_ref)
    row_off = (0, lc)
    npair = MIX_W // PAIR_W
    seg = seg_ref[...]
    ti = _iota((CHUNK, CHUNK), 0)
    si = _iota((CHUNK, CHUNK), 1)
    tri = [(si <= ti).astype(BF16), (si >= ti).astype(BF16)]
    ones = jnp.ones((CHUNK, CHUNK), BF16)
    tp = _iota((CHUNK, PAIR_W), 0)
    sp = _iota((CHUNK, PAIR_W), 1) % CHUNK
    le = sp <= tp
    ge = sp >= tp
    m_tri = [le, ge]
    m_strict = [sp < tp, sp > tp]
    m_sum = [ge.astype(F32), le.astype(F32)]
    blk16 = (tp // SUB) == (sp // SUB)
    eye = (tp == sp).astype(F32)
    bd2 = (_iota((PAIR_W, PAIR_W), 0) // HEAD_DIM) == (_iota((PAIR_W, PAIR_W), 1) // HEAD_DIM)
    lane_p = _iota((1, PAIR_W), 1)
    hm2 = [(lane_p // HEAD_DIM == h).astype(F32) for h in range(2)]
    cw = cw_ref[...]

    def pk2(y):
        return jnp.where(bd2, jnp.concatenate([y, y], axis=0), 0.0).astype(BF16)

    def mm(x, y):
        return jnp.dot(x.astype(BF16), pk2(y), preferred_element_type=F32)

    def front(seg_i, c):
        x_ref, s_ref = x_refs[seg_i], s_refs[seg_i]
        ln = x_ref.shape[1]
        nch = ln // CHUNK
        r0 = pl.multiple_of(c * CHUNK, CHUNK)
        center = x_ref[0, pl.ds(r0, CHUNK), 0:3 * MIX_W]
        p0 = pl.multiple_of(jnp.maximum(r0 - 8, 0), 8)
        n0 = pl.multiple_of(jnp.minimum(r0 + CHUNK, ln - 8), 8)
        prev = x_ref[0, pl.ds(p0, 8), 0:3 * MIX_W] * jnp.where(c > 0, 1.0, 0.0)
        nxt = x_ref[0, pl.ds(n0, 8), 0:3 * MIX_W] * jnp.where(c < nch - 1, 1.0, 0.0)
        ext = jnp.concatenate([prev, center, nxt], axis=0)
        pad = SHORT_CONV // 2
        acc = None
        for j in range(SHORT_CONV):
            term = ext[8 - pad + j:8 - pad + j + CHUNK, :] * cw[j:j + 1, :]
            acc = term if acc is None else acc + term
        y = _silu(acc)
        qk = y[:, 0:2 * MIX_W]
        qk = qk * lax.rsqrt(_seg_sum64(qk * qk, seg) + EPS)
        f = dict(row=row_off[seg_i] + r0, qn=qk[:, 0:MIX_W] * (HEAD_DIM ** -0.5), kn=qk[:, MIX_W:2 * MIX_W],
                 v=y[:, 2 * MIX_W:3 * MIX_W])
        sm = s_ref[0, pl.ds(r0, CHUNK), :]
        beta = _sigmoid(sm)
        f["kk"], f["qk"] = [], []
        for p in range(npair):
            ls = slice(PAIR_W * p, PAIR_W * (p + 1))
            kstack = jnp.concatenate([f["kn"][:, ls] * hm2[0], f["kn"][:, ls] * hm2[1]], axis=0)
            f["kk"].append(_dot_nt(f["kn"][:, ls], kstack))
            f["qk"].append(_dot_nt(f["qn"][:, ls], kstack))
        f["gexp"] = [_dot_xhl(-jnp.exp(alog_ref[d]) * _softplus(sm + dtb_ref[d]), sel_ref[d]) for d in range(2)]
        f["bexp"] = [_dot_xhl(beta, selb_ref[d]) for d in range(2)]
        f["gam"] = [_dot_whl(tri[d], f["gexp"][d]) for d in range(2)]
        return f

    def prep(seg_i, c2):
        fs = [front(seg_i, c2 * PREP_CHUNKS + k) for k in range(PREP_CHUNKS)]
        chains = [(f, d, p) for f in fs for d in range(2) for p in range(npair)]
        lss = [slice(PAIR_W * p, PAIR_W * (p + 1)) for _, _, p in chains]
        gam_t = [f["gam"][d][:, ls] for (f, d, _), ls in zip(chains, lss)]
        gam_s = [_dot_whl(ones, f["gexp"][d][:, ls] * m_sum[d]) for (f, d, _), ls in zip(chains, lss)]
        bx = [f["bexp"][d][:, ls] for (f, d, _), ls in zip(chains, lss)]
        dec = [jnp.where(m_tri[d], jnp.exp(jnp.minimum(gt - gs, 0.0)), 0.0)
               for (_, d, _), gt, gs in zip(chains, gam_t, gam_s)]
        a = [jnp.where(m_strict[d], b_ * dc * f["kk"][p], 0.0) for (f, d, p), b_, dc in zip(chains, bx, dec)]
        dgn = [jnp.where(blk16, a_, 0.0) for a_ in a]
        lo = [a_ - g_ for a_, g_ in zip(a, dgn)]
        d2 = [mm(g_, g_) for g_ in dgn]
        t1 = [mm(eye - g_, eye + s_) for g_, s_ in zip(dgn, d2)]
        d4 = [mm(s_, s_) for s_ in d2]
        t2 = [mm(t_, eye + s_) for t_, s_ in zip(t1, d4)]
        d8 = [mm(s_, s_) for s_ in d4]
        t_inv = [mm(t_, eye + s_) for t_, s_ in zip(t2, d8)]
        m = [mm(t_, l_) for t_, l_ in zip(t_inv, lo)]
        m2 = [mm(m_, m_) for m_ in m]
        w1 = [mm(eye - m_, eye + s_) for m_, s_ in zip(m, m2)]
        w = [mm(w_, t_) for w_, t_ in zip(w1, t_inv)]
        egam = [jnp.exp(gt) for gt in gam_t]
        solv = [mm(w_, b_ * f["v"][:, ls]) for (f, _, _), w_, b_, ls in zip(chains, w, bx, lss)]
        solk = [mm(w_, b_ * eg * f["kn"][:, ls]) for (f, _, _), w_, b_, eg, ls in zip(chains, w, bx, egam, lss)]
        for i, (f, d, p) in enumerate(chains):
            ls, row = lss[i], f["row"]
            e = CHUNK - 1 if d == 0 else 0
            g_end = gam_t[i][e:e + 1, :]
            sv_ref[d, pl.ds(row, CHUNK), ls] = solv[i]
            sk_ref[d, pl.ds(row, CHUNK), ls] = solk[i].astype(BF16)
            p_ref[d, pl.ds(row, CHUNK), ls] = jnp.where(m_tri[d], f["qk"][p] * dec[i], 0.0).astype(BF16)
            qe_ref[d, pl.ds(row, CHUNK), ls] = (egam[i] * f["qn"][:, ls]).astype(BF16)
            kh_ref[d, pl.ds(row, CHUNK), ls] = f["kn"][:, ls] * jnp.exp(g_end - gam_t[i])
            dg_ref[d, pl.ds(pl.multiple_of(row // 8, 8), 8), ls] = jnp.broadcast_to(jnp.exp(g_end), (8, PAIR_W))

    for seg_i, x_ref in enumerate(x_refs):
        def prep_body(c, carry, seg_i=seg_i):
            prep(seg_i, c)
            return carry

        lax.fori_loop(0, x_ref.shape[1] // (CHUNK * PREP_CHUNKS), prep_body, 0)

    st_ref[...] = jnp.zeros(st_ref.shape, F32)

    def scan_body(seg_i, i, nch):
        chains = [(d, p) for d in range(2) for p in range(npair)]
        rows = [row_off[seg_i] + pl.multiple_of((i if d == 0 else nch - 1 - i) * CHUNK, CHUNK) for d, _ in chains]
        lss = [slice(PAIR_W * p, PAIR_W * (p + 1)) for _, p in chains]
        st = [st_ref[d, p] for d, p in chains]
        stb = [s_.astype(BF16) for s_ in st]
        u = [sv_ref[d, pl.ds(r, CHUNK), ls] - jnp.dot(sk_ref[d, pl.ds(r, CHUNK), ls], sb, preferred_element_type=F32)
             for (d, _), r, ls, sb in zip(chains, rows, lss, stb)]
        oq = [jnp.dot(qe_ref[d, pl.ds(r, CHUNK), ls], sb, preferred_element_type=F32)
              for (d, _), r, ls, sb in zip(chains, rows, lss, stb)]
        ou = [jnp.dot(p_ref[d, pl.ds(r, CHUNK), ls], pk2(u_), preferred_element_type=F32)
              for (d, _), r, ls, u_ in zip(chains, rows, lss, u)]
        ku = [_dot(kh_ref[d, pl.ds(r, CHUNK), ls].T, u_) for (d, _), r, ls, u_ in zip(chains, rows, lss, u)]
        for j, (d, p) in enumerate(chains):
            dst = of_ref if d == 0 else ob_ref
            dst[pl.ds(rows[j], CHUNK), lss[j]] = oq[j] + ou[j]
            dgr = dg_ref[d, pl.ds(pl.multiple_of(rows[j] // 8, 8), 8), lss[j]][0:1, :]
            st_ref[d, p] = st[j] * dgr + jnp.where(bd2, ku[j], 0.0)

    for seg_i, x_ref in enumerate(x_refs):
        nch = x_ref.shape[1] // CHUNK

        def scan_iter(i, carry, seg_i=seg_i, nch=nch):
            scan_body(seg_i, i, nch)
            return carry

        lax.fori_loop(0, nch, scan_iter, 0)

    gain = gain_ref[...]
    blk = 256
    outs = ((oc_ref, xc_ref, 0), (ol_ref, xl_ref, lc))
    for o_ref, x_ref, off in outs:
        if o_ref is None:
            continue

        def fin(i, carry, o_ref=o_ref, x_ref=x_ref, off=off):
            r0 = pl.multiple_of(i * blk, blk)
            o = of_ref[pl.ds(off + r0, blk), :] + ob_ref[pl.ds(off + r0, blk), :]
            gate = x_ref[0, pl.ds(r0, blk), 3 * MIX_W:4 * MIX_W]
            o_ref[0, pl.ds(r0, blk), :] = _finish_rows(o, gate, gain, seg).astype(BF16)
            return carry

        lax.fori_loop(0, x_ref.shape[1] // blk, fin, 0)


def _gdn2(x_ctx, x_lat, s_ctx, s_lat, cw, alog, dtb, sel, selb, gain, seg, need_ctx):
    b, lc, _ = x_ctx.shape
    ll = x_lat.shape[1]
    lt = lc + ll
    in_specs = [pl.BlockSpec((1, lc, 1024), lambda i: (i, 0, 0)),
                pl.BlockSpec((1, ll, 1024), lambda i: (i, 0, 0)),
                pl.BlockSpec((1, lc, LANES), lambda i: (i, 0, 0)),
                pl.BlockSpec((1, ll, LANES), lambda i: (i, 0, 0)),
                pl.BlockSpec((8, 3 * MIX_W), lambda i: (0, 0)),
                pl.BlockSpec((2, 1, LANES), lambda i: (0, 0, 0)),
                pl.BlockSpec((2, 1, LANES), lambda i: (0, 0, 0)),
                pl.BlockSpec((2, LANES, MIX_W), lambda i: (0, 0, 0)),
                pl.BlockSpec((2, LANES, MIX_W), lambda i: (0, 0, 0)),
                pl.BlockSpec((1, MIX_W), lambda i: (0, 0)),
                pl.BlockSpec((LANES, LANES), lambda i: (0, 0))]
    out_shape = [jax.ShapeDtypeStruct((b, ll, MIX_W), BF16)]
    out_specs = [pl.BlockSpec((1, ll, MIX_W), lambda i: (i, 0, 0))]
    if need_ctx:
        out_shape.insert(0, jax.ShapeDtypeStruct((b, lc, MIX_W), BF16))
        out_specs.insert(0, pl.BlockSpec((1, lc, MIX_W), lambda i: (i, 0, 0)))
    res = pl.pallas_call(
        functools.partial(_gdn2_kernel, need_ctx=need_ctx),
        grid=(b,), in_specs=in_specs, out_specs=out_specs, out_shape=out_shape,
        scratch_shapes=[pltpu.VMEM((2, lt, MIX_W), BF16), pltpu.VMEM((2, lt, MIX_W), BF16),
                        pltpu.VMEM((2, lt, MIX_W), BF16), pltpu.VMEM((2, lt, MIX_W), F32),
                        pltpu.VMEM((2, lt, MIX_W), F32), pltpu.VMEM((2, lt // 8, MIX_W), F32),
                        pltpu.VMEM((lt, MIX_W), F32), pltpu.VMEM((lt, MIX_W), F32),
                        pltpu.VMEM((2, MIX_W // PAIR_W, PAIR_W, PAIR_W), F32)],
        compiler_params=_cparams(("parallel",)),
        name="gdn_scan",
    )(x_ctx, x_lat, s_ctx, s_lat, cw, alog, dtb, sel, selb, gain, seg)
    return (res[0], res[1]) if need_ctx else (None, res[0])


def _outproj_kernel(gla_ref, gdn_ref, att_ref, h_ref, g_ref, w_ref, o_ref):
    y = (jnp.dot(gla_ref[...], w_ref[0:256, :], preferred_element_type=F32)
         + jnp.dot(gdn_ref[...], w_ref[256:512, :], preferred_element_type=F32)
         + jnp.dot(att_ref[...], w_ref[512:1024, :], preferred_element_type=F32))
    o_ref[...] = h_ref[...] + g_ref[0] * y


def _outproj(gla, gdn, att, h2d, mod144, mod_row_fn, w, seq_len, tm):
    rows, d = h2d.shape
    tiles_per_seq = seq_len // tm
    return pl.pallas_call(
        _outproj_kernel,
        grid=(rows // tm,),
        in_specs=[pl.BlockSpec((tm, 256), lambda i: (i, 0)),
                  pl.BlockSpec((tm, 256), lambda i: (i, 0)),
                  pl.BlockSpec((tm, 512), lambda i: (i, 0)),
                  pl.BlockSpec((tm, d), lambda i: (i, 0)),
                  pl.BlockSpec((1, 1, d), lambda i: (mod_row_fn(i // tiles_per_seq) * 6 + 2, 0, 0)),
                  pl.BlockSpec((d, d), lambda i: (0, 0))],
        out_specs=pl.BlockSpec((tm, d), lambda i: (i, 0)),
        out_shape=jax.ShapeDtypeStruct((rows, d), F32),
        compiler_params=_cparams(("parallel",)),
        name="outproj",
    )(gla, gdn, att, h2d, mod144, w)


def _norm_mod(h_ref, sh_ref, sc_ref, gain_ref):
    x = h_ref[...]
    ms = jnp.mean(x * x, axis=-1, keepdims=True)
    return x * lax.rsqrt(ms + EPS) * (gain_ref[...] * (1.0 + sc_ref[0])) + sh_ref[0]


def _ffn_kernel(h_ref, sh_ref, sc_ref, g_ref, gain_ref, wg_ref, wu_ref, wd_ref, o_ref, b_scr, acc_scr, *, nf):
    f = pl.program_id(1)

    @pl.when(f == 0)
    def _():
        b_scr[...] = _norm_mod(h_ref, sh_ref, sc_ref, gain_ref).astype(BF16)
        acc_scr[...] = jnp.zeros(acc_scr.shape, F32)

    b = b_scr[...]
    gg = jnp.dot(b, wg_ref[...], preferred_element_type=F32)
    uu = jnp.dot(b, wu_ref[...], preferred_element_type=F32)
    hid = (_silu(gg) * uu).astype(BF16)
    acc_scr[...] += jnp.dot(hid, wd_ref[...], preferred_element_type=F32)

    @pl.when(f == nf - 1)
    def _():
        o_ref[...] = h_ref[...] + g_ref[0] * acc_scr[...]


def _ffn(h2d, mod144, mod_row_fn, gain, w_gu, w_down, seq_len, tm, tf):
    rows, d = h2d.shape
    dff = w_down.shape[0]
    nf = dff // tf
    tiles_per_seq = seq_len // tm

    def mod_spec(k):
        return pl.BlockSpec((1, 1, d), lambda i, f: (mod_row_fn(i // tiles_per_seq) * 6 + k, 0, 0))

    return pl.pallas_call(
        functools.partial(_ffn_kernel, nf=nf),
        grid=(rows // tm, nf),
        in_specs=[pl.BlockSpec((tm, d), lambda i, f: (i, 0)),
                  mod_spec(3), mod_spec(4), mod_spec(5),
                  pl.BlockSpec((1, d), lambda i, f: (0, 0)),
                  pl.BlockSpec((d, tf), lambda i, f: (0, f)),
                  pl.BlockSpec((d, tf), lambda i, f: (0, nf + f)),
                  pl.BlockSpec((tf, d), lambda i, f: (f, 0))],
        out_specs=pl.BlockSpec((tm, d), lambda i, f: (i, 0)),
        out_shape=jax.ShapeDtypeStruct((rows, d), F32),
        scratch_shapes=[pltpu.VMEM((tm, d), BF16), pltpu.VMEM((tm, d), F32)],
        compiler_params=_cparams(("parallel", "arbitrary")),
        name="ffn",
    )(h2d, mod144, mod144, mod144, gain, w_gu, w_gu, w_down)


def _moe_kernel(h_ref, sh_ref, sc_ref, g_ref, gain_ref, wr_ref, br_ref, wg_ref, wu_ref, wd_ref, o_ref,
                b_scr, comb_scr, acc_scr, *, nf):
    e = pl.program_id(1)
    f = pl.program_id(2)
    lane = _iota((1, LANES), 1)
    lane_f = lane.astype(F32)

    @pl.when((e == 0) & (f == 0))
    def _():
        b = _norm_mod(h_ref, sh_ref, sc_ref, gain_ref)
        b_scr[...] = b.astype(BF16)
        acc_scr[...] = jnp.zeros(acc_scr.shape, F32)
        logits = _dot3(b, wr_ref[...]) + br_ref[...]
        logits = jnp.where(lane < N_EXPERTS, logits, -jnp.inf)
        m1 = jnp.max(logits, axis=-1, keepdims=True)
        i1 = jnp.min(jnp.where(logits == m1, lane_f, float(LANES)), axis=-1, keepdims=True)
        rest = jnp.where(lane_f == i1, -jnp.inf, logits)
        m2 = jnp.max(rest, axis=-1, keepdims=True)
        i2 = jnp.min(jnp.where(rest == m2, lane_f, float(LANES)), axis=-1, keepdims=True)
        t = jnp.exp(m2 - m1)
        w1 = 1.0 / (1.0 + t)
        comb_scr[...] = jnp.where(lane_f == i1, w1, 0.0) + jnp.where(lane_f == i2, t * w1, 0.0)

    b = b_scr[...]
    gg = jnp.dot(b, wg_ref[0], preferred_element_type=F32)
    uu = jnp.dot(b, wu_ref[0], preferred_element_type=F32)
    hid = (_silu(gg) * uu).astype(BF16)
    w_e = jnp.sum(jnp.where(lane == e, comb_scr[...], 0.0), axis=-1, keepdims=True)
    acc_scr[...] += w_e * jnp.dot(hid, wd_ref[0], preferred_element_type=F32)

    @pl.when((e == N_EXPERTS - 1) & (f == nf - 1))
    def _():
        o_ref[...] = h_ref[...] + g_ref[0] * acc_scr[...]


def _moe(h2d, mod144, mod_row_fn, gain, w_router, b_router, w_gu, w_down, seq_len, tm, tf):
    rows, d = h2d.shape
    ne, dff, _ = w_down.shape
    nf = dff // tf
    tiles_per_seq = seq_len // tm

    def mod_spec(k):
        return pl.BlockSpec((1, 1, d), lambda i, e, f: (mod_row_fn(i // tiles_per_seq) * 6 + k, 0, 0))

    return pl.pallas_call(
        functools.partial(_moe_kernel, nf=nf),
        grid=(rows // tm, ne, nf),
        in_specs=[pl.BlockSpec((tm, d), lambda i, e, f: (i, 0)),
                  mod_spec(3), mod_spec(4), mod_spec(5),
                  pl.BlockSpec((1, d), lambda i, e, f: (0, 0)),
                  pl.BlockSpec((d, LANES), lambda i, e, f: (0, 0)),
                  pl.BlockSpec((1, LANES), lambda i, e, f: (0, 0)),
                  pl.BlockSpec((1, d, tf), lambda i, e, f: (e, 0, f)),
                  pl.BlockSpec((1, d, tf), lambda i, e, f: (e, 0, nf + f)),
                  pl.BlockSpec((1, tf, d), lambda i, e, f: (e, f, 0))],
        out_specs=pl.BlockSpec((tm, d), lambda i, e, f: (i, 0)),
        out_shape=jax.ShapeDtypeStruct((rows, d), F32),
        scratch_shapes=[pltpu.VMEM((tm, d), BF16), pltpu.VMEM((tm, LANES), F32), pltpu.VMEM((tm, d), F32)],
        compiler_params=_cparams(("parallel", "arbitrary", "arbitrary")),
        name="moe",
    )(h2d, mod144, mod144, mod144, gain, w_router, b_router, w_gu, w_gu, w_down)


MOE_TILE = 512
COMBINE_TOKENS = 256


def _router_kernel(h_ref, sh_ref, sc_ref, gain_ref, wr_ref, br_ref, b_ref, route_ref):
    lane = _iota((1, LANES), 1)
    lane_f = lane.astype(F32)
    b = _norm_mod(h_ref, sh_ref, sc_ref, gain_ref)
    b_ref[...] = b
    logits = _dot3(b, wr_ref[...]) + br_ref[...]
    logits = jnp.where(lane < N_EXPERTS, logits, -jnp.inf)
    m1 = jnp.max(logits, axis=-1, keepdims=True)
    i1 = jnp.min(jnp.where(logits == m1, lane_f, float(LANES)), axis=-1, keepdims=True)
    rest = jnp.where(lane_f == i1, -jnp.inf, logits)
    m2 = jnp.max(rest, axis=-1, keepdims=True)
    i2 = jnp.min(jnp.where(rest == m2, lane_f, float(LANES)), axis=-1, keepdims=True)
    t = jnp.exp(m2 - m1)
    w1 = 1.0 / (1.0 + t)
    route_ref[...] = (jnp.where(lane == 0, i1, 0.0) + jnp.where(lane == 1, i2, 0.0)
                      + jnp.where(lane == 2, w1, 0.0) + jnp.where(lane == 3, t * w1, 0.0))


def _router(h2d, mod144, mod_row_fn, gain, w_router, b_router, seq_len, tm):
    rows, d = h2d.shape
    tiles_per_seq = seq_len // tm

    def mod_spec(k):
        return pl.BlockSpec((1, 1, d), lambda i: (mod_row_fn(i // tiles_per_seq) * 6 + k, 0, 0))

    return pl.pallas_call(
        _router_kernel,
        grid=(rows // tm,),
        in_specs=[pl.BlockSpec((tm, d), lambda i: (i, 0)), mod_spec(3), mod_spec(4),
                  pl.BlockSpec((1, d), lambda i: (0, 0)),
                  pl.BlockSpec((d, LANES), lambda i: (0, 0)),
                  pl.BlockSpec((1, LANES), lambda i: (0, 0))],
        out_specs=(pl.BlockSpec((tm, d), lambda i: (i, 0)), pl.BlockSpec((tm, LANES), lambda i: (i, 0))),
        out_shape=(jax.ShapeDtypeStruct((rows, d), F32), jax.ShapeDtypeStruct((rows, LANES), F32)),
        compiler_params=_cparams(("parallel",)),
        name="moe_router",
    )(h2d, mod144, mod144, gain, w_router, b_router)


def _moe_group_kernel(te_ref, idx_ref, idx_next_ref, b_ref, wg_ref, wu_ref, wd_ref, ys_ref,
                      xbuf, sems, xb_scr, acc_scr, *, nf):
    i = pl.program_id(0)
    f = pl.program_id(1)
    last_tile = pl.num_programs(0) - 1
    slot = i % 2
    share = MOE_TILE // nf

    def whole_tile_copy(s):
        return pltpu.make_async_copy(b_ref.at[pl.ds(0, MOE_TILE)], xbuf.at[s], sems.at[s])

    @pl.when((f == 0) & (i == 0))
    def _():
        def issue(r, carry):
            pltpu.make_async_copy(b_ref.at[pl.ds(idx_ref[0, 0, r], 1)], xbuf.at[0, pl.ds(r, 1)], sems.at[0]).start()
            return carry

        lax.fori_loop(0, MOE_TILE, issue, 0, unroll=8)

    @pl.when(f == 0)
    def _():
        whole_tile_copy(slot).wait()
        xb_scr[...] = xbuf[slot].astype(BF16)

    for r in range(share):
        row = f * share + r
        pltpu.make_async_copy(b_ref.at[pl.ds(idx_next_ref[0, 0, row], 1)], xbuf.at[1 - slot, pl.ds(row, 1)],
                              sems.at[1 - slot]).start()
    x = xb_scr[...]
    gg = jnp.dot(x, wg_ref[0], preferred_element_type=F32)
    uu = jnp.dot(x, wu_ref[0], preferred_element_type=F32)
    hid = (_silu(gg) * uu).astype(BF16)
    part = jnp.dot(hid, wd_ref[0], preferred_element_type=F32)

    @pl.when(f == 0)
    def _():
        acc_scr[...] = part

    @pl.when(f > 0)
    def _():
        acc_scr[...] += part

    @pl.when(f == nf - 1)
    def _():
        ys_ref[...] = acc_scr[...]

    @pl.when((i == last_tile) & (f == nf - 1))
    def _():
        whole_tile_copy(1 - slot).wait()


def _moe_group(tile_expert, tok_sorted, b, w_gu, w_down, tf):
    p = tok_sorted.shape[0]
    d = b.shape[1]
    dff = w_down.shape[1]
    nf = dff // tf
    n_tiles = p // MOE_TILE
    idx3 = tok_sorted.reshape(n_tiles, 1, MOE_TILE)
    grid_spec = pltpu.PrefetchScalarGridSpec(
        num_scalar_prefetch=1,
        grid=(n_tiles, nf),
        in_specs=[pl.BlockSpec((1, 1, MOE_TILE), lambda i, f, te: (i, 0, 0), memory_space=pltpu.SMEM),
                  pl.BlockSpec((1, 1, MOE_TILE), lambda i, f, te: (jnp.minimum(i + 1, n_tiles - 1), 0, 0),
                               memory_space=pltpu.SMEM),
                  pl.BlockSpec(memory_space=pl.ANY),
                  pl.BlockSpec((1, d, tf), lambda i, f, te: (te[i], 0, f)),
                  pl.BlockSpec((1, d, tf), lambda i, f, te: (te[i], 0, nf + f)),
                  pl.BlockSpec((1, tf, d), lambda i, f, te: (te[i], f, 0))],
        out_specs=pl.BlockSpec((MOE_TILE, d), lambda i, f, te: (i, 0)),
        scratch_shapes=[pltpu.VMEM((2, MOE_TILE, d), F32), pltpu.SemaphoreType.DMA((2,)),
                        pltpu.VMEM((MOE_TILE, d), BF16), pltpu.VMEM((MOE_TILE, d), F32)])
    return pl.pallas_call(
        functools.partial(_moe_group_kernel, nf=nf),
        grid_spec=grid_spec,
        out_shape=jax.ShapeDtypeStruct((p, d), F32),
        compiler_params=_cparams(("arbitrary", "arbitrary")),
        name="moe_experts",
    )(tile_expert, idx3, idx3, b, w_gu, w_gu, w_down)


def _moe_combine_kernel(pos_ref, ys_ref, h_ref, route_ref, g_ref, o_ref, buf, sem):
    n = 2 * COMBINE_TOKENS

    def issue(r, carry):
        pltpu.make_async_copy(ys_ref.at[pl.ds(pos_ref[0, 0, r], 1)], buf.at[pl.ds(r, 1)], sem).start()
        return carry

    lax.fori_loop(0, n, issue, 0, unroll=8)
    pltpu.make_async_copy(ys_ref.at[pl.ds(0, n)], buf, sem).wait()
    w1 = route_ref[:, 2:3]
    w2 = route_ref[:, 3:4]
    o_ref[...] = h_ref[...] + g_ref[0] * (w1 * buf[0:COMBINE_TOKENS, :] + w2 * buf[COMBINE_TOKENS:n, :])


def _moe_combine(pos, ys, h2d, route, mod144, mod_row_fn, seq_len):
    rows, d = h2d.shape
    tm = COMBINE_TOKENS
    tiles_per_seq = seq_len // tm
    steps = rows // tm
    return pl.pallas_call(
        _moe_combine_kernel,
        grid=(steps,),
        in_specs=[pl.BlockSpec((1, 1, 2 * tm), lambda i: (i, 0, 0), memory_space=pltpu.SMEM),
                  pl.BlockSpec(memory_space=pl.ANY),
                  pl.BlockSpec((tm, d), lambda i: (i, 0)),
                  pl.BlockSpec((tm, LANES), lambda i: (i, 0)),
                  pl.BlockSpec((1, 1, d), lambda i: (mod_row_fn(i // tiles_per_seq) * 6 + 5, 0, 0))],
        out_specs=pl.BlockSpec((tm, d), lambda i: (i, 0)),
        out_shape=jax.ShapeDtypeStruct((rows, d), F32),
        scratch_shapes=[pltpu.VMEM((2 * tm, d), F32), pltpu.SemaphoreType.DMA(())],
        compiler_params=_cparams(("arbitrary",)),
        name="moe_combine",
    )(pos, ys, h2d, route, mod144)


def _moe_routed(h2d, mod144, mod_row_fn, gain, w_router, b_router, w_gu, w_down, seq_len):
    rows, d = h2d.shape
    b, route = _router(h2d, mod144, mod_row_fn, gain, w_router, b_router, seq_len, min(512, seq_len))
    ex = jnp.concatenate([route[:, 0], route[:, 1]]).astype(jnp.int32)
    tok = jnp.concatenate([jnp.arange(rows, dtype=jnp.int32)] * 2)
    onehot = (ex[:, None] == jnp.arange(N_EXPERTS, dtype=jnp.int32)[None, :]).astype(jnp.int32)
    rank = jnp.sum((jnp.cumsum(onehot, axis=0) - onehot) * onehot, axis=1)
    counts = jnp.sum(onehot, axis=0)
    padded = ((counts + MOE_TILE - 1) // MOE_TILE) * MOE_TILE
    ends = jnp.cumsum(padded)
    starts = ends - padded
    pos = jnp.sum(onehot * starts[None, :], axis=1) + rank
    p_rows = 2 * rows + N_EXPERTS * MOE_TILE
    tok_sorted = jnp.zeros((p_rows,), jnp.int32).at[pos].set(tok)
    tile_first = jnp.arange(p_rows // MOE_TILE, dtype=jnp.int32) * MOE_TILE
    tile_expert = jnp.minimum(jnp.sum((tile_first[:, None] >= ends[None, :]).astype(jnp.int32), axis=1),
                              N_EXPERTS - 1).astype(jnp.int32)
    ys = _moe_group(tile_expert, tok_sorted, b, w_gu, w_down, 1408)
    steps = rows // COMBINE_TOKENS
    pos2 = jnp.concatenate([pos[:rows].reshape(steps, 1, COMBINE_TOKENS),
                            pos[rows:].reshape(steps, 1, COMBINE_TOKENS)], axis=2).astype(jnp.int32)
    return _moe_combine(pos2, ys, h2d, route, mod144, mod_row_fn, seq_len)


ATT_HEAD_ORDER = (0, 4, 1, 5, 2, 6, 3, 7)


def _layout_w_in(w):
    gla = w[:, 0:1024]
    glow = w[:, 1024:1056]
    gdn = w[:, 1056:2080]
    ab = w[:, 2080:2096]
    q = jnp.concatenate([w[:, 2096 + HEAD_DIM * h:2096 + HEAD_DIM * (h + 1)] for h in ATT_HEAD_ORDER], axis=1)
    kv = w[:, 2608:2864]
    pad = jnp.zeros((w.shape[0], LANES - 48), w.dtype)
    return jnp.concatenate([gla, gdn, q, kv, glow, ab, pad], axis=1).astype(BF16)


def _layout_w_out(w):
    att = [w[512 + HEAD_DIM * h:512 + HEAD_DIM * (h + 1)] for h in ATT_HEAD_ORDER]
    return jnp.concatenate([w[0:512]] + att, axis=0).astype(BF16)


def _rope_tables(seq_len):
    rows = seq_len // GRID_W
    row = jnp.repeat(jnp.arange(rows), GRID_W).astype(F32)
    col = jnp.tile(jnp.arange(GRID_W), rows).astype(F32)
    inv_freq = ROPE_THETA ** (-jnp.arange(0, HEAD_DIM // 2, 2, dtype=F32) / (HEAD_DIM // 2))
    ar = row[:, None] * inv_freq
    ac = col[:, None] * inv_freq
    cos = jnp.concatenate([jnp.cos(ar), jnp.cos(ar), jnp.cos(ac), jnp.cos(ac)], axis=-1)
    sin = jnp.concatenate([-jnp.sin(ar), jnp.sin(ar), -jnp.sin(ac), jnp.sin(ac)], axis=-1)
    return jnp.tile(cos, (1, 2)), jnp.tile(sin, (1, 2))


def _seg_matrix():
    i = np.arange(LANES)
    return jnp.asarray((i[:, None] // HEAD_DIM) == (i[None, :] // HEAD_DIM), dtype=BF16)


def _gdn_select():
    sel = np.zeros((2, LANES, MIX_W), np.float32)
    selb = np.zeros((2, LANES, MIX_W), np.float32)
    for d in range(2):
        for h in range(GDN_HEADS):
            sel[d, 32 + GDN_HEADS * d + h, HEAD_DIM * h:HEAD_DIM * (h + 1)] = 1.0
            selb[d, 40 + GDN_HEADS * d + h, HEAD_DIM * h:HEAD_DIM * (h + 1)] = 1.0
    return jnp.asarray(sel, BF16), jnp.asarray(selb, BF16)


def _lane_rows(vals, base):
    out = jnp.zeros((2, 1, LANES), F32)
    for d in range(2):
        out = out.at[d, 0, base + GDN_HEADS * d:base + GDN_HEADS * (d + 1)].set(vals[d].astype(F32))
    return out


def kernel(x, c, ctx, c_ctx, w_mod, b_mod, norm_mix, norm_ffn, w_in, gla_gate_up, gla_gate_bias, gla_out_gain,
           gdn_conv, gdn_a_log, gdn_dt_bias, gdn_out_gain, att_q_gain, att_k_gain, w_out, ffn_gate_up, ffn_down,
           moe_router, moe_router_bias, moe_gate_up, moe_down):
    bsz, seq, d = x.shape
    lctx = ctx.shape[1]
    depth = w_mod.shape[0]
    ctx_row = bsz

    mod_rows = ((bsz + 1 + 7) // 8) * 8
    cvec = jnp.concatenate([c, c_ctx[None, :], jnp.zeros((mod_rows - bsz - 1, d), F32)], axis=0)
    mods = _modulation(cvec, w_mod, b_mod)

    seg = _seg_matrix()
    tables = _rope_tables(seq)
    sel, selb = _gdn_select()
    lat_row = lambda b: b
    ctx_row_fn = lambda b: ctx_row

    h_lat = x.reshape(bsz * seq, d)
    h_ctx = ctx.reshape(bsz * lctx, d)
    for layer in range(depth):
        need_ctx = layer < depth - 1
        mod144 = mods[layer].reshape(mod_rows * 6, 1, d)
        w_p = _layout_w_in(w_in[layer])
        w_o = _layout_w_out(w_out[layer])
        hg = jnp.concatenate([jnp.tile(att_q_gain[layer], ATT_Q_HEADS) * (HEAD_DIM ** -0.5 * LOG2E),
                              jnp.tile(att_k_gain[layer], ATT_KV_HEADS)])[None, :].astype(F32)
        gain_mix = norm_mix[layer][None, :]
        gain_ffn = norm_ffn[layer][None, :]

        gla_l, gdn_l, q_l, kv_l, sm_l = _inproj(h_lat, mod144, lat_row, gain_mix, w_p, hg, seg, tables, seq, 256)
        gla_c, gdn_c, q_c, kv_c, sm_c = _inproj(h_ctx, mod144, ctx_row_fn, gain_mix, w_p, hg, seg, None, lctx, 256)

        r3 = lambda t, n: t.reshape(bsz, n, t.shape[-1])
        wg = jnp.zeros((2, LANES, MIX_W), F32)
        for dd in range(2):
            wg = wg.at[dd, GLA_GATE_RANK * dd:GLA_GATE_RANK * (dd + 1), :].set(gla_gate_up[layer, dd].astype(F32))
        bg = gla_gate_bias[layer].reshape(2, 1, MIX_W).astype(F32)
        gla_gain = jnp.tile(gla_out_gain[layer], GLA_HEADS)[None, :].astype(F32)
        o_gla_c, o_gla_l = _gla(r3(gla_c, lctx), r3(gla_l, seq), r3(sm_c, lctx), r3(sm_l, seq),
                                wg, bg, gla_gain, seg, need_ctx)

        cw = jnp.concatenate([gdn_conv[layer].astype(F32), jnp.zeros((8 - SHORT_CONV, 3 * MIX_W), F32)], axis=0)
        alog = _lane_rows(gdn_a_log[layer], 32)
        dtb = _lane_rows(gdn_dt_bias[layer], 32)
        gdn_gain = jnp.tile(gdn_out_gain[layer], GDN_HEADS)[None, :].astype(F32)
        o_gdn_c, o_gdn_l = _gdn2(r3(gdn_c, lctx), r3(gdn_l, seq), r3(sm_c, lctx), r3(sm_l, seq),
                                cw, alog, dtb, sel, selb, gdn_gain, seg, need_ctx)

        o_att_l = _attention(r3(q_l, seq), [r3(kv_l, seq), r3(kv_c, lctx)], 256)
        h_lat = _outproj(o_gla_l.reshape(-1, MIX_W), o_gdn_l.reshape(-1, MIX_W), o_att_l.reshape(-1, 512),
                         h_lat, mod144, lat_row, w_o, seq, 512)
        if need_ctx:
            o_att_c = _attention(r3(q_c, lctx), [r3(kv_c, lctx)], 128)
            h_ctx = _outproj(o_gla_c.reshape(-1, MIX_W), o_gdn_c.reshape(-1, MIX_W), o_att_c.reshape(-1, 512),
                             h_ctx, mod144, ctx_row_fn, w_o, lctx, 256)

        j = layer // 2
        if layer % 2 == 0:
            w_gu = ffn_gate_up[j].astype(BF16)
            w_dn = ffn_down[j].astype(BF16)
            h_lat = _ffn(h_lat, mod144, lat_row, gain_ffn, w_gu, w_dn, seq, 512, 1408)
            if need_ctx:
                h_ctx = _ffn(h_ctx, mod144, ctx_row_fn, gain_ffn, w_gu, w_dn, lctx, 256, 1408)
        else:
            w_gu = moe_gate_up[j].astype(BF16)
            w_dn = moe_down[j].astype(BF16)
            w_r = jnp.concatenate([moe_router[j].astype(F32), jnp.zeros((d, LANES - N_EXPERTS), F32)], axis=1)
            b_r = jnp.concatenate([moe_router_bias[j].astype(F32), jnp.zeros((LANES - N_EXPERTS,), F32)])[None, :]
            h_lat = _moe_routed(h_lat, mod144, lat_row, gain_ffn, w_r, b_r, w_gu, w_dn, seq)
            if need_ctx:
                h_ctx = _moe_routed(h_ctx, mod144, ctx_row_fn, gain_ffn, w_r, b_r, w_gu, w_dn, lctx)
    return h_lat.reshape(bsz, seq, d)
```

```python
import functools

import numpy as np
import jax
import jax.numpy as jnp
from jax import lax
from jax.experimental import pallas as pl
from jax.experimental.pallas import tpu as pltpu

F32 = jnp.float32
BF16 = jnp.bfloat16

GRID_W = 64
HEAD_DIM = 64
CHUNK = 64
SUB = 16
EPS = 1e-6
GLA_HEADS = 4
GLA_GATE_RANK = 16
GLA_TAU = 16.0
GDN_HEADS = 4
SHORT_CONV = 5
ATT_Q_HEADS = 8
ATT_KV_HEADS = 2
ROPE_THETA = 10000.0
N_EXPERTS = 8
MIX_W = GLA_HEADS * HEAD_DIM
LOG2E = 1.4426950408889634
EXP_CLAMP = 80.0

LANES = 128
V7X_VMEM_BYTES = 64 * 1024 * 1024
VMEM_LIMIT = 56 * 1024 * 1024


def _cparams(sem):
    return pltpu.CompilerParams(dimension_semantics=sem, vmem_limit_bytes=VMEM_LIMIT)


def _silu(x):
    return x / (1.0 + jnp.exp(-x))


def _sigmoid(x):
    return 1.0 / (1.0 + jnp.exp(-x))


def _softplus(x):
    return jnp.maximum(x, 0.0) + jnp.log(1.0 + jnp.exp(-jnp.abs(x)))


def _dot(a, b):
    return jnp.dot(a.astype(BF16), b.astype(BF16), preferred_element_type=F32)


def _dot_nt(a, b):
    return lax.dot_general(a.astype(BF16), b.astype(BF16), (((1,), (1,)), ((), ())),
                           preferred_element_type=F32)


def _split(x):
    hi = x.astype(BF16)
    lo = (x - hi.astype(F32)).astype(BF16)
    return hi, lo


def _dot_xhl(x, w):
    hi, lo = _split(x)
    w = w.astype(BF16)
    return (jnp.dot(hi, w, preferred_element_type=F32) + jnp.dot(lo, w, preferred_element_type=F32))


def _dot_whl(w, x):
    hi, lo = _split(x)
    w = w.astype(BF16)
    return (jnp.dot(w, hi, preferred_element_type=F32) + jnp.dot(w, lo, preferred_element_type=F32))


def _dot3(a, b):
    ah, al = _split(a)
    bh, bl = _split(b)
    return (jnp.dot(ah, bh, preferred_element_type=F32) + jnp.dot(ah, bl, preferred_element_type=F32)
            + jnp.dot(al, bh, preferred_element_type=F32))


def _seg_sum64(sq, seg):
    outs = []
    for j in range(sq.shape[1] // LANES):
        outs.append(_dot_xhl(sq[:, LANES * j:LANES * (j + 1)], seg))
    return outs[0] if len(outs) == 1 else jnp.concatenate(outs, axis=1)


def _iota(shape, dim):
    return lax.broadcasted_iota(jnp.int32, shape, dim)


def _mod_kernel(c_ref, w_ref, b_ref, o_ref):
    s = _silu(c_ref[...])
    o_ref[0] = _dot(s, w_ref[0]) + b_ref[0]


def _modulation(cvec, w_mod, b_mod):
    depth, d, n = w_mod.shape
    rows = cvec.shape[0]
    tn = 1536
    return pl.pallas_call(
        _mod_kernel,
        grid=(depth, n // tn),
        in_specs=[pl.BlockSpec((rows, d), lambda l, j: (0, 0)),
                  pl.BlockSpec((1, d, tn), lambda l, j: (l, 0, j)),
                  pl.BlockSpec((1, 1, tn), lambda l, j: (l, 0, j))],
        out_specs=pl.BlockSpec((1, rows, tn), lambda l, j: (l, 0, j)),
        out_shape=jax.ShapeDtypeStruct((depth, rows, n), F32),
        compiler_params=_cparams(("arbitrary", "arbitrary")),
        name="modulation",
    )(cvec, w_mod, b_mod.reshape(depth, 1, n))


def _swap16(n, lane):
    fwd = pltpu.roll(n, LANES - 16, 1)
    bwd = pltpu.roll(n, 16, 1)
    return jnp.where((lane % 32) < 16, fwd, bwd)


def _inproj_kernel(*refs, rope):
    if rope:
        (h_ref, sh_ref, sc_ref, gain_ref, w_ref, hg_ref, seg_ref, cos_ref, sin_ref,
         gla_ref, gdn_ref, q_ref, kv_ref, small_ref) = refs
    else:
        (h_ref, sh_ref, sc_ref, gain_ref, w_ref, hg_ref, seg_ref,
         gla_ref, gdn_ref, q_ref, kv_ref, small_ref) = refs
    x = h_ref[...]
    ms = jnp.mean(x * x, axis=-1, keepdims=True)
    a = x * lax.rsqrt(ms + EPS) * (gain_ref[...] * (1.0 + sc_ref[0])) + sh_ref[0]
    p = jnp.dot(a.astype(BF16), w_ref[...], preferred_element_type=F32)
    gla_ref[...] = p[:, 0:1024]
    gdn_ref[...] = p[:, 1024:2048]
    small_ref[...] = p[:, 2816:2944]
    seg = seg_ref[...]
    lane = _iota((1, LANES), 1)
    outs = []
    for j in range(5):
        t = p[:, 2048 + LANES * j:2048 + LANES * (j + 1)]
        ss = _dot_xhl(t * t, seg)
        n = t * lax.rsqrt(ss * (1.0 / HEAD_DIM) + EPS) * hg_ref[:, LANES * j:LANES * (j + 1)]
        if rope:
            n = n * cos_ref[...] + _swap16(n, lane) * sin_ref[...]
        outs.append(n)
    q_ref[...] = jnp.concatenate(outs[:4], axis=1).astype(BF16)
    kv_ref[...] = jnp.concatenate([outs[4], p[:, 2688:2816]], axis=1).astype(BF16)


def _inproj(h2d, mod144, mod_row_fn, gain, w_p, hg, seg, tables, seq_len, tm):
    rows, d = h2d.shape
    n_all = w_p.shape[1]
    rope = tables is not None
    tiles_per_seq = seq_len // tm
    in_specs = [pl.BlockSpec((tm, d), lambda i: (i, 0)),
                pl.BlockSpec((1, 1, d), lambda i: (mod_row_fn(i // tiles_per_seq) * 6 + 0, 0, 0)),
                pl.BlockSpec((1, 1, d), lambda i: (mod_row_fn(i // tiles_per_seq) * 6 + 1, 0, 0)),
                pl.BlockSpec((1, d), lambda i: (0, 0)),
                pl.BlockSpec((d, n_all), lambda i: (0, 0)),
                pl.BlockSpec((1, 640), lambda i: (0, 0)),
                pl.BlockSpec((LANES, LANES), lambda i: (0, 0))]
    args = [h2d, mod144, mod144, gain, w_p, hg, seg]
    if rope:
        in_specs += [pl.BlockSpec((tm, LANES), lambda i: (i % tiles_per_seq, 0)),
                     pl.BlockSpec((tm, LANES), lambda i: (i % tiles_per_seq, 0))]
        args += list(tables)
    out_shape = (jax.ShapeDtypeStruct((rows, 1024), F32), jax.ShapeDtypeStruct((rows, 1024), F32),
                 jax.ShapeDtypeStruct((rows, 512), BF16), jax.ShapeDtypeStruct((rows, 256), BF16),
                 jax.ShapeDtypeStruct((rows, LANES), F32))
    out_specs = (pl.BlockSpec((tm, 1024), lambda i: (i, 0)), pl.BlockSpec((tm, 1024), lambda i: (i, 0)),
                 pl.BlockSpec((tm, 512), lambda i: (i, 0)), pl.BlockSpec((tm, 256), lambda i: (i, 0)),
                 pl.BlockSpec((tm, LANES), lambda i: (i, 0)))
    return pl.pallas_call(
        functools.partial(_inproj_kernel, rope=rope),
        grid=(rows // tm,), in_specs=in_specs, out_specs=out_specs, out_shape=out_shape,
        compiler_params=_cparams(("parallel",)),
        name="inproj_rope" if rope else "inproj",
    )(*args)


ATT_CHUNKS_PER_DOT = 1


def _attn_kernel(*refs, nkv, tq):
    q_ref = refs[0]
    kv_refs = refs[1:1 + nkv]
    o_ref = refs[1 + nkv]
    q = q_ref[0]
    lane = _iota((1, LANES), 1)
    mlo = (lane < HEAD_DIM).astype(BF16)
    mhi = (lane >= HEAD_DIM).astype(BF16)
    kvs = [r[0] for r in kv_refs]
    outs = []
    for jj in range(0, 4, ATT_CHUNKS_PER_DOT):
        pieces = []
        for j in range(jj, jj + ATT_CHUNKS_PER_DOT):
            qc = q[:, LANES * j:LANES * (j + 1)]
            pieces += [qc * mlo, qc * mhi]
        q_all = jnp.concatenate(pieces, axis=0)
        ss = [lax.dot_general(q_all, kv[:, 0:LANES], (((1,), (1,)), ((), ())), preferred_element_type=F32)
              for kv in kvs]
        m = functools.reduce(jnp.maximum, [jnp.max(s, axis=-1, keepdims=True) for s in ss])
        ps = [jnp.exp2(s - m) for s in ss]
        l = functools.reduce(lambda a, b: a + b, [jnp.sum(p, axis=-1, keepdims=True) for p in ps])
        o = functools.reduce(lambda a, b: a + b,
                             [jnp.dot(p.astype(BF16), kv[:, LANES:2 * LANES], preferred_element_type=F32)
                              for p, kv in zip(ps, kvs)])
        o = o / l
        for j in range(ATT_CHUNKS_PER_DOT):
            outs.append(jnp.where(lane < HEAD_DIM, o[2 * j * tq:(2 * j + 1) * tq],
                                  o[(2 * j + 1) * tq:(2 * j + 2) * tq]))
    o_ref[0] = jnp.concatenate(outs, axis=1).astype(BF16)


def _attention(q, kvs, tq):
    b, lq, _ = q.shape
    in_specs = [pl.BlockSpec((1, tq, 512), lambda i, j: (i, j, 0))]
    for kv in kvs:
        in_specs.append(pl.BlockSpec((1, kv.shape[1], 256), lambda i, j: (i, 0, 0)))
    return pl.pallas_call(
        functools.partial(_attn_kernel, nkv=len(kvs), tq=tq),
        grid=(b, lq // tq), in_specs=in_specs,
        out_specs=pl.BlockSpec((1, tq, 512), lambda i, j: (i, j, 0)),
        out_shape=jax.ShapeDtypeStruct((b, lq, 512), BF16),
        compiler_params=_cparams(("parallel", "arbitrary")),
        name="attention",
    )(q, *kvs)


def _head_masks():
    lane = _iota((1, MIX_W), 1)
    return [(lane // HEAD_DIM == h).astype(F32) for h in range(GLA_HEADS)]


def _blockdiag_mask():
    r = _iota((MIX_W, MIX_W), 0) // HEAD_DIM
    c = _iota((MIX_W, MIX_W), 1) // HEAD_DIM
    return r == c


def _finish_rows(o, gate, gain, seg):
    ss = _seg_sum64(o * o, seg)
    return o * lax.rsqrt(ss * (1.0 / HEAD_DIM) + EPS) * gain * _silu(gate)


def _scan_segments(xc_ref, xl_ref, step):
    for seg_i, x_ref in enumerate((xc_ref, xl_ref)):
        nch = x_ref.shape[1] // CHUNK

        def body(i, carry, seg_i=seg_i, nch=nch):
            step(seg_i, i, 0)
            step(seg_i, nch - 1 - i, 1)
            return carry

        lax.fori_loop(0, nch, body, 0)


def _gla_kernel(*refs, need_ctx):
    if need_ctx:
        (xc_ref, xl_ref, sc_ref, sl_ref, wg_ref, bg_ref, gain_ref, seg_ref,
         oc_ref, ol_ref, of_ref, ob_ref, qh_ref, kh_ref, dg_ref, st_ref) = refs
    else:
        (xc_ref, xl_ref, sc_ref, sl_ref, wg_ref, bg_ref, gain_ref, seg_ref,
         ol_ref, of_ref, ob_ref, qh_ref, kh_ref, dg_ref, st_ref) = refs
        oc_ref = None
    lc = xc_ref.shape[1]
    x_refs = (xc_ref, xl_ref)
    s_refs = (sc_ref, sl_ref)
    row_off = (0, lc)
    o_refs = (of_ref, ob_ref)
    hmask = _head_masks()
    bd = _blockdiag_mask()
    ti = _iota((CHUNK, CHUNK), 0)
    si = _iota((CHUNK, CHUNK), 1)
    tri = [(si <= ti).astype(BF16), (si >= ti).astype(BF16)]
    rr = _iota((4 * CHUNK, CHUNK), 0)
    cc = _iota((4 * CHUNK, CHUNK), 1)
    t_of_row = (rr // (GLA_HEADS * SUB)) * SUB + rr % SUB
    causal = [cc <= t_of_row, cc >= t_of_row]
    nblk = CHUNK // SUB

    def prep(seg_i, c2):
        x_ref, s_ref = x_refs[seg_i], s_refs[seg_i]
        fs = []
        for kk in range(PREP_CHUNKS):
            r0 = pl.multiple_of((c2 * PREP_CHUNKS + kk) * CHUNK, CHUNK)
            fs.append(dict(row=row_off[seg_i] + r0,
                           q=x_ref[0, pl.ds(r0, CHUNK), 0:MIX_W] * (HEAD_DIM ** -0.5),
                           k=x_ref[0, pl.ds(r0, CHUNK), MIX_W:2 * MIX_W],
                           v=x_ref[0, pl.ds(r0, CHUNK), 2 * MIX_W:3 * MIX_W],
                           sm=s_ref[0, pl.ds(r0, CHUNK), :]))
        chains = [(f, d) for f in fs for d in range(2)]
        xg = [_dot3(f["sm"], wg_ref[d]) + bg_ref[d] for f, d in chains]
        g = [(jnp.minimum(x_, 0.0) - jnp.log(1.0 + jnp.exp(-jnp.abs(x_)))) * (1.0 / GLA_TAU) for x_ in xg]
        b = [_dot_whl(tri[d], g_) for (_, d), g_ in zip(chains, g)]
        pieces = []
        for i in range(nblk):
            row_pieces = []
            for (f, d), g_, b_ in zip(chains, g, b):
                e = SUB * i if d == 0 else SUB * i + SUB - 1
                bref = b_[e:e + 1, :] - g_[e:e + 1, :]
                kt = f["k"] * jnp.exp(jnp.minimum(bref - b_, EXP_CLAMP))
                qt = f["q"][SUB * i:SUB * (i + 1), :] * jnp.exp(b_[SUB * i:SUB * (i + 1), :] - bref)
                qs = jnp.concatenate([qt * hmask[h] for h in range(GLA_HEADS)], axis=0)
                row_pieces.append(_dot_nt(qs, kt))
            pieces.append(row_pieces)
        scores = [jnp.where(causal[d], jnp.concatenate([pieces[i][j] for i in range(nblk)], axis=0), 0.0)
                  for j, (_, d) in enumerate(chains)]
        r = [_dot(s_, f["v"]) for (f, _), s_ in zip(chains, scores)]
        for j, (f, d) in enumerate(chains):
            intra = []
            for i in range(nblk):
                acc = None
                for h in range(GLA_HEADS):
                    lo = (i * GLA_HEADS + h) * SUB
                    term = r[j][lo:lo + SUB, :] * hmask[h]
                    acc = term if acc is None else acc + term
                intra.append(acc)
            row = f["row"]
            e = CHUNK - 1 if d == 0 else 0
            b_end = b[j][e:e + 1, :]
            o_refs[d][pl.ds(row, CHUNK), :] = jnp.concatenate(intra, axis=0)
            qh_ref[d, pl.ds(row, CHUNK), :] = (f["q"] * jnp.exp(b[j])).astype(BF16)
            kh_ref[d, pl.ds(row, CHUNK), :] = (f["k"] * jnp.exp(b_end - b[j])).astype(BF16)
            dg_ref[d, pl.ds(pl.multiple_of(row // 8, 8), 8), :] = jnp.broadcast_to(jnp.exp(b_end), (8, MIX_W))

    for seg_i, x_ref in enumerate(x_refs):
        def prep_body(c, carry, seg_i=seg_i):
            prep(seg_i, c)
            return carry

        lax.fori_loop(0, x_ref.shape[1] // (CHUNK * PREP_CHUNKS), prep_body, 0)

    st_ref[...] = jnp.zeros(st_ref.shape, F32)

    def scan_body(seg_i, i, nch):
        x_ref = x_refs[seg_i]
        r0s = [pl.multiple_of((i if d == 0 else nch - 1 - i) * CHUNK, CHUNK) for d in range(2)]
        rows = [row_off[seg_i] + r0 for r0 in r0s]
        st = [st_ref[d] for d in range(2)]
        inter = [_dot_nt(qh_ref[d, pl.ds(rows[d], CHUNK), :], st[d]) for d in range(2)]
        upd = [jnp.dot(x_ref[0, pl.ds(r0s[d], CHUNK), 2 * MIX_W:3 * MIX_W].T.astype(BF16),
                       kh_ref[d, pl.ds(rows[d], CHUNK), :], preferred_element_type=F32) for d in range(2)]
        for d in range(2):
            o_refs[d][pl.ds(rows[d], CHUNK), :] += inter[d]
            dgr = dg_ref[d, pl.ds(pl.multiple_of(rows[d] // 8, 8), 8), :][0:1, :]
            st_ref[d] = st[d] * dgr + jnp.where(bd, upd[d], 0.0)

    for seg_i, x_ref in enumerate(x_refs):
        nch = x_ref.shape[1] // CHUNK

        def scan_iter(i, carry, seg_i=seg_i, nch=nch):
            scan_body(seg_i, i, nch)
            return carry

        lax.fori_loop(0, nch, scan_iter, 0)

    gain = gain_ref[...]
    seg = seg_ref[...]
    blk = 256
    outs = ((oc_ref, xc_ref, 0), (ol_ref, xl_ref, lc))
    for o_ref, x_ref, off in outs:
        if o_ref is None:
            continue

        def fin(i, carry, o_ref=o_ref, x_ref=x_ref, off=off):
            r0 = pl.multiple_of(i * blk, blk)
            o = of_ref[pl.ds(off + r0, blk), :] + ob_ref[pl.ds(off + r0, blk), :]
            gate = x_ref[0, pl.ds(r0, blk), 3 * MIX_W:4 * MIX_W]
            o_ref[0, pl.ds(r0, blk), :] = _finish_rows(o, gate, gain, seg).astype(BF16)
            return carry

        lax.fori_loop(0, x_ref.shape[1] // blk, fin, 0)


def _gla(x_ctx, x_lat, s_ctx, s_lat, wg, bg, gain, seg, need_ctx):
    b, lc, _ = x_ctx.shape
    ll = x_lat.shape[1]
    in_specs = [pl.BlockSpec((1, lc, 1024), lambda i: (i, 0, 0)),
                pl.BlockSpec((1, ll, 1024), lambda i: (i, 0, 0)),
                pl.BlockSpec((1, lc, LANES), lambda i: (i, 0, 0)),
                pl.BlockSpec((1, ll, LANES), lambda i: (i, 0, 0)),
                pl.BlockSpec((2, LANES, MIX_W), lambda i: (0, 0, 0)),
                pl.BlockSpec((2, 1, MIX_W), lambda i: (0, 0, 0)),
                pl.BlockSpec((1, MIX_W), lambda i: (0, 0)),
                pl.BlockSpec((LANES, LANES), lambda i: (0, 0))]
    out_shape = [jax.ShapeDtypeStruct((b, ll, MIX_W), BF16)]
    out_specs = [pl.BlockSpec((1, ll, MIX_W), lambda i: (i, 0, 0))]
    if need_ctx:
        out_shape.insert(0, jax.ShapeDtypeStruct((b, lc, MIX_W), BF16))
        out_specs.insert(0, pl.BlockSpec((1, lc, MIX_W), lambda i: (i, 0, 0)))
    res = pl.pallas_call(
        functools.partial(_gla_kernel, need_ctx=need_ctx),
        grid=(b,), in_specs=in_specs, out_specs=out_specs, out_shape=out_shape,
        scratch_shapes=[pltpu.VMEM((lc + ll, MIX_W), F32), pltpu.VMEM((lc + ll, MIX_W), F32),
                        pltpu.VMEM((2, lc + ll, MIX_W), BF16), pltpu.VMEM((2, lc + ll, MIX_W), BF16),
                        pltpu.VMEM((2, (lc + ll) // 8, MIX_W), F32),
                        pltpu.VMEM((2, MIX_W, MIX_W), F32)],
        compiler_params=_cparams(("parallel",)),
        name="gla_scan",
    )(x_ctx, x_lat, s_ctx, s_lat, wg, bg, gain, seg)
    return (res[0], res[1]) if need_ctx else (None, res[0])


def _gdn_kernel(*refs, need_ctx):
    if need_ctx:
        (xc_ref, xl_ref, sc_ref, sl_ref, cw_ref, alog_ref, dtb_ref, sel_ref, selb_ref, gain_ref, seg_ref,
         oc_ref, ol_ref, qkv_ref, of_ref, ob_ref, st_ref) = refs
    else:
        (xc_ref, xl_ref, sc_ref, sl_ref, cw_ref, alog_ref, dtb_ref, sel_ref, selb_ref, gain_ref, seg_ref,
         ol_ref, qkv_ref, of_ref, ob_ref, st_ref) = refs
        oc_ref = None
    lc = xc_ref.shape[1]
    x_refs = (xc_ref, xl_ref)
    s_refs = (sc_ref, sl_ref)
    row_off = (0, lc)
    seg = seg_ref[...]
    hmask = _head_masks()
    bd = _blockdiag_mask()
    ti = _iota((CHUNK, CHUNK), 0)
    si = _iota((CHUNK, CHUNK), 1)
    tri = [(si <= ti).astype(BF16), (si >= ti).astype(BF16)]
    ones = jnp.ones((CHUNK, CHUNK), BF16)
    tp = _iota((CHUNK, MIX_W), 0)
    sp = _iota((CHUNK, MIX_W), 1) % CHUNK
    le = sp <= tp
    ge = sp >= tp
    m_tri = [le, ge]
    m_strict = [sp < tp, sp > tp]
    m_sum = [ge.astype(F32), le.astype(F32)]
    blk16 = (tp // SUB) == (sp // SUB)
    eye = (tp == sp).astype(F32)

    cw = cw_ref[...]
    for seg_i, x_ref in enumerate(x_refs):
        ln = x_ref.shape[1]
        nch = ln // CHUNK

        def conv(c, carry, x_ref=x_ref, ln=ln, nch=nch, off=row_off[seg_i]):
            r0 = pl.multiple_of(c * CHUNK, CHUNK)
            center = x_ref[0, pl.ds(r0, CHUNK), 0:3 * MIX_W]
            p0 = pl.multiple_of(jnp.maximum(r0 - 8, 0), 8)
            n0 = pl.multiple_of(jnp.minimum(r0 + CHUNK, ln - 8), 8)
            prev = x_ref[0, pl.ds(p0, 8), 0:3 * MIX_W] * jnp.where(c > 0, 1.0, 0.0)
            nxt = x_ref[0, pl.ds(n0, 8), 0:3 * MIX_W] * jnp.where(c < nch - 1, 1.0, 0.0)
            ext = jnp.concatenate([prev, center, nxt], axis=0)
            pad = SHORT_CONV // 2
            acc = None
            for j in range(SHORT_CONV):
                term = ext[8 - pad + j:8 - pad + j + CHUNK, :] * cw[j:j + 1, :]
                acc = term if acc is None else acc + term
            y = _silu(acc)
            qk = y[:, 0:2 * MIX_W]
            ss = _seg_sum64(qk * qk, seg)
            qk = qk * lax.rsqrt(ss + EPS)
            qkv_ref[pl.ds(off + r0, CHUNK), 0:MIX_W] = qk[:, 0:MIX_W] * (HEAD_DIM ** -0.5)
            qkv_ref[pl.ds(off + r0, CHUNK), MIX_W:2 * MIX_W] = qk[:, MIX_W:2 * MIX_W]
            qkv_ref[pl.ds(off + r0, CHUNK), 2 * MIX_W:3 * MIX_W] = y[:, 2 * MIX_W:3 * MIX_W]
            return carry

        lax.fori_loop(0, nch, conv, 0)

    st_ref[...] = jnp.zeros(st_ref.shape, F32)

    def pk(y):
        return jnp.where(bd, jnp.concatenate([y] * GDN_HEADS, axis=0), 0.0)

    def mm(x, y):
        return _dot(x, pk(y))

    def step(seg_i, c, d):
        s_ref = s_refs[seg_i]
        r0 = pl.multiple_of(c * CHUNK, CHUNK)
        row = row_off[seg_i] + r0
        qn = qkv_ref[pl.ds(row, CHUNK), 0:MIX_W]
        kn = qkv_ref[pl.ds(row, CHUNK), MIX_W:2 * MIX_W]
        v = qkv_ref[pl.ds(row, CHUNK), 2 * MIX_W:3 * MIX_W]
        sm = s_ref[0, pl.ds(r0, CHUNK), :]
        g = -jnp.exp(alog_ref[d]) * _softplus(sm + dtb_ref[d])
        beta = _sigmoid(sm)
        gexp = _dot_xhl(g, sel_ref[d])
        bexp = _dot_xhl(beta, selb_ref[d])
        gam_t = _dot_whl(tri[d], gexp)
        gam_s = _dot_whl(ones, gexp * m_sum[d])
        dec = jnp.where(m_tri[d], jnp.exp(jnp.minimum(gam_t - gam_s, 0.0)), 0.0)
        kstack = jnp.concatenate([kn * hmask[h] for h in range(GDN_HEADS)], axis=0)
        kk = _dot_nt(kn, kstack)
        qk = _dot_nt(qn, kstack)
        a = jnp.where(m_strict[d], bexp * dec * kk, 0.0)
        dg = jnp.where(blk16, a, 0.0)
        lo = a - dg
        d2 = mm(dg, dg)
        d4 = mm(d2, d2)
        d8 = mm(d4, d4)
        t_inv = mm(mm(mm(eye - dg, eye + d2), eye + d4), eye + d8)
        m = mm(t_inv, lo)
        egam = jnp.exp(gam_t)
        sols = []
        for rhs in (bexp * v, bexp * egam * kn):
            y = mm(t_inv, rhs)
            z = y + mm(m, mm(m, y))
            sols.append(z - mm(m, z))
        sol_v, sol_k = sols
        st = st_ref[d]
        u = sol_v - _dot(sol_k, st)
        p = jnp.where(m_tri[d], qk * dec, 0.0)
        o = egam * _dot(qn, st) + _dot(p, pk(u))
        dst = of_ref if d == 0 else ob_ref
        dst[pl.ds(row, CHUNK), :] = o
        e = CHUNK - 1 if d == 0 else 0
        g_end = gam_t[e:e + 1, :]
        kh = kn * jnp.exp(g_end - gam_t)
        st_ref[d] = st * jnp.exp(g_end) + jnp.where(bd, _dot(kh.T, u), 0.0)

    _scan_segments(xc_ref, xl_ref, step)

    gain = gain_ref[...]
    blk = 256
    outs = ((oc_ref, xc_ref, 0), (ol_ref, xl_ref, lc))
    for o_ref, x_ref, off in outs:
        if o_ref is None:
            continue

        def fin(i, carry, o_ref=o_ref, x_ref=x_ref, off=off):
            r0 = pl.multiple_of(i * blk, blk)
            o = of_ref[pl.ds(off + r0, blk), :] + ob_ref[pl.ds(off + r0, blk), :]
            gate = x_ref[0, pl.ds(r0, blk), 3 * MIX_W:4 * MIX_W]
            o_ref[0, pl.ds(r0, blk), :] = _finish_rows(o, gate, gain, seg).astype(BF16)
            return carry

        lax.fori_loop(0, x_ref.shape[1] // blk, fin, 0)


def _gdn(x_ctx, x_lat, s_ctx, s_lat, cw, alog, dtb, sel, selb, gain, seg, need_ctx):
    b, lc, _ = x_ctx.shape
    ll = x_lat.shape[1]
    in_specs = [pl.BlockSpec((1, lc, 1024), lambda i: (i, 0, 0)),
                pl.BlockSpec((1, ll, 1024), lambda i: (i, 0, 0)),
                pl.BlockSpec((1, lc, LANES), lambda i: (i, 0, 0)),
                pl.BlockSpec((1, ll, LANES), lambda i: (i, 0, 0)),
                pl.BlockSpec((8, 3 * MIX_W), lambda i: (0, 0)),
                pl.BlockSpec((2, 1, LANES), lambda i: (0, 0, 0)),
                pl.BlockSpec((2, 1, LANES), lambda i: (0, 0, 0)),
                pl.BlockSpec((2, LANES, MIX_W), lambda i: (0, 0, 0)),
                pl.BlockSpec((2, LANES, MIX_W), lambda i: (0, 0, 0)),
                pl.BlockSpec((1, MIX_W), lambda i: (0, 0)),
                pl.BlockSpec((LANES, LANES), lambda i: (0, 0))]
    out_shape = [jax.ShapeDtypeStruct((b, ll, MIX_W), BF16)]
    out_specs = [pl.BlockSpec((1, ll, MIX_W), lambda i: (i, 0, 0))]
    if need_ctx:
        out_shape.insert(0, jax.ShapeDtypeStruct((b, lc, MIX_W), BF16))
        out_specs.insert(0, pl.BlockSpec((1, lc, MIX_W), lambda i: (i, 0, 0)))
    res = pl.pallas_call(
        functools.partial(_gdn_kernel, need_ctx=need_ctx),
        grid=(b,), in_specs=in_specs, out_specs=out_specs, out_shape=out_shape,
        scratch_shapes=[pltpu.VMEM((lc + ll, 3 * MIX_W), F32),
                        pltpu.VMEM((lc + ll, MIX_W), F32), pltpu.VMEM((lc + ll, MIX_W), F32),
                        pltpu.VMEM((2, MIX_W, MIX_W), F32)],
        compiler_params=_cparams(("parallel",)),
        name="gdn_scan",
    )(x_ctx, x_lat, s_ctx, s_lat, cw, alog, dtb, sel, selb, gain, seg)
    return (res[0], res[1]) if need_ctx else (None, res[0])


PAIR_W = 2 * HEAD_DIM
PREP_CHUNKS = 4


def _gdn2_kernel(*refs, need_ctx):
    if need_ctx:
        (xc_ref, xl_ref, sc_ref, sl_ref, cw_ref, alog_ref, dtb_ref, sel_ref, selb_ref, gain_ref, seg_ref,
         oc_ref, ol_ref, sk_ref, p_ref, qe_ref, sv_ref, kh_ref, dg_ref, of_ref, ob_ref, st_ref) = refs
    else:
        (xc_ref, xl_ref, sc_ref, sl_ref, cw_ref, alog_ref, dtb_ref, sel_ref, selb_ref, gain_ref, seg_ref,
         ol_ref, sk_ref, p_ref, qe_ref, sv_ref, kh_ref, dg_ref, of_ref, ob_ref, st_ref) = refs
        oc_ref = None
    lc = xc_ref.shape[1]
    x_refs = (xc_ref, xl_ref)
    s_refs = (sc_ref, sl_ref)
    row_off = (0, lc)
    npair = MIX_W // PAIR_W
    seg = seg_ref[...]
    ti = _iota((CHUNK, CHUNK), 0)
    si = _iota((CHUNK, CHUNK), 1)
    tri = [(si <= ti).astype(BF16), (si >= ti).astype(BF16)]
    ones = jnp.ones((CHUNK, CHUNK), BF16)
    tp = _iota((CHUNK, PAIR_W), 0)
    sp = _iota((CHUNK, PAIR_W), 1) % CHUNK
    le = sp <= tp
    ge = sp >= tp
    m_tri = [le, ge]
    m_strict = [sp < tp, sp > tp]
    m_sum = [ge.astype(F32), le.astype(F32)]
    blk16 = (tp // SUB) == (sp // SUB)
    eye = (tp == sp).astype(F32)
    bd2 = (_iota((PAIR_W, PAIR_W), 0) // HEAD_DIM) == (_iota((PAIR_W, PAIR_W), 1) // HEAD_DIM)
    lane_p = _iota((1, PAIR_W), 1)
    hm2 = [(lane_p // HEAD_DIM == h).astype(F32) for h in range(2)]
    cw = cw_ref[...]

    def pk2(y):
        yb = y.astype(BF16)
        return jnp.where(bd2, jnp.concatenate([yb, yb], axis=0), jnp.zeros((), BF16))

    def mm(x, y):
        return jnp.dot(x.astype(BF16), pk2(y), preferred_element_type=F32)

    def front(seg_i, c):
        x_ref, s_ref = x_refs[seg_i], s_refs[seg_i]
        ln = x_ref.shape[1]
        nch = ln // CHUNK
        r0 = pl.multiple_of(c * CHUNK, CHUNK)
        center = x_ref[0, pl.ds(r0, CHUNK), 0:3 * MIX_W]
        p0 = pl.multiple_of(jnp.maximum(r0 - 8, 0), 8)
        n0 = pl.multiple_of(jnp.minimum(r0 + CHUNK, ln - 8), 8)
        prev = x_ref[0, pl.ds(p0, 8), 0:3 * MIX_W] * jnp.where(c > 0, 1.0, 0.0)
        nxt = x_ref[0, pl.ds(n0, 8), 0:3 * MIX_W] * jnp.where(c < nch - 1, 1.0, 0.0)
        ext = jnp.concatenate([prev, center, nxt], axis=0)
        pad = SHORT_CONV // 2
        acc = None
        for j in range(SHORT_CONV):
            term = ext[8 - pad + j:8 - pad + j + CHUNK, :] * cw[j:j + 1, :]
            acc = term if acc is None else acc + term
        y = _silu(acc)
        qk = y[:, 0:2 * MIX_W]
        qk = qk * lax.rsqrt(_seg_sum64(qk * qk, seg) + EPS)
        f = dict(row=row_off[seg_i] + r0, qn=qk[:, 0:MIX_W] * (HEAD_DIM ** -0.5), kn=qk[:, MIX_W:2 * MIX_W],
                 v=y[:, 2 * MIX_W:3 * MIX_W])
        sm = s_ref[0, pl.ds(r0, CHUNK), :]
        beta = _sigmoid(sm)
        f["kk"], f["qk"] = [], []
        for p in range(npair):
            ls = slice(PAIR_W * p, PAIR_W * (p + 1))
            kstack = jnp.concatenate([f["kn"][:, ls] * hm2[0], f["kn"][:, ls] * hm2[1]], axis=0)
            f["kk"].append(_dot_nt(f["kn"][:, ls], kstack))
            f["qk"].append(_dot_nt(f["qn"][:, ls], kstack))
        f["gexp"] = [_dot_xhl(-jnp.exp(alog_ref[d]) * _softplus(sm + dtb_ref[d]), sel_ref[d]) for d in range(2)]
        f["bexp"] = [_dot_xhl(beta, selb_ref[d]) for d in range(2)]
        f["gam"] = [_dot_whl(tri[d], f["gexp"][d]) for d in range(2)]
        return f

    def prep(seg_i, c2):
        fs = [front(seg_i, c2 * PREP_CHUNKS + k) for k in range(PREP_CHUNKS)]
        chains = [(f, d, p) for f in fs for d in range(2) for p in range(npair)]
        lss = [slice(PAIR_W * p, PAIR_W * (p + 1)) for _, _, p in chains]
        gam_t = [f["gam"][d][:, ls] for (f, d, _), ls in zip(chains, lss)]
        gam_s = [_dot_whl(ones, f["gexp"][d][:, ls] * m_sum[d]) for (f, d, _), ls in zip(chains, lss)]
        bx = [f["bexp"][d][:, ls] for (f, d, _), ls in zip(chains, lss)]
        dec = [jnp.where(m_tri[d], jnp.exp(jnp.minimum(gt - gs, 0.0)), 0.0)
               for (_, d, _), gt, gs in zip(chains, gam_t, gam_s)]
        a = [jnp.where(m_strict[d], b_ * dc * f["kk"][p], 0.0) for (f, d, p), b_, dc in zip(chains, bx, dec)]
        dgn = [jnp.where(blk16, a_, 0.0) for a_ in a]
        lo = [a_ - g_ for a_, g_ in zip(a, dgn)]
        d2 = [mm(g_, g_) for g_ in dgn]
        t1 = [mm(eye - g_, eye + s_) for g_, s_ in zip(dgn, d2)]
        d4 = [mm(s_, s_) for s_ in d2]
        t2 = [mm(t_, eye + s_) for t_, s_ in zip(t1, d4)]
        d8 = [mm(s_, s_) for s_ in d4]
        t_inv = [mm(t_, eye + s_) for t_, s_ in zip(t2, d8)]
        m = [mm(t_, l_) for t_, l_ in zip(t_inv, lo)]
        m2 = [mm(m_, m_) for m_ in m]
        w1 = [mm(eye - m_, eye + s_) for m_, s_ in zip(m, m2)]
        w = [mm(w_, t_) for w_, t_ in zip(w1, t_inv)]
        egam = [jnp.exp(gt) for gt in gam_t]
        solv = [mm(w_, b_ * f["v"][:, ls]) for (f, _, _), w_, b_, ls in zip(chains, w, bx, lss)]
        solk = [mm(w_, b_ * eg * f["kn"][:, ls]) for (f, _, _), w_, b_, eg, ls in zip(chains, w, bx, egam, lss)]
        for i, (f, d, p) in enumerate(chains):
            ls, row = lss[i], f["row"]
            e = CHUNK - 1 if d == 0 else 0
            g_end = gam_t[i][e:e + 1, :]
            sv_ref[d, pl.ds(row, CHUNK), ls] = solv[i]
            sk_ref[d, pl.ds(row, CHUNK), ls] = solk[i].astype(BF16)
            p_ref[d, pl.ds(row, CHUNK), ls] = jnp.where(m_tri[d], f["qk"][p] * dec[i], 0.0).astype(BF16)
            qe_ref[d, pl.ds(row, CHUNK), ls] = (egam[i] * f["qn"][:, ls]).astype(BF16)
            kh_ref[d, pl.ds(row, CHUNK), ls] = f["kn"][:, ls] * jnp.exp(g_end - gam_t[i])
            dg_ref[d, pl.ds(pl.multiple_of(row // 8, 8), 8), ls] = jnp.broadcast_to(jnp.exp(g_end), (8, PAIR_W))

    for seg_i, x_ref in enumerate(x_refs):
        def prep_body(c, carry, seg_i=seg_i):
            prep(seg_i, c)
            return carry

        lax.fori_loop(0, x_ref.shape[1] // (CHUNK * PREP_CHUNKS), prep_body, 0)

    st_ref[...] = jnp.zeros(st_ref.shape, F32)

    def scan_body(seg_i, i, nch):
        chains = [(d, p) for d in range(2) for p in range(npair)]
        rows = [row_off[seg_i] + pl.multiple_of((i if d == 0 else nch - 1 - i) * CHUNK, CHUNK) for d, _ in chains]
        lss = [slice(PAIR_W * p, PAIR_W * (p + 1)) for _, p in chains]
        st = [st_ref[d, p] for d, p in chains]
        stb = [s_.astype(BF16) for s_ in st]
        u = [sv_ref[d, pl.ds(r, CHUNK), ls] - jnp.dot(sk_ref[d, pl.ds(r, CHUNK), ls], sb, preferred_element_type=F32)
             for (d, _), r, ls, sb in zip(chains, rows, lss, stb)]
        oq = [jnp.dot(qe_ref[d, pl.ds(r, CHUNK), ls], sb, preferred_element_type=F32)
              for (d, _), r, ls, sb in zip(chains, rows, lss, stb)]
        ou = [jnp.dot(p_ref[d, pl.ds(r, CHUNK), ls], pk2(u_), preferred_element_type=F32)
              for (d, _), r, ls, u_ in zip(chains, rows, lss, u)]
        ku = [_dot(kh_ref[d, pl.ds(r, CHUNK), ls].T, u_) for (d, _), r, ls, u_ in zip(chains, rows, lss, u)]
        for j, (d, p) in enumerate(chains):
            dst = of_ref if d == 0 else ob_ref
            dst[pl.ds(rows[j], CHUNK), lss[j]] = oq[j] + ou[j]
            dgr = dg_ref[d, pl.ds(pl.multiple_of(rows[j] // 8, 8), 8), lss[j]][0:1, :]
            st_ref[d, p] = st[j] * dgr + jnp.where(bd2, ku[j], 0.0)

    for seg_i, x_ref in enumerate(x_refs):
        nch = x_ref.shape[1] // CHUNK

        def scan_iter(i, carry, seg_i=seg_i, nch=nch):
            scan_body(seg_i, i, nch)
            return carry

        lax.fori_loop(0, nch, scan_iter, 0)

    gain = gain_ref[...]
    blk = 256
    outs = ((oc_ref, xc_ref, 0), (ol_ref, xl_ref, lc))
    for o_ref, x_ref, off in outs:
        if o_ref is None:
            continue

        def fin(i, carry, o_ref=o_ref, x_ref=x_ref, off=off):
            r0 = pl.multiple_of(i * blk, blk)
            o = of_ref[pl.ds(off + r0, blk), :] + ob_ref[pl.ds(off + r0, blk), :]
            gate = x_ref[0, pl.ds(r0, blk), 3 * MIX_W:4 * MIX_W]
            o_ref[0, pl.ds(r0, blk), :] = _finish_rows(o, gate, gain, seg).astype(BF16)
            return carry

        lax.fori_loop(0, x_ref.shape[1] // blk, fin, 0)


def _gdn2(x_ctx, x_lat, s_ctx, s_lat, cw, alog, dtb, sel, selb, gain, seg, need_ctx):
    b, lc, _ = x_ctx.shape
    ll = x_lat.shape[1]
    lt = lc + ll
    in_specs = [pl.BlockSpec((1, lc, 1024), lambda i: (i, 0, 0)),
                pl.BlockSpec((1, ll, 1024), lambda i: (i, 0, 0)),
                pl.BlockSpec((1, lc, LANES), lambda i: (i, 0, 0)),
                pl.BlockSpec((1, ll, LANES), lambda i: (i, 0, 0)),
                pl.BlockSpec((8, 3 * MIX_W), lambda i: (0, 0)),
                pl.BlockSpec((2, 1, LANES), lambda i: (0, 0, 0)),
                pl.BlockSpec((2, 1, LANES), lambda i: (0, 0, 0)),
                pl.BlockSpec((2, LANES, MIX_W), lambda i: (0, 0, 0)),
                pl.BlockSpec((2, LANES, MIX_W), lambda i: (0, 0, 0)),
                pl.BlockSpec((1, MIX_W), lambda i: (0, 0)),
                pl.BlockSpec((LANES, LANES), lambda i: (0, 0))]
    out_shape = [jax.ShapeDtypeStruct((b, ll, MIX_W), BF16)]
    out_specs = [pl.BlockSpec((1, ll, MIX_W), lambda i: (i, 0, 0))]
    if need_ctx:
        out_shape.insert(0, jax.ShapeDtypeStruct((b, lc, MIX_W), BF16))
        out_specs.insert(0, pl.BlockSpec((1, lc, MIX_W), lambda i: (i, 0, 0)))
    res = pl.pallas_call(
        functools.partial(_gdn2_kernel, need_ctx=need_ctx),
        grid=(b,), in_specs=in_specs, out_specs=out_specs, out_shape=out_shape,
        scratch_shapes=[pltpu.VMEM((2, lt, MIX_W), BF16), pltpu.VMEM((2, lt, MIX_W), BF16),
                        pltpu.VMEM((2, lt, MIX_W), BF16), pltpu.VMEM((2, lt, MIX_W), F32),
                        pltpu.VMEM((2, lt, MIX_W), F32), pltpu.VMEM((2, lt // 8, MIX_W), F32),
                        pltpu.VMEM((lt, MIX_W), F32), pltpu.VMEM((lt, MIX_W), F32),
                        pltpu.VMEM((2, MIX_W // PAIR_W, PAIR_W, PAIR_W), F32)],
        compiler_params=_cparams(("parallel",)),
        name="gdn_scan",
    )(x_ctx, x_lat, s_ctx, s_lat, cw, alog, dtb, sel, selb, gain, seg)
    return (res[0], res[1]) if need_ctx else (None, res[0])


def _outproj_kernel(gla_ref, gdn_ref, att_ref, h_ref, g_ref, w_ref, o_ref):
    y = (jnp.dot(gla_ref[...], w_ref[0:256, :], preferred_element_type=F32)
         + jnp.dot(gdn_ref[...], w_ref[256:512, :], preferred_element_type=F32)
         + jnp.dot(att_ref[...], w_ref[512:1024, :], preferred_element_type=F32))
    o_ref[...] = h_ref[...] + g_ref[0] * y


def _outproj(gla, gdn, att, h2d, mod144, mod_row_fn, w, seq_len, tm):
    rows, d = h2d.shape
    tiles_per_seq = seq_len // tm
    return pl.pallas_call(
        _outproj_kernel,
        grid=(rows // tm,),
        in_specs=[pl.BlockSpec((tm, 256), lambda i: (i, 0)),
                  pl.BlockSpec((tm, 256), lambda i: (i, 0)),
                  pl.BlockSpec((tm, 512), lambda i: (i, 0)),
                  pl.BlockSpec((tm, d), lambda i: (i, 0)),
                  pl.BlockSpec((1, 1, d), lambda i: (mod_row_fn(i // tiles_per_seq) * 6 + 2, 0, 0)),
                  pl.BlockSpec((d, d), lambda i: (0, 0))],
        out_specs=pl.BlockSpec((tm, d), lambda i: (i, 0)),
        out_shape=jax.ShapeDtypeStruct((rows, d), F32),
        compiler_params=_cparams(("parallel",)),
        name="outproj",
    )(gla, gdn, att, h2d, mod144, w)


def _norm_mod(h_ref, sh_ref, sc_ref, gain_ref):
    x = h_ref[...]
    ms = jnp.mean(x * x, axis=-1, keepdims=True)
    return x * lax.rsqrt(ms + EPS) * (gain_ref[...] * (1.0 + sc_ref[0])) + sh_ref[0]


def _ffn_kernel(h_ref, sh_ref, sc_ref, g_ref, gain_ref, wg_ref, wu_ref, wd_ref, o_ref, b_scr, acc_scr, *, nf):
    f = pl.program_id(1)

    @pl.when(f == 0)
    def _():
        b_scr[...] = _norm_mod(h_ref, sh_ref, sc_ref, gain_ref).astype(BF16)
        acc_scr[...] = jnp.zeros(acc_scr.shape, F32)

    b = b_scr[...]
    gg = jnp.dot(b, wg_ref[...], preferred_element_type=F32)
    uu = jnp.dot(b, wu_ref[...], preferred_element_type=F32)
    hid = (_silu(gg) * uu).astype(BF16)
    acc_scr[...] += jnp.dot(hid, wd_ref[...], preferred_element_type=F32)

    @pl.when(f == nf - 1)
    def _():
        o_ref[...] = h_ref[...] + g_ref[0] * acc_scr[...]


def _ffn(h2d, mod144, mod_row_fn, gain, w_gu, w_down, seq_len, tm, tf):
    rows, d = h2d.shape
    dff = w_down.shape[0]
    nf = dff // tf
    tiles_per_seq = seq_len // tm

    def mod_spec(k):
        return pl.BlockSpec((1, 1, d), lambda i, f: (mod_row_fn(i // tiles_per_seq) * 6 + k, 0, 0))

    return pl.pallas_call(
        functools.partial(_ffn_kernel, nf=nf),
        grid=(rows // tm, nf),
        in_specs=[pl.BlockSpec((tm, d), lambda i, f: (i, 0)),
                  mod_spec(3), mod_spec(4), mod_spec(5),
                  pl.BlockSpec((1, d), lambda i, f: (0, 0)),
                  pl.BlockSpec((d, tf), lambda i, f: (0, f)),
                  pl.BlockSpec((d, tf), lambda i, f: (0, nf + f)),
                  pl.BlockSpec((tf, d), lambda i, f: (f, 0))],
        out_specs=pl.BlockSpec((tm, d), lambda i, f: (i, 0)),
        out_shape=jax.ShapeDtypeStruct((rows, d), F32),
        scratch_shapes=[pltpu.VMEM((tm, d), BF16), pltpu.VMEM((tm, d), F32)],
        compiler_params=_cparams(("parallel", "arbitrary")),
        name="ffn",
    )(h2d, mod144, mod144, mod144, gain, w_gu, w_gu, w_down)


def _moe_kernel(h_ref, sh_ref, sc_ref, g_ref, gain_ref, wr_ref, br_ref, wg_ref, wu_ref, wd_ref, o_ref,
                b_scr, comb_scr, acc_scr, *, nf):
    e = pl.program_id(1)
    f = pl.program_id(2)
    lane = _iota((1, LANES), 1)
    lane_f = lane.astype(F32)

    @pl.when((e == 0) & (f == 0))
    def _():
        b = _norm_mod(h_ref, sh_ref, sc_ref, gain_ref)
        b_scr[...] = b.astype(BF16)
        acc_scr[...] = jnp.zeros(acc_scr.shape, F32)
        logits = _dot3(b, wr_ref[...]) + br_ref[...]
        logits = jnp.where(lane < N_EXPERTS, logits, -jnp.inf)
        m1 = jnp.max(logits, axis=-1, keepdims=True)
        i1 = jnp.min(jnp.where(logits == m1, lane_f, float(LANES)), axis=-1, keepdims=True)
        rest = jnp.where(lane_f == i1, -jnp.inf, logits)
        m2 = jnp.max(rest, axis=-1, keepdims=True)
        i2 = jnp.min(jnp.where(rest == m2, lane_f, float(LANES)), axis=-1, keepdims=True)
        t = jnp.exp(m2 - m1)
        w1 = 1.0 / (1.0 + t)
        comb_scr[...] = jnp.where(lane_f == i1, w1, 0.0) + jnp.where(lane_f == i2, t * w1, 0.0)

    b = b_scr[...]
    gg = jnp.dot(b, wg_ref[0], preferred_element_type=F32)
    uu = jnp.dot(b, wu_ref[0], preferred_element_type=F32)
    hid = (_silu(gg) * uu).astype(BF16)
    w_e = jnp.sum(jnp.where(lane == e, comb_scr[...], 0.0), axis=-1, keepdims=True)
    acc_scr[...] += w_e * jnp.dot(hid, wd_ref[0], preferred_element_type=F32)

    @pl.when((e == N_EXPERTS - 1) & (f == nf - 1))
    def _():
        o_ref[...] = h_ref[...] + g_ref[0] * acc_scr[...]


def _moe(h2d, mod144, mod_row_fn, gain, w_router, b_router, w_gu, w_down, seq_len, tm, tf):
    rows, d = h2d.shape
    ne, dff, _ = w_down.shape
    nf = dff // tf
    tiles_per_seq = seq_len // tm

    def mod_spec(k):
        return pl.BlockSpec((1, 1, d), lambda i, e, f: (mod_row_fn(i // tiles_per_seq) * 6 + k, 0, 0))

    return pl.pallas_call(
        functools.partial(_moe_kernel, nf=nf),
        grid=(rows // tm, ne, nf),
        in_specs=[pl.BlockSpec((tm, d), lambda i, e, f: (i, 0)),
                  mod_spec(3), mod_spec(4), mod_spec(5),
                  pl.BlockSpec((1, d), lambda i, e, f: (0, 0)),
                  pl.BlockSpec((d, LANES), lambda i, e, f: (0, 0)),
                  pl.BlockSpec((1, LANES), lambda i, e, f: (0, 0)),
                  pl.BlockSpec((1, d, tf), lambda i, e, f: (e, 0, f)),
                  pl.BlockSpec((1, d, tf), lambda i, e, f: (e, 0, nf + f)),
                  pl.BlockSpec((1, tf, d), lambda i, e, f: (e, f, 0))],
        out_specs=pl.BlockSpec((tm, d), lambda i, e, f: (i, 0)),
        out_shape=jax.ShapeDtypeStruct((rows, d), F32),
        scratch_shapes=[pltpu.VMEM((tm, d), BF16), pltpu.VMEM((tm, LANES), F32), pltpu.VMEM((tm, d), F32)],
        compiler_params=_cparams(("parallel", "arbitrary", "arbitrary")),
        name="moe",
    )(h2d, mod144, mod144, mod144, gain, w_router, b_router, w_gu, w_gu, w_down)


MOE_TILE = 512
COMBINE_TOKENS = 256


def _router_kernel(h_ref, sh_ref, sc_ref, gain_ref, wr_ref, br_ref, b_ref, route_ref):
    lane = _iota((1, LANES), 1)
    lane_f = lane.astype(F32)
    b = _norm_mod(h_ref, sh_ref, sc_ref, gain_ref)
    b_ref[...] = b
    logits = _dot3(b, wr_ref[...]) + br_ref[...]
    logits = jnp.where(lane < N_EXPERTS, logits, -jnp.inf)
    m1 = jnp.max(logits, axis=-1, keepdims=True)
    i1 = jnp.min(jnp.where(logits == m1, lane_f, float(LANES)), axis=-1, keepdims=True)
    rest = jnp.where(lane_f == i1, -jnp.inf, logits)
    m2 = jnp.max(rest, axis=-1, keepdims=True)
    i2 = jnp.min(jnp.where(rest == m2, lane_f, float(LANES)), axis=-1, keepdims=True)
    t = jnp.exp(m2 - m1)
    w1 = 1.0 / (1.0 + t)
    route_ref[...] = (jnp.where(lane == 0, i1, 0.0) + jnp.where(lane == 1, i2, 0.0)
                      + jnp.where(lane == 2, w1, 0.0) + jnp.where(lane == 3, t * w1, 0.0))


def _router(h2d, mod144, mod_row_fn, gain, w_router, b_router, seq_len, tm):
    rows, d = h2d.shape
    tiles_per_seq = seq_len // tm

    def mod_spec(k):
        return pl.BlockSpec((1, 1, d), lambda i: (mod_row_fn(i // tiles_per_seq) * 6 + k, 0, 0))

    return pl.pallas_call(
        _router_kernel,
        grid=(rows // tm,),
        in_specs=[pl.BlockSpec((tm, d), lambda i: (i, 0)), mod_spec(3), mod_spec(4),
                  pl.BlockSpec((1, d), lambda i: (0, 0)),
                  pl.BlockSpec((d, LANES), lambda i: (0, 0)),
                  pl.BlockSpec((1, LANES), lambda i: (0, 0))],
        out_specs=(pl.BlockSpec((tm, d), lambda i: (i, 0)), pl.BlockSpec((tm, LANES), lambda i: (i, 0))),
        out_shape=(jax.ShapeDtypeStruct((rows, d), F32), jax.ShapeDtypeStruct((rows, LANES), F32)),
        compiler_params=_cparams(("parallel",)),
        name="moe_router",
    )(h2d, mod144, mod144, gain, w_router, b_router)


def _moe_group_kernel(te_ref, idx_ref, idx_next_ref, b_ref, wg_ref, wu_ref, wd_ref, ys_ref,
                      xbuf, sems, xb_scr, acc_scr, *, nf):
    i = pl.program_id(0)
    f = pl.program_id(1)
    last_tile = pl.num_programs(0) - 1
    slot = i % 2
    share = MOE_TILE // nf

    def whole_tile_copy(s):
        return pltpu.make_async_copy(b_ref.at[pl.ds(0, MOE_TILE)], xbuf.at[s], sems.at[s])

    @pl.when((f == 0) & (i == 0))
    def _():
        def issue(r, carry):
            pltpu.make_async_copy(b_ref.at[pl.ds(idx_ref[0, 0, r], 1)], xbuf.at[0, pl.ds(r, 1)], sems.at[0]).start()
            return carry

        lax.fori_loop(0, MOE_TILE, issue, 0, unroll=8)

    @pl.when(f == 0)
    def _():
        whole_tile_copy(slot).wait()
        xb_scr[...] = xbuf[slot].astype(BF16)

    for r in range(share):
        row = f * share + r
        pltpu.make_async_copy(b_ref.at[pl.ds(idx_next_ref[0, 0, row], 1)], xbuf.at[1 - slot, pl.ds(row, 1)],
                              sems.at[1 - slot]).start(priority=r % 2)
    x = xb_scr[...]
    gg = jnp.dot(x, wg_ref[0], preferred_element_type=F32)
    uu = jnp.dot(x, wu_ref[0], preferred_element_type=F32)
    hid = (_silu(gg) * uu).astype(BF16)
    part = jnp.dot(hid, wd_ref[0], preferred_element_type=F32)

    @pl.when(f == 0)
    def _():
        acc_scr[...] = part

    @pl.when(f > 0)
    def _():
        acc_scr[...] += part

    @pl.when(f == nf - 1)
    def _():
        ys_ref[...] = acc_scr[...]

    @pl.when((i == last_tile) & (f == nf - 1))
    def _():
        whole_tile_copy(1 - slot).wait()


def _moe_group(tile_expert, tok_sorted, b, w_gu, w_down, tf):
    p = tok_sorted.shape[0]
    d = b.shape[1]
    dff = w_down.shape[1]
    nf = dff // tf
    n_tiles = p // MOE_TILE
    idx3 = tok_sorted.reshape(n_tiles, 1, MOE_TILE)
    grid_spec = pltpu.PrefetchScalarGridSpec(
        num_scalar_prefetch=1,
        grid=(n_tiles, nf),
        in_specs=[pl.BlockSpec((1, 1, MOE_TILE), lambda i, f, te: (i, 0, 0), memory_space=pltpu.SMEM),
                  pl.BlockSpec((1, 1, MOE_TILE), lambda i, f, te: (jnp.minimum(i + 1, n_tiles - 1), 0, 0),
                               memory_space=pltpu.SMEM),
                  pl.BlockSpec(memory_space=pl.ANY),
                  pl.BlockSpec((1, d, tf), lambda i, f, te: (te[i], 0, f)),
                  pl.BlockSpec((1, d, tf), lambda i, f, te: (te[i], 0, nf + f)),
                  pl.BlockSpec((1, tf, d), lambda i, f, te: (te[i], f, 0))],
        out_specs=pl.BlockSpec((MOE_TILE, d), lambda i, f, te: (i, 0)),
        scratch_shapes=[pltpu.VMEM((2, MOE_TILE, d), F32), pltpu.SemaphoreType.DMA((2,)),
                        pltpu.VMEM((MOE_TILE, d), BF16), pltpu.VMEM((MOE_TILE, d), F32)])
    return pl.pallas_call(
        functools.partial(_moe_group_kernel, nf=nf),
        grid_spec=grid_spec,
        out_shape=jax.ShapeDtypeStruct((p, d), F32),
        compiler_params=_cparams(("arbitrary", "arbitrary")),
        name="moe_experts",
    )(tile_expert, idx3, idx3, b, w_gu, w_gu, w_down)


def _moe_combine_kernel(pos_ref, ys_ref, h_ref, route_ref, g_ref, o_ref, buf, sem):
    n = 2 * COMBINE_TOKENS

    for r in range(n):
        pltpu.make_async_copy(ys_ref.at[pl.ds(pos_ref[0, 0, r], 1)], buf.at[pl.ds(r, 1)], sem).start(priority=r % 2)
    pltpu.make_async_copy(ys_ref.at[pl.ds(0, n)], buf, sem).wait()
    w1 = route_ref[:, 2:3]
    w2 = route_ref[:, 3:4]
    o_ref[...] = h_ref[...] + g_ref[0] * (w1 * buf[0:COMBINE_TOKENS, :] + w2 * buf[COMBINE_TOKENS:n, :])


def _moe_combine(pos, ys, h2d, route, mod144, mod_row_fn, seq_len):
    rows, d = h2d.shape
    tm = COMBINE_TOKENS
    tiles_per_seq = seq_len // tm
    steps = rows // tm
    return pl.pallas_call(
        _moe_combine_kernel,
        grid=(steps,),
        in_specs=[pl.BlockSpec((1, 1, 2 * tm), lambda i: (i, 0, 0), memory_space=pltpu.SMEM),
                  pl.BlockSpec(memory_space=pl.ANY),
                  pl.BlockSpec((tm, d), lambda i: (i, 0)),
                  pl.BlockSpec((tm, LANES), lambda i: (i, 0)),
                  pl.BlockSpec((1, 1, d), lambda i: (mod_row_fn(i // tiles_per_seq) * 6 + 5, 0, 0))],
        out_specs=pl.BlockSpec((tm, d), lambda i: (i, 0)),
        out_shape=jax.ShapeDtypeStruct((rows, d), F32),
        scratch_shapes=[pltpu.VMEM((2 * tm, d), F32), pltpu.SemaphoreType.DMA(())],
        compiler_params=_cparams(("arbitrary",)),
        name="moe_combine",
    )(pos, ys, h2d, route, mod144)


def _moe_routed(h2d, mod144, mod_row_fn, gain, w_router, b_router, w_gu, w_down, seq_len):
    rows, d = h2d.shape
    b, route = _router(h2d, mod144, mod_row_fn, gain, w_router, b_router, seq_len, min(512, seq_len))
    ex = jnp.concatenate([route[:, 0], route[:, 1]]).astype(jnp.int32)
    tok = jnp.concatenate([jnp.arange(rows, dtype=jnp.int32)] * 2)
    onehot = (ex[:, None] == jnp.arange(N_EXPERTS, dtype=jnp.int32)[None, :]).astype(jnp.int32)
    rank = jnp.sum((jnp.cumsum(onehot, axis=0) - onehot) * onehot, axis=1)
    counts = jnp.sum(onehot, axis=0)
    padded = ((counts + MOE_TILE - 1) // MOE_TILE) * MOE_TILE
    ends = jnp.cumsum(padded)
    starts = ends - padded
    pos = jnp.sum(onehot * starts[None, :], axis=1) + rank
    p_rows = 2 * rows + N_EXPERTS * MOE_TILE
    tok_sorted = jnp.zeros((p_rows,), jnp.int32).at[pos].set(tok)
    tile_first = jnp.arange(p_rows // MOE_TILE, dtype=jnp.int32) * MOE_TILE
    tile_expert = jnp.minimum(jnp.sum((tile_first[:, None] >= ends[None, :]).astype(jnp.int32), axis=1),
                              N_EXPERTS - 1).astype(jnp.int32)
    ys = _moe_group(tile_expert, tok_sorted, b, w_gu, w_down, 1408)
    steps = rows // COMBINE_TOKENS
    pos2 = jnp.concatenate([pos[:rows].reshape(steps, 1, COMBINE_TOKENS),
                            pos[rows:].reshape(steps, 1, COMBINE_TOKENS)], axis=2).astype(jnp.int32)
    return _moe_combine(pos2, ys, h2d, route, mod144, mod_row_fn, seq_len)


ATT_HEAD_ORDER = (0, 4, 1, 5, 2, 6, 3, 7)


def _layout_w_in(w):
    gla = w[:, 0:1024]
    glow = w[:, 1024:1056]
    gdn = w[:, 1056:2080]
    ab = w[:, 2080:2096]
    q = jnp.concatenate([w[:, 2096 + HEAD_DIM * h:2096 + HEAD_DIM * (h + 1)] for h in ATT_HEAD_ORDER], axis=1)
    kv = w[:, 2608:2864]
    pad = jnp.zeros((w.shape[0], LANES - 48), w.dtype)
    return jnp.concatenate([gla, gdn, q, kv, glow, ab, pad], axis=1).astype(BF16)


def _layout_w_out(w):
    att = [w[512 + HEAD_DIM * h:512 + HEAD_DIM * (h + 1)] for h in ATT_HEAD_ORDER]
    return jnp.concatenate([w[0:512]] + att, axis=0).astype(BF16)


def _rope_tables(seq_len):
    rows = seq_len // GRID_W
    row = jnp.repeat(jnp.arange(rows), GRID_W).astype(F32)
    col = jnp.tile(jnp.arange(GRID_W), rows).astype(F32)
    inv_freq = ROPE_THETA ** (-jnp.arange(0, HEAD_DIM // 2, 2, dtype=F32) / (HEAD_DIM // 2))
    ar = row[:, None] * inv_freq
    ac = col[:, None] * inv_freq
    cos = jnp.concatenate([jnp.cos(ar), jnp.cos(ar), jnp.cos(ac), jnp.cos(ac)], axis=-1)
    sin = jnp.concatenate([-jnp.sin(ar), jnp.sin(ar), -jnp.sin(ac), jnp.sin(ac)], axis=-1)
    return jnp.tile(cos, (1, 2)), jnp.tile(sin, (1, 2))


def _seg_matrix():
    i = np.arange(LANES)
    return jnp.asarray((i[:, None] // HEAD_DIM) == (i[None, :] // HEAD_DIM), dtype=BF16)


def _gdn_select():
    sel = np.zeros((2, LANES, MIX_W), np.float32)
    selb = np.zeros((2, LANES, MIX_W), np.float32)
    for d in range(2):
        for h in range(GDN_HEADS):
            sel[d, 32 + GDN_HEADS * d + h, HEAD_DIM * h:HEAD_DIM * (h + 1)] = 1.0
            selb[d, 40 + GDN_HEADS * d + h, HEAD_DIM * h:HEAD_DIM * (h + 1)] = 1.0
    return jnp.asarray(sel, BF16), jnp.asarray(selb, BF16)


def _lane_rows(vals, base):
    out = jnp.zeros((2, 1, LANES), F32)
    for d in range(2):
        out = out.at[d, 0, base + GDN_HEADS * d:base + GDN_HEADS * (d + 1)].set(vals[d].astype(F32))
    return out


def kernel(x, c, ctx, c_ctx, w_mod, b_mod, norm_mix, norm_ffn, w_in, gla_gate_up, gla_gate_bias, gla_out_gain,
           gdn_conv, gdn_a_log, gdn_dt_bias, gdn_out_gain, att_q_gain, att_k_gain, w_out, ffn_gate_up, ffn_down,
           moe_router, moe_router_bias, moe_gate_up, moe_down):
    bsz, seq, d = x.shape
    lctx = ctx.shape[1]
    depth = w_mod.shape[0]
    ctx_row = bsz

    mod_rows = ((bsz + 1 + 7) // 8) * 8
    cvec = jnp.concatenate([c, c_ctx[None, :], jnp.zeros((mod_rows - bsz - 1, d), F32)], axis=0)
    mods = _modulation(cvec, w_mod, b_mod)

    seg = _seg_matrix()
    tables = _rope_tables(seq)
    sel, selb = _gdn_select()
    lat_row = lambda b: b
    ctx_row_fn = lambda b: ctx_row

    h_lat = x.reshape(bsz * seq, d)
    h_ctx = ctx.reshape(bsz * lctx, d)
    for layer in range(depth):
        need_ctx = layer < depth - 1
        mod144 = mods[layer].reshape(mod_rows * 6, 1, d)
        w_p = _layout_w_in(w_in[layer])
        w_o = _layout_w_out(w_out[layer])
        hg = jnp.concatenate([jnp.tile(att_q_gain[layer], ATT_Q_HEADS) * (HEAD_DIM ** -0.5 * LOG2E),
                              jnp.tile(att_k_gain[layer], ATT_KV_HEADS)])[None, :].astype(F32)
        gain_mix = norm_mix[layer][None, :]
        gain_ffn = norm_ffn[layer][None, :]

        gla_l, gdn_l, q_l, kv_l, sm_l = _inproj(h_lat, mod144, lat_row, gain_mix, w_p, hg, seg, tables, seq, 256)
        gla_c, gdn_c, q_c, kv_c, sm_c = _inproj(h_ctx, mod144, ctx_row_fn, gain_mix, w_p, hg, seg, None, lctx, 256)

        r3 = lambda t, n: t.reshape(bsz, n, t.shape[-1])
        wg = jnp.zeros((2, LANES, MIX_W), F32)
        for dd in range(2):
            wg = wg.at[dd, GLA_GATE_RANK * dd:GLA_GATE_RANK * (dd + 1), :].set(gla_gate_up[layer, dd].astype(F32))
        bg = gla_gate_bias[layer].reshape(2, 1, MIX_W).astype(F32)
        gla_gain = jnp.tile(gla_out_gain[layer], GLA_HEADS)[None, :].astype(F32)
        o_gla_c, o_gla_l = _gla(r3(gla_c, lctx), r3(gla_l, seq), r3(sm_c, lctx), r3(sm_l, seq),
                                wg, bg, gla_gain, seg, need_ctx)

        cw = jnp.concatenate([gdn_conv[layer].astype(F32), jnp.zeros((8 - SHORT_CONV, 3 * MIX_W), F32)], axis=0)
        alog = _lane_rows(gdn_a_log[layer], 32)
        dtb = _lane_rows(gdn_dt_bias[layer], 32)
        gdn_gain = jnp.tile(gdn_out_gain[layer], GDN_HEADS)[None, :].astype(F32)
        o_gdn_c, o_gdn_l = _gdn2(r3(gdn_c, lctx), r3(gdn_l, seq), r3(sm_c, lctx), r3(sm_l, seq),
                                cw, alog, dtb, sel, selb, gdn_gain, seg, need_ctx)

        o_att_l = _attention(r3(q_l, seq), [r3(kv_l, seq), r3(kv_c, lctx)], 256)
        h_lat = _outproj(o_gla_l.reshape(-1, MIX_W), o_gdn_l.reshape(-1, MIX_W), o_att_l.reshape(-1, 512),
                         h_lat, mod144, lat_row, w_o, seq, 512)
        if need_ctx:
            o_att_c = _attention(r3(q_c, lctx), [r3(kv_c, lctx)], 128)
            h_ctx = _outproj(o_gla_c.reshape(-1, MIX_W), o_gdn_c.reshape(-1, MIX_W), o_att_c.reshape(-1, 512),
                             h_ctx, mod144, ctx_row_fn, w_o, lctx, 256)

        j = layer // 2
        if layer % 2 == 0:
            w_gu = ffn_gate_up[j].astype(BF16)
            w_dn = ffn_down[j].astype(BF16)
            h_lat = _ffn(h_lat, mod144, lat_row, gain_ffn, w_gu, w_dn, seq, 512, 1408)
            if need_ctx:
                h_ctx = _ffn(h_ctx, mod144, ctx_row_fn, gain_ffn, w_gu, w_dn, lctx, 256, 1408)
        else:
            w_gu = moe_gate_up[j].astype(BF16)
            w_dn = moe_down[j].astype(BF16)
            w_r = jnp.concatenate([moe_router[j].astype(F32), jnp.zeros((d, LANES - N_EXPERTS), F32)], axis=1)
            b_r = jnp.concatenate([moe_router_bias[j].astype(F32), jnp.zeros((LANES - N_EXPERTS,), F32)])[None, :]
            h_lat = _moe_routed(h_lat, mod144, lat_row, gain_ffn, w_r, b_r, w_gu, w_dn, seq)
            if need_ctx:
                h_ctx = _moe_routed(h_ctx, mod144, ctx_row_fn, gain_ffn, w_r, b_r, w_gu, w_dn, lctx)
    return h_lat.reshape(bsz, seq, d)
```

```python
import functools

import numpy as np
import jax
import jax.numpy as jnp
from jax import lax
from jax.experimental import pallas as pl
from jax.experimental.pallas import tpu as pltpu

F32 = jnp.float32
BF16 = jnp.bfloat16

GRID_W = 64
HEAD_DIM = 64
CHUNK = 64
SUB = 16
EPS = 1e-6
GLA_HEADS = 4
GLA_GATE_RANK = 16
GLA_TAU = 16.0
GDN_HEADS = 4
SHORT_CONV = 5
ATT_Q_HEADS = 8
ATT_KV_HEADS = 2
ROPE_THETA = 10000.0
N_EXPERTS = 8
MIX_W = GLA_HEADS * HEAD_DIM
LOG2E = 1.4426950408889634
EXP_CLAMP = 80.0

LANES = 128
V7X_VMEM_BYTES = 64 * 1024 * 1024
VMEM_LIMIT = V7X_VMEM_BYTES - 8 * 1024 * 1024


def _cparams(sem):
    return pltpu.CompilerParams(dimension_semantics=sem, vmem_limit_bytes=VMEM_LIMIT)


def _silu(x):
    return x / (1.0 + jnp.exp(-x))


def _sigmoid(x):
    return 1.0 / (1.0 + jnp.exp(-x))


def _softplus(x):
    return jnp.maximum(x, 0.0) + jnp.log(1.0 + jnp.exp(-jnp.abs(x)))


def _dot(a, b):
    return jnp.dot(a.astype(BF16), b.astype(BF16), preferred_element_type=F32)


def _dot_nt(a, b):
    return lax.dot_general(a.astype(BF16), b.astype(BF16), (((1,), (1,)), ((), ())),
                           preferred_element_type=F32)


def _split(x):
    hi = x.astype(BF16)
    lo = (x - hi.astype(F32)).astype(BF16)
    return hi, lo


def _dot_xhl(x, w):
    hi, lo = _split(x)
    w = w.astype(BF16)
    return (jnp.dot(hi, w, preferred_element_type=F32) + jnp.dot(lo, w, preferred_element_type=F32))


def _dot_whl(w, x):
    hi, lo = _split(x)
    w = w.astype(BF16)
    return (jnp.dot(w, hi, preferred_element_type=F32) + jnp.dot(w, lo, preferred_element_type=F32))


def _dot3(a, b):
    ah, al = _split(a)
    bh, bl = _split(b)
    return (jnp.dot(ah, bh, preferred_element_type=F32) + jnp.dot(ah, bl, preferred_element_type=F32)
            + jnp.dot(al, bh, preferred_element_type=F32))


def _seg_sum64(sq, seg):
    outs = []
    for j in range(sq.shape[1] // LANES):
        outs.append(_dot_xhl(sq[:, LANES * j:LANES * (j + 1)], seg))
    return outs[0] if len(outs) == 1 else jnp.concatenate(outs, axis=1)


def _iota(shape, dim):
    return lax.broadcasted_iota(jnp.int32, shape, dim)


def _mod_kernel(c_ref, w_ref, b_ref, o_ref):
    s = _silu(c_ref[...])
    o_ref[0] = _dot(s, w_ref[0]) + b_ref[0]


def _modulation(cvec, w_mod, b_mod):
    depth, d, n = w_mod.shape
    rows = cvec.shape[0]
    tn = 1536
    return pl.pallas_call(
        _mod_kernel,
        grid=(depth, n // tn),
        in_specs=[pl.BlockSpec((rows, d), lambda l, j: (0, 0)),
                  pl.BlockSpec((1, d, tn), lambda l, j: (l, 0, j)),
                  pl.BlockSpec((1, 1, tn), lambda l, j: (l, 0, j))],
        out_specs=pl.BlockSpec((1, rows, tn), lambda l, j: (l, 0, j)),
        out_shape=jax.ShapeDtypeStruct((depth, rows, n), F32),
        compiler_params=_cparams(("arbitrary", "arbitrary")),
        name="modulation",
    )(cvec, w_mod, b_mod.reshape(depth, 1, n))


def _swap16(n, lane):
    fwd = pltpu.roll(n, LANES - 16, 1)
    bwd = pltpu.roll(n, 16, 1)
    return jnp.where((lane % 32) < 16, fwd, bwd)


def _inproj_kernel(*refs, rope):
    if rope:
        (h_ref, sh_ref, sc_ref, gain_ref, w_ref, hg_ref, seg_ref, cos_ref, sin_ref,
         gla_ref, gdn_ref, q_ref, kv_ref, small_ref) = refs
    else:
        (h_ref, sh_ref, sc_ref, gain_ref, w_ref, hg_ref, seg_ref,
         gla_ref, gdn_ref, q_ref, kv_ref, small_ref) = refs
    x = h_ref[...]
    ms = jnp.mean(x * x, axis=-1, keepdims=True)
    a = x * lax.rsqrt(ms + EPS) * (gain_ref[...] * (1.0 + sc_ref[0])) + sh_ref[0]
    p = jnp.dot(a.astype(BF16), w_ref[...], preferred_element_type=F32)
    gla_ref[...] = p[:, 0:1024]
    gdn_ref[...] = p[:, 1024:2048]
    small_ref[...] = p[:, 2816:2944]
    seg = seg_ref[...]
    lane = _iota((1, LANES), 1)
    outs = []
    for j in range(5):
        t = p[:, 2048 + LANES * j:2048 + LANES * (j + 1)]
        ss = _dot_xhl(t * t, seg)
        n = t * lax.rsqrt(ss * (1.0 / HEAD_DIM) + EPS) * hg_ref[:, LANES * j:LANES * (j + 1)]
        if rope:
            n = n * cos_ref[...] + _swap16(n, lane) * sin_ref[...]
        outs.append(n)
    q_ref[...] = jnp.concatenate(outs[:4], axis=1).astype(BF16)
    kv_ref[...] = jnp.concatenate([outs[4], p[:, 2688:2816]], axis=1).astype(BF16)


def _inproj(h2d, mod144, mod_row_fn, gain, w_p, hg, seg, tables, seq_len, tm):
    rows, d = h2d.shape
    n_all = w_p.shape[1]
    rope = tables is not None
    tiles_per_seq = seq_len // tm
    in_specs = [pl.BlockSpec((tm, d), lambda i: (i, 0)),
                pl.BlockSpec((1, 1, d), lambda i: (mod_row_fn(i // tiles_per_seq) * 6 + 0, 0, 0)),
                pl.BlockSpec((1, 1, d), lambda i: (mod_row_fn(i // tiles_per_seq) * 6 + 1, 0, 0)),
                pl.BlockSpec((1, d), lambda i: (0, 0)),
                pl.BlockSpec((d, n_all), lambda i: (0, 0)),
                pl.BlockSpec((1, 640), lambda i: (0, 0)),
                pl.BlockSpec((LANES, LANES), lambda i: (0, 0))]
    args = [h2d, mod144, mod144, gain, w_p, hg, seg]
    if rope:
        in_specs += [pl.BlockSpec((tm, LANES), lambda i: (i % tiles_per_seq, 0)),
                     pl.BlockSpec((tm, LANES), lambda i: (i % tiles_per_seq, 0))]
        args += list(tables)
    out_shape = (jax.ShapeDtypeStruct((rows, 1024), F32), jax.ShapeDtypeStruct((rows, 1024), F32),
                 jax.ShapeDtypeStruct((rows, 512), BF16), jax.ShapeDtypeStruct((rows, 256), BF16),
                 jax.ShapeDtypeStruct((rows, LANES), F32))
    out_specs = (pl.BlockSpec((tm, 1024), lambda i: (i, 0)), pl.BlockSpec((tm, 1024), lambda i: (i, 0)),
                 pl.BlockSpec((tm, 512), lambda i: (i, 0)), pl.BlockSpec((tm, 256), lambda i: (i, 0)),
                 pl.BlockSpec((tm, LANES), lambda i: (i, 0)))
    return pl.pallas_call(
        functools.partial(_inproj_kernel, rope=rope),
        grid=(rows // tm,), in_specs=in_specs, out_specs=out_specs, out_shape=out_shape,
        compiler_params=_cparams(("parallel",)),
        name="inproj_rope" if rope else "inproj",
    )(*args)


ATT_CHUNKS_PER_DOT = 1


def _attn_kernel(*refs, nkv, tq):
    q_ref = refs[0]
    kv_refs = refs[1:1 + nkv]
    o_ref = refs[1 + nkv]
    q = q_ref[0]
    lane = _iota((1, LANES), 1)
    mlo = (lane < HEAD_DIM).astype(BF16)
    mhi = (lane >= HEAD_DIM).astype(BF16)
    kvs = [r[0] for r in kv_refs]
    outs = []
    for jj in range(0, 4, ATT_CHUNKS_PER_DOT):
        pieces = []
        for j in range(jj, jj + ATT_CHUNKS_PER_DOT):
            qc = q[:, LANES * j:LANES * (j + 1)]
            pieces += [qc * mlo, qc * mhi]
        q_all = jnp.concatenate(pieces, axis=0)
        ss = [lax.dot_general(q_all, kv[:, 0:LANES], (((1,), (1,)), ((), ())), preferred_element_type=F32)
              for kv in kvs]
        m = functools.reduce(jnp.maximum, [jnp.max(s, axis=-1, keepdims=True) for s in ss])
        ps = [jnp.exp2(s - m) for s in ss]
        l = functools.reduce(lambda a, b: a + b, [jnp.sum(p, axis=-1, keepdims=True) for p in ps])
        o = functools.reduce(lambda a, b: a + b,
                             [jnp.dot(p.astype(BF16), kv[:, LANES:2 * LANES], preferred_element_type=F32)
                              for p, kv in zip(ps, kvs)])
        o = o / l
        for j in range(ATT_CHUNKS_PER_DOT):
            outs.append(jnp.where(lane < HEAD_DIM, o[2 * j * tq:(2 * j + 1) * tq],
                                  o[(2 * j + 1) * tq:(2 * j + 2) * tq]))
    o_ref[0] = jnp.concatenate(outs, axis=1).astype(BF16)


def _attention(q, kvs, tq):
    b, lq, _ = q.shape
    in_specs = [pl.BlockSpec((1, tq, 512), lambda i, j: (i, j, 0))]
    for kv in kvs:
        in_specs.append(pl.BlockSpec((1, kv.shape[1], 256), lambda i, j: (i, 0, 0)))
    return pl.pallas_call(
        functools.partial(_attn_kernel, nkv=len(kvs), tq=tq),
        grid=(b, lq // tq), in_specs=in_specs,
        out_specs=pl.BlockSpec((1, tq, 512), lambda i, j: (i, j, 0)),
        out_shape=jax.ShapeDtypeStruct((b, lq, 512), BF16),
        compiler_params=_cparams(("parallel", "arbitrary")),
        name="attention",
    )(q, *kvs)


def _head_masks():
    lane = _iota((1, MIX_W), 1)
    return [(lane // HEAD_DIM == h).astype(F32) for h in range(GLA_HEADS)]


def _blockdiag_mask():
    r = _iota((MIX_W, MIX_W), 0) // HEAD_DIM
    c = _iota((MIX_W, MIX_W), 1) // HEAD_DIM
    return r == c


def _finish_rows(o, gate, gain, seg):
    ss = _seg_sum64(o * o, seg)
    return o * lax.rsqrt(ss * (1.0 / HEAD_DIM) + EPS) * gain * _silu(gate)


PAIR_W = 2 * HEAD_DIM
PREP_CHUNKS = 4


def _gla_kernel(*refs, need_ctx):
    if need_ctx:
        (xc_ref, xl_ref, sc_ref, sl_ref, wg_ref, bg_ref, gain_ref, seg_ref,
         oc_ref, ol_ref, of_ref, ob_ref, qh_ref, kh_ref, dg_ref, st_ref) = refs
    else:
        (xc_ref, xl_ref, sc_ref, sl_ref, wg_ref, bg_ref, gain_ref, seg_ref,
         ol_ref, of_ref, ob_ref, qh_ref, kh_ref, dg_ref, st_ref) = refs
        oc_ref = None
    lc = xc_ref.shape[1]
    x_refs = (xc_ref, xl_ref)
    s_refs = (sc_ref, sl_ref)
    row_off = (0, lc)
    o_refs = (of_ref, ob_ref)
    hmask = _head_masks()
    bd = _blockdiag_mask()
    ti = _iota((CHUNK, CHUNK), 0)
    si = _iota((CHUNK, CHUNK), 1)
    tri = [(si <= ti).astype(BF16), (si >= ti).astype(BF16)]
    rr = _iota((4 * CHUNK, CHUNK), 0)
    cc = _iota((4 * CHUNK, CHUNK), 1)
    t_of_row = (rr // (GLA_HEADS * SUB)) * SUB + rr % SUB
    causal = [cc <= t_of_row, cc >= t_of_row]
    nblk = CHUNK // SUB

    def prep(seg_i, c2):
        x_ref, s_ref = x_refs[seg_i], s_refs[seg_i]
        fs = []
        for kk in range(PREP_CHUNKS):
            r0 = pl.multiple_of((c2 * PREP_CHUNKS + kk) * CHUNK, CHUNK)
            fs.append(dict(row=row_off[seg_i] + r0,
                           q=x_ref[0, pl.ds(r0, CHUNK), 0:MIX_W] * (HEAD_DIM ** -0.5),
                           k=x_ref[0, pl.ds(r0, CHUNK), MIX_W:2 * MIX_W],
                           v=x_ref[0, pl.ds(r0, CHUNK), 2 * MIX_W:3 * MIX_W],
                           sm=s_ref[0, pl.ds(r0, CHUNK), :]))
        chains = [(f, d) for f in fs for d in range(2)]
        xg = [_dot3(f["sm"], wg_ref[d]) + bg_ref[d] for f, d in chains]
        g = [(jnp.minimum(x_, 0.0) - jnp.log(1.0 + jnp.exp(-jnp.abs(x_)))) * (1.0 / GLA_TAU) for x_ in xg]
        b = [_dot_whl(tri[d], g_) for (_, d), g_ in zip(chains, g)]
        pieces = []
        for i in range(nblk):
            row_pieces = []
            for (f, d), g_, b_ in zip(chains, g, b):
                e = SUB * i if d == 0 else SUB * i + SUB - 1
                bref = b_[e:e + 1, :] - g_[e:e + 1, :]
                kt = f["k"] * jnp.exp(jnp.minimum(bref - b_, EXP_CLAMP))
                qt = f["q"][SUB * i:SUB * (i + 1), :] * jnp.exp(b_[SUB * i:SUB * (i + 1), :] - bref)
                qs = jnp.concatenate([qt * hmask[h] for h in range(GLA_HEADS)], axis=0)
                row_pieces.append(_dot_nt(qs, kt))
            pieces.append(row_pieces)
        scores = [jnp.where(causal[d], jnp.concatenate([pieces[i][j] for i in range(nblk)], axis=0), 0.0)
                  for j, (_, d) in enumerate(chains)]
        r = [_dot(s_, f["v"]) for (f, _), s_ in zip(chains, scores)]
        for j, (f, d) in enumerate(chains):
            intra = []
            for i in range(nblk):
                acc = None
                for h in range(GLA_HEADS):
                    lo = (i * GLA_HEADS + h) * SUB
                    term = r[j][lo:lo + SUB, :] * hmask[h]
                    acc = term if acc is None else acc + term
                intra.append(acc)
            row = f["row"]
            e = CHUNK - 1 if d == 0 else 0
            b_end = b[j][e:e + 1, :]
            o_refs[d][pl.ds(row, CHUNK), :] = jnp.concatenate(intra, axis=0)
            qh_ref[d, pl.ds(row, CHUNK), :] = (f["q"] * jnp.exp(b[j])).astype(BF16)
            kh_ref[d, pl.ds(row, CHUNK), :] = (f["k"] * jnp.exp(b_end - b[j])).astype(BF16)
            dg_ref[d, pl.ds(pl.multiple_of(row // 8, 8), 8), :] = jnp.broadcast_to(jnp.exp(b_end), (8, MIX_W))

    for seg_i, x_ref in enumerate(x_refs):
        def prep_body(c, carry, seg_i=seg_i):
            prep(seg_i, c)
            return carry

        lax.fori_loop(0, x_ref.shape[1] // (CHUNK * PREP_CHUNKS), prep_body, 0)

    st_ref[...] = jnp.zeros(st_ref.shape, F32)

    def scan_body(seg_i, i, nch):
        x_ref = x_refs[seg_i]
        r0s = [pl.multiple_of((i if d == 0 else nch - 1 - i) * CHUNK, CHUNK) for d in range(2)]
        rows = [row_off[seg_i] + r0 for r0 in r0s]
        st = [st_ref[d] for d in range(2)]
        inter = [_dot_nt(qh_ref[d, pl.ds(rows[d], CHUNK), :], st[d]) for d in range(2)]
        upd = [jnp.dot(x_ref[0, pl.ds(r0s[d], CHUNK), 2 * MIX_W:3 * MIX_W].T.astype(BF16),
                       kh_ref[d, pl.ds(rows[d], CHUNK), :], preferred_element_type=F32) for d in range(2)]
        for d in range(2):
            o_refs[d][pl.ds(rows[d], CHUNK), :] += inter[d]
            dgr = dg_ref[d, pl.ds(pl.multiple_of(rows[d] // 8, 8), 8), :][0:1, :]
            st_ref[d] = st[d] * dgr + jnp.where(bd, upd[d], 0.0)

    for seg_i, x_ref in enumerate(x_refs):
        nch = x_ref.shape[1] // CHUNK

        def scan_iter(i, carry, seg_i=seg_i, nch=nch):
            scan_body(seg_i, i, nch)
            return carry

        lax.fori_loop(0, nch, scan_iter, 0)

    gain = gain_ref[...]
    seg = seg_ref[...]
    blk = 256
    outs = ((oc_ref, xc_ref, 0), (ol_ref, xl_ref, lc))
    for o_ref, x_ref, off in outs:
        if o_ref is None:
            continue

        def fin(i, carry, o_ref=o_ref, x_ref=x_ref, off=off):
            r0 = pl.multiple_of(i * blk, blk)
            o = of_ref[pl.ds(off + r0, blk), :] + ob_ref[pl.ds(off + r0, blk), :]
            gate = x_ref[0, pl.ds(r0, blk), 3 * MIX_W:4 * MIX_W]
            o_ref[0, pl.ds(r0, blk), :] = _finish_rows(o, gate, gain, seg).astype(BF16)
            return carry

        lax.fori_loop(0, x_ref.shape[1] // blk, fin, 0)


def _gla(x_ctx, x_lat, s_ctx, s_lat, wg, bg, gain, seg, need_ctx):
    b, lc, _ = x_ctx.shape
    ll = x_lat.shape[1]
    in_specs = [pl.BlockSpec((1, lc, 1024), lambda i: (i, 0, 0)),
                pl.BlockSpec((1, ll, 1024), lambda i: (i, 0, 0)),
                pl.BlockSpec((1, lc, LANES), lambda i: (i, 0, 0)),
                pl.BlockSpec((1, ll, LANES), lambda i: (i, 0, 0)),
                pl.BlockSpec((2, LANES, MIX_W), lambda i: (0, 0, 0)),
                pl.BlockSpec((2, 1, MIX_W), lambda i: (0, 0, 0)),
                pl.BlockSpec((1, MIX_W), lambda i: (0, 0)),
                pl.BlockSpec((LANES, LANES), lambda i: (0, 0))]
    out_shape = [jax.ShapeDtypeStruct((b, ll, MIX_W), BF16)]
    out_specs = [pl.BlockSpec((1, ll, MIX_W), lambda i: (i, 0, 0))]
    if need_ctx:
        out_shape.insert(0, jax.ShapeDtypeStruct((b, lc, MIX_W), BF16))
        out_specs.insert(0, pl.BlockSpec((1, lc, MIX_W), lambda i: (i, 0, 0)))
    res = pl.pallas_call(
        functools.partial(_gla_kernel, need_ctx=need_ctx),
        grid=(b,), in_specs=in_specs, out_specs=out_specs, out_shape=out_shape,
        scratch_shapes=[pltpu.VMEM((lc + ll, MIX_W), F32), pltpu.VMEM((lc + ll, MIX_W), F32),
                        pltpu.VMEM((2, lc + ll, MIX_W), BF16), pltpu.VMEM((2, lc + ll, MIX_W), BF16),
                        pltpu.VMEM((2, (lc + ll) // 8, MIX_W), F32),
                        pltpu.VMEM((2, MIX_W, MIX_W), F32)],
        compiler_params=_cparams(("parallel",)),
        name="gla_scan",
    )(x_ctx, x_lat, s_ctx, s_lat, wg, bg, gain, seg)
    return (res[0], res[1]) if need_ctx else (None, res[0])


def _gdn2_kernel(*refs, need_ctx):
    if need_ctx:
        (xc_ref, xl_ref, sc_ref, sl_ref, cw_ref, alog_ref, dtb_ref, sel_ref, selb_ref, gain_ref, seg_ref,
         oc_ref, ol_ref, sk_ref, p_ref, qe_ref, sv_ref, kh_ref, dg_ref, of_ref, ob_ref, st_ref) = refs
    else:
        (xc_ref, xl_ref, sc_ref, sl_ref, cw_ref, alog_ref, dtb_ref, sel_ref, selb_ref, gain_ref, seg_ref,
         ol_ref, sk_ref, p_ref, qe_ref, sv_ref, kh_ref, dg_ref, of_ref, ob_ref, st_ref) = refs
        oc_ref = None
    lc = xc_ref.shape[1]
    x_refs = (xc_ref, xl_ref)
    s_refs = (sc_ref, sl_ref)
    row_off = (0, lc)
    npair = MIX_W // PAIR_W
    seg = seg_ref[...]
    ti = _iota((CHUNK, CHUNK), 0)
    si = _iota((CHUNK, CHUNK), 1)
    tri = [(si <= ti).astype(BF16), (si >= ti).astype(BF16)]
    ones = jnp.ones((CHUNK, CHUNK), BF16)
    tp = _iota((CHUNK, PAIR_W), 0)
    sp = _iota((CHUNK, PAIR_W), 1) % CHUNK
    le = sp <= tp
    ge = sp >= tp
    m_tri = [le, ge]
    m_strict = [sp < tp, sp > tp]
    m_sum = [ge.astype(F32), le.astype(F32)]
    blk16 = (tp // SUB) == (sp // SUB)
    eye = (tp == sp).astype(F32)
    bd2 = (_iota((PAIR_W, PAIR_W), 0) // HEAD_DIM) == (_iota((PAIR_W, PAIR_W), 1) // HEAD_DIM)
    lane_p = _iota((1, PAIR_W), 1)
    hm2 = [(lane_p // HEAD_DIM == h).astype(F32) for h in range(2)]
    cw = cw_ref[...]

    def pk2(y):
        yb = y.astype(BF16)
        return jnp.where(bd2, jnp.concatenate([yb, yb], axis=0), jnp.zeros((), BF16))

    def mm(x, y):
        return jnp.dot(x.astype(BF16), pk2(y), preferred_element_type=F32)

    def front(seg_i, c):
        x_ref, s_ref = x_refs[seg_i], s_refs[seg_i]
        ln = x_ref.shape[1]
        nch = ln // CHUNK
        r0 = pl.multiple_of(c * CHUNK, CHUNK)
        center = x_ref[0, pl.ds(r0, CHUNK), 0:3 * MIX_W]
        p0 = pl.multiple_of(jnp.maximum(r0 - 8, 0), 8)
        n0 = pl.multiple_of(jnp.minimum(r0 + CHUNK, ln - 8), 8)
        prev = x_ref[0, pl.ds(p0, 8), 0:3 * MIX_W] * jnp.where(c > 0, 1.0, 0.0)
        nxt = x_ref[0, pl.ds(n0, 8), 0:3 * MIX_W] * jnp.where(c < nch - 1, 1.0, 0.0)
        ext = jnp.concatenate([prev, center, nxt], axis=0)
        pad = SHORT_CONV // 2
        acc = None
        for j in range(SHORT_CONV):
            term = ext[8 - pad + j:8 - pad + j + CHUNK, :] * cw[j:j + 1, :]
            acc = term if acc is None else acc + term
        y = _silu(acc)
        qk = y[:, 0:2 * MIX_W]
        qk = qk * lax.rsqrt(_seg_sum64(qk * qk, seg) + EPS)
        f = dict(row=row_off[seg_i] + r0, qn=qk[:, 0:MIX_W] * (HEAD_DIM ** -0.5), kn=qk[:, MIX_W:2 * MIX_W],
                 v=y[:, 2 * MIX_W:3 * MIX_W])
        sm = s_ref[0, pl.ds(r0, CHUNK), :]
        beta = _sigmoid(sm)
        f["kk"], f["qk"] = [], []
        for p in range(npair):
            ls = slice(PAIR_W * p, PAIR_W * (p + 1))
            kstack = jnp.concatenate([f["kn"][:, ls] * hm2[0], f["kn"][:, ls] * hm2[1]], axis=0)
            f["kk"].append(_dot_nt(f["kn"][:, ls], kstack))
            f["qk"].append(_dot_nt(f["qn"][:, ls], kstack))
        f["gexp"] = [_dot_xhl(-jnp.exp(alog_ref[d]) * _softplus(sm + dtb_ref[d]), sel_ref[d]) for d in range(2)]
        f["bexp"] = [_dot_xhl(beta, selb_ref[d]) for d in range(2)]
        f["gam"] = [_dot_whl(tri[d], f["gexp"][d]) for d in range(2)]
        return f

    def prep(seg_i, c2):
        fs = [front(seg_i, c2 * PREP_CHUNKS + k) for k in range(PREP_CHUNKS)]
        chains = [(f, d, p) for f in fs for d in range(2) for p in range(npair)]
        lss = [slice(PAIR_W * p, PAIR_W * (p + 1)) for _, _, p in chains]
        gam_t = [f["gam"][d][:, ls] for (f, d, _), ls in zip(chains, lss)]
        gam_s = [_dot_whl(ones, f["gexp"][d][:, ls] * m_sum[d]) for (f, d, _), ls in zip(chains, lss)]
        bx = [f["bexp"][d][:, ls] for (f, d, _), ls in zip(chains, lss)]
        dec = [jnp.where(m_tri[d], jnp.exp(jnp.minimum(gt - gs, 0.0)), 0.0)
               for (_, d, _), gt, gs in zip(chains, gam_t, gam_s)]
        a = [jnp.where(m_strict[d], b_ * dc * f["kk"][p], 0.0) for (f, d, p), b_, dc in zip(chains, bx, dec)]
        dgn = [jnp.where(blk16, a_, 0.0) for a_ in a]
        lo = [a_ - g_ for a_, g_ in zip(a, dgn)]
        d2 = [mm(g_, g_) for g_ in dgn]
        t1 = [mm(eye - g_, eye + s_) for g_, s_ in zip(dgn, d2)]
        d4 = [mm(s_, s_) for s_ in d2]
        t2 = [mm(t_, eye + s_) for t_, s_ in zip(t1, d4)]
        d8 = [mm(s_, s_) for s_ in d4]
        t_inv = [mm(t_, eye + s_) for t_, s_ in zip(t2, d8)]
        m = [mm(t_, l_) for t_, l_ in zip(t_inv, lo)]
        m2 = [mm(m_, m_) for m_ in m]
        w1 = [mm(eye - m_, eye + s_) for m_, s_ in zip(m, m2)]
        w = [mm(w_, t_) for w_, t_ in zip(w1, t_inv)]
        egam = [jnp.exp(gt) for gt in gam_t]
        solv = [mm(w_, b_ * f["v"][:, ls]) for (f, _, _), w_, b_, ls in zip(chains, w, bx, lss)]
        solk = [mm(w_, b_ * eg * f["kn"][:, ls]) for (f, _, _), w_, b_, eg, ls in zip(chains, w, bx, egam, lss)]
        for i, (f, d, p) in enumerate(chains):
            ls, row = lss[i], f["row"]
            e = CHUNK - 1 if d == 0 else 0
            g_end = gam_t[i][e:e + 1, :]
            sv_ref[d, pl.ds(row, CHUNK), ls] = solv[i]
            sk_ref[d, pl.ds(row, CHUNK), ls] = solk[i].astype(BF16)
            p_ref[d, pl.ds(row, CHUNK), ls] = jnp.where(m_tri[d], f["qk"][p] * dec[i], 0.0).astype(BF16)
            qe_ref[d, pl.ds(row, CHUNK), ls] = (egam[i] * f["qn"][:, ls]).astype(BF16)
            kh_ref[d, pl.ds(row, CHUNK), ls] = f["kn"][:, ls] * jnp.exp(g_end - gam_t[i])
            dg_ref[d, pl.ds(pl.multiple_of(row // 8, 8), 8), ls] = jnp.broadcast_to(jnp.exp(g_end), (8, PAIR_W))

    for seg_i, x_ref in enumerate(x_refs):
        def prep_body(c, carry, seg_i=seg_i):
            prep(seg_i, c)
            return carry

        lax.fori_loop(0, x_ref.shape[1] // (CHUNK * PREP_CHUNKS), prep_body, 0)

    st_ref[...] = jnp.zeros(st_ref.shape, F32)

    def scan_body(seg_i, i, nch):
        chains = [(d, p) for d in range(2) for p in range(npair)]
        rows = [row_off[seg_i] + pl.multiple_of((i if d == 0 else nch - 1 - i) * CHUNK, CHUNK) for d, _ in chains]
        lss = [slice(PAIR_W * p, PAIR_W * (p + 1)) for _, p in chains]
        st = [st_ref[d, p] for d, p in chains]
        stb = [s_.astype(BF16) for s_ in st]
        u = [sv_ref[d, pl.ds(r, CHUNK), ls] - jnp.dot(sk_ref[d, pl.ds(r, CHUNK), ls], sb, preferred_element_type=F32)
             for (d, _), r, ls, sb in zip(chains, rows, lss, stb)]
        oq = [jnp.dot(qe_ref[d, pl.ds(r, CHUNK), ls], sb, preferred_element_type=F32)
              for (d, _), r, ls, sb in zip(chains, rows, lss, stb)]
        ou = [jnp.dot(p_ref[d, pl.ds(r, CHUNK), ls], pk2(u_), preferred_element_type=F32)
              for (d, _), r, ls, u_ in zip(chains, rows, lss, u)]
        ku = [_dot(kh_ref[d, pl.ds(r, CHUNK), ls].T, u_) for (d, _), r, ls, u_ in zip(chains, rows, lss, u)]
        for j, (d, p) in enumerate(chains):
            dst = of_ref if d == 0 else ob_ref
            dst[pl.ds(rows[j], CHUNK), lss[j]] = oq[j] + ou[j]
            dgr = dg_ref[d, pl.ds(pl.multiple_of(rows[j] // 8, 8), 8), lss[j]][0:1, :]
            st_ref[d, p] = st[j] * dgr + jnp.where(bd2, ku[j], 0.0)

    for seg_i, x_ref in enumerate(x_refs):
        nch = x_ref.shape[1] // CHUNK

        def scan_iter(i, carry, seg_i=seg_i, nch=nch):
            scan_body(seg_i, i, nch)
            return carry

        lax.fori_loop(0, nch, scan_iter, 0)

    gain = gain_ref[...]
    blk = 256
    outs = ((oc_ref, xc_ref, 0), (ol_ref, xl_ref, lc))
    for o_ref, x_ref, off in outs:
        if o_ref is None:
            continue

        def fin(i, carry, o_ref=o_ref, x_ref=x_ref, off=off):
            r0 = pl.multiple_of(i * blk, blk)
            o = of_ref[pl.ds(off + r0, blk), :] + ob_ref[pl.ds(off + r0, blk), :]
            gate = x_ref[0, pl.ds(r0, blk), 3 * MIX_W:4 * MIX_W]
            o_ref[0, pl.ds(r0, blk), :] = _finish_rows(o, gate, gain, seg).astype(BF16)
            return carry

        lax.fori_loop(0, x_ref.shape[1] // blk, fin, 0)


def _gdn2(x_ctx, x_lat, s_ctx, s_lat, cw, alog, dtb, sel, selb, gain, seg, need_ctx):
    b, lc, _ = x_ctx.shape
    ll = x_lat.shape[1]
    lt = lc + ll
    in_specs = [pl.BlockSpec((1, lc, 1024), lambda i: (i, 0, 0)),
                pl.BlockSpec((1, ll, 1024), lambda i: (i, 0, 0)),
                pl.BlockSpec((1, lc, LANES), lambda i: (i, 0, 0)),
                pl.BlockSpec((1, ll, LANES), lambda i: (i, 0, 0)),
                pl.BlockSpec((8, 3 * MIX_W), lambda i: (0, 0)),
                pl.BlockSpec((2, 1, LANES), lambda i: (0, 0, 0)),
                pl.BlockSpec((2, 1, LANES), lambda i: (0, 0, 0)),
                pl.BlockSpec((2, LANES, MIX_W), lambda i: (0, 0, 0)),
                pl.BlockSpec((2, LANES, MIX_W), lambda i: (0, 0, 0)),
                pl.BlockSpec((1, MIX_W), lambda i: (0, 0)),
                pl.BlockSpec((LANES, LANES), lambda i: (0, 0))]
    out_shape = [jax.ShapeDtypeStruct((b, ll, MIX_W), BF16)]
    out_specs = [pl.BlockSpec((1, ll, MIX_W), lambda i: (i, 0, 0))]
    if need_ctx:
        out_shape.insert(0, jax.ShapeDtypeStruct((b, lc, MIX_W), BF16))
        out_specs.insert(0, pl.BlockSpec((1, lc, MIX_W), lambda i: (i, 0, 0)))
    res = pl.pallas_call(
        functools.partial(_gdn2_kernel, need_ctx=need_ctx),
        grid=(b,), in_specs=in_specs, out_specs=out_specs, out_shape=out_shape,
        scratch_shapes=[pltpu.VMEM((2, lt, MIX_W), BF16), pltpu.VMEM((2, lt, MIX_W), BF16),
                        pltpu.VMEM((2, lt, MIX_W), BF16), pltpu.VMEM((2, lt, MIX_W), F32),
                        pltpu.VMEM((2, lt, MIX_W), F32), pltpu.VMEM((2, lt // 8, MIX_W), F32),
                        pltpu.VMEM((lt, MIX_W), F32), pltpu.VMEM((lt, MIX_W), F32),
                        pltpu.VMEM((2, MIX_W // PAIR_W, PAIR_W, PAIR_W), F32)],
        compiler_params=_cparams(("parallel",)),
        name="gdn_scan",
    )(x_ctx, x_lat, s_ctx, s_lat, cw, alog, dtb, sel, selb, gain, seg)
    return (res[0], res[1]) if need_ctx else (None, res[0])


def _outproj_kernel(gla_ref, gdn_ref, att_ref, h_ref, g_ref, w_ref, o_ref):
    y = (jnp.dot(gla_ref[...], w_ref[0:256, :], preferred_element_type=F32)
         + jnp.dot(gdn_ref[...], w_ref[256:512, :], preferred_element_type=F32)
         + jnp.dot(att_ref[...], w_ref[512:1024, :], preferred_element_type=F32))
    o_ref[...] = h_ref[...] + g_ref[0] * y


def _outproj(gla, gdn, att, h2d, mod144, mod_row_fn, w, seq_len, tm):
    rows, d = h2d.shape
    tiles_per_seq = seq_len // tm
    return pl.pallas_call(
        _outproj_kernel,
        grid=(rows // tm,),
        in_specs=[pl.BlockSpec((tm, 256), lambda i: (i, 0)),
                  pl.BlockSpec((tm, 256), lambda i: (i, 0)),
                  pl.BlockSpec((tm, 512), lambda i: (i, 0)),
                  pl.BlockSpec((tm, d), lambda i: (i, 0)),
                  pl.BlockSpec((1, 1, d), lambda i: (mod_row_fn(i // tiles_per_seq) * 6 + 2, 0, 0)),
                  pl.BlockSpec((d, d), lambda i: (0, 0))],
        out_specs=pl.BlockSpec((tm, d), lambda i: (i, 0)),
        out_shape=jax.ShapeDtypeStruct((rows, d), F32),
        compiler_params=_cparams(("parallel",)),
        name="outproj",
    )(gla, gdn, att, h2d, mod144, w)


def _norm_mod(h_ref, sh_ref, sc_ref, gain_ref):
    x = h_ref[...]
    ms = jnp.mean(x * x, axis=-1, keepdims=True)
    return x * lax.rsqrt(ms + EPS) * (gain_ref[...] * (1.0 + sc_ref[0])) + sh_ref[0]


def _ffn_kernel(h_ref, sh_ref, sc_ref, g_ref, gain_ref, wg_ref, wu_ref, wd_ref, o_ref, b_scr, acc_scr, *, nf):
    f = pl.program_id(1)

    @pl.when(f == 0)
    def _():
        b_scr[...] = _norm_mod(h_ref, sh_ref, sc_ref, gain_ref).astype(BF16)
        acc_scr[...] = jnp.zeros(acc_scr.shape, F32)

    b = b_scr[...]
    gg = jnp.dot(b, wg_ref[...], preferred_element_type=F32)
    uu = jnp.dot(b, wu_ref[...], preferred_element_type=F32)
    hid = (_silu(gg) * uu).astype(BF16)
    acc_scr[...] += jnp.dot(hid, wd_ref[...], preferred_element_type=F32)

    @pl.when(f == nf - 1)
    def _():
        o_ref[...] = h_ref[...] + g_ref[0] * acc_scr[...]


def _ffn(h2d, mod144, mod_row_fn, gain, w_gu, w_down, seq_len, tm, tf):
    rows, d = h2d.shape
    dff = w_down.shape[0]
    nf = dff // tf
    tiles_per_seq = seq_len // tm

    def mod_spec(k):
        return pl.BlockSpec((1, 1, d), lambda i, f: (mod_row_fn(i // tiles_per_seq) * 6 + k, 0, 0))

    return pl.pallas_call(
        functools.partial(_ffn_kernel, nf=nf),
        grid=(rows // tm, nf),
        in_specs=[pl.BlockSpec((tm, d), lambda i, f: (i, 0)),
                  mod_spec(3), mod_spec(4), mod_spec(5),
                  pl.BlockSpec((1, d), lambda i, f: (0, 0)),
                  pl.BlockSpec((d, tf), lambda i, f: (0, f)),
                  pl.BlockSpec((d, tf), lambda i, f: (0, nf + f)),
                  pl.BlockSpec((tf, d), lambda i, f: (f, 0))],
        out_specs=pl.BlockSpec((tm, d), lambda i, f: (i, 0)),
        out_shape=jax.ShapeDtypeStruct((rows, d), F32),
        scratch_shapes=[pltpu.VMEM((tm, d), BF16), pltpu.VMEM((tm, d), F32)],
        compiler_params=_cparams(("parallel", "arbitrary")),
        name="ffn",
    )(h2d, mod144, mod144, mod144, gain, w_gu, w_gu, w_down)


MOE_TILE = 512
COMBINE_TOKENS = 256


def _router_kernel(h_ref, sh_ref, sc_ref, gain_ref, wr_ref, br_ref, b_ref, route_ref):
    lane = _iota((1, LANES), 1)
    lane_f = lane.astype(F32)
    b = _norm_mod(h_ref, sh_ref, sc_ref, gain_ref)
    b_ref[...] = b
    logits = _dot3(b, wr_ref[...]) + br_ref[...]
    logits = jnp.where(lane < N_EXPERTS, logits, -jnp.inf)
    m1 = jnp.max(logits, axis=-1, keepdims=True)
    i1 = jnp.min(jnp.where(logits == m1, lane_f, float(LANES)), axis=-1, keepdims=True)
    rest = jnp.where(lane_f == i1, -jnp.inf, logits)
    m2 = jnp.max(rest, axis=-1, keepdims=True)
    i2 = jnp.min(jnp.where(rest == m2, lane_f, float(LANES)), axis=-1, keepdims=True)
    t = jnp.exp(m2 - m1)
    w1 = 1.0 / (1.0 + t)
    route_ref[...] = (jnp.where(lane == 0, i1, 0.0) + jnp.where(lane == 1, i2, 0.0)
                      + jnp.where(lane == 2, w1, 0.0) + jnp.where(lane == 3, t * w1, 0.0))


def _router(h2d, mod144, mod_row_fn, gain, w_router, b_router, seq_len, tm):
    rows, d = h2d.shape
    tiles_per_seq = seq_len // tm

    def mod_spec(k):
        return pl.BlockSpec((1, 1, d), lambda i: (mod_row_fn(i // tiles_per_seq) * 6 + k, 0, 0))

    return pl.pallas_call(
        _router_kernel,
        grid=(rows // tm,),
        in_specs=[pl.BlockSpec((tm, d), lambda i: (i, 0)), mod_spec(3), mod_spec(4),
                  pl.BlockSpec((1, d), lambda i: (0, 0)),
                  pl.BlockSpec((d, LANES), lambda i: (0, 0)),
                  pl.BlockSpec((1, LANES), lambda i: (0, 0))],
        out_specs=(pl.BlockSpec((tm, d), lambda i: (i, 0)), pl.BlockSpec((tm, LANES), lambda i: (i, 0))),
        out_shape=(jax.ShapeDtypeStruct((rows, d), F32), jax.ShapeDtypeStruct((rows, LANES), F32)),
        compiler_params=_cparams(("parallel",)),
        name="moe_router",
    )(h2d, mod144, mod144, gain, w_router, b_router)


DISPATCH_TOKENS = 512


def _moe_dispatch_kernel(pos_ref, b_ref, xs_in_ref, xs_ref, sem):
    del xs_in_ref
    n = DISPATCH_TOKENS
    for r in range(2 * n):
        pltpu.make_async_copy(b_ref.at[pl.ds(r % n, 1)], xs_ref.at[pl.ds(pos_ref[0, 0, r], 1)],
                              sem).start(priority=r % 2)
    for _ in range(2):
        pltpu.make_async_copy(b_ref, xs_ref.at[pl.ds(0, n)], sem).wait()


def _moe_dispatch(pos2, b, p_rows):
    rows, d = b.shape
    tm = DISPATCH_TOKENS
    return pl.pallas_call(
        _moe_dispatch_kernel,
        grid=(rows // tm,),
        in_specs=[pl.BlockSpec((1, 1, 2 * tm), lambda i: (i, 0, 0), memory_space=pltpu.SMEM),
                  pl.BlockSpec((tm, d), lambda i: (i, 0)),
                  pl.BlockSpec(memory_space=pl.ANY)],
        out_specs=pl.BlockSpec(memory_space=pl.ANY),
        out_shape=jax.ShapeDtypeStruct((p_rows, d), F32),
        scratch_shapes=[pltpu.SemaphoreType.DMA(())],
        input_output_aliases={2: 0},
        compiler_params=_cparams(("arbitrary",)),
        name="moe_dispatch",
    )(pos2, b, jnp.zeros((p_rows, d), F32))


def _moe_group_kernel(te_ref, nt_ref, xs_ref, wg_ref, wu_ref, wd_ref, ys_ref, xb_scr, acc_scr, *, nf):
    i = pl.program_id(0)
    f = pl.program_id(1)
    used = i < nt_ref[0]

    @pl.when(used & (f == 0))
    def _():
        xb_scr[...] = xs_ref[...].astype(BF16)

    @pl.when(used)
    def _():
        x = xb_scr[...]
        gg = jnp.dot(x, wg_ref[0], preferred_element_type=F32)
        uu = jnp.dot(x, wu_ref[0], preferred_element_type=F32)
        hid = (_silu(gg) * uu).astype(BF16)
        part = jnp.dot(hid, wd_ref[0], preferred_element_type=F32)

        @pl.when(f == 0)
        def _():
            acc_scr[...] = part

        @pl.when(f > 0)
        def _():
            acc_scr[...] += part

    @pl.when(used & (f == nf - 1))
    def _():
        ys_ref[...] = acc_scr[...]

    @pl.when(jnp.logical_not(used) & (f == nf - 1))
    def _():
        ys_ref[...] = jnp.zeros(ys_ref.shape, F32)


def _moe_group(tile_expert, n_tiles_used, xs, w_gu, w_down, tf):
    p, d = xs.shape
    dff = w_down.shape[1]
    nf = dff // tf
    grid_spec = pltpu.PrefetchScalarGridSpec(
        num_scalar_prefetch=2,
        grid=(p // MOE_TILE, nf),
        in_specs=[pl.BlockSpec((MOE_TILE, d), lambda i, f, te, nt: (i, 0)),
                  pl.BlockSpec((1, d, tf), lambda i, f, te, nt: (te[i], 0, f)),
                  pl.BlockSpec((1, d, tf), lambda i, f, te, nt: (te[i], 0, nf + f)),
                  pl.BlockSpec((1, tf, d), lambda i, f, te, nt: (te[i], f, 0))],
        out_specs=pl.BlockSpec((MOE_TILE, d), lambda i, f, te, nt: (i, 0)),
        scratch_shapes=[pltpu.VMEM((MOE_TILE, d), BF16), pltpu.VMEM((MOE_TILE, d), F32)])
    return pl.pallas_call(
        functools.partial(_moe_group_kernel, nf=nf),
        grid_spec=grid_spec,
        out_shape=jax.ShapeDtypeStruct((p, d), F32),
        compiler_params=_cparams(("arbitrary", "arbitrary")),
        name="moe_experts",
    )(tile_expert, n_tiles_used, xs, w_gu, w_gu, w_down)


def _moe_combine_kernel(pos_ref, ys_ref, h_ref, route_ref, g_ref, o_ref, buf, sem):
    n = 2 * COMBINE_TOKENS

    for r in range(n):
        pltpu.make_async_copy(ys_ref.at[pl.ds(pos_ref[0, 0, r], 1)], buf.at[pl.ds(r, 1)], sem).start(priority=r % 2)
    pltpu.make_async_copy(ys_ref.at[pl.ds(0, n)], buf, sem).wait()
    w1 = route_ref[:, 2:3]
    w2 = route_ref[:, 3:4]
    o_ref[...] = h_ref[...] + g_ref[0] * (w1 * buf[0:COMBINE_TOKENS, :] + w2 * buf[COMBINE_TOKENS:n, :])


def _moe_combine(pos, ys, h2d, route, mod144, mod_row_fn, seq_len):
    rows, d = h2d.shape
    tm = COMBINE_TOKENS
    tiles_per_seq = seq_len // tm
    steps = rows // tm
    return pl.pallas_call(
        _moe_combine_kernel,
        grid=(steps,),
        in_specs=[pl.BlockSpec((1, 1, 2 * tm), lambda i: (i, 0, 0), memory_space=pltpu.SMEM),
                  pl.BlockSpec(memory_space=pl.ANY),
                  pl.BlockSpec((tm, d), lambda i: (i, 0)),
                  pl.BlockSpec((tm, LANES), lambda i: (i, 0)),
                  pl.BlockSpec((1, 1, d), lambda i: (mod_row_fn(i // tiles_per_seq) * 6 + 5, 0, 0))],
        out_specs=pl.BlockSpec((tm, d), lambda i: (i, 0)),
        out_shape=jax.ShapeDtypeStruct((rows, d), F32),
        scratch_shapes=[pltpu.VMEM((2 * tm, d), F32), pltpu.SemaphoreType.DMA(())],
        compiler_params=_cparams(("arbitrary",)),
        name="moe_combine",
    )(pos, ys, h2d, route, mod144)


def _moe_routed(h2d, mod144, mod_row_fn, gain, w_router, b_router, w_gu, w_down, seq_len):
    rows, d = h2d.shape
    b, route = _router(h2d, mod144, mod_row_fn, gain, w_router, b_router, seq_len, min(512, seq_len))
    ex = jnp.concatenate([route[:, 0], route[:, 1]]).astype(jnp.int32)
    onehot = (ex[:, None] == jnp.arange(N_EXPERTS, dtype=jnp.int32)[None, :]).astype(jnp.int32)
    rank = jnp.sum((jnp.cumsum(onehot, axis=0) - onehot) * onehot, axis=1)
    counts = jnp.sum(onehot, axis=0)
    padded = ((counts + MOE_TILE - 1) // MOE_TILE) * MOE_TILE
    ends = jnp.cumsum(padded)
    starts = ends - padded
    pos = jnp.sum(onehot * starts[None, :], axis=1) + rank
    pos = pos.astype(jnp.int32)
    p_rows = 2 * rows + N_EXPERTS * MOE_TILE
    tile_first = jnp.arange(p_rows // MOE_TILE, dtype=jnp.int32) * MOE_TILE
    tile_expert = jnp.minimum(jnp.sum((tile_first[:, None] >= ends[None, :]).astype(jnp.int32), axis=1),
                              N_EXPERTS - 1).astype(jnp.int32)
    n_tiles_used = (ends[-1:] // MOE_TILE).astype(jnp.int32)

    def per_tile(tm):
        return jnp.concatenate([pos[:rows].reshape(rows // tm, 1, tm), pos[rows:].reshape(rows // tm, 1, tm)], axis=2)

    xs = _moe_dispatch(per_tile(DISPATCH_TOKENS), b, p_rows)
    ys = _moe_group(tile_expert, n_tiles_used, xs, w_gu, w_down, 1408)
    return _moe_combine(per_tile(COMBINE_TOKENS), ys, h2d, route, mod144, mod_row_fn, seq_len)


ATT_HEAD_ORDER = (0, 4, 1, 5, 2, 6, 3, 7)


def _layout_w_in(w):
    gla = w[:, 0:1024]
    glow = w[:, 1024:1056]
    gdn = w[:, 1056:2080]
    ab = w[:, 2080:2096]
    q = jnp.concatenate([w[:, 2096 + HEAD_DIM * h:2096 + HEAD_DIM * (h + 1)] for h in ATT_HEAD_ORDER], axis=1)
    kv = w[:, 2608:2864]
    pad = jnp.zeros((w.shape[0], LANES - 48), w.dtype)
    return jnp.concatenate([gla, gdn, q, kv, glow, ab, pad], axis=1).astype(BF16)


def _layout_w_out(w):
    att = [w[512 + HEAD_DIM * h:512 + HEAD_DIM * (h + 1)] for h in ATT_HEAD_ORDER]
    return jnp.concatenate([w[0:512]] + att, axis=0).astype(BF16)


def _rope_tables(seq_len):
    rows = seq_len // GRID_W
    row = jnp.repeat(jnp.arange(rows), GRID_W).astype(F32)
    col = jnp.tile(jnp.arange(GRID_W), rows).astype(F32)
    inv_freq = ROPE_THETA ** (-jnp.arange(0, HEAD_DIM // 2, 2, dtype=F32) / (HEAD_DIM // 2))
    ar = row[:, None] * inv_freq
    ac = col[:, None] * inv_freq
    cos = jnp.concatenate([jnp.cos(ar), jnp.cos(ar), jnp.cos(ac), jnp.cos(ac)], axis=-1)
    sin = jnp.concatenate([-jnp.sin(ar), jnp.sin(ar), -jnp.sin(ac), jnp.sin(ac)], axis=-1)
    return jnp.tile(cos, (1, 2)), jnp.tile(sin, (1, 2))


def _seg_matrix():
    i = np.arange(LANES)
    return jnp.asarray((i[:, None] // HEAD_DIM) == (i[None, :] // HEAD_DIM), dtype=BF16)


def _gdn_select():
    sel = np.zeros((2, LANES, MIX_W), np.float32)
    selb = np.zeros((2, LANES, MIX_W), np.float32)
    for d in range(2):
        for h in range(GDN_HEADS):
            sel[d, 32 + GDN_HEADS * d + h, HEAD_DIM * h:HEAD_DIM * (h + 1)] = 1.0
            selb[d, 40 + GDN_HEADS * d + h, HEAD_DIM * h:HEAD_DIM * (h + 1)] = 1.0
    return jnp.asarray(sel, BF16), jnp.asarray(selb, BF16)


def _lane_rows(vals, base):
    out = jnp.zeros((2, 1, LANES), F32)
    for d in range(2):
        out = out.at[d, 0, base + GDN_HEADS * d:base + GDN_HEADS * (d + 1)].set(vals[d].astype(F32))
    return out


def kernel(x, c, ctx, c_ctx, w_mod, b_mod, norm_mix, norm_ffn, w_in, gla_gate_up, gla_gate_bias, gla_out_gain,
           gdn_conv, gdn_a_log, gdn_dt_bias, gdn_out_gain, att_q_gain, att_k_gain, w_out, ffn_gate_up, ffn_down,
           moe_router, moe_router_bias, moe_gate_up, moe_down):
    bsz, seq, d = x.shape
    lctx = ctx.shape[1]
    depth = w_mod.shape[0]
    ctx_row = bsz

    mod_rows = ((bsz + 1 + 7) // 8) * 8
    cvec = jnp.concatenate([c, c_ctx[None, :], jnp.zeros((mod_rows - bsz - 1, d), F32)], axis=0)
    mods = _modulation(cvec, w_mod, b_mod)

    seg = _seg_matrix()
    tables = _rope_tables(seq)
    sel, selb = _gdn_select()
    lat_row = lambda b: b
    ctx_row_fn = lambda b: ctx_row

    h_lat = x.reshape(bsz * seq, d)
    h_ctx = ctx.reshape(bsz * lctx, d)
    for layer in range(depth):
        need_ctx = layer < depth - 1
        mod144 = mods[layer].reshape(mod_rows * 6, 1, d)
        w_p = _layout_w_in(w_in[layer])
        w_o = _layout_w_out(w_out[layer])
        hg = jnp.concatenate([jnp.tile(att_q_gain[layer], ATT_Q_HEADS) * (HEAD_DIM ** -0.5 * LOG2E),
                              jnp.tile(att_k_gain[layer], ATT_KV_HEADS)])[None, :].astype(F32)
        gain_mix = norm_mix[layer][None, :]
        gain_ffn = norm_ffn[layer][None, :]

        gla_l, gdn_l, q_l, kv_l, sm_l = _inproj(h_lat, mod144, lat_row, gain_mix, w_p, hg, seg, tables, seq, 256)
        gla_c, gdn_c, q_c, kv_c, sm_c = _inproj(h_ctx, mod144, ctx_row_fn, gain_mix, w_p, hg, seg, None, lctx, 256)

        r3 = lambda t, n: t.reshape(bsz, n, t.shape[-1])
        wg = jnp.zeros((2, LANES, MIX_W), F32)
        for dd in range(2):
            wg = wg.at[dd, GLA_GATE_RANK * dd:GLA_GATE_RANK * (dd + 1), :].set(gla_gate_up[layer, dd].astype(F32))
        bg = gla_gate_bias[layer].reshape(2, 1, MIX_W).astype(F32)
        gla_gain = jnp.tile(gla_out_gain[layer], GLA_HEADS)[None, :].astype(F32)
        o_gla_c, o_gla_l = _gla(r3(gla_c, lctx), r3(gla_l, seq), r3(sm_c, lctx), r3(sm_l, seq),
                                wg, bg, gla_gain, seg, need_ctx)

        cw = jnp.concatenate([gdn_conv[layer].astype(F32), jnp.zeros((8 - SHORT_CONV, 3 * MIX_W), F32)], axis=0)
        alog = _lane_rows(gdn_a_log[layer], 32)
        dtb = _lane_rows(gdn_dt_bias[layer], 32)
        gdn_gain = jnp.tile(gdn_out_gain[layer], GDN_HEADS)[None, :].astype(F32)
        o_gdn_c, o_gdn_l = _gdn2(r3(gdn_c, lctx), r3(gdn_l, seq), r3(sm_c, lctx), r3(sm_l, seq),
                                cw, alog, dtb, sel, selb, gdn_gain, seg, need_ctx)

        o_att_l = _attention(r3(q_l, seq), [r3(kv_l, seq), r3(kv_c, lctx)], 256)
        h_lat = _outproj(o_gla_l.reshape(-1, MIX_W), o_gdn_l.reshape(-1, MIX_W), o_att_l.reshape(-1, 512),
                         h_lat, mod144, lat_row, w_o, seq, 512)
        if need_ctx:
            o_att_c = _attention(r3(q_c, lctx), [r3(kv_c, lctx)], 128)
            h_ctx = _outproj(o_gla_c.reshape(-1, MIX_W), o_gdn_c.reshape(-1, MIX_W), o_att_c.reshape(-1, 512),
                             h_ctx, mod144, ctx_row_fn, w_o, lctx, 256)

        j = layer // 2
        if layer % 2 == 0:
            w_gu = ffn_gate_up[j].astype(BF16)
            w_dn = ffn_down[j].astype(BF16)
            h_lat = _ffn(h_lat, mod144, lat_row, gain_ffn, w_gu, w_dn, seq, 512, 1408)
            if need_ctx:
                h_ctx = _ffn(h_ctx, mod144, ctx_row_fn, gain_ffn, w_gu, w_dn, lctx, 256, 1408)
        else:
            w_gu = moe_gate_up[j].astype(BF16)
            w_dn = moe_down[j].astype(BF16)
            w_r = jnp.concatenate([moe_router[j].astype(F32), jnp.zeros((d, LANES - N_EXPERTS), F32)], axis=1)
            b_r = jnp.concatenate([moe_router_bias[j].astype(F32), jnp.zeros((LANES - N_EXPERTS,), F32)])[None, :]
            h_lat = _moe_routed(h_lat, mod144, lat_row, gain_ffn, w_r, b_r, w_gu, w_dn, seq)
            if need_ctx:
                h_ctx = _moe_routed(h_ctx, mod144, ctx_row_fn, gain_ffn, w_r, b_r, w_gu, w_dn, lctx)
    return h_lat.reshape(bsz, seq, d)
```

```python
import functools

import numpy as np
import jax
import jax.numpy as jnp
from jax import lax
from jax.experimental import pallas as pl
from jax.experimental.pallas import tpu as pltpu

F32 = jnp.float32
BF16 = jnp.bfloat16

GRID_W = 64
HEAD_DIM = 64
CHUNK = 64
SUB = 16
EPS = 1e-6
GLA_HEADS = 4
GLA_GATE_RANK = 16
GLA_TAU = 16.0
GDN_HEADS = 4
SHORT_CONV = 5
ATT_Q_HEADS = 8
ATT_KV_HEADS = 2
ROPE_THETA = 10000.0
N_EXPERTS = 8
MIX_W = GLA_HEADS * HEAD_DIM
LOG2E = 1.4426950408889634
EXP_CLAMP = 80.0

LANES = 128
V7X_VMEM_BYTES = 64 * 1024 * 1024
VMEM_LIMIT = V7X_VMEM_BYTES - 8 * 1024 * 1024


def _cparams(sem):
    return pltpu.CompilerParams(dimension_semantics=sem, vmem_limit_bytes=VMEM_LIMIT)


def _silu(x):
    return x / (1.0 + jnp.exp(-x))


def _sigmoid(x):
    return 1.0 / (1.0 + jnp.exp(-x))


def _softplus(x):
    return jnp.maximum(x, 0.0) + jnp.log(1.0 + jnp.exp(-jnp.abs(x)))


def _dot(a, b):
    return jnp.dot(a.astype(BF16), b.astype(BF16), preferred_element_type=F32)


def _dot_nt(a, b):
    return lax.dot_general(a.astype(BF16), b.astype(BF16), (((1,), (1,)), ((), ())),
                           preferred_element_type=F32)


def _split(x):
    hi = x.astype(BF16)
    lo = (x - hi.astype(F32)).astype(BF16)
    return hi, lo


def _dot_xhl(x, w):
    hi, lo = _split(x)
    w = w.astype(BF16)
    return (jnp.dot(hi, w, preferred_element_type=F32) + jnp.dot(lo, w, preferred_element_type=F32))


def _dot_whl(w, x):
    hi, lo = _split(x)
    w = w.astype(BF16)
    return (jnp.dot(w, hi, preferred_element_type=F32) + jnp.dot(w, lo, preferred_element_type=F32))


def _dot3(a, b):
    ah, al = _split(a)
    bh, bl = _split(b)
    return (jnp.dot(ah, bh, preferred_element_type=F32) + jnp.dot(ah, bl, preferred_element_type=F32)
            + jnp.dot(al, bh, preferred_element_type=F32))


def _seg_sum64(sq, seg, split=True):
    outs = []
    for j in range(sq.shape[1] // LANES):
        part = sq[:, LANES * j:LANES * (j + 1)]
        outs.append(_dot_xhl(part, seg) if split else _dot(part, seg))
    return outs[0] if len(outs) == 1 else jnp.concatenate(outs, axis=1)


def _iota(shape, dim):
    return lax.broadcasted_iota(jnp.int32, shape, dim)


def _mod_kernel(c_ref, w_ref, b_ref, o_ref):
    s = _silu(c_ref[...])
    o_ref[0] = _dot(s, w_ref[0]) + b_ref[0]


def _modulation(cvec, w_mod, b_mod):
    depth, d, n = w_mod.shape
    rows = cvec.shape[0]
    tn = 1536
    return pl.pallas_call(
        _mod_kernel,
        grid=(depth, n // tn),
        in_specs=[pl.BlockSpec((rows, d), lambda l, j: (0, 0)),
                  pl.BlockSpec((1, d, tn), lambda l, j: (l, 0, j)),
                  pl.BlockSpec((1, 1, tn), lambda l, j: (l, 0, j))],
        out_specs=pl.BlockSpec((1, rows, tn), lambda l, j: (l, 0, j)),
        out_shape=jax.ShapeDtypeStruct((depth, rows, n), F32),
        compiler_params=_cparams(("arbitrary", "arbitrary")),
        name="modulation",
    )(cvec, w_mod, b_mod.reshape(depth, 1, n))


def _swap16(n, lane):
    fwd = pltpu.roll(n, LANES - 16, 1)
    bwd = pltpu.roll(n, 16, 1)
    return jnp.where((lane % 32) < 16, fwd, bwd)


def _inproj_kernel(*refs, rope):
    if rope:
        (h_ref, sh_ref, sc_ref, gain_ref, w_ref, hg_ref, seg_ref, cos_ref, sin_ref,
         gla_ref, gdn_ref, q_ref, kv_ref, small_ref) = refs
    else:
        (h_ref, sh_ref, sc_ref, gain_ref, w_ref, hg_ref, seg_ref,
         gla_ref, gdn_ref, q_ref, kv_ref, small_ref) = refs
    x = h_ref[...]
    ms = jnp.mean(x * x, axis=-1, keepdims=True)
    a = x * lax.rsqrt(ms + EPS) * (gain_ref[...] * (1.0 + sc_ref[0])) + sh_ref[0]
    p = jnp.dot(a.astype(BF16), w_ref[...], preferred_element_type=F32)
    gla_ref[...] = p[:, 0:1024]
    gdn_ref[...] = p[:, 1024:2048]
    small_ref[...] = p[:, 2816:2944]
    seg = seg_ref[...]
    lane = _iota((1, LANES), 1)
    outs = []
    for j in range(5):
        t = p[:, 2048 + LANES * j:2048 + LANES * (j + 1)]
        ss = _dot_xhl(t * t, seg)
        n = t * lax.rsqrt(ss * (1.0 / HEAD_DIM) + EPS) * hg_ref[:, LANES * j:LANES * (j + 1)]
        if rope:
            n = n * cos_ref[...] + _swap16(n, lane) * sin_ref[...]
        outs.append(n)
    q_ref[...] = jnp.concatenate(outs[:4], axis=1).astype(BF16)
    kv_ref[...] = jnp.concatenate([outs[4], p[:, 2688:2816]], axis=1).astype(BF16)


def _inproj(h2d, mod144, mod_row_fn, gain, w_p, hg, seg, tables, seq_len, tm):
    rows, d = h2d.shape
    n_all = w_p.shape[1]
    rope = tables is not None
    tiles_per_seq = seq_len // tm
    in_specs = [pl.BlockSpec((tm, d), lambda i: (i, 0)),
                pl.BlockSpec((1, 1, d), lambda i: (mod_row_fn(i // tiles_per_seq) * 6 + 0, 0, 0)),
                pl.BlockSpec((1, 1, d), lambda i: (mod_row_fn(i // tiles_per_seq) * 6 + 1, 0, 0)),
                pl.BlockSpec((1, d), lambda i: (0, 0)),
                pl.BlockSpec((d, n_all), lambda i: (0, 0)),
                pl.BlockSpec((1, 640), lambda i: (0, 0)),
                pl.BlockSpec((LANES, LANES), lambda i: (0, 0))]
    args = [h2d, mod144, mod144, gain, w_p, hg, seg]
    if rope:
        in_specs += [pl.BlockSpec((tm, LANES), lambda i: (i % tiles_per_seq, 0)),
                     pl.BlockSpec((tm, LANES), lambda i: (i % tiles_per_seq, 0))]
        args += list(tables)
    out_shape = (jax.ShapeDtypeStruct((rows, 1024), F32), jax.ShapeDtypeStruct((rows, 1024), F32),
                 jax.ShapeDtypeStruct((rows, 512), BF16), jax.ShapeDtypeStruct((rows, 256), BF16),
                 jax.ShapeDtypeStruct((rows, LANES), F32))
    out_specs = (pl.BlockSpec((tm, 1024), lambda i: (i, 0)), pl.BlockSpec((tm, 1024), lambda i: (i, 0)),
                 pl.BlockSpec((tm, 512), lambda i: (i, 0)), pl.BlockSpec((tm, 256), lambda i: (i, 0)),
                 pl.BlockSpec((tm, LANES), lambda i: (i, 0)))
    return pl.pallas_call(
        functools.partial(_inproj_kernel, rope=rope),
        grid=(rows // tm,), in_specs=in_specs, out_specs=out_specs, out_shape=out_shape,
        compiler_params=_cparams(("parallel",)),
        name="inproj_rope" if rope else "inproj",
    )(*args)


ATT_CHUNKS_PER_DOT = 1


def _attn_kernel(*refs, nkv, tq):
    q_ref = refs[0]
    kv_refs = refs[1:1 + nkv]
    o_ref = refs[1 + nkv]
    q = q_ref[0]
    lane = _iota((1, LANES), 1)
    mlo = (lane < HEAD_DIM).astype(BF16)
    mhi = (lane >= HEAD_DIM).astype(BF16)
    kvs = [r[0] for r in kv_refs]
    outs = []
    for jj in range(0, 4, ATT_CHUNKS_PER_DOT):
        pieces = []
        for j in range(jj, jj + ATT_CHUNKS_PER_DOT):
            qc = q[:, LANES * j:LANES * (j + 1)]
            pieces += [qc * mlo, qc * mhi]
        q_all = jnp.concatenate(pieces, axis=0)
        ss = [lax.dot_general(q_all, kv[:, 0:LANES], (((1,), (1,)), ((), ())), preferred_element_type=F32)
              for kv in kvs]
        m = functools.reduce(jnp.maximum, [jnp.max(s, axis=-1, keepdims=True) for s in ss])
        ps = [jnp.exp2(s - m) for s in ss]
        l = functools.reduce(lambda a, b: a + b, [jnp.sum(p, axis=-1, keepdims=True) for p in ps])
        o = functools.reduce(lambda a, b: a + b,
                             [jnp.dot(p.astype(BF16), kv[:, LANES:2 * LANES], preferred_element_type=F32)
                              for p, kv in zip(ps, kvs)])
        o = o / l
        for j in range(ATT_CHUNKS_PER_DOT):
            outs.append(jnp.where(lane < HEAD_DIM, o[2 * j * tq:(2 * j + 1) * tq],
                                  o[(2 * j + 1) * tq:(2 * j + 2) * tq]))
    o_ref[0] = jnp.concatenate(outs, axis=1).astype(BF16)


def _attention(q, kvs, tq):
    b, lq, _ = q.shape
    in_specs = [pl.BlockSpec((1, tq, 512), lambda i, j: (i, j, 0))]
    for kv in kvs:
        in_specs.append(pl.BlockSpec((1, kv.shape[1], 256), lambda i, j: (i, 0, 0)))
    return pl.pallas_call(
        functools.partial(_attn_kernel, nkv=len(kvs), tq=tq),
        grid=(b, lq // tq), in_specs=in_specs,
        out_specs=pl.BlockSpec((1, tq, 512), lambda i, j: (i, j, 0)),
        out_shape=jax.ShapeDtypeStruct((b, lq, 512), BF16),
        compiler_params=_cparams(("parallel", "arbitrary")),
        name="attention",
    )(q, *kvs)


def _head_masks():
    lane = _iota((1, MIX_W), 1)
    return [(lane // HEAD_DIM == h).astype(F32) for h in range(GLA_HEADS)]


def _blockdiag_mask():
    r = _iota((MIX_W, MIX_W), 0) // HEAD_DIM
    c = _iota((MIX_W, MIX_W), 1) // HEAD_DIM
    return r == c


def _finish_rows(o, gate, gain, seg):
    ss = _seg_sum64(o * o, seg)
    return o * lax.rsqrt(ss * (1.0 / HEAD_DIM) + EPS) * gain * _silu(gate)


PAIR_W = 2 * HEAD_DIM
PREP_CHUNKS = 4
SCAN_UNROLL = 2


def _gla_kernel(*refs, need_ctx):
    if need_ctx:
        (xc_ref, xl_ref, sc_ref, sl_ref, wg_ref, bg_ref, gain_ref, seg_ref,
         oc_ref, ol_ref, of_ref, ob_ref, qh_ref, kh_ref, dg_ref, st_ref) = refs
    else:
        (xc_ref, xl_ref, sc_ref, sl_ref, wg_ref, bg_ref, gain_ref, seg_ref,
         ol_ref, of_ref, ob_ref, qh_ref, kh_ref, dg_ref, st_ref) = refs
        oc_ref = None
    lc = xc_ref.shape[1]
    x_refs = (xc_ref, xl_ref)
    s_refs = (sc_ref, sl_ref)
    row_off = (0, lc)
    o_refs = (of_ref, ob_ref)
    hmask = _head_masks()
    bd = _blockdiag_mask()
    ti = _iota((CHUNK, CHUNK), 0)
    si = _iota((CHUNK, CHUNK), 1)
    tri = [(si <= ti).astype(BF16), (si >= ti).astype(BF16)]
    rr = _iota((4 * CHUNK, CHUNK), 0)
    cc = _iota((4 * CHUNK, CHUNK), 1)
    t_of_row = (rr // (GLA_HEADS * SUB)) * SUB + rr % SUB
    causal = [cc <= t_of_row, cc >= t_of_row]
    nblk = CHUNK // SUB

    def prep(seg_i, c2):
        x_ref, s_ref = x_refs[seg_i], s_refs[seg_i]
        fs = []
        for kk in range(PREP_CHUNKS):
            r0 = pl.multiple_of((c2 * PREP_CHUNKS + kk) * CHUNK, CHUNK)
            fs.append(dict(row=row_off[seg_i] + r0,
                           q=x_ref[0, pl.ds(r0, CHUNK), 0:MIX_W] * (HEAD_DIM ** -0.5),
                           k=x_ref[0, pl.ds(r0, CHUNK), MIX_W:2 * MIX_W],
                           v=x_ref[0, pl.ds(r0, CHUNK), 2 * MIX_W:3 * MIX_W],
                           sm=s_ref[0, pl.ds(r0, CHUNK), :]))
        chains = [(f, d) for f in fs for d in range(2)]
        xg = [_dot3(f["sm"], wg_ref[d]) + bg_ref[d] for f, d in chains]
        g = [(jnp.minimum(x_, 0.0) - jnp.log(1.0 + jnp.exp(-jnp.abs(x_)))) * (1.0 / GLA_TAU) for x_ in xg]
        b = [_dot_whl(tri[d], g_) for (_, d), g_ in zip(chains, g)]
        pieces = []
        for i in range(nblk):
            row_pieces = []
            for (f, d), g_, b_ in zip(chains, g, b):
                e = SUB * i if d == 0 else SUB * i + SUB - 1
                bref = b_[e:e + 1, :] - g_[e:e + 1, :]
                kt = f["k"] * jnp.exp(jnp.minimum(bref - b_, EXP_CLAMP))
                qt = f["q"][SUB * i:SUB * (i + 1), :] * jnp.exp(b_[SUB * i:SUB * (i + 1), :] - bref)
                qs = jnp.concatenate([qt * hmask[h] for h in range(GLA_HEADS)], axis=0)
                row_pieces.append(_dot_nt(qs, kt))
            pieces.append(row_pieces)
        scores = [jnp.where(causal[d], jnp.concatenate([pieces[i][j] for i in range(nblk)], axis=0), 0.0)
                  for j, (_, d) in enumerate(chains)]
        r = [_dot(s_, f["v"]) for (f, _), s_ in zip(chains, scores)]
        for j, (f, d) in enumerate(chains):
            intra = []
            for i in range(nblk):
                acc = None
                for h in range(GLA_HEADS):
                    lo = (i * GLA_HEADS + h) * SUB
                    term = r[j][lo:lo + SUB, :] * hmask[h]
                    acc = term if acc is None else acc + term
                intra.append(acc)
            row = f["row"]
            e = CHUNK - 1 if d == 0 else 0
            b_end = b[j][e:e + 1, :]
            o_refs[d][pl.ds(row, CHUNK), :] = jnp.concatenate(intra, axis=0)
            qh_ref[d, pl.ds(row, CHUNK), :] = (f["q"] * jnp.exp(b[j])).astype(BF16)
            kh_ref[d, pl.ds(row, CHUNK), :] = (f["k"] * jnp.exp(b_end - b[j])).astype(BF16)
            dg_ref[d, pl.ds(pl.multiple_of(row // 8, 8), 8), :] = jnp.broadcast_to(jnp.exp(b_end), (8, MIX_W))

    for seg_i, x_ref in enumerate(x_refs):
        def prep_body(c, carry, seg_i=seg_i):
            prep(seg_i, c)
            return carry

        lax.fori_loop(0, x_ref.shape[1] // (CHUNK * PREP_CHUNKS), prep_body, 0)

    st_ref[...] = jnp.zeros(st_ref.shape, F32)

    def scan_body(seg_i, i0, nch):
        x_ref = x_refs[seg_i]
        steps = [i0 * SCAN_UNROLL + k for k in range(SCAN_UNROLL)]
        r0s = [[pl.multiple_of((i if d == 0 else nch - 1 - i) * CHUNK, CHUNK) for d in range(2)] for i in steps]
        rows = [[row_off[seg_i] + r0 for r0 in r] for r in r0s]
        upd = [[jnp.dot(x_ref[0, pl.ds(r0s[k][d], CHUNK), 2 * MIX_W:3 * MIX_W].T.astype(BF16),
                        kh_ref[d, pl.ds(rows[k][d], CHUNK), :], preferred_element_type=F32) for d in range(2)]
               for k in range(SCAN_UNROLL)]
        st = [st_ref[d] for d in range(2)]
        for k in range(SCAN_UNROLL):
            inter = [_dot_nt(qh_ref[d, pl.ds(rows[k][d], CHUNK), :], st[d]) for d in range(2)]
            for d in range(2):
                o_refs[d][pl.ds(rows[k][d], CHUNK), :] += inter[d]
                dgr = dg_ref[d, pl.ds(pl.multiple_of(rows[k][d] // 8, 8), 8), :][0:1, :]
                st[d] = st[d] * dgr + jnp.where(bd, upd[k][d], 0.0)
        for d in range(2):
            st_ref[d] = st[d]

    for seg_i, x_ref in enumerate(x_refs):
        nch = x_ref.shape[1] // CHUNK

        def scan_iter(i, carry, seg_i=seg_i, nch=nch):
            scan_body(seg_i, i, nch)
            return carry

        lax.fori_loop(0, nch // SCAN_UNROLL, scan_iter, 0)

    gain = gain_ref[...]
    seg = seg_ref[...]
    blk = 256
    outs = ((oc_ref, xc_ref, 0), (ol_ref, xl_ref, lc))
    for o_ref, x_ref, off in outs:
        if o_ref is None:
            continue

        def fin(i, carry, o_ref=o_ref, x_ref=x_ref, off=off):
            r0 = pl.multiple_of(i * blk, blk)
            o = of_ref[pl.ds(off + r0, blk), :] + ob_ref[pl.ds(off + r0, blk), :]
            gate = x_ref[0, pl.ds(r0, blk), 3 * MIX_W:4 * MIX_W]
            o_ref[0, pl.ds(r0, blk), :] = _finish_rows(o, gate, gain, seg).astype(BF16)
            return carry

        lax.fori_loop(0, x_ref.shape[1] // blk, fin, 0)


def _gla(x_ctx, x_lat, s_ctx, s_lat, wg, bg, gain, seg, need_ctx):
    b, lc, _ = x_ctx.shape
    ll = x_lat.shape[1]
    in_specs = [pl.BlockSpec((1, lc, 1024), lambda i: (i, 0, 0)),
                pl.BlockSpec((1, ll, 1024), lambda i: (i, 0, 0)),
                pl.BlockSpec((1, lc, LANES), lambda i: (i, 0, 0)),
                pl.BlockSpec((1, ll, LANES), lambda i: (i, 0, 0)),
                pl.BlockSpec((2, LANES, MIX_W), lambda i: (0, 0, 0)),
                pl.BlockSpec((2, 1, MIX_W), lambda i: (0, 0, 0)),
                pl.BlockSpec((1, MIX_W), lambda i: (0, 0)),
                pl.BlockSpec((LANES, LANES), lambda i: (0, 0))]
    out_shape = [jax.ShapeDtypeStruct((b, ll, MIX_W), BF16)]
    out_specs = [pl.BlockSpec((1, ll, MIX_W), lambda i: (i, 0, 0))]
    if need_ctx:
        out_shape.insert(0, jax.ShapeDtypeStruct((b, lc, MIX_W), BF16))
        out_specs.insert(0, pl.BlockSpec((1, lc, MIX_W), lambda i: (i, 0, 0)))
    res = pl.pallas_call(
        functools.partial(_gla_kernel, need_ctx=need_ctx),
        grid=(b,), in_specs=in_specs, out_specs=out_specs, out_shape=out_shape,
        scratch_shapes=[pltpu.VMEM((lc + ll, MIX_W), F32), pltpu.VMEM((lc + ll, MIX_W), F32),
                        pltpu.VMEM((2, lc + ll, MIX_W), BF16), pltpu.VMEM((2, lc + ll, MIX_W), BF16),
                        pltpu.VMEM((2, (lc + ll) // 8, MIX_W), F32),
                        pltpu.VMEM((2, MIX_W, MIX_W), F32)],
        compiler_params=_cparams(("parallel",)),
        name="gla_scan",
    )(x_ctx, x_lat, s_ctx, s_lat, wg, bg, gain, seg)
    return (res[0], res[1]) if need_ctx else (None, res[0])


def _gdn2_kernel(*refs, need_ctx):
    if need_ctx:
        (xc_ref, xl_ref, sc_ref, sl_ref, cw_ref, alog_ref, dtb_ref, sel_ref, selb_ref, gain_ref, seg_ref,
         oc_ref, ol_ref, sk_ref, p_ref, qe_ref, sv_ref, kh_ref, dg_ref, of_ref, ob_ref, st_ref) = refs
    else:
        (xc_ref, xl_ref, sc_ref, sl_ref, cw_ref, alog_ref, dtb_ref, sel_ref, selb_ref, gain_ref, seg_ref,
         ol_ref, sk_ref, p_ref, qe_ref, sv_ref, kh_ref, dg_ref, of_ref, ob_ref, st_ref) = refs
        oc_ref = None
    lc = xc_ref.shape[1]
    x_refs = (xc_ref, xl_ref)
    s_refs = (sc_ref, sl_ref)
    row_off = (0, lc)
    npair = MIX_W // PAIR_W
    seg = seg_ref[...]
    ti = _iota((CHUNK, CHUNK), 0)
    si = _iota((CHUNK, CHUNK), 1)
    tri = [(si <= ti).astype(BF16), (si >= ti).astype(BF16)]
    ones = jnp.ones((CHUNK, CHUNK), BF16)
    tp = _iota((CHUNK, PAIR_W), 0)
    sp = _iota((CHUNK, PAIR_W), 1) % CHUNK
    le = sp <= tp
    ge = sp >= tp
    m_tri = [le, ge]
    m_strict = [sp < tp, sp > tp]
    m_sum = [ge.astype(F32).astype(BF16), le.astype(F32).astype(BF16)]
    blk16 = (tp // SUB) == (sp // SUB)
    eye = (tp == sp).astype(F32)
    bd2 = (_iota((PAIR_W, PAIR_W), 0) // HEAD_DIM) == (_iota((PAIR_W, PAIR_W), 1) // HEAD_DIM)
    lane_p = _iota((1, PAIR_W), 1)
    hm2 = [(lane_p // HEAD_DIM == h).astype(F32) for h in range(2)]
    cw = cw_ref[...]

    def pk2(y):
        yb = y.astype(BF16)
        return jnp.where(bd2, jnp.concatenate([yb, yb], axis=0), jnp.zeros((), BF16))

    def mm(x, y):
        return jnp.dot(x.astype(BF16), pk2(y), preferred_element_type=F32)

    def front(seg_i, c):
        x_ref, s_ref = x_refs[seg_i], s_refs[seg_i]
        ln = x_ref.shape[1]
        nch = ln // CHUNK
        r0 = pl.multiple_of(c * CHUNK, CHUNK)
        center = x_ref[0, pl.ds(r0, CHUNK), 0:3 * MIX_W]
        p0 = pl.multiple_of(jnp.maximum(r0 - 8, 0), 8)
        n0 = pl.multiple_of(jnp.minimum(r0 + CHUNK, ln - 8), 8)
        prev = x_ref[0, pl.ds(p0, 8), 0:3 * MIX_W] * jnp.where(c > 0, 1.0, 0.0)
        nxt = x_ref[0, pl.ds(n0, 8), 0:3 * MIX_W] * jnp.where(c < nch - 1, 1.0, 0.0)
        ext = jnp.concatenate([prev, center, nxt], axis=0)
        pad = SHORT_CONV // 2
        acc = None
        for j in range(SHORT_CONV):
            term = ext[8 - pad + j:8 - pad + j + CHUNK, :] * cw[j:j + 1, :]
            acc = term if acc is None else acc + term
        y = _silu(acc)
        qk = y[:, 0:2 * MIX_W]
        qk = qk * lax.rsqrt(_seg_sum64(qk * qk, seg, split=False) + EPS)
        f = dict(row=row_off[seg_i] + r0, qn=qk[:, 0:MIX_W] * (HEAD_DIM ** -0.5), kn=qk[:, MIX_W:2 * MIX_W],
                 v=y[:, 2 * MIX_W:3 * MIX_W])
        sm = s_ref[0, pl.ds(r0, CHUNK), :]
        beta = _sigmoid(sm)
        f["kk"], f["qk"] = [], []
        for p in range(npair):
            ls = slice(PAIR_W * p, PAIR_W * (p + 1))
            kstack = jnp.concatenate([f["kn"][:, ls] * hm2[0], f["kn"][:, ls] * hm2[1]], axis=0)
            f["kk"].append(_dot_nt(f["kn"][:, ls], kstack))
            f["qk"].append(_dot_nt(f["qn"][:, ls], kstack))
        gexp = [_dot_xhl(-jnp.exp(alog_ref[d]) * _softplus(sm + dtb_ref[d]), sel_ref[d]) for d in range(2)]
        f["ghl"] = [_split(g_) for g_ in gexp]
        f["bexp"] = [_dot(beta, selb_ref[d]) for d in range(2)]
        f["gam"] = [jnp.dot(tri[d], f["ghl"][d][0], preferred_element_type=F32)
                    + jnp.dot(tri[d], f["ghl"][d][1], preferred_element_type=F32) for d in range(2)]
        return f

    def prep(seg_i, c2):
        fs = [front(seg_i, c2 * PREP_CHUNKS + k) for k in range(PREP_CHUNKS)]
        chains = [(f, d, p) for f in fs for d in range(2) for p in range(npair)]
        lss = [slice(PAIR_W * p, PAIR_W * (p + 1)) for _, _, p in chains]
        gam_t = [f["gam"][d][:, ls] for (f, d, _), ls in zip(chains, lss)]
        gam_s = [jnp.dot(ones, f["ghl"][d][0][:, ls] * m_sum[d], preferred_element_type=F32)
                 + jnp.dot(ones, f["ghl"][d][1][:, ls] * m_sum[d], preferred_element_type=F32)
                 for (f, d, _), ls in zip(chains, lss)]
        bx = [f["bexp"][d][:, ls] for (f, d, _), ls in zip(chains, lss)]
        dec = [jnp.where(m_tri[d], jnp.exp(jnp.minimum(gt - gs, 0.0)), 0.0)
               for (_, d, _), gt, gs in zip(chains, gam_t, gam_s)]
        a = [jnp.where(m_strict[d], b_ * dc * f["kk"][p], 0.0) for (f, d, p), b_, dc in zip(chains, bx, dec)]
        dgn = [jnp.where(blk16, a_, 0.0) for a_ in a]
        lo = [a_ - g_ for a_, g_ in zip(a, dgn)]
        d2 = [mm(g_, g_) for g_ in dgn]
        t1 = [mm(eye - g_, eye + s_) for g_, s_ in zip(dgn, d2)]
        d4 = [mm(s_, s_) for s_ in d2]
        t2 = [mm(t_, eye + s_) for t_, s_ in zip(t1, d4)]
        d8 = [mm(s_, s_) for s_ in d4]
        t_inv = [mm(t_, eye + s_) for t_, s_ in zip(t2, d8)]
        m = [mm(t_, l_) for t_, l_ in zip(t_inv, lo)]
        m2 = [mm(m_, m_) for m_ in m]
        w1 = [mm(eye - m_, eye + s_) for m_, s_ in zip(m, m2)]
        w = [mm(w_, t_) for w_, t_ in zip(w1, t_inv)]
        egam = [jnp.exp(gt) for gt in gam_t]
        solv = [mm(w_, b_ * f["v"][:, ls]) for (f, _, _), w_, b_, ls in zip(chains, w, bx, lss)]
        solk = [mm(w_, b_ * eg * f["kn"][:, ls]) for (f, _, _), w_, b_, eg, ls in zip(chains, w, bx, egam, lss)]
        for i, (f, d, p) in enumerate(chains):
            ls, row = lss[i], f["row"]
            e = CHUNK - 1 if d == 0 else 0
            g_end = gam_t[i][e:e + 1, :]
            sv_ref[d, pl.ds(row, CHUNK), ls] = solv[i]
            sk_ref[d, pl.ds(row, CHUNK), ls] = solk[i].astype(BF16)
            p_ref[d, pl.ds(row, CHUNK), ls] = jnp.where(m_tri[d], f["qk"][p] * dec[i], 0.0).astype(BF16)
            qe_ref[d, pl.ds(row, CHUNK), ls] = (egam[i] * f["qn"][:, ls]).astype(BF16)
            kh_ref[d, pl.ds(row, CHUNK), ls] = f["kn"][:, ls] * jnp.exp(g_end - gam_t[i])
            dg_ref[d, pl.ds(pl.multiple_of(row // 8, 8), 8), ls] = jnp.broadcast_to(jnp.exp(g_end), (8, PAIR_W))

    for seg_i, x_ref in enumerate(x_refs):
        def prep_body(c, carry, seg_i=seg_i):
            prep(seg_i, c)
            return carry

        lax.fori_loop(0, x_ref.shape[1] // (CHUNK * PREP_CHUNKS), prep_body, 0)

    st_ref[...] = jnp.zeros(st_ref.shape, F32)

    def scan_body(seg_i, i0, nch):
        chains = [(d, p) for d in range(2) for p in range(npair)]
        lss = [slice(PAIR_W * p, PAIR_W * (p + 1)) for _, p in chains]
        st = [st_ref[d, p] for d, p in chains]
        pending = []
        for k in range(SCAN_UNROLL):
            i = i0 * SCAN_UNROLL + k
            rows = [row_off[seg_i] + pl.multiple_of((i if d == 0 else nch - 1 - i) * CHUNK, CHUNK)
                    for d, _ in chains]
            stb = [s_.astype(BF16) for s_ in st]
            u = [sv_ref[d, pl.ds(r, CHUNK), ls]
                 - jnp.dot(sk_ref[d, pl.ds(r, CHUNK), ls], sb, preferred_element_type=F32)
                 for (d, _), r, ls, sb in zip(chains, rows, lss, stb)]
            ku = [_dot(kh_ref[d, pl.ds(r, CHUNK), ls].T, u_) for (d, _), r, ls, u_ in zip(chains, rows, lss, u)]
            dgr = [dg_ref[d, pl.ds(pl.multiple_of(r // 8, 8), 8), ls][0:1, :]
                   for (d, _), r, ls in zip(chains, rows, lss)]
            st = [s_ * g_ + jnp.where(bd2, k_, 0.0) for s_, g_, k_ in zip(st, dgr, ku)]
            pending.append((rows, stb, u))
        for rows, stb, u in pending:
            oq = [jnp.dot(qe_ref[d, pl.ds(r, CHUNK), ls], sb, preferred_element_type=F32)
                  for (d, _), r, ls, sb in zip(chains, rows, lss, stb)]
            ou = [jnp.dot(p_ref[d, pl.ds(r, CHUNK), ls], pk2(u_), preferred_element_type=F32)
                  for (d, _), r, ls, u_ in zip(chains, rows, lss, u)]
            for j, (d, _) in enumerate(chains):
                dst = of_ref if d == 0 else ob_ref
                dst[pl.ds(rows[j], CHUNK), lss[j]] = oq[j] + ou[j]
        for j, (d, p) in enumerate(chains):
            st_ref[d, p] = st[j]

    for seg_i, x_ref in enumerate(x_refs):
        nch = x_ref.shape[1] // CHUNK

        def scan_iter(i, carry, seg_i=seg_i, nch=nch):
            scan_body(seg_i, i, nch)
            return carry

        lax.fori_loop(0, nch // SCAN_UNROLL, scan_iter, 0)

    gain = gain_ref[...]
    blk = 256
    outs = ((oc_ref, xc_ref, 0), (ol_ref, xl_ref, lc))
    for o_ref, x_ref, off in outs:
        if o_ref is None:
            continue

        def fin(i, carry, o_ref=o_ref, x_ref=x_ref, off=off):
            r0 = pl.multiple_of(i * blk, blk)
            o = of_ref[pl.ds(off + r0, blk), :] + ob_ref[pl.ds(off + r0, blk), :]
            gate = x_ref[0, pl.ds(r0, blk), 3 * MIX_W:4 * MIX_W]
            o_ref[0, pl.ds(r0, blk), :] = _finish_rows(o, gate, gain, seg).astype(BF16)
            return carry

        lax.fori_loop(0, x_ref.shape[1] // blk, fin, 0)


def _gdn2(x_ctx, x_lat, s_ctx, s_lat, cw, alog, dtb, sel, selb, gain, seg, need_ctx):
    b, lc, _ = x_ctx.shape
    ll = x_lat.shape[1]
    lt = lc + ll
    in_specs = [pl.BlockSpec((1, lc, 1024), lambda i: (i, 0, 0)),
                pl.BlockSpec((1, ll, 1024), lambda i: (i, 0, 0)),
                pl.BlockSpec((1, lc, LANES), lambda i: (i, 0, 0)),
                pl.BlockSpec((1, ll, LANES), lambda i: (i, 0, 0)),
                pl.BlockSpec((8, 3 * MIX_W), lambda i: (0, 0)),
                pl.BlockSpec((2, 1, LANES), lambda i: (0, 0, 0)),
                pl.BlockSpec((2, 1, LANES), lambda i: (0, 0, 0)),
                pl.BlockSpec((2, LANES, MIX_W), lambda i: (0, 0, 0)),
                pl.BlockSpec((2, LANES, MIX_W), lambda i: (0, 0, 0)),
                pl.BlockSpec((1, MIX_W), lambda i: (0, 0)),
                pl.BlockSpec((LANES, LANES), lambda i: (0, 0))]
    out_shape = [jax.ShapeDtypeStruct((b, ll, MIX_W), BF16)]
    out_specs = [pl.BlockSpec((1, ll, MIX_W), lambda i: (i, 0, 0))]
    if need_ctx:
        out_shape.insert(0, jax.ShapeDtypeStruct((b, lc, MIX_W), BF16))
        out_specs.insert(0, pl.BlockSpec((1, lc, MIX_W), lambda i: (i, 0, 0)))
    res = pl.pallas_call(
        functools.partial(_gdn2_kernel, need_ctx=need_ctx),
        grid=(b,), in_specs=in_specs, out_specs=out_specs, out_shape=out_shape,
        scratch_shapes=[pltpu.VMEM((2, lt, MIX_W), BF16), pltpu.VMEM((2, lt, MIX_W), BF16),
                        pltpu.VMEM((2, lt, MIX_W), BF16), pltpu.VMEM((2, lt, MIX_W), F32),
                        pltpu.VMEM((2, lt, MIX_W), F32), pltpu.VMEM((2, lt // 8, MIX_W), F32),
                        pltpu.VMEM((lt, MIX_W), F32), pltpu.VMEM((lt, MIX_W), F32),
                        pltpu.VMEM((2, MIX_W // PAIR_W, PAIR_W, PAIR_W), F32)],
        compiler_params=_cparams(("parallel",)),
        name="gdn_scan",
    )(x_ctx, x_lat, s_ctx, s_lat, cw, alog, dtb, sel, selb, gain, seg)
    return (res[0], res[1]) if need_ctx else (None, res[0])


def _outproj_kernel(gla_ref, gdn_ref, att_ref, h_ref, g_ref, w_ref, o_ref):
    y = (jnp.dot(gla_ref[...], w_ref[0:256, :], preferred_element_type=F32)
         + jnp.dot(gdn_ref[...], w_ref[256:512, :], preferred_element_type=F32)
         + jnp.dot(att_ref[...], w_ref[512:1024, :], preferred_element_type=F32))
    o_ref[...] = h_ref[...] + g_ref[0] * y


def _outproj(gla, gdn, att, h2d, mod144, mod_row_fn, w, seq_len, tm):
    rows, d = h2d.shape
    tiles_per_seq = seq_len // tm
    return pl.pallas_call(
        _outproj_kernel,
        grid=(rows // tm,),
        in_specs=[pl.BlockSpec((tm, 256), lambda i: (i, 0)),
                  pl.BlockSpec((tm, 256), lambda i: (i, 0)),
                  pl.BlockSpec((tm, 512), lambda i: (i, 0)),
                  pl.BlockSpec((tm, d), lambda i: (i, 0)),
                  pl.BlockSpec((1, 1, d), lambda i: (mod_row_fn(i // tiles_per_seq) * 6 + 2, 0, 0)),
                  pl.BlockSpec((d, d), lambda i: (0, 0))],
        out_specs=pl.BlockSpec((tm, d), lambda i: (i, 0)),
        out_shape=jax.ShapeDtypeStruct((rows, d), F32),
        compiler_params=_cparams(("parallel",)),
        name="outproj",
    )(gla, gdn, att, h2d, mod144, w)


def _norm_mod(h_ref, sh_ref, sc_ref, gain_ref):
    x = h_ref[...]
    ms = jnp.mean(x * x, axis=-1, keepdims=True)
    return x * lax.rsqrt(ms + EPS) * (gain_ref[...] * (1.0 + sc_ref[0])) + sh_ref[0]


def _ffn_kernel(h_ref, sh_ref, sc_ref, g_ref, gain_ref, wg_ref, wu_ref, wd_ref, o_ref, b_scr, acc_scr, *, nf):
    f = pl.program_id(1)

    @pl.when(f == 0)
    def _():
        b_scr[...] = _norm_mod(h_ref, sh_ref, sc_ref, gain_ref).astype(BF16)
        acc_scr[...] = jnp.zeros(acc_scr.shape, F32)

    b = b_scr[...]
    gg = jnp.dot(b, wg_ref[...], preferred_element_type=F32)
    uu = jnp.dot(b, wu_ref[...], preferred_element_type=F32)
    hid = (_silu(gg) * uu).astype(BF16)
    acc_scr[...] += jnp.dot(hid, wd_ref[...], preferred_element_type=F32)

    @pl.when(f == nf - 1)
    def _():
        o_ref[...] = h_ref[...] + g_ref[0] * acc_scr[...]


def _ffn(h2d, mod144, mod_row_fn, gain, w_gu, w_down, seq_len, tm, tf):
    rows, d = h2d.shape
    dff = w_down.shape[0]
    nf = dff // tf
    tiles_per_seq = seq_len // tm

    def mod_spec(k):
        return pl.BlockSpec((1, 1, d), lambda i, f: (mod_row_fn(i // tiles_per_seq) * 6 + k, 0, 0))

    return pl.pallas_call(
        functools.partial(_ffn_kernel, nf=nf),
        grid=(rows // tm, nf),
        in_specs=[pl.BlockSpec((tm, d), lambda i, f: (i, 0)),
                  mod_spec(3), mod_spec(4), mod_spec(5),
                  pl.BlockSpec((1, d), lambda i, f: (0, 0)),
                  pl.BlockSpec((d, tf), lambda i, f: (0, f)),
                  pl.BlockSpec((d, tf), lambda i, f: (0, nf + f)),
                  pl.BlockSpec((tf, d), lambda i, f: (f, 0))],
        out_specs=pl.BlockSpec((tm, d), lambda i, f: (i, 0)),
        out_shape=jax.ShapeDtypeStruct((rows, d), F32),
        scratch_shapes=[pltpu.VMEM((tm, d), BF16), pltpu.VMEM((tm, d), F32)],
        compiler_params=_cparams(("parallel", "arbitrary")),
        name="ffn",
    )(h2d, mod144, mod144, mod144, gain, w_gu, w_gu, w_down)


MOE_TILE = 512
COMBINE_TOKENS = 256


def _router_kernel(h_ref, sh_ref, sc_ref, gain_ref, wr_ref, br_ref, b_ref, route_ref):
    lane = _iota((1, LANES), 1)
    lane_f = lane.astype(F32)
    b = _norm_mod(h_ref, sh_ref, sc_ref, gain_ref)
    b_ref[...] = b
    logits = _dot3(b, wr_ref[...]) + br_ref[...]
    logits = jnp.where(lane < N_EXPERTS, logits, -jnp.inf)
    m1 = jnp.max(logits, axis=-1, keepdims=True)
    i1 = jnp.min(jnp.where(logits == m1, lane_f, float(LANES)), axis=-1, keepdims=True)
    rest = jnp.where(lane_f == i1, -jnp.inf, logits)
    m2 = jnp.max(rest, axis=-1, keepdims=True)
    i2 = jnp.min(jnp.where(rest == m2, lane_f, float(LANES)), axis=-1, keepdims=True)
    t = jnp.exp(m2 - m1)
    w1 = 1.0 / (1.0 + t)
    route_ref[...] = (jnp.where(lane == 0, i1, 0.0) + jnp.where(lane == 1, i2, 0.0)
                      + jnp.where(lane == 2, w1, 0.0) + jnp.where(lane == 3, t * w1, 0.0))


def _router(h2d, mod144, mod_row_fn, gain, w_router, b_router, seq_len, tm):
    rows, d = h2d.shape
    tiles_per_seq = seq_len // tm

    def mod_spec(k):
        return pl.BlockSpec((1, 1, d), lambda i: (mod_row_fn(i // tiles_per_seq) * 6 + k, 0, 0))

    return pl.pallas_call(
        _router_kernel,
        grid=(rows // tm,),
        in_specs=[pl.BlockSpec((tm, d), lambda i: (i, 0)), mod_spec(3), mod_spec(4),
                  pl.BlockSpec((1, d), lambda i: (0, 0)),
                  pl.BlockSpec((d, LANES), lambda i: (0, 0)),
                  pl.BlockSpec((1, LANES), lambda i: (0, 0))],
        out_specs=(pl.BlockSpec((tm, d), lambda i: (i, 0)), pl.BlockSpec((tm, LANES), lambda i: (i, 0))),
        out_shape=(jax.ShapeDtypeStruct((rows, d), F32), jax.ShapeDtypeStruct((rows, LANES), F32)),
        compiler_params=_cparams(("parallel",)),
        name="moe_router",
    )(h2d, mod144, mod144, gain, w_router, b_router)


DISPATCH_TOKENS = 512


def _moe_dispatch_kernel(pos_ref, b_ref, xs_in_ref, xs_ref, sem):
    del xs_in_ref
    n = DISPATCH_TOKENS
    for r in range(2 * n):
        pltpu.make_async_copy(b_ref.at[pl.ds(r % n, 1)], xs_ref.at[pl.ds(pos_ref[0, 0, r], 1)],
                              sem).start(priority=r % 2)
    for _ in range(2):
        pltpu.make_async_copy(b_ref, xs_ref.at[pl.ds(0, n)], sem).wait()


def _moe_dispatch(pos2, b, p_rows):
    rows, d = b.shape
    tm = DISPATCH_TOKENS
    return pl.pallas_call(
        _moe_dispatch_kernel,
        grid=(rows // tm,),
        in_specs=[pl.BlockSpec((1, 1, 2 * tm), lambda i: (i, 0, 0), memory_space=pltpu.SMEM),
                  pl.BlockSpec((tm, d), lambda i: (i, 0)),
                  pl.BlockSpec(memory_space=pl.ANY)],
        out_specs=pl.BlockSpec(memory_space=pl.ANY),
        out_shape=jax.ShapeDtypeStruct((p_rows, d), F32),
        scratch_shapes=[pltpu.SemaphoreType.DMA(())],
        input_output_aliases={2: 0},
        compiler_params=_cparams(("arbitrary",)),
        name="moe_dispatch",
    )(pos2, b, jnp.zeros((p_rows, d), F32))


def _moe_group_kernel(te_ref, nt_ref, xs_ref, wg_ref, wu_ref, wd_ref, ys_ref, xb_scr, acc_scr, *, nf):
    i = pl.program_id(0)
    f = pl.program_id(1)
    used = i < nt_ref[0]

    @pl.when(used & (f == 0))
    def _():
        xb_scr[...] = xs_ref[...].astype(BF16)

    @pl.when(used)
    def _():
        x = xb_scr[...]
        gg = jnp.dot(x, wg_ref[0], preferred_element_type=F32)
        uu = jnp.dot(x, wu_ref[0], preferred_element_type=F32)
        hid = (_silu(gg) * uu).astype(BF16)
        part = jnp.dot(hid, wd_ref[0], preferred_element_type=F32)

        @pl.when(f == 0)
        def _():
            acc_scr[...] = part

        @pl.when(f > 0)
        def _():
            acc_scr[...] += part

    @pl.when(used & (f == nf - 1))
    def _():
        ys_ref[...] = acc_scr[...]

    @pl.when(jnp.logical_not(used) & (f == nf - 1))
    def _():
        ys_ref[...] = jnp.zeros(ys_ref.shape, F32)


def _moe_group(tile_expert, n_tiles_used, xs, w_gu, w_down, tf):
    p, d = xs.shape
    dff = w_down.shape[1]
    nf = dff // tf
    grid_spec = pltpu.PrefetchScalarGridSpec(
        num_scalar_prefetch=2,
        grid=(p // MOE_TILE, nf),
        in_specs=[pl.BlockSpec((MOE_TILE, d), lambda i, f, te, nt: (i, 0)),
                  pl.BlockSpec((1, d, tf), lambda i, f, te, nt: (te[i], 0, f)),
                  pl.BlockSpec((1, d, tf), lambda i, f, te, nt: (te[i], 0, nf + f)),
                  pl.BlockSpec((1, tf, d), lambda i, f, te, nt: (te[i], f, 0))],
        out_specs=pl.BlockSpec((MOE_TILE, d), lambda i, f, te, nt: (i, 0)),
        scratch_shapes=[pltpu.VMEM((MOE_TILE, d), BF16), pltpu.VMEM((MOE_TILE, d), F32)])
    return pl.pallas_call(
        functools.partial(_moe_group_kernel, nf=nf),
        grid_spec=grid_spec,
        out_shape=jax.ShapeDtypeStruct((p, d), F32),
        compiler_params=_cparams(("arbitrary", "arbitrary")),
        name="moe_experts",
    )(tile_expert, n_tiles_used, xs, w_gu, w_gu, w_down)


def _moe_combine_kernel(pos_ref, ys_ref, h_ref, route_ref, g_ref, o_ref, buf, sem):
    n = 2 * COMBINE_TOKENS

    for r in range(n):
        pltpu.make_async_copy(ys_ref.at[pl.ds(pos_ref[0, 0, r], 1)], buf.at[pl.ds(r, 1)], sem).start(priority=r % 2)
    pltpu.make_async_copy(ys_ref.at[pl.ds(0, n)], buf, sem).wait()
    w1 = route_ref[:, 2:3]
    w2 = route_ref[:, 3:4]
    o_ref[...] = h_ref[...] + g_ref[0] * (w1 * buf[0:COMBINE_TOKENS, :] + w2 * buf[COMBINE_TOKENS:n, :])


def _moe_combine(pos, ys, h2d, route, mod144, mod_row_fn, seq_len):
    rows, d = h2d.shape
    tm = COMBINE_TOKENS
    tiles_per_seq = seq_len // tm
    steps = rows // tm
    return pl.pallas_call(
        _moe_combine_kernel,
        grid=(steps,),
        in_specs=[pl.BlockSpec((1, 1, 2 * tm), lambda i: (i, 0, 0), memory_space=pltpu.SMEM),
                  pl.BlockSpec(memory_space=pl.ANY),
                  pl.BlockSpec((tm, d), lambda i: (i, 0)),
                  pl.BlockSpec((tm, LANES), lambda i: (i, 0)),
                  pl.BlockSpec((1, 1, d), lambda i: (mod_row_fn(i // tiles_per_seq) * 6 + 5, 0, 0))],
        out_specs=pl.BlockSpec((tm, d), lambda i: (i, 0)),
        out_shape=jax.ShapeDtypeStruct((rows, d), F32),
        scratch_shapes=[pltpu.VMEM((2 * tm, d), F32), pltpu.SemaphoreType.DMA(())],
        compiler_params=_cparams(("arbitrary",)),
        name="moe_combine",
    )(pos, ys, h2d, route, mod144)


def _moe_routed(h2d, mod144, mod_row_fn, gain, w_router, b_router, w_gu, w_down, seq_len):
    rows, d = h2d.shape
    b, route = _router(h2d, mod144, mod_row_fn, gain, w_router, b_router, seq_len, min(512, seq_len))
    ex = jnp.concatenate([route[:, 0], route[:, 1]]).astype(jnp.int32)
    onehot = (ex[:, None] == jnp.arange(N_EXPERTS, dtype=jnp.int32)[None, :]).astype(jnp.int32)
    rank = jnp.sum((jnp.cumsum(onehot, axis=0) - onehot) * onehot, axis=1)
    counts = jnp.sum(onehot, axis=0)
    padded = ((counts + MOE_TILE - 1) // MOE_TILE) * MOE_TILE
    ends = jnp.cumsum(padded)
    starts = ends - padded
    pos = jnp.sum(onehot * starts[None, :], axis=1) + rank
    pos = pos.astype(jnp.int32)
    p_rows = 2 * rows + N_EXPERTS * MOE_TILE
    tile_first = jnp.arange(p_rows // MOE_TILE, dtype=jnp.int32) * MOE_TILE
    tile_expert = jnp.minimum(jnp.sum((tile_first[:, None] >= ends[None, :]).astype(jnp.int32), axis=1),
                              N_EXPERTS - 1).astype(jnp.int32)
    n_tiles_used = (ends[-1:] // MOE_TILE).astype(jnp.int32)

    def per_tile(tm):
        return jnp.concatenate([pos[:rows].reshape(rows // tm, 1, tm), pos[rows:].reshape(rows // tm, 1, tm)], axis=2)

    xs = _moe_dispatch(per_tile(DISPATCH_TOKENS), b, p_rows)
    ys = _moe_group(tile_expert, n_tiles_used, xs, w_gu, w_down, 1408)
    return _moe_combine(per_tile(COMBINE_TOKENS), ys, h2d, route, mod144, mod_row_fn, seq_len)


ATT_HEAD_ORDER = (0, 4, 1, 5, 2, 6, 3, 7)


def _layout_w_in(w):
    gla = w[:, 0:1024]
    glow = w[:, 1024:1056]
    gdn = w[:, 1056:2080]
    ab = w[:, 2080:2096]
    q = jnp.concatenate([w[:, 2096 + HEAD_DIM * h:2096 + HEAD_DIM * (h + 1)] for h in ATT_HEAD_ORDER], axis=1)
    kv = w[:, 2608:2864]
    pad = jnp.zeros((w.shape[0], LANES - 48), w.dtype)
    return jnp.concatenate([gla, gdn, q, kv, glow, ab, pad], axis=1).astype(BF16)


def _layout_w_out(w):
    att = [w[512 + HEAD_DIM * h:512 + HEAD_DIM * (h + 1)] for h in ATT_HEAD_ORDER]
    return jnp.concatenate([w[0:512]] + att, axis=0).astype(BF16)


def _rope_tables(seq_len):
    rows = seq_len // GRID_W
    row = jnp.repeat(jnp.arange(rows), GRID_W).astype(F32)
    col = jnp.tile(jnp.arange(GRID_W), rows).astype(F32)
    inv_freq = ROPE_THETA ** (-jnp.arange(0, HEAD_DIM // 2, 2, dtype=F32) / (HEAD_DIM // 2))
    ar = row[:, None] * inv_freq
    ac = col[:, None] * inv_freq
    cos = jnp.concatenate([jnp.cos(ar), jnp.cos(ar), jnp.cos(ac), jnp.cos(ac)], axis=-1)
    sin = jnp.concatenate([-jnp.sin(ar), jnp.sin(ar), -jnp.sin(ac), jnp.sin(ac)], axis=-1)
    return jnp.tile(cos, (1, 2)), jnp.tile(sin, (1, 2))


def _seg_matrix():
    i = np.arange(LANES)
    return jnp.asarray((i[:, None] // HEAD_DIM) == (i[None, :] // HEAD_DIM), dtype=BF16)


def _gdn_select():
    sel = np.zeros((2, LANES, MIX_W), np.float32)
    selb = np.zeros((2, LANES, MIX_W), np.float32)
    for d in range(2):
        for h in range(GDN_HEADS):
            sel[d, 32 + GDN_HEADS * d + h, HEAD_DIM * h:HEAD_DIM * (h + 1)] = 1.0
            selb[d, 40 + GDN_HEADS * d + h, HEAD_DIM * h:HEAD_DIM * (h + 1)] = 1.0
    return jnp.asarray(sel, BF16), jnp.asarray(selb, BF16)


def _lane_rows(vals, base):
    out = jnp.zeros((2, 1, LANES), F32)
    for d in range(2):
        out = out.at[d, 0, base + GDN_HEADS * d:base + GDN_HEADS * (d + 1)].set(vals[d].astype(F32))
    return out


def kernel(x, c, ctx, c_ctx, w_mod, b_mod, norm_mix, norm_ffn, w_in, gla_gate_up, gla_gate_bias, gla_out_gain,
           gdn_conv, gdn_a_log, gdn_dt_bias, gdn_out_gain, att_q_gain, att_k_gain, w_out, ffn_gate_up, ffn_down,
           moe_router, moe_router_bias, moe_gate_up, moe_down):
    bsz, seq, d = x.shape
    lctx = ctx.shape[1]
    depth = w_mod.shape[0]
    ctx_row = bsz

    mod_rows = ((bsz + 1 + 7) // 8) * 8
    cvec = jnp.concatenate([c, c_ctx[None, :], jnp.zeros((mod_rows - bsz - 1, d), F32)], axis=0)
    mods = _modulation(cvec, w_mod, b_mod)

    seg = _seg_matrix()
    tables = _rope_tables(seq)
    sel, selb = _gdn_select()
    lat_row = lambda b: b
    ctx_row_fn = lambda b: ctx_row

    h_lat = x.reshape(bsz * seq, d)
    h_ctx = ctx.reshape(bsz * lctx, d)
    for layer in range(depth):
        need_ctx = layer < depth - 1
        mod144 = mods[layer].reshape(mod_rows * 6, 1, d)
        w_p = _layout_w_in(w_in[layer])
        w_o = _layout_w_out(w_out[layer])
        hg = jnp.concatenate([jnp.tile(att_q_gain[layer], ATT_Q_HEADS) * (HEAD_DIM ** -0.5 * LOG2E),
                              jnp.tile(att_k_gain[layer], ATT_KV_HEADS)])[None, :].astype(F32)
        gain_mix = norm_mix[layer][None, :]
        gain_ffn = norm_ffn[layer][None, :]

        gla_l, gdn_l, q_l, kv_l, sm_l = _inproj(h_lat, mod144, lat_row, gain_mix, w_p, hg, seg, tables, seq, 256)
        gla_c, gdn_c, q_c, kv_c, sm_c = _inproj(h_ctx, mod144, ctx_row_fn, gain_mix, w_p, hg, seg, None, lctx, 256)

        r3 = lambda t, n: t.reshape(bsz, n, t.shape[-1])
        wg = jnp.zeros((2, LANES, MIX_W), F32)
        for dd in range(2):
            wg = wg.at[dd, GLA_GATE_RANK * dd:GLA_GATE_RANK * (dd + 1), :].set(gla_gate_up[layer, dd].astype(F32))
        bg = gla_gate_bias[layer].reshape(2, 1, MIX_W).astype(F32)
        gla_gain = jnp.tile(gla_out_gain[layer], GLA_HEADS)[None, :].astype(F32)
        o_gla_c, o_gla_l = _gla(r3(gla_c, lctx), r3(gla_l, seq), r3(sm_c, lctx), r3(sm_l, seq),
                                wg, bg, gla_gain, seg, need_ctx)

        cw = jnp.concatenate([gdn_conv[layer].astype(F32), jnp.zeros((8 - SHORT_CONV, 3 * MIX_W), F32)], axis=0)
        alog = _lane_rows(gdn_a_log[layer], 32)
        dtb = _lane_rows(gdn_dt_bias[layer], 32)
        gdn_gain = jnp.tile(gdn_out_gain[layer], GDN_HEADS)[None, :].astype(F32)
        o_gdn_c, o_gdn_l = _gdn2(r3(gdn_c, lctx), r3(gdn_l, seq), r3(sm_c, lctx), r3(sm_l, seq),
                                cw, alog, dtb, sel, selb, gdn_gain, seg, need_ctx)

        o_att_l = _attention(r3(q_l, seq), [r3(kv_l, seq), r3(kv_c, lctx)], 256)
        h_lat = _outproj(o_gla_l.reshape(-1, MIX_W), o_gdn_l.reshape(-1, MIX_W), o_att_l.reshape(-1, 512),
                         h_lat, mod144, lat_row, w_o, seq, 512)
        if need_ctx:
            o_att_c = _attention(r3(q_c, lctx), [r3(kv_c, lctx)], 128)
            h_ctx = _outproj(o_gla_c.reshape(-1, MIX_W), o_gdn_c.reshape(-1, MIX_W), o_att_c.reshape(-1, 512),
                             h_ctx, mod144, ctx_row_fn, w_o, lctx, 256)

        j = layer // 2
        if layer % 2 == 0:
            w_gu = ffn_gate_up[j].astype(BF16)
            w_dn = ffn_down[j].astype(BF16)
            h_lat = _ffn(h_lat, mod144, lat_row, gain_ffn, w_gu, w_dn, seq, 512, 1408)
            if need_ctx:
                h_ctx = _ffn(h_ctx, mod144, ctx_row_fn, gain_ffn, w_gu, w_dn, lctx, 256, 1408)
        else:
            w_gu = moe_gate_up[j].astype(BF16)
            w_dn = moe_down[j].astype(BF16)
            w_r = jnp.concatenate([moe_router[j].astype(F32), jnp.zeros((d, LANES - N_EXPERTS), F32)], axis=1)
            b_r = jnp.concatenate([moe_router_bias[j].astype(F32), jnp.zeros((LANES - N_EXPERTS,), F32)])[None, :]
            h_lat = _moe_routed(h_lat, mod144, lat_row, gain_ffn, w_r, b_r, w_gu, w_dn, seq)
            if need_ctx:
                h_ctx = _moe_routed(h_ctx, mod144, ctx_row_fn, gain_ffn, w_r, b_r, w_gu, w_dn, lctx)
    return h_lat.reshape(bsz, seq, d)
```

```python
import functools

import numpy as np
import jax
import jax.numpy as jnp
from jax import lax
from jax.experimental import pallas as pl
from jax.experimental.pallas import tpu as pltpu

F32 = jnp.float32
BF16 = jnp.bfloat16

GRID_W = 64
HEAD_DIM = 64
CHUNK = 64
SUB = 16
EPS = 1e-6
GLA_HEADS = 4
GLA_GATE_RANK = 16
GLA_TAU = 16.0
GDN_HEADS = 4
SHORT_CONV = 5
ATT_Q_HEADS = 8
ATT_KV_HEADS = 2
ROPE_THETA = 10000.0
N_EXPERTS = 8
MIX_W = GLA_HEADS * HEAD_DIM
LOG2E = 1.4426950408889634
EXP_CLAMP = 80.0

LANES = 128
V7X_VMEM_BYTES = 64 * 1024 * 1024
VMEM_LIMIT = V7X_VMEM_BYTES - 8 * 1024 * 1024


def _cparams(sem):
    return pltpu.CompilerParams(dimension_semantics=sem, vmem_limit_bytes=VMEM_LIMIT)


def _silu(x):
    return x / (1.0 + jnp.exp(-x))


def _sigmoid(x):
    return 1.0 / (1.0 + jnp.exp(-x))


def _softplus(x):
    return jnp.maximum(x, 0.0) + jnp.log(1.0 + jnp.exp(-jnp.abs(x)))


def _dot(a, b):
    return jnp.dot(a.astype(BF16), b.astype(BF16), preferred_element_type=F32)


def _dot_nt(a, b):
    return lax.dot_general(a.astype(BF16), b.astype(BF16), (((1,), (1,)), ((), ())),
                           preferred_element_type=F32)


def _split(x):
    hi = x.astype(BF16)
    lo = (x - hi.astype(F32)).astype(BF16)
    return hi, lo


def _dot_xhl(x, w):
    hi, lo = _split(x)
    w = w.astype(BF16)
    return (jnp.dot(hi, w, preferred_element_type=F32) + jnp.dot(lo, w, preferred_element_type=F32))


def _dot_whl(w, x):
    hi, lo = _split(x)
    w = w.astype(BF16)
    return (jnp.dot(w, hi, preferred_element_type=F32) + jnp.dot(w, lo, preferred_element_type=F32))


def _dot3(a, b):
    ah, al = _split(a)
    bh, bl = _split(b)
    return (jnp.dot(ah, bh, preferred_element_type=F32) + jnp.dot(ah, bl, preferred_element_type=F32)
            + jnp.dot(al, bh, preferred_element_type=F32))


def _seg_sum64(sq, seg, split=True):
    outs = []
    for j in range(sq.shape[1] // LANES):
        part = sq[:, LANES * j:LANES * (j + 1)]
        outs.append(_dot_xhl(part, seg) if split else _dot(part, seg))
    return outs[0] if len(outs) == 1 else jnp.concatenate(outs, axis=1)


def _iota(shape, dim):
    return lax.broadcasted_iota(jnp.int32, shape, dim)


def _mod_kernel(c_ref, w_ref, b_ref, o_ref):
    s = _silu(c_ref[...])
    o_ref[0] = _dot(s, w_ref[0]) + b_ref[0]


def _modulation(cvec, w_mod, b_mod):
    depth, d, n = w_mod.shape
    rows = cvec.shape[0]
    tn = 1536
    return pl.pallas_call(
        _mod_kernel,
        grid=(depth, n // tn),
        in_specs=[pl.BlockSpec((rows, d), lambda l, j: (0, 0)),
                  pl.BlockSpec((1, d, tn), lambda l, j: (l, 0, j)),
                  pl.BlockSpec((1, 1, tn), lambda l, j: (l, 0, j))],
        out_specs=pl.BlockSpec((1, rows, tn), lambda l, j: (l, 0, j)),
        out_shape=jax.ShapeDtypeStruct((depth, rows, n), F32),
        compiler_params=_cparams(("arbitrary", "arbitrary")),
        name="modulation",
    )(cvec, w_mod, b_mod.reshape(depth, 1, n))


def _swap16(n, lane):
    fwd = pltpu.roll(n, LANES - 16, 1)
    bwd = pltpu.roll(n, 16, 1)
    return jnp.where((lane % 32) < 16, fwd, bwd)


def _inproj_kernel(*refs, rope):
    if rope:
        (h_ref, sh_ref, sc_ref, gain_ref, w_ref, hg_ref, seg_ref, cos_ref, sin_ref,
         gla_ref, gdn_ref, q_ref, kv_ref, small_ref) = refs
    else:
        (h_ref, sh_ref, sc_ref, gain_ref, w_ref, hg_ref, seg_ref,
         gla_ref, gdn_ref, q_ref, kv_ref, small_ref) = refs
    x = h_ref[...]
    ms = jnp.mean(x * x, axis=-1, keepdims=True)
    a = x * lax.rsqrt(ms + EPS) * (gain_ref[...] * (1.0 + sc_ref[0])) + sh_ref[0]
    p = jnp.dot(a.astype(BF16), w_ref[...], preferred_element_type=F32)
    gla_ref[...] = p[:, 0:1024]
    gdn_ref[...] = p[:, 1024:2048]
    small_ref[...] = p[:, 2816:2944]
    seg = seg_ref[...]
    lane = _iota((1, LANES), 1)
    outs = []
    for j in range(5):
        t = p[:, 2048 + LANES * j:2048 + LANES * (j + 1)]
        ss = _dot_xhl(t * t, seg)
        n = t * lax.rsqrt(ss * (1.0 / HEAD_DIM) + EPS) * hg_ref[:, LANES * j:LANES * (j + 1)]
        if rope:
            n = n * cos_ref[...] + _swap16(n, lane) * sin_ref[...]
        outs.append(n)
    q_ref[...] = jnp.concatenate(outs[:4], axis=1).astype(BF16)
    kv_ref[...] = jnp.concatenate([outs[4], p[:, 2688:2816]], axis=1).astype(BF16)


def _inproj(h2d, mod144, mod_row_fn, gain, w_p, hg, seg, tables, seq_len, tm):
    rows, d = h2d.shape
    n_all = w_p.shape[1]
    rope = tables is not None
    tiles_per_seq = seq_len // tm
    in_specs = [pl.BlockSpec((tm, d), lambda i: (i, 0)),
                pl.BlockSpec((1, 1, d), lambda i: (mod_row_fn(i // tiles_per_seq) * 6 + 0, 0, 0)),
                pl.BlockSpec((1, 1, d), lambda i: (mod_row_fn(i // tiles_per_seq) * 6 + 1, 0, 0)),
                pl.BlockSpec((1, d), lambda i: (0, 0)),
                pl.BlockSpec((d, n_all), lambda i: (0, 0)),
                pl.BlockSpec((1, 640), lambda i: (0, 0)),
                pl.BlockSpec((LANES, LANES), lambda i: (0, 0))]
    args = [h2d, mod144, mod144, gain, w_p, hg, seg]
    if rope:
        in_specs += [pl.BlockSpec((tm, LANES), lambda i: (i % tiles_per_seq, 0)),
                     pl.BlockSpec((tm, LANES), lambda i: (i % tiles_per_seq, 0))]
        args += list(tables)
    out_shape = (jax.ShapeDtypeStruct((rows, 1024), F32), jax.ShapeDtypeStruct((rows, 1024), F32),
                 jax.ShapeDtypeStruct((rows, 512), BF16), jax.ShapeDtypeStruct((rows, 256), BF16),
                 jax.ShapeDtypeStruct((rows, LANES), F32))
    out_specs = (pl.BlockSpec((tm, 1024), lambda i: (i, 0)), pl.BlockSpec((tm, 1024), lambda i: (i, 0)),
                 pl.BlockSpec((tm, 512), lambda i: (i, 0)), pl.BlockSpec((tm, 256), lambda i: (i, 0)),
                 pl.BlockSpec((tm, LANES), lambda i: (i, 0)))
    return pl.pallas_call(
        functools.partial(_inproj_kernel, rope=rope),
        grid=(rows // tm,), in_specs=in_specs, out_specs=out_specs, out_shape=out_shape,
        compiler_params=_cparams(("parallel",)),
        name="inproj_rope" if rope else "inproj",
    )(*args)


ATT_CHUNKS_PER_DOT = 1


def _attn_kernel(*refs, nkv, tq):
    q_ref = refs[0]
    kv_refs = refs[1:1 + nkv]
    o_ref = refs[1 + nkv]
    q = q_ref[0]
    lane = _iota((1, LANES), 1)
    mlo = (lane < HEAD_DIM).astype(BF16)
    mhi = (lane >= HEAD_DIM).astype(BF16)
    kvs = [r[0] for r in kv_refs]
    def scores(jj):
        pieces = []
        for j in range(jj, jj + ATT_CHUNKS_PER_DOT):
            qc = q[:, LANES * j:LANES * (j + 1)]
            pieces += [qc * mlo, qc * mhi]
        q_all = jnp.concatenate(pieces, axis=0)
        return [lax.dot_general(q_all, kv[:, 0:LANES], (((1,), (1,)), ((), ())), preferred_element_type=F32)
                for kv in kvs]

    outs = []
    starts = list(range(0, 4, ATT_CHUNKS_PER_DOT))
    ss_next = scores(starts[0])
    for n, jj in enumerate(starts):
        ss = ss_next
        if n + 1 < len(starts):
            ss_next = scores(starts[n + 1])
        m = functools.reduce(jnp.maximum, [jnp.max(s, axis=-1, keepdims=True) for s in ss])
        ps = [jnp.exp2(s - m) for s in ss]
        l = functools.reduce(lambda a, b: a + b, [jnp.sum(p, axis=-1, keepdims=True) for p in ps])
        o = functools.reduce(lambda a, b: a + b,
                             [jnp.dot(p.astype(BF16), kv[:, LANES:2 * LANES], preferred_element_type=F32)
                              for p, kv in zip(ps, kvs)])
        o = o / l
        for j in range(ATT_CHUNKS_PER_DOT):
            outs.append(jnp.where(lane < HEAD_DIM, o[2 * j * tq:(2 * j + 1) * tq],
                                  o[(2 * j + 1) * tq:(2 * j + 2) * tq]))
    o_ref[0] = jnp.concatenate(outs, axis=1).astype(BF16)


def _attention(q, kvs, tq):
    b, lq, _ = q.shape
    in_specs = [pl.BlockSpec((1, tq, 512), lambda i, j: (i, j, 0))]
    for kv in kvs:
        in_specs.append(pl.BlockSpec((1, kv.shape[1], 256), lambda i, j: (i, 0, 0)))
    return pl.pallas_call(
        functools.partial(_attn_kernel, nkv=len(kvs), tq=tq),
        grid=(b, lq // tq), in_specs=in_specs,
        out_specs=pl.BlockSpec((1, tq, 512), lambda i, j: (i, j, 0)),
        out_shape=jax.ShapeDtypeStruct((b, lq, 512), BF16),
        compiler_params=_cparams(("parallel", "arbitrary")),
        name="attention",
    )(q, *kvs)


def _head_masks():
    lane = _iota((1, MIX_W), 1)
    return [(lane // HEAD_DIM == h).astype(F32) for h in range(GLA_HEADS)]


def _blockdiag_mask():
    r = _iota((MIX_W, MIX_W), 0) // HEAD_DIM
    c = _iota((MIX_W, MIX_W), 1) // HEAD_DIM
    return r == c


def _finish_rows(o, gate, gain, seg):
    ss = _seg_sum64(o * o, seg)
    return o * lax.rsqrt(ss * (1.0 / HEAD_DIM) + EPS) * gain * _silu(gate)


PAIR_W = 2 * HEAD_DIM
PREP_CHUNKS = 4
SCAN_UNROLL = 2


def _gla_kernel(*refs, need_ctx):
    if need_ctx:
        (xc_ref, xl_ref, sc_ref, sl_ref, wg_ref, bg_ref, gain_ref, seg_ref,
         oc_ref, ol_ref, of_ref, ob_ref, qh_ref, kh_ref, dg_ref, st_ref) = refs
    else:
        (xc_ref, xl_ref, sc_ref, sl_ref, wg_ref, bg_ref, gain_ref, seg_ref,
         ol_ref, of_ref, ob_ref, qh_ref, kh_ref, dg_ref, st_ref) = refs
        oc_ref = None
    lc = xc_ref.shape[1]
    x_refs = (xc_ref, xl_ref)
    s_refs = (sc_ref, sl_ref)
    row_off = (0, lc)
    o_refs = (of_ref, ob_ref)
    hmask = _head_masks()
    bd = _blockdiag_mask()
    ti = _iota((CHUNK, CHUNK), 0)
    si = _iota((CHUNK, CHUNK), 1)
    tri = [(si <= ti).astype(BF16), (si >= ti).astype(BF16)]
    rr = _iota((4 * CHUNK, CHUNK), 0)
    cc = _iota((4 * CHUNK, CHUNK), 1)
    t_of_row = (rr // (GLA_HEADS * SUB)) * SUB + rr % SUB
    causal = [cc <= t_of_row, cc >= t_of_row]
    nblk = CHUNK // SUB

    def prep(seg_i, c2):
        x_ref, s_ref = x_refs[seg_i], s_refs[seg_i]
        fs = []
        for kk in range(PREP_CHUNKS):
            r0 = pl.multiple_of((c2 * PREP_CHUNKS + kk) * CHUNK, CHUNK)
            fs.append(dict(row=row_off[seg_i] + r0,
                           q=x_ref[0, pl.ds(r0, CHUNK), 0:MIX_W] * (HEAD_DIM ** -0.5),
                           k=x_ref[0, pl.ds(r0, CHUNK), MIX_W:2 * MIX_W],
                           v=x_ref[0, pl.ds(r0, CHUNK), 2 * MIX_W:3 * MIX_W],
                           sm=s_ref[0, pl.ds(r0, CHUNK), :]))
        chains = [(f, d) for f in fs for d in range(2)]
        xg = [_dot3(f["sm"], wg_ref[d]) + bg_ref[d] for f, d in chains]
        g = [(jnp.minimum(x_, 0.0) - jnp.log(1.0 + jnp.exp(-jnp.abs(x_)))) * (1.0 / GLA_TAU) for x_ in xg]
        b = [_dot_whl(tri[d], g_) for (_, d), g_ in zip(chains, g)]
        pieces = []
        for i in range(nblk):
            row_pieces = []
            for (f, d), g_, b_ in zip(chains, g, b):
                e = SUB * i if d == 0 else SUB * i + SUB - 1
                bref = b_[e:e + 1, :] - g_[e:e + 1, :]
                kt = f["k"] * jnp.exp(jnp.minimum(bref - b_, EXP_CLAMP))
                qt = f["q"][SUB * i:SUB * (i + 1), :] * jnp.exp(b_[SUB * i:SUB * (i + 1), :] - bref)
                qs = jnp.concatenate([qt * hmask[h] for h in range(GLA_HEADS)], axis=0)
                row_pieces.append(_dot_nt(qs, kt))
            pieces.append(row_pieces)
        scores = [jnp.where(causal[d], jnp.concatenate([pieces[i][j] for i in range(nblk)], axis=0), 0.0)
                  for j, (_, d) in enumerate(chains)]
        r = [_dot(s_, f["v"]) for (f, _), s_ in zip(chains, scores)]
        for j, (f, d) in enumerate(chains):
            intra = []
            for i in range(nblk):
                acc = None
                for h in range(GLA_HEADS):
                    lo = (i * GLA_HEADS + h) * SUB
                    term = r[j][lo:lo + SUB, :] * hmask[h]
                    acc = term if acc is None else acc + term
                intra.append(acc)
            row = f["row"]
            e = CHUNK - 1 if d == 0 else 0
            b_end = b[j][e:e + 1, :]
            o_refs[d][pl.ds(row, CHUNK), :] = jnp.concatenate(intra, axis=0)
            qh_ref[d, pl.ds(row, CHUNK), :] = (f["q"] * jnp.exp(b[j])).astype(BF16)
            kh_ref[d, pl.ds(row, CHUNK), :] = (f["k"] * jnp.exp(b_end - b[j])).astype(BF16)
            dg_ref[d, pl.ds(pl.multiple_of(row // 8, 8), 8), :] = jnp.broadcast_to(jnp.exp(b_end), (8, MIX_W))

    for seg_i, x_ref in enumerate(x_refs):
        def prep_body(c, carry, seg_i=seg_i):
            prep(seg_i, c)
            return carry

        lax.fori_loop(0, x_ref.shape[1] // (CHUNK * PREP_CHUNKS), prep_body, 0)

    st_ref[...] = jnp.zeros(st_ref.shape, F32)

    def scan_body(seg_i, i0, nch):
        x_ref = x_refs[seg_i]
        steps = [i0 * SCAN_UNROLL + k for k in range(SCAN_UNROLL)]
        r0s = [[pl.multiple_of((i if d == 0 else nch - 1 - i) * CHUNK, CHUNK) for d in range(2)] for i in steps]
        rows = [[row_off[seg_i] + r0 for r0 in r] for r in r0s]
        upd = [[jnp.dot(x_ref[0, pl.ds(r0s[k][d], CHUNK), 2 * MIX_W:3 * MIX_W].T.astype(BF16),
                        kh_ref[d, pl.ds(rows[k][d], CHUNK), :], preferred_element_type=F32) for d in range(2)]
               for k in range(SCAN_UNROLL)]
        st = [st_ref[d] for d in range(2)]
        for k in range(SCAN_UNROLL):
            inter = [_dot_nt(qh_ref[d, pl.ds(rows[k][d], CHUNK), :], st[d]) for d in range(2)]
            for d in range(2):
                o_refs[d][pl.ds(rows[k][d], CHUNK), :] += inter[d]
                dgr = dg_ref[d, pl.ds(pl.multiple_of(rows[k][d] // 8, 8), 8), :][0:1, :]
                st[d] = st[d] * dgr + jnp.where(bd, upd[k][d], 0.0)
        for d in range(2):
            st_ref[d] = st[d]

    for seg_i, x_ref in enumerate(x_refs):
        nch = x_ref.shape[1] // CHUNK

        def scan_iter(i, carry, seg_i=seg_i, nch=nch):
            scan_body(seg_i, i, nch)
            return carry

        lax.fori_loop(0, nch // SCAN_UNROLL, scan_iter, 0)

    gain = gain_ref[...]
    seg = seg_ref[...]
    blk = 256
    outs = ((oc_ref, xc_ref, 0), (ol_ref, xl_ref, lc))
    for o_ref, x_ref, off in outs:
        if o_ref is None:
            continue

        def fin(i, carry, o_ref=o_ref, x_ref=x_ref, off=off):
            r0 = pl.multiple_of(i * blk, blk)
            o = of_ref[pl.ds(off + r0, blk), :] + ob_ref[pl.ds(off + r0, blk), :]
            gate = x_ref[0, pl.ds(r0, blk), 3 * MIX_W:4 * MIX_W]
            o_ref[0, pl.ds(r0, blk), :] = _finish_rows(o, gate, gain, seg).astype(BF16)
            return carry

        lax.fori_loop(0, x_ref.shape[1] // blk, fin, 0)


def _gla(x_ctx, x_lat, s_ctx, s_lat, wg, bg, gain, seg, need_ctx):
    b, lc, _ = x_ctx.shape
    ll = x_lat.shape[1]
    in_specs = [pl.BlockSpec((1, lc, 1024), lambda i: (i, 0, 0)),
                pl.BlockSpec((1, ll, 1024), lambda i: (i, 0, 0)),
                pl.BlockSpec((1, lc, LANES), lambda i: (i, 0, 0)),
                pl.BlockSpec((1, ll, LANES), lambda i: (i, 0, 0)),
                pl.BlockSpec((2, LANES, MIX_W), lambda i: (0, 0, 0)),
                pl.BlockSpec((2, 1, MIX_W), lambda i: (0, 0, 0)),
                pl.BlockSpec((1, MIX_W), lambda i: (0, 0)),
                pl.BlockSpec((LANES, LANES), lambda i: (0, 0))]
    out_shape = [jax.ShapeDtypeStruct((b, ll, MIX_W), BF16)]
    out_specs = [pl.BlockSpec((1, ll, MIX_W), lambda i: (i, 0, 0))]
    if need_ctx:
        out_shape.insert(0, jax.ShapeDtypeStruct((b, lc, MIX_W), BF16))
        out_specs.insert(0, pl.BlockSpec((1, lc, MIX_W), lambda i: (i, 0, 0)))
    res = pl.pallas_call(
        functools.partial(_gla_kernel, need_ctx=need_ctx),
        grid=(b,), in_specs=in_specs, out_specs=out_specs, out_shape=out_shape,
        scratch_shapes=[pltpu.VMEM((lc + ll, MIX_W), F32), pltpu.VMEM((lc + ll, MIX_W), F32),
                        pltpu.VMEM((2, lc + ll, MIX_W), BF16), pltpu.VMEM((2, lc + ll, MIX_W), BF16),
                        pltpu.VMEM((2, (lc + ll) // 8, MIX_W), F32),
                        pltpu.VMEM((2, MIX_W, MIX_W), F32)],
        compiler_params=_cparams(("parallel",)),
        name="gla_scan",
    )(x_ctx, x_lat, s_ctx, s_lat, wg, bg, gain, seg)
    return (res[0], res[1]) if need_ctx else (None, res[0])


def _gdn2_kernel(*refs, need_ctx):
    if need_ctx:
        (xc_ref, xl_ref, sc_ref, sl_ref, cw_ref, alog_ref, dtb_ref, sel_ref, selb_ref, gain_ref, seg_ref,
         oc_ref, ol_ref, sk_ref, p_ref, qe_ref, sv_ref, kh_ref, dg_ref, of_ref, ob_ref, st_ref) = refs
    else:
        (xc_ref, xl_ref, sc_ref, sl_ref, cw_ref, alog_ref, dtb_ref, sel_ref, selb_ref, gain_ref, seg_ref,
         ol_ref, sk_ref, p_ref, qe_ref, sv_ref, kh_ref, dg_ref, of_ref, ob_ref, st_ref) = refs
        oc_ref = None
    lc = xc_ref.shape[1]
    x_refs = (xc_ref, xl_ref)
    s_refs = (sc_ref, sl_ref)
    row_off = (0, lc)
    npair = MIX_W // PAIR_W
    seg = seg_ref[...]
    ti = _iota((CHUNK, CHUNK), 0)
    si = _iota((CHUNK, CHUNK), 1)
    tri = [(si <= ti).astype(BF16), (si >= ti).astype(BF16)]
    ones = jnp.ones((CHUNK, CHUNK), BF16)
    tp = _iota((CHUNK, PAIR_W), 0)
    sp = _iota((CHUNK, PAIR_W), 1) % CHUNK
    le = sp <= tp
    ge = sp >= tp
    m_tri = [le, ge]
    m_strict = [sp < tp, sp > tp]
    m_sum = [ge.astype(F32).astype(BF16), le.astype(F32).astype(BF16)]
    blk16 = (tp // SUB) == (sp // SUB)
    eye = (tp == sp).astype(F32)
    bd2 = (_iota((PAIR_W, PAIR_W), 0) // HEAD_DIM) == (_iota((PAIR_W, PAIR_W), 1) // HEAD_DIM)
    lane_p = _iota((1, PAIR_W), 1)
    hm2 = [(lane_p // HEAD_DIM == h).astype(F32) for h in range(2)]
    cw = cw_ref[...]

    def pk2(y):
        yb = y.astype(BF16)
        return jnp.where(bd2, jnp.concatenate([yb, yb], axis=0), jnp.zeros((), BF16))

    def mm(x, y):
        return jnp.dot(x.astype(BF16), pk2(y), preferred_element_type=F32)

    def front(seg_i, c):
        x_ref, s_ref = x_refs[seg_i], s_refs[seg_i]
        ln = x_ref.shape[1]
        nch = ln // CHUNK
        r0 = pl.multiple_of(c * CHUNK, CHUNK)
        center = x_ref[0, pl.ds(r0, CHUNK), 0:3 * MIX_W]
        p0 = pl.multiple_of(jnp.maximum(r0 - 8, 0), 8)
        n0 = pl.multiple_of(jnp.minimum(r0 + CHUNK, ln - 8), 8)
        prev = x_ref[0, pl.ds(p0, 8), 0:3 * MIX_W] * jnp.where(c > 0, 1.0, 0.0)
        nxt = x_ref[0, pl.ds(n0, 8), 0:3 * MIX_W] * jnp.where(c < nch - 1, 1.0, 0.0)
        ext = jnp.concatenate([prev, center, nxt], axis=0)
        pad = SHORT_CONV // 2
        acc = None
        for j in range(SHORT_CONV):
            term = ext[8 - pad + j:8 - pad + j + CHUNK, :] * cw[j:j + 1, :]
            acc = term if acc is None else acc + term
        y = _silu(acc)
        qk = y[:, 0:2 * MIX_W]
        qk = qk * lax.rsqrt(_seg_sum64(qk * qk, seg, split=False) + EPS)
        f = dict(row=row_off[seg_i] + r0, qn=qk[:, 0:MIX_W] * (HEAD_DIM ** -0.5), kn=qk[:, MIX_W:2 * MIX_W],
                 v=y[:, 2 * MIX_W:3 * MIX_W])
        sm = s_ref[0, pl.ds(r0, CHUNK), :]
        beta = _sigmoid(sm)
        f["kk"], f["qk"] = [], []
        for p in range(npair):
            ls = slice(PAIR_W * p, PAIR_W * (p + 1))
            kstack = jnp.concatenate([f["kn"][:, ls] * hm2[0], f["kn"][:, ls] * hm2[1]], axis=0)
            f["kk"].append(_dot_nt(f["kn"][:, ls], kstack))
            f["qk"].append(_dot_nt(f["qn"][:, ls], kstack))
        gexp = [_dot_xhl(-jnp.exp(alog_ref[d]) * _softplus(sm + dtb_ref[d]), sel_ref[d]) for d in range(2)]
        f["ghl"] = [_split(g_) for g_ in gexp]
        f["bexp"] = [_dot(beta, selb_ref[d]) for d in range(2)]
        f["gam"] = [jnp.dot(tri[d], f["ghl"][d][0], preferred_element_type=F32)
                    + jnp.dot(tri[d], f["ghl"][d][1], preferred_element_type=F32) for d in range(2)]
        return f

    def prep(seg_i, c2):
        fs = [front(seg_i, c2 * PREP_CHUNKS + k) for k in range(PREP_CHUNKS)]
        chains = [(f, d, p) for f in fs for d in range(2) for p in range(npair)]
        lss = [slice(PAIR_W * p, PAIR_W * (p + 1)) for _, _, p in chains]
        gam_t = [f["gam"][d][:, ls] for (f, d, _), ls in zip(chains, lss)]
        gam_s = [jnp.dot(ones, f["ghl"][d][0][:, ls] * m_sum[d], preferred_element_type=F32)
                 + jnp.dot(ones, f["ghl"][d][1][:, ls] * m_sum[d], preferred_element_type=F32)
                 for (f, d, _), ls in zip(chains, lss)]
        bx = [f["bexp"][d][:, ls] for (f, d, _), ls in zip(chains, lss)]
        dec = [jnp.where(m_tri[d], jnp.exp(jnp.minimum(gt - gs, 0.0)), 0.0)
               for (_, d, _), gt, gs in zip(chains, gam_t, gam_s)]
        a = [jnp.where(m_strict[d], b_ * dc * f["kk"][p], 0.0) for (f, d, p), b_, dc in zip(chains, bx, dec)]
        dgn = [jnp.where(blk16, a_, 0.0) for a_ in a]
        lo = [a_ - g_ for a_, g_ in zip(a, dgn)]
        d2 = [mm(g_, g_) for g_ in dgn]
        t1 = [mm(eye - g_, eye + s_) for g_, s_ in zip(dgn, d2)]
        d4 = [mm(s_, s_) for s_ in d2]
        t2 = [mm(t_, eye + s_) for t_, s_ in zip(t1, d4)]
        d8 = [mm(s_, s_) for s_ in d4]
        t_inv = [mm(t_, eye + s_) for t_, s_ in zip(t2, d8)]
        m = [mm(t_, l_) for t_, l_ in zip(t_inv, lo)]
        m2 = [mm(m_, m_) for m_ in m]
        w1 = [mm(eye - m_, eye + s_) for m_, s_ in zip(m, m2)]
        w = [mm(w_, t_) for w_, t_ in zip(w1, t_inv)]
        egam = [jnp.exp(gt) for gt in gam_t]
        solv = [mm(w_, b_ * f["v"][:, ls]) for (f, _, _), w_, b_, ls in zip(chains, w, bx, lss)]
        solk = [mm(w_, b_ * eg * f["kn"][:, ls]) for (f, _, _), w_, b_, eg, ls in zip(chains, w, bx, egam, lss)]
        for i, (f, d, p) in enumerate(chains):
            ls, row = lss[i], f["row"]
            e = CHUNK - 1 if d == 0 else 0
            g_end = gam_t[i][e:e + 1, :]
            sv_ref[d, pl.ds(row, CHUNK), ls] = solv[i]
            sk_ref[d, pl.ds(row, CHUNK), ls] = solk[i].astype(BF16)
            p_ref[d, pl.ds(row, CHUNK), ls] = jnp.where(m_tri[d], f["qk"][p] * dec[i], 0.0).astype(BF16)
            qe_ref[d, pl.ds(row, CHUNK), ls] = (egam[i] * f["qn"][:, ls]).astype(BF16)
            kh_ref[d, pl.ds(row, CHUNK), ls] = f["kn"][:, ls] * jnp.exp(g_end - gam_t[i])
            dg_ref[d, pl.ds(pl.multiple_of(row // 8, 8), 8), ls] = jnp.broadcast_to(jnp.exp(g_end), (8, PAIR_W))

    for seg_i, x_ref in enumerate(x_refs):
        def prep_body(c, carry, seg_i=seg_i):
            prep(seg_i, c)
            return carry

        lax.fori_loop(0, x_ref.shape[1] // (CHUNK * PREP_CHUNKS), prep_body, 0)

    st_ref[...] = jnp.zeros(st_ref.shape, F32)

    def scan_body(seg_i, i0, nch):
        chains = [(d, p) for d in range(2) for p in range(npair)]
        lss = [slice(PAIR_W * p, PAIR_W * (p + 1)) for _, p in chains]
        st = [st_ref[d, p] for d, p in chains]
        pending = []
        for k in range(SCAN_UNROLL):
            i = i0 * SCAN_UNROLL + k
            rows = [row_off[seg_i] + pl.multiple_of((i if d == 0 else nch - 1 - i) * CHUNK, CHUNK)
                    for d, _ in chains]
            stb = [s_.astype(BF16) for s_ in st]
            u = [sv_ref[d, pl.ds(r, CHUNK), ls]
                 - jnp.dot(sk_ref[d, pl.ds(r, CHUNK), ls], sb, preferred_element_type=F32)
                 for (d, _), r, ls, sb in zip(chains, rows, lss, stb)]
            ku = [_dot(kh_ref[d, pl.ds(r, CHUNK), ls].T, u_) for (d, _), r, ls, u_ in zip(chains, rows, lss, u)]
            dgr = [dg_ref[d, pl.ds(pl.multiple_of(r // 8, 8), 8), ls][0:1, :]
                   for (d, _), r, ls in zip(chains, rows, lss)]
            st = [s_ * g_ + jnp.where(bd2, k_, 0.0) for s_, g_, k_ in zip(st, dgr, ku)]
            pending.append((rows, stb, u))
        for rows, stb, u in pending:
            oq = [jnp.dot(qe_ref[d, pl.ds(r, CHUNK), ls], sb, preferred_element_type=F32)
                  for (d, _), r, ls, sb in zip(chains, rows, lss, stb)]
            ou = [jnp.dot(p_ref[d, pl.ds(r, CHUNK), ls], pk2(u_), preferred_element_type=F32)
                  for (d, _), r, ls, u_ in zip(chains, rows, lss, u)]
            for j, (d, _) in enumerate(chains):
                dst = of_ref if d == 0 else ob_ref
                dst[pl.ds(rows[j], CHUNK), lss[j]] = oq[j] + ou[j]
        for j, (d, p) in enumerate(chains):
            st_ref[d, p] = st[j]

    for seg_i, x_ref in enumerate(x_refs):
        nch = x_ref.shape[1] // CHUNK

        def scan_iter(i, carry, seg_i=seg_i, nch=nch):
            scan_body(seg_i, i, nch)
            return carry

        lax.fori_loop(0, nch // SCAN_UNROLL, scan_iter, 0)

    gain = gain_ref[...]
    blk = 256
    outs = ((oc_ref, xc_ref, 0), (ol_ref, xl_ref, lc))
    for o_ref, x_ref, off in outs:
        if o_ref is None:
            continue

        def fin(i, carry, o_ref=o_ref, x_ref=x_ref, off=off):
            r0 = pl.multiple_of(i * blk, blk)
            o = of_ref[pl.ds(off + r0, blk), :] + ob_ref[pl.ds(off + r0, blk), :]
            gate = x_ref[0, pl.ds(r0, blk), 3 * MIX_W:4 * MIX_W]
            o_ref[0, pl.ds(r0, blk), :] = _finish_rows(o, gate, gain, seg).astype(BF16)
            return carry

        lax.fori_loop(0, x_ref.shape[1] // blk, fin, 0)


def _gdn2(x_ctx, x_lat, s_ctx, s_lat, cw, alog, dtb, sel, selb, gain, seg, need_ctx):
    b, lc, _ = x_ctx.shape
    ll = x_lat.shape[1]
    lt = lc + ll
    in_specs = [pl.BlockSpec((1, lc, 1024), lambda i: (i, 0, 0)),
                pl.BlockSpec((1, ll, 1024), lambda i: (i, 0, 0)),
                pl.BlockSpec((1, lc, LANES), lambda i: (i, 0, 0)),
                pl.BlockSpec((1, ll, LANES), lambda i: (i, 0, 0)),
                pl.BlockSpec((8, 3 * MIX_W), lambda i: (0, 0)),
                pl.BlockSpec((2, 1, LANES), lambda i: (0, 0, 0)),
                pl.BlockSpec((2, 1, LANES), lambda i: (0, 0, 0)),
                pl.BlockSpec((2, LANES, MIX_W), lambda i: (0, 0, 0)),
                pl.BlockSpec((2, LANES, MIX_W), lambda i: (0, 0, 0)),
                pl.BlockSpec((1, MIX_W), lambda i: (0, 0)),
                pl.BlockSpec((LANES, LANES), lambda i: (0, 0))]
    out_shape = [jax.ShapeDtypeStruct((b, ll, MIX_W), BF16)]
    out_specs = [pl.BlockSpec((1, ll, MIX_W), lambda i: (i, 0, 0))]
    if need_ctx:
        out_shape.insert(0, jax.ShapeDtypeStruct((b, lc, MIX_W), BF16))
        out_specs.insert(0, pl.BlockSpec((1, lc, MIX_W), lambda i: (i, 0, 0)))
    res = pl.pallas_call(
        functools.partial(_gdn2_kernel, need_ctx=need_ctx),
        grid=(b,), in_specs=in_specs, out_specs=out_specs, out_shape=out_shape,
        scratch_shapes=[pltpu.VMEM((2, lt, MIX_W), BF16), pltpu.VMEM((2, lt, MIX_W), BF16),
                        pltpu.VMEM((2, lt, MIX_W), BF16), pltpu.VMEM((2, lt, MIX_W), F32),
                        pltpu.VMEM((2, lt, MIX_W), F32), pltpu.VMEM((2, lt // 8, MIX_W), F32),
                        pltpu.VMEM((lt, MIX_W), F32), pltpu.VMEM((lt, MIX_W), F32),
                        pltpu.VMEM((2, MIX_W // PAIR_W, PAIR_W, PAIR_W), F32)],
        compiler_params=_cparams(("parallel",)),
        name="gdn_scan",
    )(x_ctx, x_lat, s_ctx, s_lat, cw, alog, dtb, sel, selb, gain, seg)
    return (res[0], res[1]) if need_ctx else (None, res[0])


def _outproj_kernel(gla_ref, gdn_ref, att_ref, h_ref, g_ref, w_ref, o_ref):
    y = (jnp.dot(gla_ref[...], w_ref[0:256, :], preferred_element_type=F32)
         + jnp.dot(gdn_ref[...], w_ref[256:512, :], preferred_element_type=F32)
         + jnp.dot(att_ref[...], w_ref[512:1024, :], preferred_element_type=F32))
    o_ref[...] = h_ref[...] + g_ref[0] * y


def _outproj(gla, gdn, att, h2d, mod144, mod_row_fn, w, seq_len, tm):
    rows, d = h2d.shape
    tiles_per_seq = seq_len // tm
    return pl.pallas_call(
        _outproj_kernel,
        grid=(rows // tm,),
        in_specs=[pl.BlockSpec((tm, 256), lambda i: (i, 0)),
                  pl.BlockSpec((tm, 256), lambda i: (i, 0)),
                  pl.BlockSpec((tm, 512), lambda i: (i, 0)),
                  pl.BlockSpec((tm, d), lambda i: (i, 0)),
                  pl.BlockSpec((1, 1, d), lambda i: (mod_row_fn(i // tiles_per_seq) * 6 + 2, 0, 0)),
                  pl.BlockSpec((d, d), lambda i: (0, 0))],
        out_specs=pl.BlockSpec((tm, d), lambda i: (i, 0)),
        out_shape=jax.ShapeDtypeStruct((rows, d), F32),
        compiler_params=_cparams(("parallel",)),
        name="outproj",
    )(gla, gdn, att, h2d, mod144, w)


def _norm_mod(h_ref, sh_ref, sc_ref, gain_ref):
    x = h_ref[...]
    ms = jnp.mean(x * x, axis=-1, keepdims=True)
    return x * lax.rsqrt(ms + EPS) * (gain_ref[...] * (1.0 + sc_ref[0])) + sh_ref[0]


def _swiglu_part(x, wg, wu, wd):
    gg = jnp.dot(x, wg[...], preferred_element_type=F32)
    uu = jnp.dot(x, wu[...], preferred_element_type=F32)
    hid = (_silu(gg) * uu).astype(BF16)
    return jnp.dot(hid, wd[...], preferred_element_type=F32)


def _ffn_kernel(h_ref, sh_ref, sc_ref, g_ref, gain_ref, wg_ref, wu_ref, wd_ref, o_ref, b_scr, acc_scr, *, nf):
    f = pl.program_id(1)

    @pl.when(f == 0)
    def _():
        b_scr[...] = _norm_mod(h_ref, sh_ref, sc_ref, gain_ref).astype(BF16)
        acc_scr[...] = jnp.zeros(acc_scr.shape, F32)

    b = b_scr[...]
    acc_scr[...] += _swiglu_part(b, wg_ref, wu_ref, wd_ref)

    @pl.when(f == nf - 1)
    def _():
        o_ref[...] = h_ref[...] + g_ref[0] * acc_scr[...]


def _ffn(h2d, mod144, mod_row_fn, gain, w_gu, w_down, seq_len, tm, tf):
    rows, d = h2d.shape
    dff = w_down.shape[0]
    nf = dff // tf
    tiles_per_seq = seq_len // tm

    def mod_spec(k):
        return pl.BlockSpec((1, 1, d), lambda i, f: (mod_row_fn(i // tiles_per_seq) * 6 + k, 0, 0))

    return pl.pallas_call(
        functools.partial(_ffn_kernel, nf=nf),
        grid=(rows // tm, nf),
        in_specs=[pl.BlockSpec((tm, d), lambda i, f: (i, 0)),
                  mod_spec(3), mod_spec(4), mod_spec(5),
                  pl.BlockSpec((1, d), lambda i, f: (0, 0)),
                  pl.BlockSpec((d, tf), lambda i, f: (0, f)),
                  pl.BlockSpec((d, tf), lambda i, f: (0, nf + f)),
                  pl.BlockSpec((tf, d), lambda i, f: (f, 0))],
        out_specs=pl.BlockSpec((tm, d), lambda i, f: (i, 0)),
        out_shape=jax.ShapeDtypeStruct((rows, d), F32),
        scratch_shapes=[pltpu.VMEM((tm, d), BF16), pltpu.VMEM((tm, d), F32)],
        compiler_params=_cparams(("parallel", "arbitrary")),
        name="ffn",
    )(h2d, mod144, mod144, mod144, gain, w_gu, w_gu, w_down)


MOE_TILE = 512
COMBINE_TOKENS = 256


def _router_kernel(h_ref, sh_ref, sc_ref, gain_ref, wr_ref, br_ref, b_ref, route_ref):
    lane = _iota((1, LANES), 1)
    lane_f = lane.astype(F32)
    b = _norm_mod(h_ref, sh_ref, sc_ref, gain_ref)
    b_ref[...] = b
    logits = _dot3(b, wr_ref[...]) + br_ref[...]
    logits = jnp.where(lane < N_EXPERTS, logits, -jnp.inf)
    m1 = jnp.max(logits, axis=-1, keepdims=True)
    i1 = jnp.min(jnp.where(logits == m1, lane_f, float(LANES)), axis=-1, keepdims=True)
    rest = jnp.where(lane_f == i1, -jnp.inf, logits)
    m2 = jnp.max(rest, axis=-1, keepdims=True)
    i2 = jnp.min(jnp.where(rest == m2, lane_f, float(LANES)), axis=-1, keepdims=True)
    t = jnp.exp(m2 - m1)
    w1 = 1.0 / (1.0 + t)
    route_ref[...] = (jnp.where(lane == 0, i1, 0.0) + jnp.where(lane == 1, i2, 0.0)
                      + jnp.where(lane == 2, w1, 0.0) + jnp.where(lane == 3, t * w1, 0.0))


def _router(h2d, mod144, mod_row_fn, gain, w_router, b_router, seq_len, tm):
    rows, d = h2d.shape
    tiles_per_seq = seq_len // tm

    def mod_spec(k):
        return pl.BlockSpec((1, 1, d), lambda i: (mod_row_fn(i // tiles_per_seq) * 6 + k, 0, 0))

    return pl.pallas_call(
        _router_kernel,
        grid=(rows // tm,),
        in_specs=[pl.BlockSpec((tm, d), lambda i: (i, 0)), mod_spec(3), mod_spec(4),
                  pl.BlockSpec((1, d), lambda i: (0, 0)),
                  pl.BlockSpec((d, LANES), lambda i: (0, 0)),
                  pl.BlockSpec((1, LANES), lambda i: (0, 0))],
        out_specs=(pl.BlockSpec((tm, d), lambda i: (i, 0)), pl.BlockSpec((tm, LANES), lambda i: (i, 0))),
        out_shape=(jax.ShapeDtypeStruct((rows, d), F32), jax.ShapeDtypeStruct((rows, LANES), F32)),
        compiler_params=_cparams(("parallel",)),
        name="moe_router",
    )(h2d, mod144, mod144, gain, w_router, b_router)


DISPATCH_TOKENS = 512


def _moe_dispatch_kernel(pos_ref, b_ref, xs_in_ref, xs_ref, sem):
    del xs_in_ref
    n = DISPATCH_TOKENS
    for r in range(2 * n):
        pltpu.make_async_copy(b_ref.at[pl.ds(r % n, 1)], xs_ref.at[pl.ds(pos_ref[0, 0, r], 1)],
                              sem).start(priority=r % 2)
    for _ in range(2):
        pltpu.make_async_copy(b_ref, xs_ref.at[pl.ds(0, n)], sem).wait()


def _moe_dispatch(pos2, b, p_rows):
    rows, d = b.shape
    tm = DISPATCH_TOKENS
    return pl.pallas_call(
        _moe_dispatch_kernel,
        grid=(rows // tm,),
        in_specs=[pl.BlockSpec((1, 1, 2 * tm), lambda i: (i, 0, 0), memory_space=pltpu.SMEM),
                  pl.BlockSpec((tm, d), lambda i: (i, 0)),
                  pl.BlockSpec(memory_space=pl.ANY)],
        out_specs=pl.BlockSpec(memory_space=pl.ANY),
        out_shape=jax.ShapeDtypeStruct((p_rows, d), F32),
        scratch_shapes=[pltpu.SemaphoreType.DMA(())],
        input_output_aliases={2: 0},
        compiler_params=_cparams(("arbitrary",)),
        name="moe_dispatch",
    )(pos2, b, jnp.zeros((p_rows, d), F32))


def _moe_group_kernel(te_ref, nt_ref, xs_ref, wg_ref, wu_ref, wd_ref, ys_ref, xb_scr, acc_scr, *, nf):
    i = pl.program_id(0)
    f = pl.program_id(1)
    used = i < nt_ref[0]

    @pl.when(used & (f == 0))
    def _():
        xb_scr[...] = xs_ref[...].astype(BF16)

    @pl.when(used)
    def _():
        x = xb_scr[...]
        part = _swiglu_part(x, wg_ref.at[0], wu_ref.at[0], wd_ref.at[0])

        @pl.when(f == 0)
        def _():
            acc_scr[...] = part

        @pl.when(f > 0)
        def _():
            acc_scr[...] += part

    @pl.when(used & (f == nf - 1))
    def _():
        ys_ref[...] = acc_scr[...]

    @pl.when(jnp.logical_not(used) & (f == nf - 1))
    def _():
        ys_ref[...] = jnp.zeros(ys_ref.shape, F32)


def _moe_group(tile_expert, n_tiles_used, xs, w_gu, w_down, tf):
    p, d = xs.shape
    dff = w_down.shape[1]
    nf = dff // tf
    grid_spec = pltpu.PrefetchScalarGridSpec(
        num_scalar_prefetch=2,
        grid=(p // MOE_TILE, nf),
        in_specs=[pl.BlockSpec((MOE_TILE, d), lambda i, f, te, nt: (i, 0)),
                  pl.BlockSpec((1, d, tf), lambda i, f, te, nt: (te[i], 0, f)),
                  pl.BlockSpec((1, d, tf), lambda i, f, te, nt: (te[i], 0, nf + f)),
                  pl.BlockSpec((1, tf, d), lambda i, f, te, nt: (te[i], f, 0))],
        out_specs=pl.BlockSpec((MOE_TILE, d), lambda i, f, te, nt: (i, 0)),
        scratch_shapes=[pltpu.VMEM((MOE_TILE, d), BF16), pltpu.VMEM((MOE_TILE, d), F32)])
    return pl.pallas_call(
        functools.partial(_moe_group_kernel, nf=nf),
        grid_spec=grid_spec,
        out_shape=jax.ShapeDtypeStruct((p, d), F32),
        compiler_params=_cparams(("arbitrary", "arbitrary")),
        name="moe_experts",
    )(tile_expert, n_tiles_used, xs, w_gu, w_gu, w_down)


def _moe_combine_kernel(pos_ref, ys_ref, h_ref, route_ref, g_ref, o_ref, buf, sem):
    n = 2 * COMBINE_TOKENS

    for r in range(n):
        pltpu.make_async_copy(ys_ref.at[pl.ds(pos_ref[0, 0, r], 1)], buf.at[pl.ds(r, 1)], sem).start(priority=r % 2)
    pltpu.make_async_copy(ys_ref.at[pl.ds(0, n)], buf, sem).wait()
    w1 = route_ref[:, 2:3]
    w2 = route_ref[:, 3:4]
    o_ref[...] = h_ref[...] + g_ref[0] * (w1 * buf[0:COMBINE_TOKENS, :] + w2 * buf[COMBINE_TOKENS:n, :])


def _moe_combine(pos, ys, h2d, route, mod144, mod_row_fn, seq_len):
    rows, d = h2d.shape
    tm = COMBINE_TOKENS
    tiles_per_seq = seq_len // tm
    steps = rows // tm
    return pl.pallas_call(
        _moe_combine_kernel,
        grid=(steps,),
        in_specs=[pl.BlockSpec((1, 1, 2 * tm), lambda i: (i, 0, 0), memory_space=pltpu.SMEM),
                  pl.BlockSpec(memory_space=pl.ANY),
                  pl.BlockSpec((tm, d), lambda i: (i, 0)),
                  pl.BlockSpec((tm, LANES), lambda i: (i, 0)),
                  pl.BlockSpec((1, 1, d), lambda i: (mod_row_fn(i // tiles_per_seq) * 6 + 5, 0, 0))],
        out_specs=pl.BlockSpec((tm, d), lambda i: (i, 0)),
        out_shape=jax.ShapeDtypeStruct((rows, d), F32),
        scratch_shapes=[pltpu.VMEM((2 * tm, d), F32), pltpu.SemaphoreType.DMA(())],
        compiler_params=_cparams(("arbitrary",)),
        name="moe_combine",
    )(pos, ys, h2d, route, mod144)


def _moe_routed(h2d, mod144, mod_row_fn, gain, w_router, b_router, w_gu, w_down, seq_len):
    rows, d = h2d.shape
    b, route = _router(h2d, mod144, mod_row_fn, gain, w_router, b_router, seq_len, min(512, seq_len))
    ex = jnp.concatenate([route[:, 0], route[:, 1]]).astype(jnp.int32)
    onehot = (ex[:, None] == jnp.arange(N_EXPERTS, dtype=jnp.int32)[None, :]).astype(jnp.int32)
    rank = jnp.sum((jnp.cumsum(onehot, axis=0) - onehot) * onehot, axis=1)
    counts = jnp.sum(onehot, axis=0)
    padded = ((counts + MOE_TILE - 1) // MOE_TILE) * MOE_TILE
    ends = jnp.cumsum(padded)
    starts = ends - padded
    pos = jnp.sum(onehot * starts[None, :], axis=1) + rank
    pos = pos.astype(jnp.int32)
    p_rows = 2 * rows + N_EXPERTS * MOE_TILE
    tile_first = jnp.arange(p_rows // MOE_TILE, dtype=jnp.int32) * MOE_TILE
    tile_expert = jnp.minimum(jnp.sum((tile_first[:, None] >= ends[None, :]).astype(jnp.int32), axis=1),
                              N_EXPERTS - 1).astype(jnp.int32)
    n_tiles_used = (ends[-1:] // MOE_TILE).astype(jnp.int32)

    def per_tile(tm):
        return jnp.concatenate([pos[:rows].reshape(rows // tm, 1, tm), pos[rows:].reshape(rows // tm, 1, tm)], axis=2)

    xs = _moe_dispatch(per_tile(DISPATCH_TOKENS), b, p_rows)
    ys = _moe_group(tile_expert, n_tiles_used, xs, w_gu, w_down, 1408)
    return _moe_combine(per_tile(COMBINE_TOKENS), ys, h2d, route, mod144, mod_row_fn, seq_len)


ATT_HEAD_ORDER = (0, 4, 1, 5, 2, 6, 3, 7)


def _layout_w_in(w):
    gla = w[:, 0:1024]
    glow = w[:, 1024:1056]
    gdn = w[:, 1056:2080]
    ab = w[:, 2080:2096]
    q = jnp.concatenate([w[:, 2096 + HEAD_DIM * h:2096 + HEAD_DIM * (h + 1)] for h in ATT_HEAD_ORDER], axis=1)
    kv = w[:, 2608:2864]
    pad = jnp.zeros((w.shape[0], LANES - 48), w.dtype)
    return jnp.concatenate([gla, gdn, q, kv, glow, ab, pad], axis=1).astype(BF16)


def _layout_w_out(w):
    att = [w[512 + HEAD_DIM * h:512 + HEAD_DIM * (h + 1)] for h in ATT_HEAD_ORDER]
    return jnp.concatenate([w[0:512]] + att, axis=0).astype(BF16)


def _rope_tables(seq_len):
    rows = seq_len // GRID_W
    row = jnp.repeat(jnp.arange(rows), GRID_W).astype(F32)
    col = jnp.tile(jnp.arange(GRID_W), rows).astype(F32)
    inv_freq = ROPE_THETA ** (-jnp.arange(0, HEAD_DIM // 2, 2, dtype=F32) / (HEAD_DIM // 2))
    ar = row[:, None] * inv_freq
    ac = col[:, None] * inv_freq
    cos = jnp.concatenate([jnp.cos(ar), jnp.cos(ar), jnp.cos(ac), jnp.cos(ac)], axis=-1)
    sin = jnp.concatenate([-jnp.sin(ar), jnp.sin(ar), -jnp.sin(ac), jnp.sin(ac)], axis=-1)
    return jnp.tile(cos, (1, 2)), jnp.tile(sin, (1, 2))


def _seg_matrix():
    i = np.arange(LANES)
    return jnp.asarray((i[:, None] // HEAD_DIM) == (i[None, :] // HEAD_DIM), dtype=BF16)


def _gdn_select():
    sel = np.zeros((2, LANES, MIX_W), np.float32)
    selb = np.zeros((2, LANES, MIX_W), np.float32)
    for d in range(2):
        for h in range(GDN_HEADS):
            sel[d, 32 + GDN_HEADS * d + h, HEAD_DIM * h:HEAD_DIM * (h + 1)] = 1.0
            selb[d, 40 + GDN_HEADS * d + h, HEAD_DIM * h:HEAD_DIM * (h + 1)] = 1.0
    return jnp.asarray(sel, BF16), jnp.asarray(selb, BF16)


def _lane_rows(vals, base):
    out = jnp.zeros((2, 1, LANES), F32)
    for d in range(2):
        out = out.at[d, 0, base + GDN_HEADS * d:base + GDN_HEADS * (d + 1)].set(vals[d].astype(F32))
    return out


def kernel(x, c, ctx, c_ctx, w_mod, b_mod, norm_mix, norm_ffn, w_in, gla_gate_up, gla_gate_bias, gla_out_gain,
           gdn_conv, gdn_a_log, gdn_dt_bias, gdn_out_gain, att_q_gain, att_k_gain, w_out, ffn_gate_up, ffn_down,
           moe_router, moe_router_bias, moe_gate_up, moe_down):
    bsz, seq, d = x.shape
    lctx = ctx.shape[1]
    depth = w_mod.shape[0]
    ctx_row = bsz

    mod_rows = ((bsz + 1 + 7) // 8) * 8
    cvec = jnp.concatenate([c, c_ctx[None, :], jnp.zeros((mod_rows - bsz - 1, d), F32)], axis=0)
    mods = _modulation(cvec, w_mod, b_mod)

    seg = _seg_matrix()
    tables = _rope_tables(seq)
    sel, selb = _gdn_select()
    lat_row = lambda b: b
    ctx_row_fn = lambda b: ctx_row

    h_lat = x.reshape(bsz * seq, d)
    h_ctx = ctx.reshape(bsz * lctx, d)
    for layer in range(depth):
        need_ctx = layer < depth - 1
        mod144 = mods[layer].reshape(mod_rows * 6, 1, d)
        w_p = _layout_w_in(w_in[layer])
        w_o = _layout_w_out(w_out[layer])
        hg = jnp.concatenate([jnp.tile(att_q_gain[layer], ATT_Q_HEADS) * (HEAD_DIM ** -0.5 * LOG2E),
                              jnp.tile(att_k_gain[layer], ATT_KV_HEADS)])[None, :].astype(F32)
        gain_mix = norm_mix[layer][None, :]
        gain_ffn = norm_ffn[layer][None, :]

        gla_l, gdn_l, q_l, kv_l, sm_l = _inproj(h_lat, mod144, lat_row, gain_mix, w_p, hg, seg, tables, seq, 256)
        gla_c, gdn_c, q_c, kv_c, sm_c = _inproj(h_ctx, mod144, ctx_row_fn, gain_mix, w_p, hg, seg, None, lctx, 256)

        r3 = lambda t, n: t.reshape(bsz, n, t.shape[-1])
        wg = jnp.zeros((2, LANES, MIX_W), F32)
        for dd in range(2):
            wg = wg.at[dd, GLA_GATE_RANK * dd:GLA_GATE_RANK * (dd + 1), :].set(gla_gate_up[layer, dd].astype(F32))
        bg = gla_gate_bias[layer].reshape(2, 1, MIX_W).astype(F32)
        gla_gain = jnp.tile(gla_out_gain[layer], GLA_HEADS)[None, :].astype(F32)
        o_gla_c, o_gla_l = _gla(r3(gla_c, lctx), r3(gla_l, seq), r3(sm_c, lctx), r3(sm_l, seq),
                                wg, bg, gla_gain, seg, need_ctx)

        cw = jnp.concatenate([gdn_conv[layer].astype(F32), jnp.zeros((8 - SHORT_CONV, 3 * MIX_W), F32)], axis=0)
        alog = _lane_rows(gdn_a_log[layer], 32)
        dtb = _lane_rows(gdn_dt_bias[layer], 32)
        gdn_gain = jnp.tile(gdn_out_gain[layer], GDN_HEADS)[None, :].astype(F32)
        o_gdn_c, o_gdn_l = _gdn2(r3(gdn_c, lctx), r3(gdn_l, seq), r3(sm_c, lctx), r3(sm_l, seq),
                                cw, alog, dtb, sel, selb, gdn_gain, seg, need_ctx)

        o_att_l = _attention(r3(q_l, seq), [r3(kv_l, seq), r3(kv_c, lctx)], 256)
        h_lat = _outproj(o_gla_l.reshape(-1, MIX_W), o_gdn_l.reshape(-1, MIX_W), o_att_l.reshape(-1, 512),
                         h_lat, mod144, lat_row, w_o, seq, 512)
        if need_ctx:
            o_att_c = _attention(r3(q_c, lctx), [r3(kv_c, lctx)], 128)
            h_ctx = _outproj(o_gla_c.reshape(-1, MIX_W), o_gdn_c.reshape(-1, MIX_W), o_att_c.reshape(-1, 512),
                             h_ctx, mod144, ctx_row_fn, w_o, lctx, 256)

        j = layer // 2
        if layer % 2 == 0:
            w_gu = ffn_gate_up[j].astype(BF16)
            w_dn = ffn_down[j].astype(BF16)
            h_lat = _ffn(h_lat, mod144, lat_row, gain_ffn, w_gu, w_dn, seq, 512, 1408)
            if need_ctx:
                h_ctx = _ffn(h_ctx, mod144, ctx_row_fn, gain_ffn, w_gu, w_dn, lctx, 256, 1408)
        else:
            w_gu = moe_gate_up[j].astype(BF16)
            w_dn = moe_down[j].astype(BF16)
            w_r = jnp.concatenate([moe_router[j].astype(F32), jnp.zeros((d, LANES - N_EXPERTS), F32)], axis=1)
            b_r = jnp.concatenate([moe_router_bias[j].astype(F32), jnp.zeros((LANES - N_EXPERTS,), F32)])[None, :]
            h_lat = _moe_routed(h_lat, mod144, lat_row, gain_ffn, w_r, b_r, w_gu, w_dn, seq)
            if need_ctx:
                h_ctx = _moe_routed(h_ctx, mod144, ctx_row_fn, gain_ffn, w_r, b_r, w_gu, w_dn, lctx)
    return h_lat.reshape(bsz, seq, d)
```

```python
import functools

import numpy as np
import jax
import jax.numpy as jnp
from jax import lax
from jax.experimental import pallas as pl
from jax.experimental.pallas import tpu as pltpu

F32 = jnp.float32
BF16 = jnp.bfloat16

GRID_W = 64
HEAD_DIM = 64
CHUNK = 64
SUB = 16
EPS = 1e-6
GLA_HEADS = 4
GLA_GATE_RANK = 16
GLA_TAU = 16.0
GDN_HEADS = 4
SHORT_CONV = 5
ATT_Q_HEADS = 8
ATT_KV_HEADS = 2
ROPE_THETA = 10000.0
N_EXPERTS = 8
MIX_W = GLA_HEADS * HEAD_DIM
LOG2E = 1.4426950408889634
EXP_CLAMP = 80.0

LANES = 128
V7X_VMEM_BYTES = 64 * 1024 * 1024
VMEM_LIMIT = V7X_VMEM_BYTES - 8 * 1024 * 1024


INPROJ_ROWS = 256
ROW_TILE = 512
ATT_Q_ROWS = 256
FF_TILE = 1408


def _rows(limit, seq_len):
    return min(limit, seq_len)


def _cparams(sem):
    return pltpu.CompilerParams(dimension_semantics=sem, vmem_limit_bytes=VMEM_LIMIT)


def _silu(x):
    return x / (1.0 + jnp.exp(-x))


def _sigmoid(x):
    return 1.0 / (1.0 + jnp.exp(-x))


def _softplus(x):
    return jnp.maximum(x, 0.0) + jnp.log(1.0 + jnp.exp(-jnp.abs(x)))


def _dot(a, b):
    return jnp.dot(a.astype(BF16), b.astype(BF16), preferred_element_type=F32)


def _dot_nt(a, b):
    return lax.dot_general(a.astype(BF16), b.astype(BF16), (((1,), (1,)), ((), ())),
                           preferred_element_type=F32)


def _split(x):
    hi = x.astype(BF16)
    lo = (x - hi.astype(F32)).astype(BF16)
    return hi, lo


def _dot_xhl(x, w):
    hi, lo = _split(x)
    w = w.astype(BF16)
    return (jnp.dot(hi, w, preferred_element_type=F32) + jnp.dot(lo, w, preferred_element_type=F32))


def _dot_whl(w, x):
    hi, lo = _split(x)
    w = w.astype(BF16)
    return (jnp.dot(w, hi, preferred_element_type=F32) + jnp.dot(w, lo, preferred_element_type=F32))


def _dot3(a, b):
    ah, al = _split(a)
    bh, bl = _split(b)
    return (jnp.dot(ah, bh, preferred_element_type=F32) + jnp.dot(ah, bl, preferred_element_type=F32)
            + jnp.dot(al, bh, preferred_element_type=F32))


def _seg_sum64(sq, seg, split=True):
    outs = []
    for j in range(sq.shape[1] // LANES):
        part = sq[:, LANES * j:LANES * (j + 1)]
        outs.append(_dot_xhl(part, seg) if split else _dot(part, seg))
    return outs[0] if len(outs) == 1 else jnp.concatenate(outs, axis=1)


def _iota(shape, dim):
    return lax.broadcasted_iota(jnp.int32, shape, dim)


def _mod_kernel(c_ref, w_ref, b_ref, o_ref):
    s = _silu(c_ref[...])
    o_ref[0] = _dot(s, w_ref[0]) + b_ref[0]


def _modulation(cvec, w_mod, b_mod):
    depth, d, n = w_mod.shape
    rows = cvec.shape[0]
    tn = 1536
    return pl.pallas_call(
        _mod_kernel,
        grid=(depth, n // tn),
        in_specs=[pl.BlockSpec((rows, d), lambda l, j: (0, 0)),
                  pl.BlockSpec((1, d, tn), lambda l, j: (l, 0, j)),
                  pl.BlockSpec((1, 1, tn), lambda l, j: (l, 0, j))],
        out_specs=pl.BlockSpec((1, rows, tn), lambda l, j: (l, 0, j)),
        out_shape=jax.ShapeDtypeStruct((depth, rows, n), F32),
        compiler_params=_cparams(("arbitrary", "arbitrary")),
        name="modulation",
    )(cvec, w_mod, b_mod.reshape(depth, 1, n))


def _swap16(n, lane):
    fwd = pltpu.roll(n, LANES - 16, 1)
    bwd = pltpu.roll(n, 16, 1)
    return jnp.where((lane % 32) < 16, fwd, bwd)


def _inproj_kernel(*refs, rope):
    if rope:
        (h_ref, sh_ref, sc_ref, gain_ref, w_ref, hg_ref, seg_ref, cos_ref, sin_ref,
         gla_ref, gdn_ref, q_ref, kv_ref, small_ref) = refs
    else:
        (h_ref, sh_ref, sc_ref, gain_ref, w_ref, hg_ref, seg_ref,
         gla_ref, gdn_ref, q_ref, kv_ref, small_ref) = refs
    x = h_ref[...]
    ms = jnp.mean(x * x, axis=-1, keepdims=True)
    a = x * lax.rsqrt(ms + EPS) * (gain_ref[...] * (1.0 + sc_ref[0])) + sh_ref[0]
    p = jnp.dot(a.astype(BF16), w_ref[...], preferred_element_type=F32)
    gla_ref[...] = p[:, 0:1024]
    gdn_ref[...] = p[:, 1024:2048]
    small_ref[...] = p[:, 2816:2944]
    seg = seg_ref[...]
    lane = _iota((1, LANES), 1)
    outs = []
    for j in range(5):
        t = p[:, 2048 + LANES * j:2048 + LANES * (j + 1)]
        ss = _dot_xhl(t * t, seg)
        n = t * lax.rsqrt(ss * (1.0 / HEAD_DIM) + EPS) * hg_ref[:, LANES * j:LANES * (j + 1)]
        if rope:
            n = n * cos_ref[...] + _swap16(n, lane) * sin_ref[...]
        outs.append(n)
    q_ref[...] = jnp.concatenate(outs[:4], axis=1).astype(BF16)
    kv_ref[...] = jnp.concatenate([outs[4], p[:, 2688:2816]], axis=1).astype(BF16)


def _inproj(h2d, mod144, mod_row_fn, gain, w_p, hg, seg, tables, seq_len, tm):
    rows, d = h2d.shape
    n_all = w_p.shape[1]
    rope = tables is not None
    tiles_per_seq = seq_len // tm
    in_specs = [pl.BlockSpec((tm, d), lambda i: (i, 0)),
                pl.BlockSpec((1, 1, d), lambda i: (mod_row_fn(i // tiles_per_seq) * 6 + 0, 0, 0)),
                pl.BlockSpec((1, 1, d), lambda i: (mod_row_fn(i // tiles_per_seq) * 6 + 1, 0, 0)),
                pl.BlockSpec((1, d), lambda i: (0, 0)),
                pl.BlockSpec((d, n_all), lambda i: (0, 0)),
                pl.BlockSpec((1, 640), lambda i: (0, 0)),
                pl.BlockSpec((LANES, LANES), lambda i: (0, 0))]
    args = [h2d, mod144, mod144, gain, w_p, hg, seg]
    if rope:
        in_specs += [pl.BlockSpec((tm, LANES), lambda i: (i % tiles_per_seq, 0)),
                     pl.BlockSpec((tm, LANES), lambda i: (i % tiles_per_seq, 0))]
        args += list(tables)
    out_shape = (jax.ShapeDtypeStruct((rows, 1024), F32), jax.ShapeDtypeStruct((rows, 1024), F32),
                 jax.ShapeDtypeStruct((rows, 512), BF16), jax.ShapeDtypeStruct((rows, 256), BF16),
                 jax.ShapeDtypeStruct((rows, LANES), F32))
    out_specs = (pl.BlockSpec((tm, 1024), lambda i: (i, 0)), pl.BlockSpec((tm, 1024), lambda i: (i, 0)),
                 pl.BlockSpec((tm, 512), lambda i: (i, 0)), pl.BlockSpec((tm, 256), lambda i: (i, 0)),
                 pl.BlockSpec((tm, LANES), lambda i: (i, 0)))
    return pl.pallas_call(
        functools.partial(_inproj_kernel, rope=rope),
        grid=(rows // tm,), in_specs=in_specs, out_specs=out_specs, out_shape=out_shape,
        compiler_params=_cparams(("parallel",)),
        name="inproj_rope" if rope else "inproj",
    )(*args)


ATT_CHUNKS_PER_DOT = 1


def _attn_kernel(*refs, nkv, tq):
    q_ref = refs[0]
    kv_refs = refs[1:1 + nkv]
    o_ref = refs[1 + nkv]
    q = q_ref[0]
    lane = _iota((1, LANES), 1)
    mlo = (lane < HEAD_DIM).astype(BF16)
    mhi = (lane >= HEAD_DIM).astype(BF16)
    kvs = [r[0] for r in kv_refs]
    def scores(jj):
        pieces = []
        for j in range(jj, jj + ATT_CHUNKS_PER_DOT):
            qc = q[:, LANES * j:LANES * (j + 1)]
            pieces += [qc * mlo, qc * mhi]
        q_all = jnp.concatenate(pieces, axis=0)
        return [lax.dot_general(q_all, kv[:, 0:LANES], (((1,), (1,)), ((), ())), preferred_element_type=F32)
                for kv in kvs]

    outs = []
    starts = list(range(0, 4, ATT_CHUNKS_PER_DOT))
    ss_next = scores(starts[0])
    for n, jj in enumerate(starts):
        ss = ss_next
        if n + 1 < len(starts):
            ss_next = scores(starts[n + 1])
        m = functools.reduce(jnp.maximum, [jnp.max(s, axis=-1, keepdims=True) for s in ss])
        ps = [jnp.exp2(s - m) for s in ss]
        l = functools.reduce(lambda a, b: a + b, [jnp.sum(p, axis=-1, keepdims=True) for p in ps])
        o = functools.reduce(lambda a, b: a + b,
                             [jnp.dot(p.astype(BF16), kv[:, LANES:2 * LANES], preferred_element_type=F32)
                              for p, kv in zip(ps, kvs)])
        o = o / l
        for j in range(ATT_CHUNKS_PER_DOT):
            outs.append(jnp.where(lane < HEAD_DIM, o[2 * j * tq:(2 * j + 1) * tq],
                                  o[(2 * j + 1) * tq:(2 * j + 2) * tq]))
    o_ref[0] = jnp.concatenate(outs, axis=1).astype(BF16)


def _attention(q, kvs, tq):
    b, lq, _ = q.shape
    in_specs = [pl.BlockSpec((1, tq, 512), lambda i, j: (i, j, 0))]
    for kv in kvs:
        in_specs.append(pl.BlockSpec((1, kv.shape[1], 256), lambda i, j: (i, 0, 0)))
    return pl.pallas_call(
        functools.partial(_attn_kernel, nkv=len(kvs), tq=tq),
        grid=(b, lq // tq), in_specs=in_specs,
        out_specs=pl.BlockSpec((1, tq, 512), lambda i, j: (i, j, 0)),
        out_shape=jax.ShapeDtypeStruct((b, lq, 512), BF16),
        compiler_params=_cparams(("parallel", "arbitrary")),
        name="attention",
    )(q, *kvs)


def _head_masks():
    lane = _iota((1, MIX_W), 1)
    return [(lane // HEAD_DIM == h).astype(F32) for h in range(GLA_HEADS)]


def _blockdiag_mask():
    r = _iota((MIX_W, MIX_W), 0) // HEAD_DIM
    c = _iota((MIX_W, MIX_W), 1) // HEAD_DIM
    return r == c


def _finish_rows(o, gate, gain, seg):
    ss = _seg_sum64(o * o, seg)
    return o * lax.rsqrt(ss * (1.0 / HEAD_DIM) + EPS) * gain * _silu(gate)


PAIR_W = 2 * HEAD_DIM
PREP_CHUNKS = 4
SCAN_UNROLL = 2


def _gla_kernel(*refs, need_ctx):
    if need_ctx:
        (xc_ref, xl_ref, sc_ref, sl_ref, wg_ref, bg_ref, gain_ref, seg_ref,
         oc_ref, ol_ref, of_ref, ob_ref, qh_ref, kh_ref, dg_ref, st_ref) = refs
    else:
        (xc_ref, xl_ref, sc_ref, sl_ref, wg_ref, bg_ref, gain_ref, seg_ref,
         ol_ref, of_ref, ob_ref, qh_ref, kh_ref, dg_ref, st_ref) = refs
        oc_ref = None
    lc = xc_ref.shape[1]
    x_refs = (xc_ref, xl_ref)
    s_refs = (sc_ref, sl_ref)
    row_off = (0, lc)
    o_refs = (of_ref, ob_ref)
    hmask = _head_masks()
    bd = _blockdiag_mask()
    ti = _iota((CHUNK, CHUNK), 0)
    si = _iota((CHUNK, CHUNK), 1)
    tri = [(si <= ti).astype(BF16), (si >= ti).astype(BF16)]
    rr = _iota((4 * CHUNK, CHUNK), 0)
    cc = _iota((4 * CHUNK, CHUNK), 1)
    t_of_row = (rr // (GLA_HEADS * SUB)) * SUB + rr % SUB
    causal = [cc <= t_of_row, cc >= t_of_row]
    nblk = CHUNK // SUB

    def prep(seg_i, c2):
        x_ref, s_ref = x_refs[seg_i], s_refs[seg_i]
        fs = []
        for kk in range(PREP_CHUNKS):
            r0 = pl.multiple_of((c2 * PREP_CHUNKS + kk) * CHUNK, CHUNK)
            fs.append(dict(row=row_off[seg_i] + r0,
                           q=x_ref[0, pl.ds(r0, CHUNK), 0:MIX_W] * (HEAD_DIM ** -0.5),
                           k=x_ref[0, pl.ds(r0, CHUNK), MIX_W:2 * MIX_W],
                           v=x_ref[0, pl.ds(r0, CHUNK), 2 * MIX_W:3 * MIX_W],
                           sm=s_ref[0, pl.ds(r0, CHUNK), :]))
        chains = [(f, d) for f in fs for d in range(2)]
        xg = [_dot3(f["sm"], wg_ref[d]) + bg_ref[d] for f, d in chains]
        g = [(jnp.minimum(x_, 0.0) - jnp.log(1.0 + jnp.exp(-jnp.abs(x_)))) * (1.0 / GLA_TAU) for x_ in xg]
        b = [_dot_whl(tri[d], g_) for (_, d), g_ in zip(chains, g)]
        pieces = []
        for i in range(nblk):
            row_pieces = []
            for (f, d), g_, b_ in zip(chains, g, b):
                e = SUB * i if d == 0 else SUB * i + SUB - 1
                bref = b_[e:e + 1, :] - g_[e:e + 1, :]
                kt = f["k"] * jnp.exp(jnp.minimum(bref - b_, EXP_CLAMP))
                qt = f["q"][SUB * i:SUB * (i + 1), :] * jnp.exp(b_[SUB * i:SUB * (i + 1), :] - bref)
                qs = jnp.concatenate([qt * hmask[h] for h in range(GLA_HEADS)], axis=0)
                row_pieces.append(_dot_nt(qs, kt))
            pieces.append(row_pieces)
        scores = [jnp.where(causal[d], jnp.concatenate([pieces[i][j] for i in range(nblk)], axis=0), 0.0)
                  for j, (_, d) in enumerate(chains)]
        r = [_dot(s_, f["v"]) for (f, _), s_ in zip(chains, scores)]
        for j, (f, d) in enumerate(chains):
            intra = []
            for i in range(nblk):
                acc = None
                for h in range(GLA_HEADS):
                    lo = (i * GLA_HEADS + h) * SUB
                    term = r[j][lo:lo + SUB, :] * hmask[h]
                    acc = term if acc is None else acc + term
                intra.append(acc)
            row = f["row"]
            e = CHUNK - 1 if d == 0 else 0
            b_end = b[j][e:e + 1, :]
            o_refs[d][pl.ds(row, CHUNK), :] = jnp.concatenate(intra, axis=0)
            qh_ref[d, pl.ds(row, CHUNK), :] = (f["q"] * jnp.exp(b[j])).astype(BF16)
            kh_ref[d, pl.ds(row, CHUNK), :] = (f["k"] * jnp.exp(b_end - b[j])).astype(BF16)
            dg_ref[d, pl.ds(pl.multiple_of(row // 8, 8), 8), :] = jnp.broadcast_to(jnp.exp(b_end), (8, MIX_W))

    for seg_i, x_ref in enumerate(x_refs):
        def prep_body(c, carry, seg_i=seg_i):
            prep(seg_i, c)
            return carry

        lax.fori_loop(0, x_ref.shape[1] // (CHUNK * PREP_CHUNKS), prep_body, 0)

    st_ref[...] = jnp.zeros(st_ref.shape, F32)

    def scan_body(seg_i, i0, nch):
        x_ref = x_refs[seg_i]
        steps = [i0 * SCAN_UNROLL + k for k in range(SCAN_UNROLL)]
        r0s = [[pl.multiple_of((i if d == 0 else nch - 1 - i) * CHUNK, CHUNK) for d in range(2)] for i in steps]
        rows = [[row_off[seg_i] + r0 for r0 in r] for r in r0s]
        upd = [[jnp.dot(x_ref[0, pl.ds(r0s[k][d], CHUNK), 2 * MIX_W:3 * MIX_W].T.astype(BF16),
                        kh_ref[d, pl.ds(rows[k][d], CHUNK), :], preferred_element_type=F32) for d in range(2)]
               for k in range(SCAN_UNROLL)]
        st = [st_ref[d] for d in range(2)]
        for k in range(SCAN_UNROLL):
            inter = [_dot_nt(qh_ref[d, pl.ds(rows[k][d], CHUNK), :], st[d]) for d in range(2)]
            for d in range(2):
                o_refs[d][pl.ds(rows[k][d], CHUNK), :] += inter[d]
                dgr = dg_ref[d, pl.ds(pl.multiple_of(rows[k][d] // 8, 8), 8), :][0:1, :]
                st[d] = st[d] * dgr + jnp.where(bd, upd[k][d], 0.0)
        for d in range(2):
            st_ref[d] = st[d]

    for seg_i, x_ref in enumerate(x_refs):
        nch = x_ref.shape[1] // CHUNK

        def scan_iter(i, carry, seg_i=seg_i, nch=nch):
            scan_body(seg_i, i, nch)
            return carry

        lax.fori_loop(0, nch // SCAN_UNROLL, scan_iter, 0)

    gain = gain_ref[...]
    seg = seg_ref[...]
    blk = 256
    outs = ((oc_ref, xc_ref, 0), (ol_ref, xl_ref, lc))
    for o_ref, x_ref, off in outs:
        if o_ref is None:
            continue

        def fin(i, carry, o_ref=o_ref, x_ref=x_ref, off=off):
            r0 = pl.multiple_of(i * blk, blk)
            o = of_ref[pl.ds(off + r0, blk), :] + ob_ref[pl.ds(off + r0, blk), :]
            gate = x_ref[0, pl.ds(r0, blk), 3 * MIX_W:4 * MIX_W]
            o_ref[0, pl.ds(r0, blk), :] = _finish_rows(o, gate, gain, seg).astype(BF16)
            return carry

        lax.fori_loop(0, x_ref.shape[1] // blk, fin, 0)


def _gla(x_ctx, x_lat, s_ctx, s_lat, wg, bg, gain, seg, need_ctx):
    b, lc, _ = x_ctx.shape
    ll = x_lat.shape[1]
    in_specs = [pl.BlockSpec((1, lc, 1024), lambda i: (i, 0, 0)),
                pl.BlockSpec((1, ll, 1024), lambda i: (i, 0, 0)),
                pl.BlockSpec((1, lc, LANES), lambda i: (i, 0, 0)),
                pl.BlockSpec((1, ll, LANES), lambda i: (i, 0, 0)),
                pl.BlockSpec((2, LANES, MIX_W), lambda i: (0, 0, 0)),
                pl.BlockSpec((2, 1, MIX_W), lambda i: (0, 0, 0)),
                pl.BlockSpec((1, MIX_W), lambda i: (0, 0)),
                pl.BlockSpec((LANES, LANES), lambda i: (0, 0))]
    out_shape = [jax.ShapeDtypeStruct((b, ll, MIX_W), BF16)]
    out_specs = [pl.BlockSpec((1, ll, MIX_W), lambda i: (i, 0, 0))]
    if need_ctx:
        out_shape.insert(0, jax.ShapeDtypeStruct((b, lc, MIX_W), BF16))
        out_specs.insert(0, pl.BlockSpec((1, lc, MIX_W), lambda i: (i, 0, 0)))
    res = pl.pallas_call(
        functools.partial(_gla_kernel, need_ctx=need_ctx),
        grid=(b,), in_specs=in_specs, out_specs=out_specs, out_shape=out_shape,
        scratch_shapes=[pltpu.VMEM((lc + ll, MIX_W), F32), pltpu.VMEM((lc + ll, MIX_W), F32),
                        pltpu.VMEM((2, lc + ll, MIX_W), BF16), pltpu.VMEM((2, lc + ll, MIX_W), BF16),
                        pltpu.VMEM((2, (lc + ll) // 8, MIX_W), F32),
                        pltpu.VMEM((2, MIX_W, MIX_W), F32)],
        compiler_params=_cparams(("parallel",)),
        name="gla_scan",
    )(x_ctx, x_lat, s_ctx, s_lat, wg, bg, gain, seg)
    return (res[0], res[1]) if need_ctx else (None, res[0])


def _gdn2_kernel(*refs, need_ctx):
    if need_ctx:
        (xc_ref, xl_ref, sc_ref, sl_ref, cw_ref, alog_ref, dtb_ref, sel_ref, selb_ref, gain_ref, seg_ref,
         oc_ref, ol_ref, sk_ref, p_ref, qe_ref, sv_ref, kh_ref, dg_ref, of_ref, ob_ref, st_ref) = refs
    else:
        (xc_ref, xl_ref, sc_ref, sl_ref, cw_ref, alog_ref, dtb_ref, sel_ref, selb_ref, gain_ref, seg_ref,
         ol_ref, sk_ref, p_ref, qe_ref, sv_ref, kh_ref, dg_ref, of_ref, ob_ref, st_ref) = refs
        oc_ref = None
    lc = xc_ref.shape[1]
    x_refs = (xc_ref, xl_ref)
    s_refs = (sc_ref, sl_ref)
    row_off = (0, lc)
    npair = MIX_W // PAIR_W
    seg = seg_ref[...]
    ti = _iota((CHUNK, CHUNK), 0)
    si = _iota((CHUNK, CHUNK), 1)
    tri = [(si <= ti).astype(BF16), (si >= ti).astype(BF16)]
    ones = jnp.ones((CHUNK, CHUNK), BF16)
    tp = _iota((CHUNK, PAIR_W), 0)
    sp = _iota((CHUNK, PAIR_W), 1) % CHUNK
    le = sp <= tp
    ge = sp >= tp
    m_tri = [le, ge]
    m_strict = [sp < tp, sp > tp]
    m_sum = [ge.astype(F32).astype(BF16), le.astype(F32).astype(BF16)]
    blk16 = (tp // SUB) == (sp // SUB)
    eye = (tp == sp).astype(F32)
    bd2 = (_iota((PAIR_W, PAIR_W), 0) // HEAD_DIM) == (_iota((PAIR_W, PAIR_W), 1) // HEAD_DIM)
    lane_p = _iota((1, PAIR_W), 1)
    hm2 = [(lane_p // HEAD_DIM == h).astype(F32) for h in range(2)]
    cw = cw_ref[...]

    def pk2(y):
        yb = y.astype(BF16)
        return jnp.where(bd2, jnp.concatenate([yb, yb], axis=0), jnp.zeros((), BF16))

    def mm(x, y):
        return jnp.dot(x.astype(BF16), pk2(y), preferred_element_type=F32)

    def front(seg_i, c):
        x_ref, s_ref = x_refs[seg_i], s_refs[seg_i]
        ln = x_ref.shape[1]
        nch = ln // CHUNK
        r0 = pl.multiple_of(c * CHUNK, CHUNK)
        center = x_ref[0, pl.ds(r0, CHUNK), 0:3 * MIX_W]
        p0 = pl.multiple_of(jnp.maximum(r0 - 8, 0), 8)
        n0 = pl.multiple_of(jnp.minimum(r0 + CHUNK, ln - 8), 8)
        prev = x_ref[0, pl.ds(p0, 8), 0:3 * MIX_W] * jnp.where(c > 0, 1.0, 0.0)
        nxt = x_ref[0, pl.ds(n0, 8), 0:3 * MIX_W] * jnp.where(c < nch - 1, 1.0, 0.0)
        ext = jnp.concatenate([prev, center, nxt], axis=0)
        pad = SHORT_CONV // 2
        acc = None
        for j in range(SHORT_CONV):
            term = ext[8 - pad + j:8 - pad + j + CHUNK, :] * cw[j:j + 1, :]
            acc = term if acc is None else acc + term
        y = _silu(acc)
        qk = y[:, 0:2 * MIX_W]
        qk = qk * lax.rsqrt(_seg_sum64(qk * qk, seg, split=False) + EPS)
        f = dict(row=row_off[seg_i] + r0, qn=qk[:, 0:MIX_W] * (HEAD_DIM ** -0.5), kn=qk[:, MIX_W:2 * MIX_W],
                 v=y[:, 2 * MIX_W:3 * MIX_W])
        sm = s_ref[0, pl.ds(r0, CHUNK), :]
        beta = _sigmoid(sm)
        f["kk"], f["qk"] = [], []
        for p in range(npair):
            ls = slice(PAIR_W * p, PAIR_W * (p + 1))
            kstack = jnp.concatenate([f["kn"][:, ls] * hm2[0], f["kn"][:, ls] * hm2[1]], axis=0)
            f["kk"].append(_dot_nt(f["kn"][:, ls], kstack))
            f["qk"].append(_dot_nt(f["qn"][:, ls], kstack))
        gexp = [_dot_xhl(-jnp.exp(alog_ref[d]) * _softplus(sm + dtb_ref[d]), sel_ref[d]) for d in range(2)]
        f["ghl"] = [_split(g_) for g_ in gexp]
        f["bexp"] = [_dot(beta, selb_ref[d]) for d in range(2)]
        f["gam"] = [jnp.dot(tri[d], f["ghl"][d][0], preferred_element_type=F32)
                    + jnp.dot(tri[d], f["ghl"][d][1], preferred_element_type=F32) for d in range(2)]
        return f

    def prep(seg_i, c2):
        fs = [front(seg_i, c2 * PREP_CHUNKS + k) for k in range(PREP_CHUNKS)]
        chains = [(f, d, p) for f in fs for d in range(2) for p in range(npair)]
        lss = [slice(PAIR_W * p, PAIR_W * (p + 1)) for _, _, p in chains]
        gam_t = [f["gam"][d][:, ls] for (f, d, _), ls in zip(chains, lss)]
        gam_s = [jnp.dot(ones, f["ghl"][d][0][:, ls] * m_sum[d], preferred_element_type=F32)
                 + jnp.dot(ones, f["ghl"][d][1][:, ls] * m_sum[d], preferred_element_type=F32)
                 for (f, d, _), ls in zip(chains, lss)]
        bx = [f["bexp"][d][:, ls] for (f, d, _), ls in zip(chains, lss)]
        dec = [jnp.where(m_tri[d], jnp.exp(jnp.minimum(gt - gs, 0.0)), 0.0)
               for (_, d, _), gt, gs in zip(chains, gam_t, gam_s)]
        a = [jnp.where(m_strict[d], b_ * dc * f["kk"][p], 0.0) for (f, d, p), b_, dc in zip(chains, bx, dec)]
        dgn = [jnp.where(blk16, a_, 0.0) for a_ in a]
        lo = [a_ - g_ for a_, g_ in zip(a, dgn)]
        d2 = [mm(g_, g_) for g_ in dgn]
        t1 = [mm(eye - g_, eye + s_) for g_, s_ in zip(dgn, d2)]
        d4 = [mm(s_, s_) for s_ in d2]
        t2 = [mm(t_, eye + s_) for t_, s_ in zip(t1, d4)]
        d8 = [mm(s_, s_) for s_ in d4]
        t_inv = [mm(t_, eye + s_) for t_, s_ in zip(t2, d8)]
        m = [mm(t_, l_) for t_, l_ in zip(t_inv, lo)]
        m2 = [mm(m_, m_) for m_ in m]
        w1 = [mm(eye - m_, eye + s_) for m_, s_ in zip(m, m2)]
        w = [mm(w_, t_) for w_, t_ in zip(w1, t_inv)]
        egam = [jnp.exp(gt) for gt in gam_t]
        solv = [mm(w_, b_ * f["v"][:, ls]) for (f, _, _), w_, b_, ls in zip(chains, w, bx, lss)]
        solk = [mm(w_, b_ * eg * f["kn"][:, ls]) for (f, _, _), w_, b_, eg, ls in zip(chains, w, bx, egam, lss)]
        for i, (f, d, p) in enumerate(chains):
            ls, row = lss[i], f["row"]
            e = CHUNK - 1 if d == 0 else 0
            g_end = gam_t[i][e:e + 1, :]
            sv_ref[d, pl.ds(row, CHUNK), ls] = solv[i]
            sk_ref[d, pl.ds(row, CHUNK), ls] = solk[i].astype(BF16)
            p_ref[d, pl.ds(row, CHUNK), ls] = jnp.where(m_tri[d], f["qk"][p] * dec[i], 0.0).astype(BF16)
            qe_ref[d, pl.ds(row, CHUNK), ls] = (egam[i] * f["qn"][:, ls]).astype(BF16)
            kh_ref[d, pl.ds(row, CHUNK), ls] = f["kn"][:, ls] * jnp.exp(g_end - gam_t[i])
            dg_ref[d, pl.ds(pl.multiple_of(row // 8, 8), 8), ls] = jnp.broadcast_to(jnp.exp(g_end), (8, PAIR_W))

    for seg_i, x_ref in enumerate(x_refs):
        def prep_body(c, carry, seg_i=seg_i):
            prep(seg_i, c)
            return carry

        lax.fori_loop(0, x_ref.shape[1] // (CHUNK * PREP_CHUNKS), prep_body, 0)

    st_ref[...] = jnp.zeros(st_ref.shape, F32)

    def scan_body(seg_i, i0, nch):
        chains = [(d, p) for d in range(2) for p in range(npair)]
        lss = [slice(PAIR_W * p, PAIR_W * (p + 1)) for _, p in chains]
        st = [st_ref[d, p] for d, p in chains]
        pending = []
        for k in range(SCAN_UNROLL):
            i = i0 * SCAN_UNROLL + k
            rows = [row_off[seg_i] + pl.multiple_of((i if d == 0 else nch - 1 - i) * CHUNK, CHUNK)
                    for d, _ in chains]
            stb = [s_.astype(BF16) for s_ in st]
            u = [sv_ref[d, pl.ds(r, CHUNK), ls]
                 - jnp.dot(sk_ref[d, pl.ds(r, CHUNK), ls], sb, preferred_element_type=F32)
                 for (d, _), r, ls, sb in zip(chains, rows, lss, stb)]
            ku = [_dot(kh_ref[d, pl.ds(r, CHUNK), ls].T, u_) for (d, _), r, ls, u_ in zip(chains, rows, lss, u)]
            dgr = [dg_ref[d, pl.ds(pl.multiple_of(r // 8, 8), 8), ls][0:1, :]
                   for (d, _), r, ls in zip(chains, rows, lss)]
            st = [s_ * g_ + jnp.where(bd2, k_, 0.0) for s_, g_, k_ in zip(st, dgr, ku)]
            pending.append((rows, stb, u))
        for rows, stb, u in pending:
            oq = [jnp.dot(qe_ref[d, pl.ds(r, CHUNK), ls], sb, preferred_element_type=F32)
                  for (d, _), r, ls, sb in zip(chains, rows, lss, stb)]
            ou = [jnp.dot(p_ref[d, pl.ds(r, CHUNK), ls], pk2(u_), preferred_element_type=F32)
                  for (d, _), r, ls, u_ in zip(chains, rows, lss, u)]
            for j, (d, _) in enumerate(chains):
                dst = of_ref if d == 0 else ob_ref
                dst[pl.ds(rows[j], CHUNK), lss[j]] = oq[j] + ou[j]
        for j, (d, p) in enumerate(chains):
            st_ref[d, p] = st[j]

    for seg_i, x_ref in enumerate(x_refs):
        nch = x_ref.shape[1] // CHUNK

        def scan_iter(i, carry, seg_i=seg_i, nch=nch):
            scan_body(seg_i, i, nch)
            return carry

        lax.fori_loop(0, nch // SCAN_UNROLL, scan_iter, 0)

    gain = gain_ref[...]
    blk = 256
    outs = ((oc_ref, xc_ref, 0), (ol_ref, xl_ref, lc))
    for o_ref, x_ref, off in outs:
        if o_ref is None:
            continue

        def fin(i, carry, o_ref=o_ref, x_ref=x_ref, off=off):
            r0 = pl.multiple_of(i * blk, blk)
            o = of_ref[pl.ds(off + r0, blk), :] + ob_ref[pl.ds(off + r0, blk), :]
            gate = x_ref[0, pl.ds(r0, blk), 3 * MIX_W:4 * MIX_W]
            o_ref[0, pl.ds(r0, blk), :] = _finish_rows(o, gate, gain, seg).astype(BF16)
            return carry

        lax.fori_loop(0, x_ref.shape[1] // blk, fin, 0)


def _gdn2(x_ctx, x_lat, s_ctx, s_lat, cw, alog, dtb, sel, selb, gain, seg, need_ctx):
    b, lc, _ = x_ctx.shape
    ll = x_lat.shape[1]
    lt = lc + ll
    in_specs = [pl.BlockSpec((1, lc, 1024), lambda i: (i, 0, 0)),
                pl.BlockSpec((1, ll, 1024), lambda i: (i, 0, 0)),
                pl.BlockSpec((1, lc, LANES), lambda i: (i, 0, 0)),
                pl.BlockSpec((1, ll, LANES), lambda i: (i, 0, 0)),
                pl.BlockSpec((8, 3 * MIX_W), lambda i: (0, 0)),
                pl.BlockSpec((2, 1, LANES), lambda i: (0, 0, 0)),
                pl.BlockSpec((2, 1, LANES), lambda i: (0, 0, 0)),
                pl.BlockSpec((2, LANES, MIX_W), lambda i: (0, 0, 0)),
                pl.BlockSpec((2, LANES, MIX_W), lambda i: (0, 0, 0)),
                pl.BlockSpec((1, MIX_W), lambda i: (0, 0)),
                pl.BlockSpec((LANES, LANES), lambda i: (0, 0))]
    out_shape = [jax.ShapeDtypeStruct((b, ll, MIX_W), BF16)]
    out_specs = [pl.BlockSpec((1, ll, MIX_W), lambda i: (i, 0, 0))]
    if need_ctx:
        out_shape.insert(0, jax.ShapeDtypeStruct((b, lc, MIX_W), BF16))
        out_specs.insert(0, pl.BlockSpec((1, lc, MIX_W), lambda i: (i, 0, 0)))
    res = pl.pallas_call(
        functools.partial(_gdn2_kernel, need_ctx=need_ctx),
        grid=(b,), in_specs=in_specs, out_specs=out_specs, out_shape=out_shape,
        scratch_shapes=[pltpu.VMEM((2, lt, MIX_W), BF16), pltpu.VMEM((2, lt, MIX_W), BF16),
                        pltpu.VMEM((2, lt, MIX_W), BF16), pltpu.VMEM((2, lt, MIX_W), F32),
                        pltpu.VMEM((2, lt, MIX_W), F32), pltpu.VMEM((2, lt // 8, MIX_W), F32),
                        pltpu.VMEM((lt, MIX_W), F32), pltpu.VMEM((lt, MIX_W), F32),
                        pltpu.VMEM((2, MIX_W // PAIR_W, PAIR_W, PAIR_W), F32)],
        compiler_params=_cparams(("parallel",)),
        name="gdn_scan",
    )(x_ctx, x_lat, s_ctx, s_lat, cw, alog, dtb, sel, selb, gain, seg)
    return (res[0], res[1]) if need_ctx else (None, res[0])


def _outproj_kernel(gla_ref, gdn_ref, att_ref, h_ref, g_ref, w_ref, o_ref):
    y = (jnp.dot(gla_ref[...], w_ref[0:256, :], preferred_element_type=F32)
         + jnp.dot(gdn_ref[...], w_ref[256:512, :], preferred_element_type=F32)
         + jnp.dot(att_ref[...], w_ref[512:1024, :], preferred_element_type=F32))
    o_ref[...] = h_ref[...] + g_ref[0] * y


def _outproj(gla, gdn, att, h2d, mod144, mod_row_fn, w, seq_len, tm):
    rows, d = h2d.shape
    tiles_per_seq = seq_len // tm
    return pl.pallas_call(
        _outproj_kernel,
        grid=(rows // tm,),
        in_specs=[pl.BlockSpec((tm, 256), lambda i: (i, 0)),
                  pl.BlockSpec((tm, 256), lambda i: (i, 0)),
                  pl.BlockSpec((tm, 512), lambda i: (i, 0)),
                  pl.BlockSpec((tm, d), lambda i: (i, 0)),
                  pl.BlockSpec((1, 1, d), lambda i: (mod_row_fn(i // tiles_per_seq) * 6 + 2, 0, 0)),
                  pl.BlockSpec((d, d), lambda i: (0, 0))],
        out_specs=pl.BlockSpec((tm, d), lambda i: (i, 0)),
        out_shape=jax.ShapeDtypeStruct((rows, d), F32),
        compiler_params=_cparams(("parallel",)),
        name="outproj",
    )(gla, gdn, att, h2d, mod144, w)


def _norm_mod(h_ref, sh_ref, sc_ref, gain_ref):
    x = h_ref[...]
    ms = jnp.mean(x * x, axis=-1, keepdims=True)
    return x * lax.rsqrt(ms + EPS) * (gain_ref[...] * (1.0 + sc_ref[0])) + sh_ref[0]


def _swiglu_part(x, wg, wu, wd):
    gg = jnp.dot(x, wg[...], preferred_element_type=F32)
    uu = jnp.dot(x, wu[...], preferred_element_type=F32)
    hid = (_silu(gg) * uu).astype(BF16)
    return jnp.dot(hid, wd[...], preferred_element_type=F32)


def _ffn_kernel(h_ref, sh_ref, sc_ref, g_ref, gain_ref, wg_ref, wu_ref, wd_ref, o_ref, b_scr, acc_scr, *, nf):
    f = pl.program_id(1)

    @pl.when(f == 0)
    def _():
        b_scr[...] = _norm_mod(h_ref, sh_ref, sc_ref, gain_ref).astype(BF16)
        acc_scr[...] = jnp.zeros(acc_scr.shape, F32)

    b = b_scr[...]
    acc_scr[...] += _swiglu_part(b, wg_ref, wu_ref, wd_ref)

    @pl.when(f == nf - 1)
    def _():
        o_ref[...] = h_ref[...] + g_ref[0] * acc_scr[...]


def _ffn(h2d, mod144, mod_row_fn, gain, w_gu, w_down, seq_len, tm, tf):
    rows, d = h2d.shape
    dff = w_down.shape[0]
    nf = dff // tf
    tiles_per_seq = seq_len // tm

    def mod_spec(k):
        return pl.BlockSpec((1, 1, d), lambda i, f: (mod_row_fn(i // tiles_per_seq) * 6 + k, 0, 0))

    return pl.pallas_call(
        functools.partial(_ffn_kernel, nf=nf),
        grid=(rows // tm, nf),
        in_specs=[pl.BlockSpec((tm, d), lambda i, f: (i, 0)),
                  mod_spec(3), mod_spec(4), mod_spec(5),
                  pl.BlockSpec((1, d), lambda i, f: (0, 0)),
                  pl.BlockSpec((d, tf), lambda i, f: (0, f)),
                  pl.BlockSpec((d, tf), lambda i, f: (0, nf + f)),
                  pl.BlockSpec((tf, d), lambda i, f: (f, 0))],
        out_specs=pl.BlockSpec((tm, d), lambda i, f: (i, 0)),
        out_shape=jax.ShapeDtypeStruct((rows, d), F32),
        scratch_shapes=[pltpu.VMEM((tm, d), BF16), pltpu.VMEM((tm, d), F32)],
        compiler_params=_cparams(("parallel", "arbitrary")),
        name="ffn",
    )(h2d, mod144, mod144, mod144, gain, w_gu, w_gu, w_down)


MOE_TILE = 512
COMBINE_TOKENS = 512


def _router_kernel(h_ref, sh_ref, sc_ref, gain_ref, wr_ref, br_ref, b_ref, route_ref):
    lane = _iota((1, LANES), 1)
    lane_f = lane.astype(F32)
    b = _norm_mod(h_ref, sh_ref, sc_ref, gain_ref)
    b_ref[...] = b
    logits = _dot3(b, wr_ref[...]) + br_ref[...]
    logits = jnp.where(lane < N_EXPERTS, logits, -jnp.inf)
    m1 = jnp.max(logits, axis=-1, keepdims=True)
    i1 = jnp.min(jnp.where(logits == m1, lane_f, float(LANES)), axis=-1, keepdims=True)
    rest = jnp.where(lane_f == i1, -jnp.inf, logits)
    m2 = jnp.max(rest, axis=-1, keepdims=True)
    i2 = jnp.min(jnp.where(rest == m2, lane_f, float(LANES)), axis=-1, keepdims=True)
    t = jnp.exp(m2 - m1)
    w1 = 1.0 / (1.0 + t)
    route_ref[...] = (jnp.where(lane == 0, i1, 0.0) + jnp.where(lane == 1, i2, 0.0)
                      + jnp.where(lane == 2, w1, 0.0) + jnp.where(lane == 3, t * w1, 0.0))


def _router(h2d, mod144, mod_row_fn, gain, w_router, b_router, seq_len, tm):
    rows, d = h2d.shape
    tiles_per_seq = seq_len // tm

    def mod_spec(k):
        return pl.BlockSpec((1, 1, d), lambda i: (mod_row_fn(i // tiles_per_seq) * 6 + k, 0, 0))

    return pl.pallas_call(
        _router_kernel,
        grid=(rows // tm,),
        in_specs=[pl.BlockSpec((tm, d), lambda i: (i, 0)), mod_spec(3), mod_spec(4),
                  pl.BlockSpec((1, d), lambda i: (0, 0)),
                  pl.BlockSpec((d, LANES), lambda i: (0, 0)),
                  pl.BlockSpec((1, LANES), lambda i: (0, 0))],
        out_specs=(pl.BlockSpec((tm, d), lambda i: (i, 0)), pl.BlockSpec((tm, LANES), lambda i: (i, 0))),
        out_shape=(jax.ShapeDtypeStruct((rows, d), F32), jax.ShapeDtypeStruct((rows, LANES), F32)),
        compiler_params=_cparams(("parallel",)),
        name="moe_router",
    )(h2d, mod144, mod144, gain, w_router, b_router)


DISPATCH_TOKENS = 1024


def _moe_dispatch_kernel(pos_ref, b_ref, xs_in_ref, xs_ref, sem):
    del xs_in_ref
    n = b_ref.shape[0]
    for r in range(2 * n):
        pltpu.make_async_copy(b_ref.at[pl.ds(r % n, 1)], xs_ref.at[pl.ds(pos_ref[0, 0, r], 1)],
                              sem).start(priority=r % 2)
    for _ in range(2):
        pltpu.make_async_copy(b_ref, xs_ref.at[pl.ds(0, n)], sem).wait()


def _moe_dispatch(pos2, b, p_rows):
    rows, d = b.shape
    tm = DISPATCH_TOKENS
    return pl.pallas_call(
        _moe_dispatch_kernel,
        grid=(rows // tm,),
        in_specs=[pl.BlockSpec((1, 1, 2 * tm), lambda i: (i, 0, 0), memory_space=pltpu.SMEM),
                  pl.BlockSpec((tm, d), lambda i: (i, 0)),
                  pl.BlockSpec(memory_space=pl.ANY)],
        out_specs=pl.BlockSpec(memory_space=pl.ANY),
        out_shape=jax.ShapeDtypeStruct((p_rows, d), F32),
        scratch_shapes=[pltpu.SemaphoreType.DMA(())],
        input_output_aliases={2: 0},
        compiler_params=_cparams(("arbitrary",)),
        name="moe_dispatch",
    )(pos2, b, jnp.zeros((p_rows, d), F32))


def _moe_group_kernel(te_ref, nt_ref, xs_ref, wg_ref, wu_ref, wd_ref, ys_ref, xb_scr, acc_scr, *, nf):
    i = pl.program_id(0)
    f = pl.program_id(1)
    used = i < nt_ref[0]

    @pl.when(used & (f == 0))
    def _():
        xb_scr[...] = xs_ref[...].astype(BF16)

    @pl.when(used)
    def _():
        x = xb_scr[...]
        part = _swiglu_part(x, wg_ref.at[0], wu_ref.at[0], wd_ref.at[0])

        @pl.when(f == 0)
        def _():
            acc_scr[...] = part

        @pl.when(f > 0)
        def _():
            acc_scr[...] += part

    @pl.when(used & (f == nf - 1))
    def _():
        ys_ref[...] = acc_scr[...]

    @pl.when(jnp.logical_not(used) & (f == nf - 1))
    def _():
        ys_ref[...] = jnp.zeros(ys_ref.shape, F32)


def _moe_group(tile_expert, n_tiles_used, xs, w_gu, w_down, tf):
    p, d = xs.shape
    dff = w_down.shape[1]
    nf = dff // tf
    grid_spec = pltpu.PrefetchScalarGridSpec(
        num_scalar_prefetch=2,
        grid=(p // MOE_TILE, nf),
        in_specs=[pl.BlockSpec((MOE_TILE, d), lambda i, f, te, nt: (i, 0)),
                  pl.BlockSpec((1, d, tf), lambda i, f, te, nt: (te[i], 0, f)),
                  pl.BlockSpec((1, d, tf), lambda i, f, te, nt: (te[i], 0, nf + f)),
                  pl.BlockSpec((1, tf, d), lambda i, f, te, nt: (te[i], f, 0))],
        out_specs=pl.BlockSpec((MOE_TILE, d), lambda i, f, te, nt: (i, 0)),
        scratch_shapes=[pltpu.VMEM((MOE_TILE, d), BF16), pltpu.VMEM((MOE_TILE, d), F32)])
    return pl.pallas_call(
        functools.partial(_moe_group_kernel, nf=nf),
        grid_spec=grid_spec,
        out_shape=jax.ShapeDtypeStruct((p, d), F32),
        compiler_params=_cparams(("arbitrary", "arbitrary")),
        name="moe_experts",
    )(tile_expert, n_tiles_used, xs, w_gu, w_gu, w_down)


def _moe_combine_kernel(pos_ref, ys_ref, h_ref, route_ref, g_ref, o_ref, buf, sem):
    n = buf.shape[0]
    half = n // 2

    for r in range(n):
        pltpu.make_async_copy(ys_ref.at[pl.ds(pos_ref[0, 0, r], 1)], buf.at[pl.ds(r, 1)], sem).start(priority=r % 2)
    pltpu.make_async_copy(ys_ref.at[pl.ds(0, n)], buf, sem).wait()
    w1 = route_ref[:, 2:3]
    w2 = route_ref[:, 3:4]
    o_ref[...] = h_ref[...] + g_ref[0] * (w1 * buf[0:half, :] + w2 * buf[half:n, :])


def _moe_combine(pos, ys, h2d, route, mod144, mod_row_fn, seq_len):
    rows, d = h2d.shape
    tm = pos.shape[2] // 2
    tiles_per_seq = seq_len // tm
    steps = rows // tm
    return pl.pallas_call(
        _moe_combine_kernel,
        grid=(steps,),
        in_specs=[pl.BlockSpec((1, 1, 2 * tm), lambda i: (i, 0, 0), memory_space=pltpu.SMEM),
                  pl.BlockSpec(memory_space=pl.ANY),
                  pl.BlockSpec((tm, d), lambda i: (i, 0)),
                  pl.BlockSpec((tm, LANES), lambda i: (i, 0)),
                  pl.BlockSpec((1, 1, d), lambda i: (mod_row_fn(i // tiles_per_seq) * 6 + 5, 0, 0))],
        out_specs=pl.BlockSpec((tm, d), lambda i: (i, 0)),
        out_shape=jax.ShapeDtypeStruct((rows, d), F32),
        scratch_shapes=[pltpu.VMEM((2 * tm, d), F32), pltpu.SemaphoreType.DMA(())],
        compiler_params=_cparams(("arbitrary",)),
        name="moe_combine",
    )(pos, ys, h2d, route, mod144)


def _moe_routed(h2d, mod144, mod_row_fn, gain, w_router, b_router, w_gu, w_down, seq_len):
    rows, d = h2d.shape
    b, route = _router(h2d, mod144, mod_row_fn, gain, w_router, b_router, seq_len, _rows(ROW_TILE, seq_len))
    ex = jnp.concatenate([route[:, 0], route[:, 1]]).astype(jnp.int32)
    onehot = (ex[:, None] == jnp.arange(N_EXPERTS, dtype=jnp.int32)[None, :]).astype(jnp.int32)
    rank = jnp.sum((jnp.cumsum(onehot, axis=0) - onehot) * onehot, axis=1)
    counts = jnp.sum(onehot, axis=0)
    padded = ((counts + MOE_TILE - 1) // MOE_TILE) * MOE_TILE
    ends = jnp.cumsum(padded)
    starts = ends - padded
    pos = jnp.sum(onehot * starts[None, :], axis=1) + rank
    pos = pos.astype(jnp.int32)
    p_rows = 2 * rows + N_EXPERTS * MOE_TILE
    tile_first = jnp.arange(p_rows // MOE_TILE, dtype=jnp.int32) * MOE_TILE
    tile_expert = jnp.minimum(jnp.sum((tile_first[:, None] >= ends[None, :]).astype(jnp.int32), axis=1),
                              N_EXPERTS - 1).astype(jnp.int32)
    n_tiles_used = (ends[-1:] // MOE_TILE).astype(jnp.int32)

    def per_tile(tm):
        return jnp.concatenate([pos[:rows].reshape(rows // tm, 1, tm), pos[rows:].reshape(rows // tm, 1, tm)], axis=2)

    xs = _moe_dispatch(per_tile(DISPATCH_TOKENS), b, p_rows)
    ys = _moe_group(tile_expert, n_tiles_used, xs, w_gu, w_down, FF_TILE)
    return _moe_combine(per_tile(_rows(COMBINE_TOKENS, seq_len)), ys, h2d, route, mod144, mod_row_fn, seq_len)


ATT_HEAD_ORDER = (0, 4, 1, 5, 2, 6, 3, 7)


def _layout_w_in(w):
    gla = w[:, 0:1024]
    glow = w[:, 1024:1056]
    gdn = w[:, 1056:2080]
    ab = w[:, 2080:2096]
    q = jnp.concatenate([w[:, 2096 + HEAD_DIM * h:2096 + HEAD_DIM * (h + 1)] for h in ATT_HEAD_ORDER], axis=1)
    kv = w[:, 2608:2864]
    pad = jnp.zeros((w.shape[0], LANES - 48), w.dtype)
    return jnp.concatenate([gla, gdn, q, kv, glow, ab, pad], axis=1).astype(BF16)


def _layout_w_out(w):
    att = [w[512 + HEAD_DIM * h:512 + HEAD_DIM * (h + 1)] for h in ATT_HEAD_ORDER]
    return jnp.concatenate([w[0:512]] + att, axis=0).astype(BF16)


def _rope_tables(seq_len):
    rows = seq_len // GRID_W
    row = jnp.repeat(jnp.arange(rows), GRID_W).astype(F32)
    col = jnp.tile(jnp.arange(GRID_W), rows).astype(F32)
    inv_freq = ROPE_THETA ** (-jnp.arange(0, HEAD_DIM // 2, 2, dtype=F32) / (HEAD_DIM // 2))
    ar = row[:, None] * inv_freq
    ac = col[:, None] * inv_freq
    cos = jnp.concatenate([jnp.cos(ar), jnp.cos(ar), jnp.cos(ac), jnp.cos(ac)], axis=-1)
    sin = jnp.concatenate([-jnp.sin(ar), jnp.sin(ar), -jnp.sin(ac), jnp.sin(ac)], axis=-1)
    return jnp.tile(cos, (1, 2)), jnp.tile(sin, (1, 2))


def _seg_matrix():
    i = np.arange(LANES)
    return jnp.asarray((i[:, None] // HEAD_DIM) == (i[None, :] // HEAD_DIM), dtype=BF16)


def _gdn_select():
    sel = np.zeros((2, LANES, MIX_W), np.float32)
    selb = np.zeros((2, LANES, MIX_W), np.float32)
    for d in range(2):
        for h in range(GDN_HEADS):
            sel[d, 32 + GDN_HEADS * d + h, HEAD_DIM * h:HEAD_DIM * (h + 1)] = 1.0
            selb[d, 40 + GDN_HEADS * d + h, HEAD_DIM * h:HEAD_DIM * (h + 1)] = 1.0
    return jnp.asarray(sel, BF16), jnp.asarray(selb, BF16)


def _lane_rows(vals, base):
    out = jnp.zeros((2, 1, LANES), F32)
    for d in range(2):
        out = out.at[d, 0, base + GDN_HEADS * d:base + GDN_HEADS * (d + 1)].set(vals[d].astype(F32))
    return out


def kernel(x, c, ctx, c_ctx, w_mod, b_mod, norm_mix, norm_ffn, w_in, gla_gate_up, gla_gate_bias, gla_out_gain,
           gdn_conv, gdn_a_log, gdn_dt_bias, gdn_out_gain, att_q_gain, att_k_gain, w_out, ffn_gate_up, ffn_down,
           moe_router, moe_router_bias, moe_gate_up, moe_down):
    bsz, seq, d = x.shape
    lctx = ctx.shape[1]
    depth = w_mod.shape[0]
    ctx_row = bsz

    mod_rows = ((bsz + 1 + 7) // 8) * 8
    cvec = jnp.concatenate([c, c_ctx[None, :], jnp.zeros((mod_rows - bsz - 1, d), F32)], axis=0)
    mods = _modulation(cvec, w_mod, b_mod)

    seg = _seg_matrix()
    tables = _rope_tables(seq)
    sel, selb = _gdn_select()
    lat_row = lambda b: b
    ctx_row_fn = lambda b: ctx_row

    h_lat = x.reshape(bsz * seq, d)
    h_ctx = ctx.reshape(bsz * lctx, d)
    for layer in range(depth):
        need_ctx = layer < depth - 1
        mod144 = mods[layer].reshape(mod_rows * 6, 1, d)
        w_p = _layout_w_in(w_in[layer])
        w_o = _layout_w_out(w_out[layer])
        hg = jnp.concatenate([jnp.tile(att_q_gain[layer], ATT_Q_HEADS) * (HEAD_DIM ** -0.5 * LOG2E),
                              jnp.tile(att_k_gain[layer], ATT_KV_HEADS)])[None, :].astype(F32)
        gain_mix = norm_mix[layer][None, :]
        gain_ffn = norm_ffn[layer][None, :]

        gla_l, gdn_l, q_l, kv_l, sm_l = _inproj(h_lat, mod144, lat_row, gain_mix, w_p, hg, seg, tables, seq,
                                                _rows(INPROJ_ROWS, seq))
        gla_c, gdn_c, q_c, kv_c, sm_c = _inproj(h_ctx, mod144, ctx_row_fn, gain_mix, w_p, hg, seg, None, lctx,
                                                _rows(INPROJ_ROWS, lctx))

        r3 = lambda t, n: t.reshape(bsz, n, t.shape[-1])
        wg = jnp.zeros((2, LANES, MIX_W), F32)
        for dd in range(2):
            wg = wg.at[dd, GLA_GATE_RANK * dd:GLA_GATE_RANK * (dd + 1), :].set(gla_gate_up[layer, dd].astype(F32))
        bg = gla_gate_bias[layer].reshape(2, 1, MIX_W).astype(F32)
        gla_gain = jnp.tile(gla_out_gain[layer], GLA_HEADS)[None, :].astype(F32)
        o_gla_c, o_gla_l = _gla(r3(gla_c, lctx), r3(gla_l, seq), r3(sm_c, lctx), r3(sm_l, seq),
                                wg, bg, gla_gain, seg, need_ctx)

        cw = jnp.concatenate([gdn_conv[layer].astype(F32), jnp.zeros((8 - SHORT_CONV, 3 * MIX_W), F32)], axis=0)
        alog = _lane_rows(gdn_a_log[layer], 32)
        dtb = _lane_rows(gdn_dt_bias[layer], 32)
        gdn_gain = jnp.tile(gdn_out_gain[layer], GDN_HEADS)[None, :].astype(F32)
        o_gdn_c, o_gdn_l = _gdn2(r3(gdn_c, lctx), r3(gdn_l, seq), r3(sm_c, lctx), r3(sm_l, seq),
                                cw, alog, dtb, sel, selb, gdn_gain, seg, need_ctx)

        o_att_l = _attention(r3(q_l, seq), [r3(kv_l, seq), r3(kv_c, lctx)], _rows(ATT_Q_ROWS, seq))
        h_lat = _outproj(o_gla_l.reshape(-1, MIX_W), o_gdn_l.reshape(-1, MIX_W), o_att_l.reshape(-1, 512),
                         h_lat, mod144, lat_row, w_o, seq, _rows(ROW_TILE, seq))
        if need_ctx:
            o_att_c = _attention(r3(q_c, lctx), [r3(kv_c, lctx)], _rows(ATT_Q_ROWS, lctx))
            h_ctx = _outproj(o_gla_c.reshape(-1, MIX_W), o_gdn_c.reshape(-1, MIX_W), o_att_c.reshape(-1, 512),
                             h_ctx, mod144, ctx_row_fn, w_o, lctx, _rows(ROW_TILE, lctx))

        j = layer // 2
        if layer % 2 == 0:
            w_gu = ffn_gate_up[j].astype(BF16)
            w_dn = ffn_down[j].astype(BF16)
            h_lat = _ffn(h_lat, mod144, lat_row, gain_ffn, w_gu, w_dn, seq, _rows(ROW_TILE, seq), FF_TILE)
            if need_ctx:
                h_ctx = _ffn(h_ctx, mod144, ctx_row_fn, gain_ffn, w_gu, w_dn, lctx, _rows(ROW_TILE, lctx), FF_TILE)
        else:
            w_gu = moe_gate_up[j].astype(BF16)
            w_dn = moe_down[j].astype(BF16)
            w_r = jnp.concatenate([moe_router[j].astype(F32), jnp.zeros((d, LANES - N_EXPERTS), F32)], axis=1)
            b_r = jnp.concatenate([moe_router_bias[j].astype(F32), jnp.zeros((LANES - N_EXPERTS,), F32)])[None, :]
            h_lat = _moe_routed(h_lat, mod144, lat_row, gain_ffn, w_r, b_r, w_gu, w_dn, seq)
            if need_ctx:
                h_ctx = _moe_routed(h_ctx, mod144, ctx_row_fn, gain_ffn, w_r, b_r, w_gu, w_dn, lctx)
    return h_lat.reshape(bsz, seq, d)
```

```python
import functools

import numpy as np
import jax
import jax.numpy as jnp
from jax import lax
from jax.experimental import pallas as pl
from jax.experimental.pallas import tpu as pltpu

F32 = jnp.float32
BF16 = jnp.bfloat16

GRID_W = 64
HEAD_DIM = 64
CHUNK = 64
SUB = 16
EPS = 1e-6
GLA_HEADS = 4
GLA_GATE_RANK = 16
GLA_TAU = 16.0
GDN_HEADS = 4
SHORT_CONV = 5
ATT_Q_HEADS = 8
ATT_KV_HEADS = 2
ROPE_THETA = 10000.0
N_EXPERTS = 8
MIX_W = GLA_HEADS * HEAD_DIM
LOG2E = 1.4426950408889634
EXP_CLAMP = 80.0

LANES = 128
V7X_VMEM_BYTES = 64 * 1024 * 1024
VMEM_LIMIT = V7X_VMEM_BYTES - 8 * 1024 * 1024


INPROJ_ROWS = 256
ROW_TILE = 512
ATT_Q_ROWS = 256
FF_TILE = 1408


def _rows(limit, seq_len):
    return min(limit, seq_len)


def _cparams(sem):
    return pltpu.CompilerParams(dimension_semantics=sem, vmem_limit_bytes=VMEM_LIMIT)


def _silu(x):
    return x / (1.0 + jnp.exp(-x))


def _sigmoid(x):
    return 1.0 / (1.0 + jnp.exp(-x))


def _softplus(x):
    return jnp.maximum(x, 0.0) + jnp.log(1.0 + jnp.exp(-jnp.abs(x)))


def _dot(a, b):
    return jnp.dot(a.astype(BF16), b.astype(BF16), preferred_element_type=F32)


def _dot_nt(a, b):
    return lax.dot_general(a.astype(BF16), b.astype(BF16), (((1,), (1,)), ((), ())),
                           preferred_element_type=F32)


def _split(x):
    hi = x.astype(BF16)
    lo = (x - hi.astype(F32)).astype(BF16)
    return hi, lo


def _dot_xhl(x, w):
    hi, lo = _split(x)
    w = w.astype(BF16)
    return (jnp.dot(hi, w, preferred_element_type=F32) + jnp.dot(lo, w, preferred_element_type=F32))


def _dot_whl(w, x):
    hi, lo = _split(x)
    w = w.astype(BF16)
    return (jnp.dot(w, hi, preferred_element_type=F32) + jnp.dot(w, lo, preferred_element_type=F32))


def _dot3(a, b):
    ah, al = _split(a)
    bh, bl = _split(b)
    return (jnp.dot(ah, bh, preferred_element_type=F32) + jnp.dot(ah, bl, preferred_element_type=F32)
            + jnp.dot(al, bh, preferred_element_type=F32))


def _seg_sum64(sq, seg, split=True):
    outs = []
    for j in range(sq.shape[1] // LANES):
        part = sq[:, LANES * j:LANES * (j + 1)]
        outs.append(_dot_xhl(part, seg) if split else _dot(part, seg))
    return outs[0] if len(outs) == 1 else jnp.concatenate(outs, axis=1)


def _iota(shape, dim):
    return lax.broadcasted_iota(jnp.int32, shape, dim)


def _mod_kernel(c_ref, w_ref, b_ref, o_ref):
    s = _silu(c_ref[...])
    o_ref[0] = _dot(s, w_ref[0]) + b_ref[0]


def _modulation(cvec, w_mod, b_mod):
    depth, d, n = w_mod.shape
    rows = cvec.shape[0]
    tn = 1536
    return pl.pallas_call(
        _mod_kernel,
        grid=(depth, n // tn),
        in_specs=[pl.BlockSpec((rows, d), lambda l, j: (0, 0)),
                  pl.BlockSpec((1, d, tn), lambda l, j: (l, 0, j)),
                  pl.BlockSpec((1, 1, tn), lambda l, j: (l, 0, j))],
        out_specs=pl.BlockSpec((1, rows, tn), lambda l, j: (l, 0, j)),
        out_shape=jax.ShapeDtypeStruct((depth, rows, n), F32),
        compiler_params=_cparams(("arbitrary", "arbitrary")),
        name="modulation",
    )(cvec, w_mod, b_mod.reshape(depth, 1, n))


def _swap16(n, lane):
    fwd = pltpu.roll(n, LANES - 16, 1)
    bwd = pltpu.roll(n, 16, 1)
    return jnp.where((lane % 32) < 16, fwd, bwd)


def _inproj_kernel(*refs, rope):
    if rope:
        (h_ref, sh_ref, sc_ref, gain_ref, w_ref, hg_ref, seg_ref, cos_ref, sin_ref,
         gla_ref, gdn_ref, q_ref, kv_ref, small_ref) = refs
    else:
        (h_ref, sh_ref, sc_ref, gain_ref, w_ref, hg_ref, seg_ref,
         gla_ref, gdn_ref, q_ref, kv_ref, small_ref) = refs
    x = h_ref[...]
    ms = jnp.mean(x * x, axis=-1, keepdims=True)
    a = x * lax.rsqrt(ms + EPS) * (gain_ref[...] * (1.0 + sc_ref[0])) + sh_ref[0]
    p = jnp.dot(a.astype(BF16), w_ref[...], preferred_element_type=F32)
    gla_ref[...] = p[:, 0:1024]
    gdn_ref[...] = p[:, 1024:2048]
    small_ref[...] = p[:, 2816:2944]
    seg = seg_ref[...]
    lane = _iota((1, LANES), 1)
    outs = []
    for j in range(5):
        t = p[:, 2048 + LANES * j:2048 + LANES * (j + 1)]
        ss = _dot_xhl(t * t, seg)
        n = t * lax.rsqrt(ss * (1.0 / HEAD_DIM) + EPS) * hg_ref[:, LANES * j:LANES * (j + 1)]
        if rope:
            n = n * cos_ref[...] + _swap16(n, lane) * sin_ref[...]
        outs.append(n)
    q_ref[...] = jnp.concatenate(outs[:4], axis=1).astype(BF16)
    kv_ref[...] = jnp.concatenate([outs[4], p[:, 2688:2816]], axis=1).astype(BF16)


def _inproj(h2d, mod144, mod_row_fn, gain, w_p, hg, seg, tables, seq_len, tm):
    rows, d = h2d.shape
    n_all = w_p.shape[1]
    rope = tables is not None
    tiles_per_seq = seq_len // tm
    in_specs = [pl.BlockSpec((tm, d), lambda i: (i, 0)),
                pl.BlockSpec((1, 1, d), lambda i: (mod_row_fn(i // tiles_per_seq) * 6 + 0, 0, 0)),
                pl.BlockSpec((1, 1, d), lambda i: (mod_row_fn(i // tiles_per_seq) * 6 + 1, 0, 0)),
                pl.BlockSpec((1, d), lambda i: (0, 0)),
                pl.BlockSpec((d, n_all), lambda i: (0, 0)),
                pl.BlockSpec((1, 640), lambda i: (0, 0)),
                pl.BlockSpec((LANES, LANES), lambda i: (0, 0))]
    args = [h2d, mod144, mod144, gain, w_p, hg, seg]
    if rope:
        in_specs += [pl.BlockSpec((tm, LANES), lambda i: (i % tiles_per_seq, 0)),
                     pl.BlockSpec((tm, LANES), lambda i: (i % tiles_per_seq, 0))]
        args += list(tables)
    out_shape = (jax.ShapeDtypeStruct((rows, 1024), F32), jax.ShapeDtypeStruct((rows, 1024), F32),
                 jax.ShapeDtypeStruct((rows, 512), BF16), jax.ShapeDtypeStruct((rows, 256), BF16),
                 jax.ShapeDtypeStruct((rows, LANES), F32))
    out_specs = (pl.BlockSpec((tm, 1024), lambda i: (i, 0)), pl.BlockSpec((tm, 1024), lambda i: (i, 0)),
                 pl.BlockSpec((tm, 512), lambda i: (i, 0)), pl.BlockSpec((tm, 256), lambda i: (i, 0)),
                 pl.BlockSpec((tm, LANES), lambda i: (i, 0)))
    return pl.pallas_call(
        functools.partial(_inproj_kernel, rope=rope),
        grid=(rows // tm,), in_specs=in_specs, out_specs=out_specs, out_shape=out_shape,
        compiler_params=_cparams(("parallel",)),
        name="inproj_rope" if rope else "inproj",
    )(*args)


ATT_CHUNKS_PER_DOT = 1


def _attn_kernel(*refs, nkv, tq):
    q_ref = refs[0]
    kv_refs = refs[1:1 + nkv]
    o_ref = refs[1 + nkv]
    q = q_ref[0]
    lane = _iota((1, LANES), 1)
    mlo = (lane < HEAD_DIM).astype(BF16)
    mhi = (lane >= HEAD_DIM).astype(BF16)
    kvs = [r[0] for r in kv_refs]
    def scores(jj):
        pieces = []
        for j in range(jj, jj + ATT_CHUNKS_PER_DOT):
            qc = q[:, LANES * j:LANES * (j + 1)]
            pieces += [qc * mlo, qc * mhi]
        q_all = jnp.concatenate(pieces, axis=0)
        return [lax.dot_general(q_all, kv[:, 0:LANES], (((1,), (1,)), ((), ())), preferred_element_type=F32)
                for kv in kvs]

    outs = []
    starts = list(range(0, 4, ATT_CHUNKS_PER_DOT))
    ss_next = scores(starts[0])
    for n, jj in enumerate(starts):
        ss = ss_next
        if n + 1 < len(starts):
            ss_next = scores(starts[n + 1])
        m = functools.reduce(jnp.maximum, [jnp.max(s, axis=-1, keepdims=True) for s in ss])
        ps = [jnp.exp2(s - m) for s in ss]
        l = functools.reduce(lambda a, b: a + b, [jnp.sum(p, axis=-1, keepdims=True) for p in ps])
        o = functools.reduce(lambda a, b: a + b,
                             [jnp.dot(p.astype(BF16), kv[:, LANES:2 * LANES], preferred_element_type=F32)
                              for p, kv in zip(ps, kvs)])
        o = o / l
        for j in range(ATT_CHUNKS_PER_DOT):
            outs.append(jnp.where(lane < HEAD_DIM, o[2 * j * tq:(2 * j + 1) * tq],
                                  o[(2 * j + 1) * tq:(2 * j + 2) * tq]))
    o_ref[0] = jnp.concatenate(outs, axis=1).astype(BF16)


def _attention(q, kvs, tq):
    b, lq, _ = q.shape
    in_specs = [pl.BlockSpec((1, tq, 512), lambda i, j: (i, j, 0))]
    for kv in kvs:
        in_specs.append(pl.BlockSpec((1, kv.shape[1], 256), lambda i, j: (i, 0, 0)))
    return pl.pallas_call(
        functools.partial(_attn_kernel, nkv=len(kvs), tq=tq),
        grid=(b, lq // tq), in_specs=in_specs,
        out_specs=pl.BlockSpec((1, tq, 512), lambda i, j: (i, j, 0)),
        out_shape=jax.ShapeDtypeStruct((b, lq, 512), BF16),
        compiler_params=_cparams(("parallel", "arbitrary")),
        name="attention",
    )(q, *kvs)


def _head_masks():
    lane = _iota((1, MIX_W), 1)
    return [(lane // HEAD_DIM == h).astype(F32) for h in range(GLA_HEADS)]


def _blockdiag_mask():
    r = _iota((MIX_W, MIX_W), 0) // HEAD_DIM
    c = _iota((MIX_W, MIX_W), 1) // HEAD_DIM
    return r == c


def _finish_rows(o, gate, gain, seg):
    ss = _seg_sum64(o * o, seg)
    return o * lax.rsqrt(ss * (1.0 / HEAD_DIM) + EPS) * gain * _silu(gate)


PAIR_W = 2 * HEAD_DIM
PREP_CHUNKS = 4
SCAN_UNROLL = 2


def _gla_kernel(*refs, need_ctx):
    if need_ctx:
        (xc_ref, xl_ref, sc_ref, sl_ref, wg_ref, bg_ref, gain_ref, seg_ref,
         oc_ref, ol_ref, of_ref, ob_ref, qh_ref, kh_ref, dg_ref, st_ref) = refs
    else:
        (xc_ref, xl_ref, sc_ref, sl_ref, wg_ref, bg_ref, gain_ref, seg_ref,
         ol_ref, of_ref, ob_ref, qh_ref, kh_ref, dg_ref, st_ref) = refs
        oc_ref = None
    lc = xc_ref.shape[1]
    x_refs = (xc_ref, xl_ref)
    s_refs = (sc_ref, sl_ref)
    row_off = (0, lc)
    o_refs = (of_ref, ob_ref)
    hmask = _head_masks()
    bd = _blockdiag_mask()
    ti = _iota((CHUNK, CHUNK), 0)
    si = _iota((CHUNK, CHUNK), 1)
    tri = [(si <= ti).astype(BF16), (si >= ti).astype(BF16)]
    rr = _iota((4 * CHUNK, CHUNK), 0)
    cc = _iota((4 * CHUNK, CHUNK), 1)
    t_of_row = (rr // (GLA_HEADS * SUB)) * SUB + rr % SUB
    causal = [cc <= t_of_row, cc >= t_of_row]
    nblk = CHUNK // SUB

    def prep(seg_i, c2):
        x_ref, s_ref = x_refs[seg_i], s_refs[seg_i]
        fs = []
        for kk in range(PREP_CHUNKS):
            r0 = pl.multiple_of((c2 * PREP_CHUNKS + kk) * CHUNK, CHUNK)
            fs.append(dict(row=row_off[seg_i] + r0,
                           q=x_ref[0, pl.ds(r0, CHUNK), 0:MIX_W] * (HEAD_DIM ** -0.5),
                           k=x_ref[0, pl.ds(r0, CHUNK), MIX_W:2 * MIX_W],
                           v=x_ref[0, pl.ds(r0, CHUNK), 2 * MIX_W:3 * MIX_W],
                           sm=s_ref[0, pl.ds(r0, CHUNK), :]))
        chains = [(f, d) for f in fs for d in range(2)]
        xg = [_dot3(f["sm"], wg_ref[d]) + bg_ref[d] for f, d in chains]
        g = [(jnp.minimum(x_, 0.0) - jnp.log(1.0 + jnp.exp(-jnp.abs(x_)))) * (1.0 / GLA_TAU) for x_ in xg]
        b = [_dot_whl(tri[d], g_) for (_, d), g_ in zip(chains, g)]
        pieces = []
        for i in range(nblk):
            row_pieces = []
            for (f, d), g_, b_ in zip(chains, g, b):
                e = SUB * i if d == 0 else SUB * i + SUB - 1
                bref = b_[e:e + 1, :] - g_[e:e + 1, :]
                kt = f["k"] * jnp.exp(jnp.minimum(bref - b_, EXP_CLAMP))
                qt = f["q"][SUB * i:SUB * (i + 1), :] * jnp.exp(b_[SUB * i:SUB * (i + 1), :] - bref)
                qs = jnp.concatenate([qt * hmask[h] for h in range(GLA_HEADS)], axis=0)
                row_pieces.append(_dot_nt(qs, kt))
            pieces.append(row_pieces)
        scores = [jnp.where(causal[d], jnp.concatenate([pieces[i][j] for i in range(nblk)], axis=0), 0.0)
                  for j, (_, d) in enumerate(chains)]
        r = [_dot(s_, f["v"]) for (f, _), s_ in zip(chains, scores)]
        for j, (f, d) in enumerate(chains):
            intra = []
            for i in range(nblk):
                acc = None
                for h in range(GLA_HEADS):
                    lo = (i * GLA_HEADS + h) * SUB
                    term = r[j][lo:lo + SUB, :] * hmask[h]
                    acc = term if acc is None else acc + term
                intra.append(acc)
            row = f["row"]
            e = CHUNK - 1 if d == 0 else 0
            b_end = b[j][e:e + 1, :]
            o_refs[d][pl.ds(row, CHUNK), :] = jnp.concatenate(intra, axis=0)
            qh_ref[d, pl.ds(row, CHUNK), :] = (f["q"] * jnp.exp(b[j])).astype(BF16)
            kh_ref[d, pl.ds(row, CHUNK), :] = (f["k"] * jnp.exp(b_end - b[j])).astype(BF16)
            dg_ref[d, pl.ds(pl.multiple_of(row // 8, 8), 8), :] = jnp.broadcast_to(jnp.exp(b_end), (8, MIX_W))

    for seg_i, x_ref in enumerate(x_refs):
        def prep_body(c, carry, seg_i=seg_i):
            prep(seg_i, c)
            return carry

        lax.fori_loop(0, x_ref.shape[1] // (CHUNK * PREP_CHUNKS), prep_body, 0)

    st_ref[...] = jnp.zeros(st_ref.shape, F32)

    def scan_body(seg_i, i0, nch):
        x_ref = x_refs[seg_i]
        steps = [i0 * SCAN_UNROLL + k for k in range(SCAN_UNROLL)]
        r0s = [[pl.multiple_of((i if d == 0 else nch - 1 - i) * CHUNK, CHUNK) for d in range(2)] for i in steps]
        rows = [[row_off[seg_i] + r0 for r0 in r] for r in r0s]
        upd = [[jnp.dot(x_ref[0, pl.ds(r0s[k][d], CHUNK), 2 * MIX_W:3 * MIX_W].T.astype(BF16),
                        kh_ref[d, pl.ds(rows[k][d], CHUNK), :], preferred_element_type=F32) for d in range(2)]
               for k in range(SCAN_UNROLL)]
        st = [st_ref[d] for d in range(2)]
        for k in range(SCAN_UNROLL):
            inter = [_dot_nt(qh_ref[d, pl.ds(rows[k][d], CHUNK), :], st[d]) for d in range(2)]
            for d in range(2):
                o_refs[d][pl.ds(rows[k][d], CHUNK), :] += inter[d]
                dgr = dg_ref[d, pl.ds(pl.multiple_of(rows[k][d] // 8, 8), 8), :][0:1, :]
                st[d] = st[d] * dgr + jnp.where(bd, upd[k][d], 0.0)
        for d in range(2):
            st_ref[d] = st[d]

    for seg_i, x_ref in enumerate(x_refs):
        nch = x_ref.shape[1] // CHUNK

        def scan_iter(i, carry, seg_i=seg_i, nch=nch):
            scan_body(seg_i, i, nch)
            return carry

        lax.fori_loop(0, nch // SCAN_UNROLL, scan_iter, 0)

    gain = gain_ref[...]
    seg = seg_ref[...]
    blk = 256
    outs = ((oc_ref, xc_ref, 0), (ol_ref, xl_ref, lc))
    for o_ref, x_ref, off in outs:
        if o_ref is None:
            continue

        def fin(i, carry, o_ref=o_ref, x_ref=x_ref, off=off):
            r0 = pl.multiple_of(i * blk, blk)
            o = of_ref[pl.ds(off + r0, blk), :] + ob_ref[pl.ds(off + r0, blk), :]
            gate = x_ref[0, pl.ds(r0, blk), 3 * MIX_W:4 * MIX_W]
            o_ref[0, pl.ds(r0, blk), :] = _finish_rows(o, gate, gain, seg).astype(BF16)
            return carry

        lax.fori_loop(0, x_ref.shape[1] // blk, fin, 0)


def _gla(x_ctx, x_lat, s_ctx, s_lat, wg, bg, gain, seg, need_ctx):
    b, lc, _ = x_ctx.shape
    ll = x_lat.shape[1]
    in_specs = [pl.BlockSpec((1, lc, 1024), lambda i: (i, 0, 0)),
                pl.BlockSpec((1, ll, 1024), lambda i: (i, 0, 0)),
                pl.BlockSpec((1, lc, LANES), lambda i: (i, 0, 0)),
                pl.BlockSpec((1, ll, LANES), lambda i: (i, 0, 0)),
                pl.BlockSpec((2, LANES, MIX_W), lambda i: (0, 0, 0)),
                pl.BlockSpec((2, 1, MIX_W), lambda i: (0, 0, 0)),
                pl.BlockSpec((1, MIX_W), lambda i: (0, 0)),
                pl.BlockSpec((LANES, LANES), lambda i: (0, 0))]
    out_shape = [jax.ShapeDtypeStruct((b, ll, MIX_W), BF16)]
    out_specs = [pl.BlockSpec((1, ll, MIX_W), lambda i: (i, 0, 0))]
    if need_ctx:
        out_shape.insert(0, jax.ShapeDtypeStruct((b, lc, MIX_W), BF16))
        out_specs.insert(0, pl.BlockSpec((1, lc, MIX_W), lambda i: (i, 0, 0)))
    res = pl.pallas_call(
        functools.partial(_gla_kernel, need_ctx=need_ctx),
        grid=(b,), in_specs=in_specs, out_specs=out_specs, out_shape=out_shape,
        scratch_shapes=[pltpu.VMEM((lc + ll, MIX_W), F32), pltpu.VMEM((lc + ll, MIX_W), F32),
                        pltpu.VMEM((2, lc + ll, MIX_W), BF16), pltpu.VMEM((2, lc + ll, MIX_W), BF16),
                        pltpu.VMEM((2, (lc + ll) // 8, MIX_W), F32),
                        pltpu.VMEM((2, MIX_W, MIX_W), F32)],
        compiler_params=_cparams(("parallel",)),
        name="gla_scan",
    )(x_ctx, x_lat, s_ctx, s_lat, wg, bg, gain, seg)
    return (res[0], res[1]) if need_ctx else (None, res[0])


def _gdn2_kernel(*refs, need_ctx):
    if need_ctx:
        (xc_ref, xl_ref, sc_ref, sl_ref, cw_ref, alog_ref, dtb_ref, sel_ref, selb_ref, gain_ref, seg_ref,
         oc_ref, ol_ref, sk_ref, p_ref, qe_ref, sv_ref, kh_ref, dg_ref, of_ref, ob_ref, st_ref) = refs
    else:
        (xc_ref, xl_ref, sc_ref, sl_ref, cw_ref, alog_ref, dtb_ref, sel_ref, selb_ref, gain_ref, seg_ref,
         ol_ref, sk_ref, p_ref, qe_ref, sv_ref, kh_ref, dg_ref, of_ref, ob_ref, st_ref) = refs
        oc_ref = None
    lc = xc_ref.shape[1]
    x_refs = (xc_ref, xl_ref)
    s_refs = (sc_ref, sl_ref)
    row_off = (0, lc)
    npair = MIX_W // PAIR_W
    seg = seg_ref[...]
    ti = _iota((CHUNK, CHUNK), 0)
    si = _iota((CHUNK, CHUNK), 1)
    tri = [(si <= ti).astype(BF16), (si >= ti).astype(BF16)]
    ones = jnp.ones((CHUNK, CHUNK), BF16)
    tp = _iota((CHUNK, PAIR_W), 0)
    sp = _iota((CHUNK, PAIR_W), 1) % CHUNK
    le = sp <= tp
    ge = sp >= tp
    m_tri = [le, ge]
    m_strict = [sp < tp, sp > tp]
    m_sum = [ge.astype(F32).astype(BF16), le.astype(F32).astype(BF16)]
    blk16 = (tp // SUB) == (sp // SUB)
    eye = (tp == sp).astype(F32)
    bd2 = (_iota((PAIR_W, PAIR_W), 0) // HEAD_DIM) == (_iota((PAIR_W, PAIR_W), 1) // HEAD_DIM)
    lane_p = _iota((1, PAIR_W), 1)
    hm2 = [(lane_p // HEAD_DIM == h).astype(F32) for h in range(2)]
    cw = cw_ref[...]

    def pk2(y):
        yb = y.astype(BF16)
        return jnp.where(bd2, jnp.concatenate([yb, yb], axis=0), jnp.zeros((), BF16))

    def mm(x, y):
        return jnp.dot(x.astype(BF16), pk2(y), preferred_element_type=F32)

    def front(seg_i, c):
        x_ref, s_ref = x_refs[seg_i], s_refs[seg_i]
        ln = x_ref.shape[1]
        nch = ln // CHUNK
        r0 = pl.multiple_of(c * CHUNK, CHUNK)
        center = x_ref[0, pl.ds(r0, CHUNK), 0:3 * MIX_W]
        p0 = pl.multiple_of(jnp.maximum(r0 - 8, 0), 8)
        n0 = pl.multiple_of(jnp.minimum(r0 + CHUNK, ln - 8), 8)
        prev = x_ref[0, pl.ds(p0, 8), 0:3 * MIX_W] * jnp.where(c > 0, 1.0, 0.0)
        nxt = x_ref[0, pl.ds(n0, 8), 0:3 * MIX_W] * jnp.where(c < nch - 1, 1.0, 0.0)
        ext = jnp.concatenate([prev, center, nxt], axis=0)
        pad = SHORT_CONV // 2
        acc = None
        for j in range(SHORT_CONV):
            term = ext[8 - pad + j:8 - pad + j + CHUNK, :] * cw[j:j + 1, :]
            acc = term if acc is None else acc + term
        y = _silu(acc)
        qk = y[:, 0:2 * MIX_W]
        qk = qk * lax.rsqrt(_seg_sum64(qk * qk, seg, split=False) + EPS)
        f = dict(row=row_off[seg_i] + r0, qn=qk[:, 0:MIX_W] * (HEAD_DIM ** -0.5), kn=qk[:, MIX_W:2 * MIX_W],
                 v=y[:, 2 * MIX_W:3 * MIX_W])
        sm = s_ref[0, pl.ds(r0, CHUNK), :]
        beta = _sigmoid(sm)
        f["kk"], f["qk"] = [], []
        for p in range(npair):
            ls = slice(PAIR_W * p, PAIR_W * (p + 1))
            kstack = jnp.concatenate([f["kn"][:, ls] * hm2[0], f["kn"][:, ls] * hm2[1]], axis=0)
            f["kk"].append(_dot_nt(f["kn"][:, ls], kstack))
            f["qk"].append(_dot_nt(f["qn"][:, ls], kstack))
        gexp = [_dot_xhl(-jnp.exp(alog_ref[d]) * _softplus(sm + dtb_ref[d]), sel_ref[d]) for d in range(2)]
        f["ghl"] = [_split(g_) for g_ in gexp]
        f["bexp"] = [_dot(beta, selb_ref[d]) for d in range(2)]
        f["gam"] = [jnp.dot(tri[d], f["ghl"][d][0], preferred_element_type=F32)
                    + jnp.dot(tri[d], f["ghl"][d][1], preferred_element_type=F32) for d in range(2)]
        return f

    def prep(seg_i, c2):
        fs = [front(seg_i, c2 * PREP_CHUNKS + k) for k in range(PREP_CHUNKS)]
        chains = [(f, d, p) for f in fs for d in range(2) for p in range(npair)]
        lss = [slice(PAIR_W * p, PAIR_W * (p + 1)) for _, _, p in chains]
        gam_t = [f["gam"][d][:, ls] for (f, d, _), ls in zip(chains, lss)]
        gam_s = [jnp.dot(ones, f["ghl"][d][0][:, ls] * m_sum[d], preferred_element_type=F32)
                 + jnp.dot(ones, f["ghl"][d][1][:, ls] * m_sum[d], preferred_element_type=F32)
                 for (f, d, _), ls in zip(chains, lss)]
        bx = [f["bexp"][d][:, ls] for (f, d, _), ls in zip(chains, lss)]
        dec = [jnp.where(m_tri[d], jnp.exp(jnp.minimum(gt - gs, 0.0)), 0.0)
               for (_, d, _), gt, gs in zip(chains, gam_t, gam_s)]
        a = [jnp.where(m_strict[d], b_ * dc * f["kk"][p], 0.0) for (f, d, p), b_, dc in zip(chains, bx, dec)]
        dgn = [jnp.where(blk16, a_, 0.0) for a_ in a]
        lo = [a_ - g_ for a_, g_ in zip(a, dgn)]
        d2 = [mm(g_, g_) for g_ in dgn]
        t1 = [mm(eye - g_, eye + s_) for g_, s_ in zip(dgn, d2)]
        d4 = [mm(s_, s_) for s_ in d2]
        t2 = [mm(t_, eye + s_) for t_, s_ in zip(t1, d4)]
        d8 = [mm(s_, s_) for s_ in d4]
        t_inv = [mm(t_, eye + s_) for t_, s_ in zip(t2, d8)]
        m = [mm(t_, l_) for t_, l_ in zip(t_inv, lo)]
        m2 = [mm(m_, m_) for m_ in m]
        w1 = [mm(eye - m_, eye + s_) for m_, s_ in zip(m, m2)]
        w = [mm(w_, t_) for w_, t_ in zip(w1, t_inv)]
        egam = [jnp.exp(gt) for gt in gam_t]
        solv = [mm(w_, b_ * f["v"][:, ls]) for (f, _, _), w_, b_, ls in zip(chains, w, bx, lss)]
        solk = [mm(w_, b_ * eg * f["kn"][:, ls]) for (f, _, _), w_, b_, eg, ls in zip(chains, w, bx, egam, lss)]
        for i, (f, d, p) in enumerate(chains):
            ls, row = lss[i], f["row"]
            e = CHUNK - 1 if d == 0 else 0
            g_end = gam_t[i][e:e + 1, :]
            sv_ref[d, pl.ds(row, CHUNK), ls] = solv[i]
            sk_ref[d, pl.ds(row, CHUNK), ls] = solk[i].astype(BF16)
            p_ref[d, pl.ds(row, CHUNK), ls] = jnp.where(m_tri[d], f["qk"][p] * dec[i], 0.0).astype(BF16)
            qe_ref[d, pl.ds(row, CHUNK), ls] = (egam[i] * f["qn"][:, ls]).astype(BF16)
            kh_ref[d, pl.ds(row, CHUNK), ls] = f["kn"][:, ls] * jnp.exp(g_end - gam_t[i])
            dg_ref[d, pl.ds(pl.multiple_of(row // 8, 8), 8), ls] = jnp.broadcast_to(jnp.exp(g_end), (8, PAIR_W))

    for seg_i, x_ref in enumerate(x_refs):
        def prep_body(c, carry, seg_i=seg_i):
            prep(seg_i, c)
            return carry

        lax.fori_loop(0, x_ref.shape[1] // (CHUNK * PREP_CHUNKS), prep_body, 0)

    st_ref[...] = jnp.zeros(st_ref.shape, F32)

    def scan_body(seg_i, i0, nch):
        chains = [(d, p) for d in range(2) for p in range(npair)]
        lss = [slice(PAIR_W * p, PAIR_W * (p + 1)) for _, p in chains]
        st = [st_ref[d, p] for d, p in chains]
        pending = []
        for k in range(SCAN_UNROLL):
            i = i0 * SCAN_UNROLL + k
            rows = [row_off[seg_i] + pl.multiple_of((i if d == 0 else nch - 1 - i) * CHUNK, CHUNK)
                    for d, _ in chains]
            stb = [s_.astype(BF16) for s_ in st]
            u = [sv_ref[d, pl.ds(r, CHUNK), ls]
                 - jnp.dot(sk_ref[d, pl.ds(r, CHUNK), ls], sb, preferred_element_type=F32)
                 for (d, _), r, ls, sb in zip(chains, rows, lss, stb)]
            ku = [_dot(kh_ref[d, pl.ds(r, CHUNK), ls].T, u_) for (d, _), r, ls, u_ in zip(chains, rows, lss, u)]
            dgr = [dg_ref[d, pl.ds(pl.multiple_of(r // 8, 8), 8), ls][0:1, :]
                   for (d, _), r, ls in zip(chains, rows, lss)]
            st = [s_ * g_ + jnp.where(bd2, k_, 0.0) for s_, g_, k_ in zip(st, dgr, ku)]
            pending.append((rows, stb, u))
        for rows, stb, u in pending:
            oq = [jnp.dot(qe_ref[d, pl.ds(r, CHUNK), ls], sb, preferred_element_type=F32)
                  for (d, _), r, ls, sb in zip(chains, rows, lss, stb)]
            ou = [jnp.dot(p_ref[d, pl.ds(r, CHUNK), ls], pk2(u_), preferred_element_type=F32)
                  for (d, _), r, ls, u_ in zip(chains, rows, lss, u)]
            for j, (d, _) in enumerate(chains):
                dst = of_ref if d == 0 else ob_ref
                dst[pl.ds(rows[j], CHUNK), lss[j]] = oq[j] + ou[j]
        for j, (d, p) in enumerate(chains):
            st_ref[d, p] = st[j]

    for seg_i, x_ref in enumerate(x_refs):
        nch = x_ref.shape[1] // CHUNK

        def scan_iter(i, carry, seg_i=seg_i, nch=nch):
            scan_body(seg_i, i, nch)
            return carry

        lax.fori_loop(0, nch // SCAN_UNROLL, scan_iter, 0)

    gain = gain_ref[...]
    blk = 256
    outs = ((oc_ref, xc_ref, 0), (ol_ref, xl_ref, lc))
    for o_ref, x_ref, off in outs:
        if o_ref is None:
            continue

        def fin(i, carry, o_ref=o_ref, x_ref=x_ref, off=off):
            r0 = pl.multiple_of(i * blk, blk)
            o = of_ref[pl.ds(off + r0, blk), :] + ob_ref[pl.ds(off + r0, blk), :]
            gate = x_ref[0, pl.ds(r0, blk), 3 * MIX_W:4 * MIX_W]
            o_ref[0, pl.ds(r0, blk), :] = _finish_rows(o, gate, gain, seg).astype(BF16)
            return carry

        lax.fori_loop(0, x_ref.shape[1] // blk, fin, 0)


def _gdn2(x_ctx, x_lat, s_ctx, s_lat, cw, alog, dtb, sel, selb, gain, seg, need_ctx):
    b, lc, _ = x_ctx.shape
    ll = x_lat.shape[1]
    lt = lc + ll
    in_specs = [pl.BlockSpec((1, lc, 1024), lambda i: (i, 0, 0)),
                pl.BlockSpec((1, ll, 1024), lambda i: (i, 0, 0)),
                pl.BlockSpec((1, lc, LANES), lambda i: (i, 0, 0)),
                pl.BlockSpec((1, ll, LANES), lambda i: (i, 0, 0)),
                pl.BlockSpec((8, 3 * MIX_W), lambda i: (0, 0)),
                pl.BlockSpec((2, 1, LANES), lambda i: (0, 0, 0)),
                pl.BlockSpec((2, 1, LANES), lambda i: (0, 0, 0)),
                pl.BlockSpec((2, LANES, MIX_W), lambda i: (0, 0, 0)),
                pl.BlockSpec((2, LANES, MIX_W), lambda i: (0, 0, 0)),
                pl.BlockSpec((1, MIX_W), lambda i: (0, 0)),
                pl.BlockSpec((LANES, LANES), lambda i: (0, 0))]
    out_shape = [jax.ShapeDtypeStruct((b, ll, MIX_W), BF16)]
    out_specs = [pl.BlockSpec((1, ll, MIX_W), lambda i: (i, 0, 0))]
    if need_ctx:
        out_shape.insert(0, jax.ShapeDtypeStruct((b, lc, MIX_W), BF16))
        out_specs.insert(0, pl.BlockSpec((1, lc, MIX_W), lambda i: (i, 0, 0)))
    res = pl.pallas_call(
        functools.partial(_gdn2_kernel, need_ctx=need_ctx),
        grid=(b,), in_specs=in_specs, out_specs=out_specs, out_shape=out_shape,
        scratch_shapes=[pltpu.VMEM((2, lt, MIX_W), BF16), pltpu.VMEM((2, lt, MIX_W), BF16),
                        pltpu.VMEM((2, lt, MIX_W), BF16), pltpu.VMEM((2, lt, MIX_W), F32),
                        pltpu.VMEM((2, lt, MIX_W), F32), pltpu.VMEM((2, lt // 8, MIX_W), F32),
                        pltpu.VMEM((lt, MIX_W), F32), pltpu.VMEM((lt, MIX_W), F32),
                        pltpu.VMEM((2, MIX_W // PAIR_W, PAIR_W, PAIR_W), F32)],
        compiler_params=_cparams(("parallel",)),
        name="gdn_scan",
    )(x_ctx, x_lat, s_ctx, s_lat, cw, alog, dtb, sel, selb, gain, seg)
    return (res[0], res[1]) if need_ctx else (None, res[0])


def _mix_residual(gla_ref, gdn_ref, att_ref, h_ref, g1_ref, wo_ref):
    y = (jnp.dot(gla_ref[...], wo_ref[0:256, :], preferred_element_type=F32)
         + jnp.dot(gdn_ref[...], wo_ref[256:512, :], preferred_element_type=F32)
         + jnp.dot(att_ref[...], wo_ref[512:1024, :], preferred_element_type=F32))
    return h_ref[...] + g1_ref[0] * y


def _mix_specs(tm, d, index):
    return [pl.BlockSpec((tm, 256), lambda *g: (index(*g), 0)),
            pl.BlockSpec((tm, 256), lambda *g: (index(*g), 0)),
            pl.BlockSpec((tm, 512), lambda *g: (index(*g), 0)),
            pl.BlockSpec((d, d), lambda *g: (0, 0))]


def _norm_mod(x, sh_ref, sc_ref, gain_ref):
    ms = jnp.mean(x * x, axis=-1, keepdims=True)
    return x * lax.rsqrt(ms + EPS) * (gain_ref[...] * (1.0 + sc_ref[0])) + sh_ref[0]


def _swiglu_part(x, wg, wu, wd):
    gg = jnp.dot(x, wg[...], preferred_element_type=F32)
    uu = jnp.dot(x, wu[...], preferred_element_type=F32)
    hid = (_silu(gg) * uu).astype(BF16)
    return jnp.dot(hid, wd[...], preferred_element_type=F32)


def _ffn_kernel(gla_ref, gdn_ref, att_ref, wo_ref, h_ref, g1_ref, sh_ref, sc_ref, g2_ref, gain_ref,
                wg_ref, wu_ref, wd_ref, o_ref, h1_scr, b_scr, acc_scr, *, nf):
    f = pl.program_id(1)

    @pl.when(f == 0)
    def _():
        h1 = _mix_residual(gla_ref, gdn_ref, att_ref, h_ref, g1_ref, wo_ref)
        h1_scr[...] = h1
        b_scr[...] = _norm_mod(h1, sh_ref, sc_ref, gain_ref).astype(BF16)
        acc_scr[...] = jnp.zeros(acc_scr.shape, F32)

    b = b_scr[...]
    acc_scr[...] += _swiglu_part(b, wg_ref, wu_ref, wd_ref)

    @pl.when(f == nf - 1)
    def _():
        o_ref[...] = h1_scr[...] + g2_ref[0] * acc_scr[...]


def _ffn(mix, w_o, h2d, mod144, mod_row_fn, gain, w_gu, w_down, seq_len, tm, tf):
    rows, d = h2d.shape
    dff = w_down.shape[0]
    nf = dff // tf
    tiles_per_seq = seq_len // tm

    def mod_spec(k):
        return pl.BlockSpec((1, 1, d), lambda i, f: (mod_row_fn(i // tiles_per_seq) * 6 + k, 0, 0))

    return pl.pallas_call(
        functools.partial(_ffn_kernel, nf=nf),
        grid=(rows // tm, nf),
        in_specs=_mix_specs(tm, d, lambda i, f: i) + [
            pl.BlockSpec((tm, d), lambda i, f: (i, 0)),
            mod_spec(2), mod_spec(3), mod_spec(4), mod_spec(5),
            pl.BlockSpec((1, d), lambda i, f: (0, 0)),
            pl.BlockSpec((d, tf), lambda i, f: (0, f)),
            pl.BlockSpec((d, tf), lambda i, f: (0, nf + f)),
            pl.BlockSpec((tf, d), lambda i, f: (f, 0))],
        out_specs=pl.BlockSpec((tm, d), lambda i, f: (i, 0)),
        out_shape=jax.ShapeDtypeStruct((rows, d), F32),
        scratch_shapes=[pltpu.VMEM((tm, d), F32), pltpu.VMEM((tm, d), BF16), pltpu.VMEM((tm, d), F32)],
        compiler_params=_cparams(("parallel", "arbitrary")),
        name="ffn",
    )(*mix, w_o, h2d, mod144, mod144, mod144, mod144, gain, w_gu, w_gu, w_down)


MOE_TILE = 512
COMBINE_TOKENS = 512


def _router_kernel(gla_ref, gdn_ref, att_ref, wo_ref, h_ref, g1_ref, sh_ref, sc_ref, gain_ref, wr_ref, br_ref,
                   h1_ref, b_ref, route_ref):
    lane = _iota((1, LANES), 1)
    lane_f = lane.astype(F32)
    h1 = _mix_residual(gla_ref, gdn_ref, att_ref, h_ref, g1_ref, wo_ref)
    h1_ref[...] = h1
    b = _norm_mod(h1, sh_ref, sc_ref, gain_ref)
    b_ref[...] = b
    logits = _dot3(b, wr_ref[...]) + br_ref[...]
    logits = jnp.where(lane < N_EXPERTS, logits, -jnp.inf)
    m1 = jnp.max(logits, axis=-1, keepdims=True)
    i1 = jnp.min(jnp.where(logits == m1, lane_f, float(LANES)), axis=-1, keepdims=True)
    rest = jnp.where(lane_f == i1, -jnp.inf, logits)
    m2 = jnp.max(rest, axis=-1, keepdims=True)
    i2 = jnp.min(jnp.where(rest == m2, lane_f, float(LANES)), axis=-1, keepdims=True)
    t = jnp.exp(m2 - m1)
    w1 = 1.0 / (1.0 + t)
    route_ref[...] = (jnp.where(lane == 0, i1, 0.0) + jnp.where(lane == 1, i2, 0.0)
                      + jnp.where(lane == 2, w1, 0.0) + jnp.where(lane == 3, t * w1, 0.0))


def _router(mix, w_o, h2d, mod144, mod_row_fn, gain, w_router, b_router, seq_len, tm):
    rows, d = h2d.shape
    tiles_per_seq = seq_len // tm

    def mod_spec(k):
        return pl.BlockSpec((1, 1, d), lambda i: (mod_row_fn(i // tiles_per_seq) * 6 + k, 0, 0))

    row_spec = pl.BlockSpec((tm, d), lambda i: (i, 0))
    return pl.pallas_call(
        _router_kernel,
        grid=(rows // tm,),
        in_specs=_mix_specs(tm, d, lambda i: i) + [
            row_spec, mod_spec(2), mod_spec(3), mod_spec(4),
            pl.BlockSpec((1, d), lambda i: (0, 0)),
            pl.BlockSpec((d, LANES), lambda i: (0, 0)),
            pl.BlockSpec((1, LANES), lambda i: (0, 0))],
        out_specs=(row_spec, row_spec, pl.BlockSpec((tm, LANES), lambda i: (i, 0))),
        out_shape=(jax.ShapeDtypeStruct((rows, d), F32), jax.ShapeDtypeStruct((rows, d), F32),
                   jax.ShapeDtypeStruct((rows, LANES), F32)),
        compiler_params=_cparams(("parallel",)),
        name="moe_router",
    )(*mix, w_o, h2d, mod144, mod144, mod144, gain, w_router, b_router)


DISPATCH_TOKENS = 1024


def _moe_dispatch_kernel(pos_ref, b_ref, xs_in_ref, xs_ref, sem):
    del xs_in_ref
    n = b_ref.shape[0]
    for r in range(2 * n):
        pltpu.make_async_copy(b_ref.at[pl.ds(r % n, 1)], xs_ref.at[pl.ds(pos_ref[0, 0, r], 1)],
                              sem).start(priority=r % 2)
    for _ in range(2):
        pltpu.make_async_copy(b_ref, xs_ref.at[pl.ds(0, n)], sem).wait()


def _moe_dispatch(pos2, b, p_rows):
    rows, d = b.shape
    tm = DISPATCH_TOKENS
    return pl.pallas_call(
        _moe_dispatch_kernel,
        grid=(rows // tm,),
        in_specs=[pl.BlockSpec((1, 1, 2 * tm), lambda i: (i, 0, 0), memory_space=pltpu.SMEM),
                  pl.BlockSpec((tm, d), lambda i: (i, 0)),
                  pl.BlockSpec(memory_space=pl.ANY)],
        out_specs=pl.BlockSpec(memory_space=pl.ANY),
        out_shape=jax.ShapeDtypeStruct((p_rows, d), F32),
        scratch_shapes=[pltpu.SemaphoreType.DMA(())],
        input_output_aliases={2: 0},
        compiler_params=_cparams(("arbitrary",)),
        name="moe_dispatch",
    )(pos2, b, jnp.zeros((p_rows, d), F32))


def _moe_group_kernel(te_ref, nt_ref, xs_ref, wg_ref, wu_ref, wd_ref, ys_ref, xb_scr, acc_scr, *, nf):
    i = pl.program_id(0)
    f = pl.program_id(1)
    used = i < nt_ref[0]

    @pl.when(used & (f == 0))
    def _():
        xb_scr[...] = xs_ref[...].astype(BF16)

    @pl.when(used)
    def _():
        x = xb_scr[...]
        part = _swiglu_part(x, wg_ref.at[0], wu_ref.at[0], wd_ref.at[0])

        @pl.when(f == 0)
        def _():
            acc_scr[...] = part

        @pl.when(f > 0)
        def _():
            acc_scr[...] += part

    @pl.when(used & (f == nf - 1))
    def _():
        ys_ref[...] = acc_scr[...]

    @pl.when(jnp.logical_not(used) & (f == nf - 1))
    def _():
        ys_ref[...] = jnp.zeros(ys_ref.shape, F32)


def _moe_group(tile_expert, n_tiles_used, xs, w_gu, w_down, tf):
    p, d = xs.shape
    dff = w_down.shape[1]
    nf = dff // tf
    grid_spec = pltpu.PrefetchScalarGridSpec(
        num_scalar_prefetch=2,
        grid=(p // MOE_TILE, nf),
        in_specs=[pl.BlockSpec((MOE_TILE, d), lambda i, f, te, nt: (i, 0)),
                  pl.BlockSpec((1, d, tf), lambda i, f, te, nt: (te[i], 0, f)),
                  pl.BlockSpec((1, d, tf), lambda i, f, te, nt: (te[i], 0, nf + f)),
                  pl.BlockSpec((1, tf, d), lambda i, f, te, nt: (te[i], f, 0))],
        out_specs=pl.BlockSpec((MOE_TILE, d), lambda i, f, te, nt: (i, 0)),
        scratch_shapes=[pltpu.VMEM((MOE_TILE, d), BF16), pltpu.VMEM((MOE_TILE, d), F32)])
    return pl.pallas_call(
        functools.partial(_moe_group_kernel, nf=nf),
        grid_spec=grid_spec,
        out_shape=jax.ShapeDtypeStruct((p, d), F32),
        compiler_params=_cparams(("arbitrary", "arbitrary")),
        name="moe_experts",
    )(tile_expert, n_tiles_used, xs, w_gu, w_gu, w_down)


def _moe_combine_kernel(pos_ref, ys_ref, h_ref, route_ref, g_ref, o_ref, buf, sem):
    n = buf.shape[0]
    half = n // 2

    for r in range(n):
        pltpu.make_async_copy(ys_ref.at[pl.ds(pos_ref[0, 0, r], 1)], buf.at[pl.ds(r, 1)], sem).start(priority=r % 2)
    pltpu.make_async_copy(ys_ref.at[pl.ds(0, n)], buf, sem).wait()
    w1 = route_ref[:, 2:3]
    w2 = route_ref[:, 3:4]
    o_ref[...] = h_ref[...] + g_ref[0] * (w1 * buf[0:half, :] + w2 * buf[half:n, :])


def _moe_combine(pos, ys, h2d, route, mod144, mod_row_fn, seq_len):
    rows, d = h2d.shape
    tm = pos.shape[2] // 2
    tiles_per_seq = seq_len // tm
    steps = rows // tm
    return pl.pallas_call(
        _moe_combine_kernel,
        grid=(steps,),
        in_specs=[pl.BlockSpec((1, 1, 2 * tm), lambda i: (i, 0, 0), memory_space=pltpu.SMEM),
                  pl.BlockSpec(memory_space=pl.ANY),
                  pl.BlockSpec((tm, d), lambda i: (i, 0)),
                  pl.BlockSpec((tm, LANES), lambda i: (i, 0)),
                  pl.BlockSpec((1, 1, d), lambda i: (mod_row_fn(i // tiles_per_seq) * 6 + 5, 0, 0))],
        out_specs=pl.BlockSpec((tm, d), lambda i: (i, 0)),
        out_shape=jax.ShapeDtypeStruct((rows, d), F32),
        scratch_shapes=[pltpu.VMEM((2 * tm, d), F32), pltpu.SemaphoreType.DMA(())],
        compiler_params=_cparams(("arbitrary",)),
        name="moe_combine",
    )(pos, ys, h2d, route, mod144)


def _moe_routed(mix, w_o, h2d, mod144, mod_row_fn, gain, w_router, b_router, w_gu, w_down, seq_len):
    rows, d = h2d.shape
    h1, b, route = _router(mix, w_o, h2d, mod144, mod_row_fn, gain, w_router, b_router, seq_len,
                           _rows(ROW_TILE, seq_len))
    ex = jnp.concatenate([route[:, 0], route[:, 1]]).astype(jnp.int32)
    onehot = (ex[:, None] == jnp.arange(N_EXPERTS, dtype=jnp.int32)[None, :]).astype(jnp.int32)
    rank = jnp.sum((jnp.cumsum(onehot, axis=0) - onehot) * onehot, axis=1)
    counts = jnp.sum(onehot, axis=0)
    padded = ((counts + MOE_TILE - 1) // MOE_TILE) * MOE_TILE
    ends = jnp.cumsum(padded)
    starts = ends - padded
    pos = jnp.sum(onehot * starts[None, :], axis=1) + rank
    pos = pos.astype(jnp.int32)
    p_rows = 2 * rows + N_EXPERTS * MOE_TILE
    tile_first = jnp.arange(p_rows // MOE_TILE, dtype=jnp.int32) * MOE_TILE
    tile_expert = jnp.minimum(jnp.sum((tile_first[:, None] >= ends[None, :]).astype(jnp.int32), axis=1),
                              N_EXPERTS - 1).astype(jnp.int32)
    n_tiles_used = (ends[-1:] // MOE_TILE).astype(jnp.int32)

    def per_tile(tm):
        return jnp.concatenate([pos[:rows].reshape(rows // tm, 1, tm), pos[rows:].reshape(rows // tm, 1, tm)], axis=2)

    xs = _moe_dispatch(per_tile(DISPATCH_TOKENS), b, p_rows)
    ys = _moe_group(tile_expert, n_tiles_used, xs, w_gu, w_down, FF_TILE)
    return _moe_combine(per_tile(_rows(COMBINE_TOKENS, seq_len)), ys, h1, route, mod144, mod_row_fn, seq_len)


ATT_HEAD_ORDER = (0, 4, 1, 5, 2, 6, 3, 7)


def _layout_w_in(w):
    gla = w[:, 0:1024]
    glow = w[:, 1024:1056]
    gdn = w[:, 1056:2080]
    ab = w[:, 2080:2096]
    q = jnp.concatenate([w[:, 2096 + HEAD_DIM * h:2096 + HEAD_DIM * (h + 1)] for h in ATT_HEAD_ORDER], axis=1)
    kv = w[:, 2608:2864]
    pad = jnp.zeros((w.shape[0], LANES - 48), w.dtype)
    return jnp.concatenate([gla, gdn, q, kv, glow, ab, pad], axis=1).astype(BF16)


def _layout_w_out(w):
    att = [w[512 + HEAD_DIM * h:512 + HEAD_DIM * (h + 1)] for h in ATT_HEAD_ORDER]
    return jnp.concatenate([w[0:512]] + att, axis=0).astype(BF16)


def _rope_tables(seq_len):
    rows = seq_len // GRID_W
    row = jnp.repeat(jnp.arange(rows), GRID_W).astype(F32)
    col = jnp.tile(jnp.arange(GRID_W), rows).astype(F32)
    inv_freq = ROPE_THETA ** (-jnp.arange(0, HEAD_DIM // 2, 2, dtype=F32) / (HEAD_DIM // 2))
    ar = row[:, None] * inv_freq
    ac = col[:, None] * inv_freq
    cos = jnp.concatenate([jnp.cos(ar), jnp.cos(ar), jnp.cos(ac), jnp.cos(ac)], axis=-1)
    sin = jnp.concatenate([-jnp.sin(ar), jnp.sin(ar), -jnp.sin(ac), jnp.sin(ac)], axis=-1)
    return jnp.tile(cos, (1, 2)), jnp.tile(sin, (1, 2))


def _seg_matrix():
    i = np.arange(LANES)
    return jnp.asarray((i[:, None] // HEAD_DIM) == (i[None, :] // HEAD_DIM), dtype=BF16)


def _gdn_select():
    sel = np.zeros((2, LANES, MIX_W), np.float32)
    selb = np.zeros((2, LANES, MIX_W), np.float32)
    for d in range(2):
        for h in range(GDN_HEADS):
            sel[d, 32 + GDN_HEADS * d + h, HEAD_DIM * h:HEAD_DIM * (h + 1)] = 1.0
            selb[d, 40 + GDN_HEADS * d + h, HEAD_DIM * h:HEAD_DIM * (h + 1)] = 1.0
    return jnp.asarray(sel, BF16), jnp.asarray(selb, BF16)


def _lane_rows(vals, base):
    out = jnp.zeros((2, 1, LANES), F32)
    for d in range(2):
        out = out.at[d, 0, base + GDN_HEADS * d:base + GDN_HEADS * (d + 1)].set(vals[d].astype(F32))
    return out


def kernel(x, c, ctx, c_ctx, w_mod, b_mod, norm_mix, norm_ffn, w_in, gla_gate_up, gla_gate_bias, gla_out_gain,
           gdn_conv, gdn_a_log, gdn_dt_bias, gdn_out_gain, att_q_gain, att_k_gain, w_out, ffn_gate_up, ffn_down,
           moe_router, moe_router_bias, moe_gate_up, moe_down):
    bsz, seq, d = x.shape
    lctx = ctx.shape[1]
    depth = w_mod.shape[0]
    ctx_row = bsz

    mod_rows = ((bsz + 1 + 7) // 8) * 8
    cvec = jnp.concatenate([c, c_ctx[None, :], jnp.zeros((mod_rows - bsz - 1, d), F32)], axis=0)
    mods = _modulation(cvec, w_mod, b_mod)

    seg = _seg_matrix()
    tables = _rope_tables(seq)
    sel, selb = _gdn_select()
    lat_row = lambda b: b
    ctx_row_fn = lambda b: ctx_row

    h_lat = x.reshape(bsz * seq, d)
    h_ctx = ctx.reshape(bsz * lctx, d)
    for layer in range(depth):
        need_ctx = layer < depth - 1
        mod144 = mods[layer].reshape(mod_rows * 6, 1, d)
        w_p = _layout_w_in(w_in[layer])
        w_o = _layout_w_out(w_out[layer])
        hg = jnp.concatenate([jnp.tile(att_q_gain[layer], ATT_Q_HEADS) * (HEAD_DIM ** -0.5 * LOG2E),
                              jnp.tile(att_k_gain[layer], ATT_KV_HEADS)])[None, :].astype(F32)
        gain_mix = norm_mix[layer][None, :]
        gain_ffn = norm_ffn[layer][None, :]

        gla_l, gdn_l, q_l, kv_l, sm_l = _inproj(h_lat, mod144, lat_row, gain_mix, w_p, hg, seg, tables, seq,
                                                _rows(INPROJ_ROWS, seq))
        gla_c, gdn_c, q_c, kv_c, sm_c = _inproj(h_ctx, mod144, ctx_row_fn, gain_mix, w_p, hg, seg, None, lctx,
                                                _rows(INPROJ_ROWS, lctx))

        r3 = lambda t, n: t.reshape(bsz, n, t.shape[-1])
        wg = jnp.zeros((2, LANES, MIX_W), F32)
        for dd in range(2):
            wg = wg.at[dd, GLA_GATE_RANK * dd:GLA_GATE_RANK * (dd + 1), :].set(gla_gate_up[layer, dd].astype(F32))
        bg = gla_gate_bias[layer].reshape(2, 1, MIX_W).astype(F32)
        gla_gain = jnp.tile(gla_out_gain[layer], GLA_HEADS)[None, :].astype(F32)
        o_gla_c, o_gla_l = _gla(r3(gla_c, lctx), r3(gla_l, seq), r3(sm_c, lctx), r3(sm_l, seq),
                                wg, bg, gla_gain, seg, need_ctx)

        cw = jnp.concatenate([gdn_conv[layer].astype(F32), jnp.zeros((8 - SHORT_CONV, 3 * MIX_W), F32)], axis=0)
        alog = _lane_rows(gdn_a_log[layer], 32)
        dtb = _lane_rows(gdn_dt_bias[layer], 32)
        gdn_gain = jnp.tile(gdn_out_gain[layer], GDN_HEADS)[None, :].astype(F32)
        o_gdn_c, o_gdn_l = _gdn2(r3(gdn_c, lctx), r3(gdn_l, seq), r3(sm_c, lctx), r3(sm_l, seq),
                                cw, alog, dtb, sel, selb, gdn_gain, seg, need_ctx)

        o_att_l = _attention(r3(q_l, seq), [r3(kv_l, seq), r3(kv_c, lctx)], _rows(ATT_Q_ROWS, seq))
        mix_l = (o_gla_l.reshape(-1, MIX_W), o_gdn_l.reshape(-1, MIX_W), o_att_l.reshape(-1, 512))
        if need_ctx:
            o_att_c = _attention(r3(q_c, lctx), [r3(kv_c, lctx)], _rows(ATT_Q_ROWS, lctx))
            mix_c = (o_gla_c.reshape(-1, MIX_W), o_gdn_c.reshape(-1, MIX_W), o_att_c.reshape(-1, 512))

        j = layer // 2
        if layer % 2 == 0:
            w_gu = ffn_gate_up[j].astype(BF16)
            w_dn = ffn_down[j].astype(BF16)
            h_lat = _ffn(mix_l, w_o, h_lat, mod144, lat_row, gain_ffn, w_gu, w_dn, seq, _rows(ROW_TILE, seq), FF_TILE)
            if need_ctx:
                h_ctx = _ffn(mix_c, w_o, h_ctx, mod144, ctx_row_fn, gain_ffn, w_gu, w_dn, lctx,
                             _rows(ROW_TILE, lctx), FF_TILE)
        else:
            w_gu = moe_gate_up[j].astype(BF16)
            w_dn = moe_down[j].astype(BF16)
            w_r = jnp.concatenate([moe_router[j].astype(F32), jnp.zeros((d, LANES - N_EXPERTS), F32)], axis=1)
            b_r = jnp.concatenate([moe_router_bias[j].astype(F32), jnp.zeros((LANES - N_EXPERTS,), F32)])[None, :]
            h_lat = _moe_routed(mix_l, w_o, h_lat, mod144, lat_row, gain_ffn, w_r, b_r, w_gu, w_dn, seq)
            if need_ctx:
                h_ctx = _moe_routed(mix_c, w_o, h_ctx, mod144, ctx_row_fn, gain_ffn, w_r, b_r, w_gu, w_dn, lctx)
    return h_lat.reshape(bsz, seq, d)
```

```python
import functools

import numpy as np
import jax
import jax.numpy as jnp
from jax import lax
from jax.experimental import pallas as pl
from jax.experimental.pallas import tpu as pltpu

F32 = jnp.float32
BF16 = jnp.bfloat16

GRID_W = 64
HEAD_DIM = 64
CHUNK = 64
SUB = 16
EPS = 1e-6
GLA_HEADS = 4
GLA_GATE_RANK = 16
GLA_TAU = 16.0
GDN_HEADS = 4
SHORT_CONV = 5
ATT_Q_HEADS = 8
ATT_KV_HEADS = 2
ROPE_THETA = 10000.0
N_EXPERTS = 8
MIX_W = GLA_HEADS * HEAD_DIM
LOG2E = 1.4426950408889634
EXP_CLAMP = 80.0

LANES = 128
V7X_VMEM_BYTES = 64 * 1024 * 1024
VMEM_LIMIT = V7X_VMEM_BYTES - 8 * 1024 * 1024


INPROJ_ROWS = 512
ROW_TILE = 512
ATT_Q_ROWS = 256
FF_TILE = 1408


def _rows(limit, seq_len):
    return min(limit, seq_len)


def _cparams(sem):
    return pltpu.CompilerParams(dimension_semantics=sem, vmem_limit_bytes=VMEM_LIMIT)


def _silu(x):
    return x / (1.0 + jnp.exp(-x))


def _sigmoid(x):
    return 1.0 / (1.0 + jnp.exp(-x))


def _softplus(x):
    return jnp.maximum(x, 0.0) + jnp.log(1.0 + jnp.exp(-jnp.abs(x)))


def _dot(a, b):
    return jnp.dot(a.astype(BF16), b.astype(BF16), preferred_element_type=F32)


def _dot_nt(a, b):
    return lax.dot_general(a.astype(BF16), b.astype(BF16), (((1,), (1,)), ((), ())),
                           preferred_element_type=F32)


def _split(x):
    hi = x.astype(BF16)
    lo = (x - hi.astype(F32)).astype(BF16)
    return hi, lo


def _dot_xhl(x, w):
    hi, lo = _split(x)
    w = w.astype(BF16)
    return (jnp.dot(hi, w, preferred_element_type=F32) + jnp.dot(lo, w, preferred_element_type=F32))


def _dot_whl(w, x):
    hi, lo = _split(x)
    w = w.astype(BF16)
    return (jnp.dot(w, hi, preferred_element_type=F32) + jnp.dot(w, lo, preferred_element_type=F32))


def _dot3(a, b):
    ah, al = _split(a)
    bh, bl = _split(b)
    return (jnp.dot(ah, bh, preferred_element_type=F32) + jnp.dot(ah, bl, preferred_element_type=F32)
            + jnp.dot(al, bh, preferred_element_type=F32))


def _seg_sum64(sq, seg, split=True):
    outs = []
    for j in range(sq.shape[1] // LANES):
        part = sq[:, LANES * j:LANES * (j + 1)]
        outs.append(_dot_xhl(part, seg) if split else _dot(part, seg))
    return outs[0] if len(outs) == 1 else jnp.concatenate(outs, axis=1)


def _iota(shape, dim):
    return lax.broadcasted_iota(jnp.int32, shape, dim)


def _mod_kernel(c_ref, w_ref, b_ref, o_ref):
    s = _silu(c_ref[...])
    o_ref[0] = _dot(s, w_ref[0]) + b_ref[0]


def _modulation(cvec, w_mod, b_mod):
    depth, d, n = w_mod.shape
    rows = cvec.shape[0]
    tn = 1536
    return pl.pallas_call(
        _mod_kernel,
        grid=(depth, n // tn),
        in_specs=[pl.BlockSpec((rows, d), lambda l, j: (0, 0)),
                  pl.BlockSpec((1, d, tn), lambda l, j: (l, 0, j)),
                  pl.BlockSpec((1, 1, tn), lambda l, j: (l, 0, j))],
        out_specs=pl.BlockSpec((1, rows, tn), lambda l, j: (l, 0, j)),
        out_shape=jax.ShapeDtypeStruct((depth, rows, n), F32),
        compiler_params=_cparams(("arbitrary", "arbitrary")),
        name="modulation",
    )(cvec, w_mod, b_mod.reshape(depth, 1, n))


def _swap16(n, lane):
    fwd = pltpu.roll(n, LANES - 16, 1)
    bwd = pltpu.roll(n, 16, 1)
    return jnp.where((lane % 32) < 16, fwd, bwd)


def _inproj_kernel(*refs, rope):
    if rope:
        (h_ref, sh_ref, sc_ref, gain_ref, w_ref, hg_ref, seg_ref, cos_ref, sin_ref,
         gla_ref, gdn_ref, q_ref, kv_ref, small_ref) = refs
    else:
        (h_ref, sh_ref, sc_ref, gain_ref, w_ref, hg_ref, seg_ref,
         gla_ref, gdn_ref, q_ref, kv_ref, small_ref) = refs
    x = h_ref[...]
    ms = jnp.mean(x * x, axis=-1, keepdims=True)
    a = x * lax.rsqrt(ms + EPS) * (gain_ref[...] * (1.0 + sc_ref[0])) + sh_ref[0]
    p = jnp.dot(a.astype(BF16), w_ref[...], preferred_element_type=F32)
    gla_ref[...] = p[:, 0:1024]
    gdn_ref[...] = p[:, 1024:2048]
    small_ref[...] = p[:, 2816:2944]
    seg = seg_ref[...]
    lane = _iota((1, LANES), 1)
    outs = []
    for j in range(5):
        t = p[:, 2048 + LANES * j:2048 + LANES * (j + 1)]
        ss = _dot_xhl(t * t, seg)
        n = t * lax.rsqrt(ss * (1.0 / HEAD_DIM) + EPS) * hg_ref[:, LANES * j:LANES * (j + 1)]
        if rope:
            n = n * cos_ref[...] + _swap16(n, lane) * sin_ref[...]
        outs.append(n)
    q_ref[...] = jnp.concatenate(outs[:4], axis=1).astype(BF16)
    kv_ref[...] = jnp.concatenate([outs[4], p[:, 2688:2816]], axis=1).astype(BF16)


def _inproj(h2d, mod144, mod_row_fn, gain, w_p, hg, seg, tables, seq_len, tm):
    rows, d = h2d.shape
    n_all = w_p.shape[1]
    rope = tables is not None
    tiles_per_seq = seq_len // tm
    in_specs = [pl.BlockSpec((tm, d), lambda i: (i, 0)),
                pl.BlockSpec((1, 1, d), lambda i: (mod_row_fn(i // tiles_per_seq) * 6 + 0, 0, 0)),
                pl.BlockSpec((1, 1, d), lambda i: (mod_row_fn(i // tiles_per_seq) * 6 + 1, 0, 0)),
                pl.BlockSpec((1, d), lambda i: (0, 0)),
                pl.BlockSpec((d, n_all), lambda i: (0, 0)),
                pl.BlockSpec((1, 640), lambda i: (0, 0)),
                pl.BlockSpec((LANES, LANES), lambda i: (0, 0))]
    args = [h2d, mod144, mod144, gain, w_p, hg, seg]
    if rope:
        in_specs += [pl.BlockSpec((tm, LANES), lambda i: (i % tiles_per_seq, 0)),
                     pl.BlockSpec((tm, LANES), lambda i: (i % tiles_per_seq, 0))]
        args += list(tables)
    out_shape = (jax.ShapeDtypeStruct((rows, 1024), F32), jax.ShapeDtypeStruct((rows, 1024), F32),
                 jax.ShapeDtypeStruct((rows, 512), BF16), jax.ShapeDtypeStruct((rows, 256), BF16),
                 jax.ShapeDtypeStruct((rows, LANES), F32))
    out_specs = (pl.BlockSpec((tm, 1024), lambda i: (i, 0)), pl.BlockSpec((tm, 1024), lambda i: (i, 0)),
                 pl.BlockSpec((tm, 512), lambda i: (i, 0)), pl.BlockSpec((tm, 256), lambda i: (i, 0)),
                 pl.BlockSpec((tm, LANES), lambda i: (i, 0)))
    return pl.pallas_call(
        functools.partial(_inproj_kernel, rope=rope),
        grid=(rows // tm,), in_specs=in_specs, out_specs=out_specs, out_shape=out_shape,
        compiler_params=_cparams(("parallel",)),
        name="inproj_rope" if rope else "inproj",
    )(*args)


ATT_CHUNKS_PER_DOT = 1


def _attn_kernel(*refs, nkv, tq):
    q_ref = refs[0]
    kv_refs = refs[1:1 + nkv]
    o_ref = refs[1 + nkv]
    q = q_ref[0]
    lane = _iota((1, LANES), 1)
    mlo = (lane < HEAD_DIM).astype(BF16)
    mhi = (lane >= HEAD_DIM).astype(BF16)
    kvs = [r[0] for r in kv_refs]
    def scores(jj):
        pieces = []
        for j in range(jj, jj + ATT_CHUNKS_PER_DOT):
            qc = q[:, LANES * j:LANES * (j + 1)]
            pieces += [qc * mlo, qc * mhi]
        q_all = jnp.concatenate(pieces, axis=0)
        return [lax.dot_general(q_all, kv[:, 0:LANES], (((1,), (1,)), ((), ())), preferred_element_type=F32)
                for kv in kvs]

    outs = []
    starts = list(range(0, 4, ATT_CHUNKS_PER_DOT))
    ss_next = scores(starts[0])
    for n, jj in enumerate(starts):
        ss = ss_next
        if n + 1 < len(starts):
            ss_next = scores(starts[n + 1])
        m = functools.reduce(jnp.maximum, [jnp.max(s, axis=-1, keepdims=True) for s in ss])
        ps = [jnp.exp2(s - m) for s in ss]
        l = functools.reduce(lambda a, b: a + b, [jnp.sum(p, axis=-1, keepdims=True) for p in ps])
        o = functools.reduce(lambda a, b: a + b,
                             [jnp.dot(p.astype(BF16), kv[:, LANES:2 * LANES], preferred_element_type=F32)
                              for p, kv in zip(ps, kvs)])
        o = o / l
        for j in range(ATT_CHUNKS_PER_DOT):
            outs.append(jnp.where(lane < HEAD_DIM, o[2 * j * tq:(2 * j + 1) * tq],
                                  o[(2 * j + 1) * tq:(2 * j + 2) * tq]))
    o_ref[0] = jnp.concatenate(outs, axis=1).astype(BF16)


def _attention(q, kvs, tq):
    b, lq, _ = q.shape
    in_specs = [pl.BlockSpec((1, tq, 512), lambda i, j: (i, j, 0))]
    for kv in kvs:
        in_specs.append(pl.BlockSpec((1, kv.shape[1], 256), lambda i, j: (i, 0, 0)))
    return pl.pallas_call(
        functools.partial(_attn_kernel, nkv=len(kvs), tq=tq),
        grid=(b, lq // tq), in_specs=in_specs,
        out_specs=pl.BlockSpec((1, tq, 512), lambda i, j: (i, j, 0)),
        out_shape=jax.ShapeDtypeStruct((b, lq, 512), BF16),
        compiler_params=_cparams(("parallel", "arbitrary")),
        name="attention",
    )(q, *kvs)


def _head_masks():
    lane = _iota((1, MIX_W), 1)
    return [(lane // HEAD_DIM == h).astype(F32) for h in range(GLA_HEADS)]


def _blockdiag_mask():
    r = _iota((MIX_W, MIX_W), 0) // HEAD_DIM
    c = _iota((MIX_W, MIX_W), 1) // HEAD_DIM
    return r == c


def _finish_rows(o, gate, gain, seg):
    ss = _seg_sum64(o * o, seg)
    return o * lax.rsqrt(ss * (1.0 / HEAD_DIM) + EPS) * gain * _silu(gate)


PAIR_W = 2 * HEAD_DIM
PREP_CHUNKS = 4
SCAN_UNROLL = 2


def _gla_kernel(*refs, need_ctx):
    if need_ctx:
        (xc_ref, xl_ref, sc_ref, sl_ref, wg_ref, bg_ref, gain_ref, seg_ref,
         oc_ref, ol_ref, of_ref, ob_ref, qh_ref, kh_ref, dg_ref, st_ref) = refs
    else:
        (xc_ref, xl_ref, sc_ref, sl_ref, wg_ref, bg_ref, gain_ref, seg_ref,
         ol_ref, of_ref, ob_ref, qh_ref, kh_ref, dg_ref, st_ref) = refs
        oc_ref = None
    lc = xc_ref.shape[1]
    x_refs = (xc_ref, xl_ref)
    s_refs = (sc_ref, sl_ref)
    row_off = (0, lc)
    o_refs = (of_ref, ob_ref)
    hmask = _head_masks()
    bd = _blockdiag_mask()
    ti = _iota((CHUNK, CHUNK), 0)
    si = _iota((CHUNK, CHUNK), 1)
    tri = [(si <= ti).astype(BF16), (si >= ti).astype(BF16)]
    rr = _iota((4 * CHUNK, CHUNK), 0)
    cc = _iota((4 * CHUNK, CHUNK), 1)
    t_of_row = (rr // (GLA_HEADS * SUB)) * SUB + rr % SUB
    causal = [cc <= t_of_row, cc >= t_of_row]
    nblk = CHUNK // SUB

    def prep(seg_i, c2):
        x_ref, s_ref = x_refs[seg_i], s_refs[seg_i]
        fs = []
        for kk in range(PREP_CHUNKS):
            r0 = pl.multiple_of((c2 * PREP_CHUNKS + kk) * CHUNK, CHUNK)
            fs.append(dict(row=row_off[seg_i] + r0,
                           q=x_ref[0, pl.ds(r0, CHUNK), 0:MIX_W] * (HEAD_DIM ** -0.5),
                           k=x_ref[0, pl.ds(r0, CHUNK), MIX_W:2 * MIX_W],
                           v=x_ref[0, pl.ds(r0, CHUNK), 2 * MIX_W:3 * MIX_W],
                           sm=s_ref[0, pl.ds(r0, CHUNK), :]))
        chains = [(f, d) for f in fs for d in range(2)]
        xg = [_dot3(f["sm"], wg_ref[d]) + bg_ref[d] for f, d in chains]
        g = [(jnp.minimum(x_, 0.0) - jnp.log(1.0 + jnp.exp(-jnp.abs(x_)))) * (1.0 / GLA_TAU) for x_ in xg]
        b = [_dot_whl(tri[d], g_) for (_, d), g_ in zip(chains, g)]
        pieces = []
        for i in range(nblk):
            row_pieces = []
            for (f, d), g_, b_ in zip(chains, g, b):
                e = SUB * i if d == 0 else SUB * i + SUB - 1
                bref = b_[e:e + 1, :] - g_[e:e + 1, :]
                kt = f["k"] * jnp.exp(jnp.minimum(bref - b_, EXP_CLAMP))
                qt = f["q"][SUB * i:SUB * (i + 1), :] * jnp.exp(b_[SUB * i:SUB * (i + 1), :] - bref)
                qs = jnp.concatenate([qt * hmask[h] for h in range(GLA_HEADS)], axis=0)
                row_pieces.append(_dot_nt(qs, kt))
            pieces.append(row_pieces)
        scores = [jnp.where(causal[d], jnp.concatenate([pieces[i][j] for i in range(nblk)], axis=0), 0.0)
                  for j, (_, d) in enumerate(chains)]
        r = [_dot(s_, f["v"]) for (f, _), s_ in zip(chains, scores)]
        for j, (f, d) in enumerate(chains):
            intra = []
            for i in range(nblk):
                acc = None
                for h in range(GLA_HEADS):
                    lo = (i * GLA_HEADS + h) * SUB
                    term = r[j][lo:lo + SUB, :] * hmask[h]
                    acc = term if acc is None else acc + term
                intra.append(acc)
            row = f["row"]
            e = CHUNK - 1 if d == 0 else 0
            b_end = b[j][e:e + 1, :]
            o_refs[d][pl.ds(row, CHUNK), :] = jnp.concatenate(intra, axis=0)
            qh_ref[d, pl.ds(row, CHUNK), :] = (f["q"] * jnp.exp(b[j])).astype(BF16)
            kh_ref[d, pl.ds(row, CHUNK), :] = (f["k"] * jnp.exp(b_end - b[j])).astype(BF16)
            dg_ref[d, pl.ds(pl.multiple_of(row // 8, 8), 8), :] = jnp.broadcast_to(jnp.exp(b_end), (8, MIX_W))

    for seg_i, x_ref in enumerate(x_refs):
        def prep_body(c, carry, seg_i=seg_i):
            prep(seg_i, c)
            return carry

        lax.fori_loop(0, x_ref.shape[1] // (CHUNK * PREP_CHUNKS), prep_body, 0)

    st_ref[...] = jnp.zeros(st_ref.shape, F32)

    def scan_body(seg_i, i0, nch):
        x_ref = x_refs[seg_i]
        steps = [i0 * SCAN_UNROLL + k for k in range(SCAN_UNROLL)]
        r0s = [[pl.multiple_of((i if d == 0 else nch - 1 - i) * CHUNK, CHUNK) for d in range(2)] for i in steps]
        rows = [[row_off[seg_i] + r0 for r0 in r] for r in r0s]
        upd = [[jnp.dot(x_ref[0, pl.ds(r0s[k][d], CHUNK), 2 * MIX_W:3 * MIX_W].T.astype(BF16),
                        kh_ref[d, pl.ds(rows[k][d], CHUNK), :], preferred_element_type=F32) for d in range(2)]
               for k in range(SCAN_UNROLL)]
        st = [st_ref[d] for d in range(2)]
        for k in range(SCAN_UNROLL):
            inter = [_dot_nt(qh_ref[d, pl.ds(rows[k][d], CHUNK), :], st[d]) for d in range(2)]
            for d in range(2):
                o_refs[d][pl.ds(rows[k][d], CHUNK), :] += inter[d]
                dgr = dg_ref[d, pl.ds(pl.multiple_of(rows[k][d] // 8, 8), 8), :][0:1, :]
                st[d] = st[d] * dgr + jnp.where(bd, upd[k][d], 0.0)
        for d in range(2):
            st_ref[d] = st[d]

    for seg_i, x_ref in enumerate(x_refs):
        nch = x_ref.shape[1] // CHUNK

        def scan_iter(i, carry, seg_i=seg_i, nch=nch):
            scan_body(seg_i, i, nch)
            return carry

        lax.fori_loop(0, nch // SCAN_UNROLL, scan_iter, 0)

    gain = gain_ref[...]
    seg = seg_ref[...]
    blk = 256
    outs = ((oc_ref, xc_ref, 0), (ol_ref, xl_ref, lc))
    for o_ref, x_ref, off in outs:
        if o_ref is None:
            continue

        def fin(i, carry, o_ref=o_ref, x_ref=x_ref, off=off):
            r0 = pl.multiple_of(i * blk, blk)
            o = of_ref[pl.ds(off + r0, blk), :] + ob_ref[pl.ds(off + r0, blk), :]
            gate = x_ref[0, pl.ds(r0, blk), 3 * MIX_W:4 * MIX_W]
            o_ref[0, pl.ds(r0, blk), :] = _finish_rows(o, gate, gain, seg).astype(BF16)
            return carry

        lax.fori_loop(0, x_ref.shape[1] // blk, fin, 0)


def _gla(x_ctx, x_lat, s_ctx, s_lat, wg, bg, gain, seg, need_ctx):
    b, lc, _ = x_ctx.shape
    ll = x_lat.shape[1]
    in_specs = [pl.BlockSpec((1, lc, 1024), lambda i: (i, 0, 0)),
                pl.BlockSpec((1, ll, 1024), lambda i: (i, 0, 0)),
                pl.BlockSpec((1, lc, LANES), lambda i: (i, 0, 0)),
                pl.BlockSpec((1, ll, LANES), lambda i: (i, 0, 0)),
                pl.BlockSpec((2, LANES, MIX_W), lambda i: (0, 0, 0)),
                pl.BlockSpec((2, 1, MIX_W), lambda i: (0, 0, 0)),
                pl.BlockSpec((1, MIX_W), lambda i: (0, 0)),
                pl.BlockSpec((LANES, LANES), lambda i: (0, 0))]
    out_shape = [jax.ShapeDtypeStruct((b, ll, MIX_W), BF16)]
    out_specs = [pl.BlockSpec((1, ll, MIX_W), lambda i: (i, 0, 0))]
    if need_ctx:
        out_shape.insert(0, jax.ShapeDtypeStruct((b, lc, MIX_W), BF16))
        out_specs.insert(0, pl.BlockSpec((1, lc, MIX_W), lambda i: (i, 0, 0)))
    res = pl.pallas_call(
        functools.partial(_gla_kernel, need_ctx=need_ctx),
        grid=(b,), in_specs=in_specs, out_specs=out_specs, out_shape=out_shape,
        scratch_shapes=[pltpu.VMEM((lc + ll, MIX_W), F32), pltpu.VMEM((lc + ll, MIX_W), F32),
                        pltpu.VMEM((2, lc + ll, MIX_W), BF16), pltpu.VMEM((2, lc + ll, MIX_W), BF16),
                        pltpu.VMEM((2, (lc + ll) // 8, MIX_W), F32),
                        pltpu.VMEM((2, MIX_W, MIX_W), F32)],
        compiler_params=_cparams(("parallel",)),
        name="gla_scan",
    )(x_ctx, x_lat, s_ctx, s_lat, wg, bg, gain, seg)
    return (res[0], res[1]) if need_ctx else (None, res[0])


def _gdn2_kernel(*refs, need_ctx):
    if need_ctx:
        (xc_ref, xl_ref, sc_ref, sl_ref, cw_ref, alog_ref, dtb_ref, sel_ref, selb_ref, gain_ref, seg_ref,
         oc_ref, ol_ref, sk_ref, p_ref, qe_ref, sv_ref, kh_ref, dg_ref, of_ref, ob_ref, st_ref) = refs
    else:
        (xc_ref, xl_ref, sc_ref, sl_ref, cw_ref, alog_ref, dtb_ref, sel_ref, selb_ref, gain_ref, seg_ref,
         ol_ref, sk_ref, p_ref, qe_ref, sv_ref, kh_ref, dg_ref, of_ref, ob_ref, st_ref) = refs
        oc_ref = None
    lc = xc_ref.shape[1]
    x_refs = (xc_ref, xl_ref)
    s_refs = (sc_ref, sl_ref)
    row_off = (0, lc)
    npair = MIX_W // PAIR_W
    seg = seg_ref[...]
    ti = _iota((CHUNK, CHUNK), 0)
    si = _iota((CHUNK, CHUNK), 1)
    tri = [(si <= ti).astype(BF16), (si >= ti).astype(BF16)]
    ones = jnp.ones((CHUNK, CHUNK), BF16)
    tp = _iota((CHUNK, PAIR_W), 0)
    sp = _iota((CHUNK, PAIR_W), 1) % CHUNK
    le = sp <= tp
    ge = sp >= tp
    m_tri = [le, ge]
    m_strict = [sp < tp, sp > tp]
    m_sum = [ge.astype(F32).astype(BF16), le.astype(F32).astype(BF16)]
    blk16 = (tp // SUB) == (sp // SUB)
    eye = (tp == sp).astype(F32)
    bd2 = (_iota((PAIR_W, PAIR_W), 0) // HEAD_DIM) == (_iota((PAIR_W, PAIR_W), 1) // HEAD_DIM)
    lane_p = _iota((1, PAIR_W), 1)
    hm2 = [(lane_p // HEAD_DIM == h).astype(F32) for h in range(2)]
    cw = cw_ref[...]

    def pk2(y):
        yb = y.astype(BF16)
        return jnp.where(bd2, jnp.concatenate([yb, yb], axis=0), jnp.zeros((), BF16))

    def mm(x, y):
        return jnp.dot(x.astype(BF16), pk2(y), preferred_element_type=F32)

    def front(seg_i, c):
        x_ref, s_ref = x_refs[seg_i], s_refs[seg_i]
        ln = x_ref.shape[1]
        nch = ln // CHUNK
        r0 = pl.multiple_of(c * CHUNK, CHUNK)
        center = x_ref[0, pl.ds(r0, CHUNK), 0:3 * MIX_W]
        p0 = pl.multiple_of(jnp.maximum(r0 - 8, 0), 8)
        n0 = pl.multiple_of(jnp.minimum(r0 + CHUNK, ln - 8), 8)
        prev = x_ref[0, pl.ds(p0, 8), 0:3 * MIX_W] * jnp.where(c > 0, 1.0, 0.0)
        nxt = x_ref[0, pl.ds(n0, 8), 0:3 * MIX_W] * jnp.where(c < nch - 1, 1.0, 0.0)
        ext = jnp.concatenate([prev, center, nxt], axis=0)
        pad = SHORT_CONV // 2
        acc = None
        for j in range(SHORT_CONV):
            term = ext[8 - pad + j:8 - pad + j + CHUNK, :] * cw[j:j + 1, :]
            acc = term if acc is None else acc + term
        y = _silu(acc)
        qk = y[:, 0:2 * MIX_W]
        qk = qk * lax.rsqrt(_seg_sum64(qk * qk, seg, split=False) + EPS)
        f = dict(row=row_off[seg_i] + r0, qn=qk[:, 0:MIX_W] * (HEAD_DIM ** -0.5), kn=qk[:, MIX_W:2 * MIX_W],
                 v=y[:, 2 * MIX_W:3 * MIX_W])
        sm = s_ref[0, pl.ds(r0, CHUNK), :]
        beta = _sigmoid(sm)
        f["kk"], f["qk"] = [], []
        for p in range(npair):
            ls = slice(PAIR_W * p, PAIR_W * (p + 1))
            kstack = jnp.concatenate([f["kn"][:, ls] * hm2[0], f["kn"][:, ls] * hm2[1]], axis=0)
            f["kk"].append(_dot_nt(f["kn"][:, ls], kstack))
            f["qk"].append(_dot_nt(f["qn"][:, ls], kstack))
        gexp = [_dot_xhl(-jnp.exp(alog_ref[d]) * _softplus(sm + dtb_ref[d]), sel_ref[d]) for d in range(2)]
        f["ghl"] = [_split(g_) for g_ in gexp]
        f["bexp"] = [_dot(beta, selb_ref[d]) for d in range(2)]
        f["gam"] = [jnp.dot(tri[d], f["ghl"][d][0], preferred_element_type=F32)
                    + jnp.dot(tri[d], f["ghl"][d][1], preferred_element_type=F32) for d in range(2)]
        return f

    def prep(seg_i, c2):
        fs = [front(seg_i, c2 * PREP_CHUNKS + k) for k in range(PREP_CHUNKS)]
        chains = [(f, d, p) for f in fs for d in range(2) for p in range(npair)]
        lss = [slice(PAIR_W * p, PAIR_W * (p + 1)) for _, _, p in chains]
        gam_t = [f["gam"][d][:, ls] for (f, d, _), ls in zip(chains, lss)]
        gam_s = [jnp.dot(ones, f["ghl"][d][0][:, ls] * m_sum[d], preferred_element_type=F32)
                 + jnp.dot(ones, f["ghl"][d][1][:, ls] * m_sum[d], preferred_element_type=F32)
                 for (f, d, _), ls in zip(chains, lss)]
        bx = [f["bexp"][d][:, ls] for (f, d, _), ls in zip(chains, lss)]
        dec = [jnp.where(m_tri[d], jnp.exp(jnp.minimum(gt - gs, 0.0)), 0.0)
               for (_, d, _), gt, gs in zip(chains, gam_t, gam_s)]
        a = [jnp.where(m_strict[d], b_ * dc * f["kk"][p], 0.0) for (f, d, p), b_, dc in zip(chains, bx, dec)]
        dgn = [jnp.where(blk16, a_, 0.0) for a_ in a]
        lo = [a_ - g_ for a_, g_ in zip(a, dgn)]
        d2 = [mm(g_, g_) for g_ in dgn]
        t1 = [mm(eye - g_, eye + s_) for g_, s_ in zip(dgn, d2)]
        d4 = [mm(s_, s_) for s_ in d2]
        t2 = [mm(t_, eye + s_) for t_, s_ in zip(t1, d4)]
        d8 = [mm(s_, s_) for s_ in d4]
        t_inv = [mm(t_, eye + s_) for t_, s_ in zip(t2, d8)]
        m = [mm(t_, l_) for t_, l_ in zip(t_inv, lo)]
        m2 = [mm(m_, m_) for m_ in m]
        w1 = [mm(eye - m_, eye + s_) for m_, s_ in zip(m, m2)]
        w = [mm(w_, t_) for w_, t_ in zip(w1, t_inv)]
        egam = [jnp.exp(gt) for gt in gam_t]
        solv = [mm(w_, b_ * f["v"][:, ls]) for (f, _, _), w_, b_, ls in zip(chains, w, bx, lss)]
        solk = [mm(w_, b_ * eg * f["kn"][:, ls]) for (f, _, _), w_, b_, eg, ls in zip(chains, w, bx, egam, lss)]
        for i, (f, d, p) in enumerate(chains):
            ls, row = lss[i], f["row"]
            e = CHUNK - 1 if d == 0 else 0
            g_end = gam_t[i][e:e + 1, :]
            sv_ref[d, pl.ds(row, CHUNK), ls] = solv[i]
            sk_ref[d, pl.ds(row, CHUNK), ls] = solk[i].astype(BF16)
            p_ref[d, pl.ds(row, CHUNK), ls] = jnp.where(m_tri[d], f["qk"][p] * dec[i], 0.0).astype(BF16)
            qe_ref[d, pl.ds(row, CHUNK), ls] = (egam[i] * f["qn"][:, ls]).astype(BF16)
            kh_ref[d, pl.ds(row, CHUNK), ls] = f["kn"][:, ls] * jnp.exp(g_end - gam_t[i])
            dg_ref[d, pl.ds(pl.multiple_of(row // 8, 8), 8), ls] = jnp.broadcast_to(jnp.exp(g_end), (8, PAIR_W))

    for seg_i, x_ref in enumerate(x_refs):
        def prep_body(c, carry, seg_i=seg_i):
            prep(seg_i, c)
            return carry

        lax.fori_loop(0, x_ref.shape[1] // (CHUNK * PREP_CHUNKS), prep_body, 0)

    st_ref[...] = jnp.zeros(st_ref.shape, F32)

    def scan_body(seg_i, i0, nch):
        chains = [(d, p) for d in range(2) for p in range(npair)]
        lss = [slice(PAIR_W * p, PAIR_W * (p + 1)) for _, p in chains]
        st = [st_ref[d, p] for d, p in chains]
        pending = []
        for k in range(SCAN_UNROLL):
            i = i0 * SCAN_UNROLL + k
            rows = [row_off[seg_i] + pl.multiple_of((i if d == 0 else nch - 1 - i) * CHUNK, CHUNK)
                    for d, _ in chains]
            stb = [s_.astype(BF16) for s_ in st]
            u = [sv_ref[d, pl.ds(r, CHUNK), ls]
                 - jnp.dot(sk_ref[d, pl.ds(r, CHUNK), ls], sb, preferred_element_type=F32)
                 for (d, _), r, ls, sb in zip(chains, rows, lss, stb)]
            ku = [_dot(kh_ref[d, pl.ds(r, CHUNK), ls].T, u_) for (d, _), r, ls, u_ in zip(chains, rows, lss, u)]
            dgr = [dg_ref[d, pl.ds(pl.multiple_of(r // 8, 8), 8), ls][0:1, :]
                   for (d, _), r, ls in zip(chains, rows, lss)]
            st = [s_ * g_ + jnp.where(bd2, k_, 0.0) for s_, g_, k_ in zip(st, dgr, ku)]
            pending.append((rows, stb, u))
        for rows, stb, u in pending:
            oq = [jnp.dot(qe_ref[d, pl.ds(r, CHUNK), ls], sb, preferred_element_type=F32)
                  for (d, _), r, ls, sb in zip(chains, rows, lss, stb)]
            ou = [jnp.dot(p_ref[d, pl.ds(r, CHUNK), ls], pk2(u_), preferred_element_type=F32)
                  for (d, _), r, ls, u_ in zip(chains, rows, lss, u)]
            for j, (d, _) in enumerate(chains):
                dst = of_ref if d == 0 else ob_ref
                dst[pl.ds(rows[j], CHUNK), lss[j]] = oq[j] + ou[j]
        for j, (d, p) in enumerate(chains):
            st_ref[d, p] = st[j]

    for seg_i, x_ref in enumerate(x_refs):
        nch = x_ref.shape[1] // CHUNK

        def scan_iter(i, carry, seg_i=seg_i, nch=nch):
            scan_body(seg_i, i, nch)
            return carry

        lax.fori_loop(0, nch // SCAN_UNROLL, scan_iter, 0)

    gain = gain_ref[...]
    blk = 256
    outs = ((oc_ref, xc_ref, 0), (ol_ref, xl_ref, lc))
    for o_ref, x_ref, off in outs:
        if o_ref is None:
            continue

        def fin(i, carry, o_ref=o_ref, x_ref=x_ref, off=off):
            r0 = pl.multiple_of(i * blk, blk)
            o = of_ref[pl.ds(off + r0, blk), :] + ob_ref[pl.ds(off + r0, blk), :]
            gate = x_ref[0, pl.ds(r0, blk), 3 * MIX_W:4 * MIX_W]
            o_ref[0, pl.ds(r0, blk), :] = _finish_rows(o, gate, gain, seg).astype(BF16)
            return carry

        lax.fori_loop(0, x_ref.shape[1] // blk, fin, 0)


def _gdn2(x_ctx, x_lat, s_ctx, s_lat, cw, alog, dtb, sel, selb, gain, seg, need_ctx):
    b, lc, _ = x_ctx.shape
    ll = x_lat.shape[1]
    lt = lc + ll
    in_specs = [pl.BlockSpec((1, lc, 1024), lambda i: (i, 0, 0)),
                pl.BlockSpec((1, ll, 1024), lambda i: (i, 0, 0)),
                pl.BlockSpec((1, lc, LANES), lambda i: (i, 0, 0)),
                pl.BlockSpec((1, ll, LANES), lambda i: (i, 0, 0)),
                pl.BlockSpec((8, 3 * MIX_W), lambda i: (0, 0)),
                pl.BlockSpec((2, 1, LANES), lambda i: (0, 0, 0)),
                pl.BlockSpec((2, 1, LANES), lambda i: (0, 0, 0)),
                pl.BlockSpec((2, LANES, MIX_W), lambda i: (0, 0, 0)),
                pl.BlockSpec((2, LANES, MIX_W), lambda i: (0, 0, 0)),
                pl.BlockSpec((1, MIX_W), lambda i: (0, 0)),
                pl.BlockSpec((LANES, LANES), lambda i: (0, 0))]
    out_shape = [jax.ShapeDtypeStruct((b, ll, MIX_W), BF16)]
    out_specs = [pl.BlockSpec((1, ll, MIX_W), lambda i: (i, 0, 0))]
    if need_ctx:
        out_shape.insert(0, jax.ShapeDtypeStruct((b, lc, MIX_W), BF16))
        out_specs.insert(0, pl.BlockSpec((1, lc, MIX_W), lambda i: (i, 0, 0)))
    res = pl.pallas_call(
        functools.partial(_gdn2_kernel, need_ctx=need_ctx),
        grid=(b,), in_specs=in_specs, out_specs=out_specs, out_shape=out_shape,
        scratch_shapes=[pltpu.VMEM((2, lt, MIX_W), BF16), pltpu.VMEM((2, lt, MIX_W), BF16),
                        pltpu.VMEM((2, lt, MIX_W), BF16), pltpu.VMEM((2, lt, MIX_W), F32),
                        pltpu.VMEM((2, lt, MIX_W), F32), pltpu.VMEM((2, lt // 8, MIX_W), F32),
                        pltpu.VMEM((lt, MIX_W), F32), pltpu.VMEM((lt, MIX_W), F32),
                        pltpu.VMEM((2, MIX_W // PAIR_W, PAIR_W, PAIR_W), F32)],
        compiler_params=_cparams(("parallel",)),
        name="gdn_scan",
    )(x_ctx, x_lat, s_ctx, s_lat, cw, alog, dtb, sel, selb, gain, seg)
    return (res[0], res[1]) if need_ctx else (None, res[0])


def _mix_residual(gla_ref, gdn_ref, att_ref, h_ref, g1_ref, wo_ref):
    y = (jnp.dot(gla_ref[...], wo_ref[0:256, :], preferred_element_type=F32)
         + jnp.dot(gdn_ref[...], wo_ref[256:512, :], preferred_element_type=F32)
         + jnp.dot(att_ref[...], wo_ref[512:1024, :], preferred_element_type=F32))
    return h_ref[...] + g1_ref[0] * y


def _mix_specs(tm, d, index):
    return [pl.BlockSpec((tm, 256), lambda *g: (index(*g), 0)),
            pl.BlockSpec((tm, 256), lambda *g: (index(*g), 0)),
            pl.BlockSpec((tm, 512), lambda *g: (index(*g), 0)),
            pl.BlockSpec((d, d), lambda *g: (0, 0))]


def _norm_mod(x, sh_ref, sc_ref, gain_ref):
    ms = jnp.mean(x * x, axis=-1, keepdims=True)
    return x * lax.rsqrt(ms + EPS) * (gain_ref[...] * (1.0 + sc_ref[0])) + sh_ref[0]


def _swiglu_part(x, wg, wu, wd):
    gg = jnp.dot(x, wg[...], preferred_element_type=F32)
    uu = jnp.dot(x, wu[...], preferred_element_type=F32)
    hid = (_silu(gg) * uu).astype(BF16)
    return jnp.dot(hid, wd[...], preferred_element_type=F32)


def _ffn_kernel(gla_ref, gdn_ref, att_ref, wo_ref, h_ref, g1_ref, sh_ref, sc_ref, g2_ref, gain_ref,
                wg_ref, wu_ref, wd_ref, o_ref, h1_scr, b_scr, acc_scr, *, nf):
    f = pl.program_id(1)

    @pl.when(f == 0)
    def _():
        h1 = _mix_residual(gla_ref, gdn_ref, att_ref, h_ref, g1_ref, wo_ref)
        h1_scr[...] = h1
        b_scr[...] = _norm_mod(h1, sh_ref, sc_ref, gain_ref).astype(BF16)
        acc_scr[...] = jnp.zeros(acc_scr.shape, F32)

    b = b_scr[...]
    acc_scr[...] += _swiglu_part(b, wg_ref, wu_ref, wd_ref)

    @pl.when(f == nf - 1)
    def _():
        o_ref[...] = h1_scr[...] + g2_ref[0] * acc_scr[...]


def _ffn(mix, w_o, h2d, mod144, mod_row_fn, gain, w_gu, w_down, seq_len, tm, tf):
    rows, d = h2d.shape
    dff = w_down.shape[0]
    nf = dff // tf
    tiles_per_seq = seq_len // tm

    def mod_spec(k):
        return pl.BlockSpec((1, 1, d), lambda i, f: (mod_row_fn(i // tiles_per_seq) * 6 + k, 0, 0))

    return pl.pallas_call(
        functools.partial(_ffn_kernel, nf=nf),
        grid=(rows // tm, nf),
        in_specs=_mix_specs(tm, d, lambda i, f: i) + [
            pl.BlockSpec((tm, d), lambda i, f: (i, 0)),
            mod_spec(2), mod_spec(3), mod_spec(4), mod_spec(5),
            pl.BlockSpec((1, d), lambda i, f: (0, 0)),
            pl.BlockSpec((d, tf), lambda i, f: (0, f)),
            pl.BlockSpec((d, tf), lambda i, f: (0, nf + f)),
            pl.BlockSpec((tf, d), lambda i, f: (f, 0))],
        out_specs=pl.BlockSpec((tm, d), lambda i, f: (i, 0)),
        out_shape=jax.ShapeDtypeStruct((rows, d), F32),
        scratch_shapes=[pltpu.VMEM((tm, d), F32), pltpu.VMEM((tm, d), BF16), pltpu.VMEM((tm, d), F32)],
        compiler_params=_cparams(("parallel", "arbitrary")),
        name="ffn",
    )(*mix, w_o, h2d, mod144, mod144, mod144, mod144, gain, w_gu, w_gu, w_down)


MOE_TILE = 512
COMBINE_TOKENS = 512


def _router_kernel(gla_ref, gdn_ref, att_ref, wo_ref, h_ref, g1_ref, sh_ref, sc_ref, gain_ref, wr_ref, br_ref,
                   h1_ref, b_ref, route_ref):
    lane = _iota((1, LANES), 1)
    lane_f = lane.astype(F32)
    h1 = _mix_residual(gla_ref, gdn_ref, att_ref, h_ref, g1_ref, wo_ref)
    h1_ref[...] = h1
    b = _norm_mod(h1, sh_ref, sc_ref, gain_ref)
    b_ref[...] = b
    logits = _dot3(b, wr_ref[...]) + br_ref[...]
    logits = jnp.where(lane < N_EXPERTS, logits, -jnp.inf)
    m1 = jnp.max(logits, axis=-1, keepdims=True)
    i1 = jnp.min(jnp.where(logits == m1, lane_f, float(LANES)), axis=-1, keepdims=True)
    rest = jnp.where(lane_f == i1, -jnp.inf, logits)
    m2 = jnp.max(rest, axis=-1, keepdims=True)
    i2 = jnp.min(jnp.where(rest == m2, lane_f, float(LANES)), axis=-1, keepdims=True)
    t = jnp.exp(m2 - m1)
    w1 = 1.0 / (1.0 + t)
    route_ref[...] = (jnp.where(lane == 0, i1, 0.0) + jnp.where(lane == 1, i2, 0.0)
                      + jnp.where(lane == 2, w1, 0.0) + jnp.where(lane == 3, t * w1, 0.0))


def _router(mix, w_o, h2d, mod144, mod_row_fn, gain, w_router, b_router, seq_len, tm):
    rows, d = h2d.shape
    tiles_per_seq = seq_len // tm

    def mod_spec(k):
        return pl.BlockSpec((1, 1, d), lambda i: (mod_row_fn(i // tiles_per_seq) * 6 + k, 0, 0))

    row_spec = pl.BlockSpec((tm, d), lambda i: (i, 0))
    return pl.pallas_call(
        _router_kernel,
        grid=(rows // tm,),
        in_specs=_mix_specs(tm, d, lambda i: i) + [
            row_spec, mod_spec(2), mod_spec(3), mod_spec(4),
            pl.BlockSpec((1, d), lambda i: (0, 0)),
            pl.BlockSpec((d, LANES), lambda i: (0, 0)),
            pl.BlockSpec((1, LANES), lambda i: (0, 0))],
        out_specs=(row_spec, row_spec, pl.BlockSpec((tm, LANES), lambda i: (i, 0))),
        out_shape=(jax.ShapeDtypeStruct((rows, d), F32), jax.ShapeDtypeStruct((rows, d), F32),
                   jax.ShapeDtypeStruct((rows, LANES), F32)),
        compiler_params=_cparams(("parallel",)),
        name="moe_router",
    )(*mix, w_o, h2d, mod144, mod144, mod144, gain, w_router, b_router)


DISPATCH_TOKENS = 1024


def _moe_dispatch_kernel(pos_ref, b_ref, xs_in_ref, xs_ref, sem):
    del xs_in_ref
    n = b_ref.shape[0]
    for r in range(2 * n):
        pltpu.make_async_copy(b_ref.at[pl.ds(r % n, 1)], xs_ref.at[pl.ds(pos_ref[0, 0, r], 1)],
                              sem).start(priority=r % 2)
    for _ in range(2):
        pltpu.make_async_copy(b_ref, xs_ref.at[pl.ds(0, n)], sem).wait()


def _moe_dispatch(pos2, b, p_rows):
    rows, d = b.shape
    tm = DISPATCH_TOKENS
    return pl.pallas_call(
        _moe_dispatch_kernel,
        grid=(rows // tm,),
        in_specs=[pl.BlockSpec((1, 1, 2 * tm), lambda i: (i, 0, 0), memory_space=pltpu.SMEM),
                  pl.BlockSpec((tm, d), lambda i: (i, 0)),
                  pl.BlockSpec(memory_space=pl.ANY)],
        out_specs=pl.BlockSpec(memory_space=pl.ANY),
        out_shape=jax.ShapeDtypeStruct((p_rows, d), F32),
        scratch_shapes=[pltpu.SemaphoreType.DMA(())],
        input_output_aliases={2: 0},
        compiler_params=_cparams(("arbitrary",)),
        name="moe_dispatch",
    )(pos2, b, jnp.zeros((p_rows, d), F32))


def _moe_group_kernel(te_ref, nt_ref, xs_ref, wg_ref, wu_ref, wd_ref, ys_ref, xb_scr, acc_scr, *, nf):
    i = pl.program_id(0)
    f = pl.program_id(1)
    used = i < nt_ref[0]

    @pl.when(used & (f == 0))
    def _():
        xb_scr[...] = xs_ref[...].astype(BF16)

    @pl.when(used)
    def _():
        x = xb_scr[...]
        part = _swiglu_part(x, wg_ref.at[0], wu_ref.at[0], wd_ref.at[0])

        @pl.when(f == 0)
        def _():
            acc_scr[...] = part

        @pl.when(f > 0)
        def _():
            acc_scr[...] += part

    @pl.when(used & (f == nf - 1))
    def _():
        ys_ref[...] = acc_scr[...]

    @pl.when(jnp.logical_not(used) & (f == nf - 1))
    def _():
        ys_ref[...] = jnp.zeros(ys_ref.shape, F32)


def _moe_group(tile_expert, n_tiles_used, xs, w_gu, w_down, tf):
    p, d = xs.shape
    dff = w_down.shape[1]
    nf = dff // tf
    grid_spec = pltpu.PrefetchScalarGridSpec(
        num_scalar_prefetch=2,
        grid=(p // MOE_TILE, nf),
        in_specs=[pl.BlockSpec((MOE_TILE, d), lambda i, f, te, nt: (i, 0)),
                  pl.BlockSpec((1, d, tf), lambda i, f, te, nt: (te[i], 0, f)),
                  pl.BlockSpec((1, d, tf), lambda i, f, te, nt: (te[i], 0, nf + f)),
                  pl.BlockSpec((1, tf, d), lambda i, f, te, nt: (te[i], f, 0))],
        out_specs=pl.BlockSpec((MOE_TILE, d), lambda i, f, te, nt: (i, 0)),
        scratch_shapes=[pltpu.VMEM((MOE_TILE, d), BF16), pltpu.VMEM((MOE_TILE, d), F32)])
    return pl.pallas_call(
        functools.partial(_moe_group_kernel, nf=nf),
        grid_spec=grid_spec,
        out_shape=jax.ShapeDtypeStruct((p, d), F32),
        compiler_params=_cparams(("arbitrary", "arbitrary")),
        name="moe_experts",
    )(tile_expert, n_tiles_used, xs, w_gu, w_gu, w_down)


def _moe_combine_kernel(pos_ref, ys_ref, h_ref, route_ref, g_ref, o_ref, buf, sem):
    n = buf.shape[0]
    half = n // 2

    for r in range(n):
        pltpu.make_async_copy(ys_ref.at[pl.ds(pos_ref[0, 0, r], 1)], buf.at[pl.ds(r, 1)], sem).start(priority=r % 2)
    pltpu.make_async_copy(ys_ref.at[pl.ds(0, n)], buf, sem).wait()
    w1 = route_ref[:, 2:3]
    w2 = route_ref[:, 3:4]
    o_ref[...] = h_ref[...] + g_ref[0] * (w1 * buf[0:half, :] + w2 * buf[half:n, :])


def _moe_combine(pos, ys, h2d, route, mod144, mod_row_fn, seq_len):
    rows, d = h2d.shape
    tm = pos.shape[2] // 2
    tiles_per_seq = seq_len // tm
    steps = rows // tm
    return pl.pallas_call(
        _moe_combine_kernel,
        grid=(steps,),
        in_specs=[pl.BlockSpec((1, 1, 2 * tm), lambda i: (i, 0, 0), memory_space=pltpu.SMEM),
                  pl.BlockSpec(memory_space=pl.ANY),
                  pl.BlockSpec((tm, d), lambda i: (i, 0)),
                  pl.BlockSpec((tm, LANES), lambda i: (i, 0)),
                  pl.BlockSpec((1, 1, d), lambda i: (mod_row_fn(i // tiles_per_seq) * 6 + 5, 0, 0))],
        out_specs=pl.BlockSpec((tm, d), lambda i: (i, 0)),
        out_shape=jax.ShapeDtypeStruct((rows, d), F32),
        scratch_shapes=[pltpu.VMEM((2 * tm, d), F32), pltpu.SemaphoreType.DMA(())],
        compiler_params=_cparams(("arbitrary",)),
        name="moe_combine",
    )(pos, ys, h2d, route, mod144)


def _moe_routed(mix, w_o, h2d, mod144, mod_row_fn, gain, w_router, b_router, w_gu, w_down, seq_len):
    rows, d = h2d.shape
    h1, b, route = _router(mix, w_o, h2d, mod144, mod_row_fn, gain, w_router, b_router, seq_len,
                           _rows(ROW_TILE, seq_len))
    ex = jnp.concatenate([route[:, 0], route[:, 1]]).astype(jnp.int32)
    onehot = (ex[:, None] == jnp.arange(N_EXPERTS, dtype=jnp.int32)[None, :]).astype(jnp.int32)
    rank = jnp.sum((jnp.cumsum(onehot, axis=0) - onehot) * onehot, axis=1)
    counts = jnp.sum(onehot, axis=0)
    padded = ((counts + MOE_TILE - 1) // MOE_TILE) * MOE_TILE
    ends = jnp.cumsum(padded)
    starts = ends - padded
    pos = jnp.sum(onehot * starts[None, :], axis=1) + rank
    pos = pos.astype(jnp.int32)
    p_rows = 2 * rows + N_EXPERTS * MOE_TILE
    tile_first = jnp.arange(p_rows // MOE_TILE, dtype=jnp.int32) * MOE_TILE
    tile_expert = jnp.minimum(jnp.sum((tile_first[:, None] >= ends[None, :]).astype(jnp.int32), axis=1),
                              N_EXPERTS - 1).astype(jnp.int32)
    n_tiles_used = (ends[-1:] // MOE_TILE).astype(jnp.int32)

    def per_tile(tm):
        return jnp.concatenate([pos[:rows].reshape(rows // tm, 1, tm), pos[rows:].reshape(rows // tm, 1, tm)], axis=2)

    xs = _moe_dispatch(per_tile(DISPATCH_TOKENS), b, p_rows)
    ys = _moe_group(tile_expert, n_tiles_used, xs, w_gu, w_down, FF_TILE)
    return _moe_combine(per_tile(_rows(COMBINE_TOKENS, seq_len)), ys, h1, route, mod144, mod_row_fn, seq_len)


ATT_HEAD_ORDER = (0, 4, 1, 5, 2, 6, 3, 7)


def _layout_w_in(w):
    gla = w[:, 0:1024]
    glow = w[:, 1024:1056]
    gdn = w[:, 1056:2080]
    ab = w[:, 2080:2096]
    q = jnp.concatenate([w[:, 2096 + HEAD_DIM * h:2096 + HEAD_DIM * (h + 1)] for h in ATT_HEAD_ORDER], axis=1)
    kv = w[:, 2608:2864]
    pad = jnp.zeros((w.shape[0], LANES - 48), w.dtype)
    return jnp.concatenate([gla, gdn, q, kv, glow, ab, pad], axis=1).astype(BF16)


def _layout_w_out(w):
    att = [w[512 + HEAD_DIM * h:512 + HEAD_DIM * (h + 1)] for h in ATT_HEAD_ORDER]
    return jnp.concatenate([w[0:512]] + att, axis=0).astype(BF16)


def _rope_tables(seq_len):
    rows = seq_len // GRID_W
    row = jnp.repeat(jnp.arange(rows), GRID_W).astype(F32)
    col = jnp.tile(jnp.arange(GRID_W), rows).astype(F32)
    inv_freq = ROPE_THETA ** (-jnp.arange(0, HEAD_DIM // 2, 2, dtype=F32) / (HEAD_DIM // 2))
    ar = row[:, None] * inv_freq
    ac = col[:, None] * inv_freq
    cos = jnp.concatenate([jnp.cos(ar), jnp.cos(ar), jnp.cos(ac), jnp.cos(ac)], axis=-1)
    sin = jnp.concatenate([-jnp.sin(ar), jnp.sin(ar), -jnp.sin(ac), jnp.sin(ac)], axis=-1)
    return jnp.tile(cos, (1, 2)), jnp.tile(sin, (1, 2))


def _seg_matrix():
    i = np.arange(LANES)
    return jnp.asarray((i[:, None] // HEAD_DIM) == (i[None, :] // HEAD_DIM), dtype=BF16)


def _gdn_select():
    sel = np.zeros((2, LANES, MIX_W), np.float32)
    selb = np.zeros((2, LANES, MIX_W), np.float32)
    for d in range(2):
        for h in range(GDN_HEADS):
            sel[d, 32 + GDN_HEADS * d + h, HEAD_DIM * h:HEAD_DIM * (h + 1)] = 1.0
            selb[d, 40 + GDN_HEADS * d + h, HEAD_DIM * h:HEAD_DIM * (h + 1)] = 1.0
    return jnp.asarray(sel, BF16), jnp.asarray(selb, BF16)


def _lane_rows(vals, base):
    out = jnp.zeros((2, 1, LANES), F32)
    for d in range(2):
        out = out.at[d, 0, base + GDN_HEADS * d:base + GDN_HEADS * (d + 1)].set(vals[d].astype(F32))
    return out


def kernel(x, c, ctx, c_ctx, w_mod, b_mod, norm_mix, norm_ffn, w_in, gla_gate_up, gla_gate_bias, gla_out_gain,
           gdn_conv, gdn_a_log, gdn_dt_bias, gdn_out_gain, att_q_gain, att_k_gain, w_out, ffn_gate_up, ffn_down,
           moe_router, moe_router_bias, moe_gate_up, moe_down):
    bsz, seq, d = x.shape
    lctx = ctx.shape[1]
    depth = w_mod.shape[0]
    ctx_row = bsz

    mod_rows = ((bsz + 1 + 7) // 8) * 8
    cvec = jnp.concatenate([c, c_ctx[None, :], jnp.zeros((mod_rows - bsz - 1, d), F32)], axis=0)
    mods = _modulation(cvec, w_mod, b_mod)

    seg = _seg_matrix()
    tables = _rope_tables(seq)
    sel, selb = _gdn_select()
    lat_row = lambda b: b
    ctx_row_fn = lambda b: ctx_row

    h_lat = x.reshape(bsz * seq, d)
    h_ctx = ctx.reshape(bsz * lctx, d)
    for layer in range(depth):
        need_ctx = layer < depth - 1
        mod144 = mods[layer].reshape(mod_rows * 6, 1, d)
        w_p = _layout_w_in(w_in[layer])
        w_o = _layout_w_out(w_out[layer])
        hg = jnp.concatenate([jnp.tile(att_q_gain[layer], ATT_Q_HEADS) * (HEAD_DIM ** -0.5 * LOG2E),
                              jnp.tile(att_k_gain[layer], ATT_KV_HEADS)])[None, :].astype(F32)
        gain_mix = norm_mix[layer][None, :]
        gain_ffn = norm_ffn[layer][None, :]

        gla_l, gdn_l, q_l, kv_l, sm_l = _inproj(h_lat, mod144, lat_row, gain_mix, w_p, hg, seg, tables, seq,
                                                _rows(INPROJ_ROWS, seq))
        gla_c, gdn_c, q_c, kv_c, sm_c = _inproj(h_ctx, mod144, ctx_row_fn, gain_mix, w_p, hg, seg, None, lctx,
                                                _rows(INPROJ_ROWS, lctx))

        r3 = lambda t, n: t.reshape(bsz, n, t.shape[-1])
        wg = jnp.zeros((2, LANES, MIX_W), F32)
        for dd in range(2):
            wg = wg.at[dd, GLA_GATE_RANK * dd:GLA_GATE_RANK * (dd + 1), :].set(gla_gate_up[layer, dd].astype(F32))
        bg = gla_gate_bias[layer].reshape(2, 1, MIX_W).astype(F32)
        gla_gain = jnp.tile(gla_out_gain[layer], GLA_HEADS)[None, :].astype(F32)
        o_gla_c, o_gla_l = _gla(r3(gla_c, lctx), r3(gla_l, seq), r3(sm_c, lctx), r3(sm_l, seq),
                                wg, bg, gla_gain, seg, need_ctx)

        cw = jnp.concatenate([gdn_conv[layer].astype(F32), jnp.zeros((8 - SHORT_CONV, 3 * MIX_W), F32)], axis=0)
        alog = _lane_rows(gdn_a_log[layer], 32)
        dtb = _lane_rows(gdn_dt_bias[layer], 32)
        gdn_gain = jnp.tile(gdn_out_gain[layer], GDN_HEADS)[None, :].astype(F32)
        o_gdn_c, o_gdn_l = _gdn2(r3(gdn_c, lctx), r3(gdn_l, seq), r3(sm_c, lctx), r3(sm_l, seq),
                                cw, alog, dtb, sel, selb, gdn_gain, seg, need_ctx)

        o_att_l = _attention(r3(q_l, seq), [r3(kv_l, seq), r3(kv_c, lctx)], _rows(ATT_Q_ROWS, seq))
        mix_l = (o_gla_l.reshape(-1, MIX_W), o_gdn_l.reshape(-1, MIX_W), o_att_l.reshape(-1, 512))
        if need_ctx:
            o_att_c = _attention(r3(q_c, lctx), [r3(kv_c, lctx)], _rows(ATT_Q_ROWS, lctx))
            mix_c = (o_gla_c.reshape(-1, MIX_W), o_gdn_c.reshape(-1, MIX_W), o_att_c.reshape(-1, 512))

        j = layer // 2
        if layer % 2 == 0:
            w_gu = ffn_gate_up[j].astype(BF16)
            w_dn = ffn_down[j].astype(BF16)
            h_lat = _ffn(mix_l, w_o, h_lat, mod144, lat_row, gain_ffn, w_gu, w_dn, seq, _rows(ROW_TILE, seq), FF_TILE)
            if need_ctx:
                h_ctx = _ffn(mix_c, w_o, h_ctx, mod144, ctx_row_fn, gain_ffn, w_gu, w_dn, lctx,
                             _rows(ROW_TILE, lctx), FF_TILE)
        else:
            w_gu = moe_gate_up[j].astype(BF16)
            w_dn = moe_down[j].astype(BF16)
            w_r = jnp.concatenate([moe_router[j].astype(F32), jnp.zeros((d, LANES - N_EXPERTS), F32)], axis=1)
            b_r = jnp.concatenate([moe_router_bias[j].astype(F32), jnp.zeros((LANES - N_EXPERTS,), F32)])[None, :]
            h_lat = _moe_routed(mix_l, w_o, h_lat, mod144, lat_row, gain_ffn, w_r, b_r, w_gu, w_dn, seq)
            if need_ctx:
                h_ctx = _moe_routed(mix_c, w_o, h_ctx, mod144, ctx_row_fn, gain_ffn, w_r, b_r, w_gu, w_dn, lctx)
    return h_lat.reshape(bsz, seq, d)
```

```python
import functools

import numpy as np
import jax
import jax.numpy as jnp
from jax import lax
from jax.experimental import pallas as pl
from jax.experimental.pallas import tpu as pltpu

F32 = jnp.float32
BF16 = jnp.bfloat16

GRID_W = 64
HEAD_DIM = 64
CHUNK = 64
SUB = 16
EPS = 1e-6
GLA_HEADS = 4
GLA_GATE_RANK = 16
GLA_TAU = 16.0
GDN_HEADS = 4
SHORT_CONV = 5
ATT_Q_HEADS = 8
ATT_KV_HEADS = 2
ROPE_THETA = 10000.0
N_EXPERTS = 8
MIX_W = GLA_HEADS * HEAD_DIM
LOG2E = 1.4426950408889634
EXP_CLAMP = 80.0

LANES = 128
V7X_VMEM_BYTES = 64 * 1024 * 1024
VMEM_LIMIT = V7X_VMEM_BYTES - 8 * 1024 * 1024


INPROJ_ROWS = 512
ROW_TILE = 512
ATT_Q_ROWS = 256
FF_TILE = 1408


def _rows(limit, seq_len):
    return min(limit, seq_len)


def _cparams(sem):
    return pltpu.CompilerParams(dimension_semantics=sem, vmem_limit_bytes=VMEM_LIMIT)


def _silu(x):
    return x / (1.0 + jnp.exp(-x))


def _sigmoid(x):
    return 1.0 / (1.0 + jnp.exp(-x))


def _softplus(x):
    return jnp.maximum(x, 0.0) + jnp.log(1.0 + jnp.exp(-jnp.abs(x)))


def _dot(a, b):
    return jnp.dot(a.astype(BF16), b.astype(BF16), preferred_element_type=F32)


def _dot_nt(a, b):
    return lax.dot_general(a.astype(BF16), b.astype(BF16), (((1,), (1,)), ((), ())),
                           preferred_element_type=F32)


def _split(x):
    hi = x.astype(BF16)
    lo = (x - hi.astype(F32)).astype(BF16)
    return hi, lo


def _dot_xhl(x, w):
    hi, lo = _split(x)
    w = w.astype(BF16)
    return (jnp.dot(hi, w, preferred_element_type=F32) + jnp.dot(lo, w, preferred_element_type=F32))


def _dot_whl(w, x):
    hi, lo = _split(x)
    w = w.astype(BF16)
    return (jnp.dot(w, hi, preferred_element_type=F32) + jnp.dot(w, lo, preferred_element_type=F32))


def _dot3(a, b):
    ah, al = _split(a)
    bh, bl = _split(b)
    return (jnp.dot(ah, bh, preferred_element_type=F32) + jnp.dot(ah, bl, preferred_element_type=F32)
            + jnp.dot(al, bh, preferred_element_type=F32))


def _seg_sum64(sq, seg, split=True):
    outs = []
    for j in range(sq.shape[1] // LANES):
        part = sq[:, LANES * j:LANES * (j + 1)]
        outs.append(_dot_xhl(part, seg) if split else _dot(part, seg))
    return outs[0] if len(outs) == 1 else jnp.concatenate(outs, axis=1)


def _iota(shape, dim):
    return lax.broadcasted_iota(jnp.int32, shape, dim)


def _mod_kernel(c_ref, w_ref, b_ref, o_ref):
    s = _silu(c_ref[...])
    o_ref[0] = _dot(s, w_ref[0]) + b_ref[0]


def _modulation(cvec, w_mod, b_mod):
    depth, d, n = w_mod.shape
    rows = cvec.shape[0]
    tn = 1536
    return pl.pallas_call(
        _mod_kernel,
        grid=(depth, n // tn),
        in_specs=[pl.BlockSpec((rows, d), lambda l, j: (0, 0)),
                  pl.BlockSpec((1, d, tn), lambda l, j: (l, 0, j)),
                  pl.BlockSpec((1, 1, tn), lambda l, j: (l, 0, j))],
        out_specs=pl.BlockSpec((1, rows, tn), lambda l, j: (l, 0, j)),
        out_shape=jax.ShapeDtypeStruct((depth, rows, n), F32),
        compiler_params=_cparams(("arbitrary", "arbitrary")),
        name="modulation",
    )(cvec, w_mod, b_mod.reshape(depth, 1, n))


def _swap16(n, lane):
    fwd = pltpu.roll(n, LANES - 16, 1)
    bwd = pltpu.roll(n, 16, 1)
    return jnp.where((lane % 32) < 16, fwd, bwd)


def _inproj_kernel(*refs, rope):
    if rope:
        (h_ref, sh_ref, sc_ref, gain_ref, w_ref, hg_ref, seg_ref, cos_ref, sin_ref,
         gla_ref, gdn_ref, q_ref, kv_ref, small_ref) = refs
    else:
        (h_ref, sh_ref, sc_ref, gain_ref, w_ref, hg_ref, seg_ref,
         gla_ref, gdn_ref, q_ref, kv_ref, small_ref) = refs
    x = h_ref[...]
    ms = jnp.mean(x * x, axis=-1, keepdims=True)
    a = x * lax.rsqrt(ms + EPS) * (gain_ref[...] * (1.0 + sc_ref[0])) + sh_ref[0]
    p = jnp.dot(a.astype(BF16), w_ref[...], preferred_element_type=F32)
    gla_ref[...] = p[:, 0:1024]
    gdn_ref[...] = p[:, 1024:2048]
    small_ref[...] = p[:, 2816:2944]
    seg = seg_ref[...]
    lane = _iota((1, LANES), 1)
    outs = []
    for j in range(5):
        t = p[:, 2048 + LANES * j:2048 + LANES * (j + 1)]
        ss = _dot_xhl(t * t, seg)
        n = t * lax.rsqrt(ss * (1.0 / HEAD_DIM) + EPS) * hg_ref[:, LANES * j:LANES * (j + 1)]
        if rope:
            n = n * cos_ref[...] + _swap16(n, lane) * sin_ref[...]
        outs.append(n)
    q_ref[...] = jnp.concatenate(outs[:4], axis=1).astype(BF16)
    kv_ref[...] = jnp.concatenate([outs[4], p[:, 2688:2816]], axis=1).astype(BF16)


def _inproj(h2d, mod144, mod_row_fn, gain, w_p, hg, seg, tables, seq_len, tm):
    rows, d = h2d.shape
    n_all = w_p.shape[1]
    rope = tables is not None
    tiles_per_seq = seq_len // tm
    in_specs = [pl.BlockSpec((tm, d), lambda i: (i, 0)),
                pl.BlockSpec((1, 1, d), lambda i: (mod_row_fn(i // tiles_per_seq) * 6 + 0, 0, 0)),
                pl.BlockSpec((1, 1, d), lambda i: (mod_row_fn(i // tiles_per_seq) * 6 + 1, 0, 0)),
                pl.BlockSpec((1, d), lambda i: (0, 0)),
                pl.BlockSpec((d, n_all), lambda i: (0, 0)),
                pl.BlockSpec((1, 640), lambda i: (0, 0)),
                pl.BlockSpec((LANES, LANES), lambda i: (0, 0))]
    args = [h2d, mod144, mod144, gain, w_p, hg, seg]
    if rope:
        in_specs += [pl.BlockSpec((tm, LANES), lambda i: (i % tiles_per_seq, 0)),
                     pl.BlockSpec((tm, LANES), lambda i: (i % tiles_per_seq, 0))]
        args += list(tables)
    out_shape = (jax.ShapeDtypeStruct((rows, 1024), F32), jax.ShapeDtypeStruct((rows, 1024), F32),
                 jax.ShapeDtypeStruct((rows, 512), BF16), jax.ShapeDtypeStruct((rows, 256), BF16),
                 jax.ShapeDtypeStruct((rows, LANES), F32))
    out_specs = (pl.BlockSpec((tm, 1024), lambda i: (i, 0)), pl.BlockSpec((tm, 1024), lambda i: (i, 0)),
                 pl.BlockSpec((tm, 512), lambda i: (i, 0)), pl.BlockSpec((tm, 256), lambda i: (i, 0)),
                 pl.BlockSpec((tm, LANES), lambda i: (i, 0)))
    return pl.pallas_call(
        functools.partial(_inproj_kernel, rope=rope),
        grid=(rows // tm,), in_specs=in_specs, out_specs=out_specs, out_shape=out_shape,
        compiler_params=_cparams(("parallel",)),
        name="inproj_rope" if rope else "inproj",
    )(*args)


ATT_CHUNKS_PER_DOT = 1


def _attn_kernel(*refs, nkv, tq):
    q_ref = refs[0]
    kv_refs = refs[1:1 + nkv]
    o_ref = refs[1 + nkv]
    q = q_ref[0]
    lane = _iota((1, LANES), 1)
    mlo = (lane < HEAD_DIM).astype(BF16)
    mhi = (lane >= HEAD_DIM).astype(BF16)
    kvs = [r[0] for r in kv_refs]
    def scores(jj):
        pieces = []
        for j in range(jj, jj + ATT_CHUNKS_PER_DOT):
            qc = q[:, LANES * j:LANES * (j + 1)]
            pieces += [qc * mlo, qc * mhi]
        q_all = jnp.concatenate(pieces, axis=0)
        return [lax.dot_general(q_all, kv[:, 0:LANES], (((1,), (1,)), ((), ())), preferred_element_type=F32)
                for kv in kvs]

    outs = []
    starts = list(range(0, 4, ATT_CHUNKS_PER_DOT))
    ss_next = scores(starts[0])
    for n, jj in enumerate(starts):
        ss = ss_next
        if n + 1 < len(starts):
            ss_next = scores(starts[n + 1])
        m = functools.reduce(jnp.maximum, [jnp.max(s, axis=-1, keepdims=True) for s in ss])
        ps = [jnp.exp2(s - m) for s in ss]
        l = functools.reduce(lambda a, b: a + b, [jnp.sum(p, axis=-1, keepdims=True) for p in ps])
        o = functools.reduce(lambda a, b: a + b,
                             [jnp.dot(p.astype(BF16), kv[:, LANES:2 * LANES], preferred_element_type=F32)
                              for p, kv in zip(ps, kvs)])
        o = o / l
        for j in range(ATT_CHUNKS_PER_DOT):
            outs.append(jnp.where(lane < HEAD_DIM, o[2 * j * tq:(2 * j + 1) * tq],
                                  o[(2 * j + 1) * tq:(2 * j + 2) * tq]))
    o_ref[0] = jnp.concatenate(outs, axis=1).astype(BF16)


def _attention(q, kvs, tq):
    b, lq, _ = q.shape
    in_specs = [pl.BlockSpec((1, tq, 512), lambda i, j: (i, j, 0))]
    for kv in kvs:
        in_specs.append(pl.BlockSpec((1, kv.shape[1], 256), lambda i, j: (i, 0, 0)))
    return pl.pallas_call(
        functools.partial(_attn_kernel, nkv=len(kvs), tq=tq),
        grid=(b, lq // tq), in_specs=in_specs,
        out_specs=pl.BlockSpec((1, tq, 512), lambda i, j: (i, j, 0)),
        out_shape=jax.ShapeDtypeStruct((b, lq, 512), BF16),
        compiler_params=_cparams(("parallel", "arbitrary")),
        name="attention",
    )(q, *kvs)


def _head_masks():
    lane = _iota((1, MIX_W), 1)
    return [(lane // HEAD_DIM == h).astype(F32) for h in range(GLA_HEADS)]


def _blockdiag_mask():
    r = _iota((MIX_W, MIX_W), 0) // HEAD_DIM
    c = _iota((MIX_W, MIX_W), 1) // HEAD_DIM
    return r == c


def _finish_rows(o, gate, gain, seg):
    ss = _seg_sum64(o * o, seg)
    return o * lax.rsqrt(ss * (1.0 / HEAD_DIM) + EPS) * gain * _silu(gate)


PAIR_W = 2 * HEAD_DIM
PREP_CHUNKS = 4
SCAN_UNROLL = 2


def _gla_kernel(*refs, need_ctx):
    if need_ctx:
        (xc_ref, xl_ref, sc_ref, sl_ref, wg_ref, bg_ref, gain_ref, seg_ref,
         oc_ref, ol_ref, of_ref, ob_ref, qh_ref, kh_ref, dg_ref, st_ref) = refs
    else:
        (xc_ref, xl_ref, sc_ref, sl_ref, wg_ref, bg_ref, gain_ref, seg_ref,
         ol_ref, of_ref, ob_ref, qh_ref, kh_ref, dg_ref, st_ref) = refs
        oc_ref = None
    lc = xc_ref.shape[1]
    x_refs = (xc_ref, xl_ref)
    s_refs = (sc_ref, sl_ref)
    row_off = (0, lc)
    o_refs = (of_ref, ob_ref)
    hmask = _head_masks()
    bd = _blockdiag_mask()
    ti = _iota((CHUNK, CHUNK), 0)
    si = _iota((CHUNK, CHUNK), 1)
    tri = [(si <= ti).astype(BF16), (si >= ti).astype(BF16)]
    rr = _iota((4 * CHUNK, CHUNK), 0)
    cc = _iota((4 * CHUNK, CHUNK), 1)
    t_of_row = (rr // (GLA_HEADS * SUB)) * SUB + rr % SUB
    causal = [cc <= t_of_row, cc >= t_of_row]
    nblk = CHUNK // SUB

    def prep(seg_i, c2):
        x_ref, s_ref = x_refs[seg_i], s_refs[seg_i]
        fs = []
        for kk in range(PREP_CHUNKS):
            r0 = pl.multiple_of((c2 * PREP_CHUNKS + kk) * CHUNK, CHUNK)
            fs.append(dict(row=row_off[seg_i] + r0,
                           q=x_ref[0, pl.ds(r0, CHUNK), 0:MIX_W] * (HEAD_DIM ** -0.5),
                           k=x_ref[0, pl.ds(r0, CHUNK), MIX_W:2 * MIX_W],
                           v=x_ref[0, pl.ds(r0, CHUNK), 2 * MIX_W:3 * MIX_W],
                           sm=s_ref[0, pl.ds(r0, CHUNK), :]))
        chains = [(f, d) for f in fs for d in range(2)]
        xg = [_dot3(f["sm"], wg_ref[d]) + bg_ref[d] for f, d in chains]
        g = [(jnp.minimum(x_, 0.0) - jnp.log(1.0 + jnp.exp(-jnp.abs(x_)))) * (1.0 / GLA_TAU) for x_ in xg]
        b = [_dot_whl(tri[d], g_) for (_, d), g_ in zip(chains, g)]
        pieces = []
        for i in range(nblk):
            row_pieces = []
            for (f, d), g_, b_ in zip(chains, g, b):
                e = SUB * i if d == 0 else SUB * i + SUB - 1
                bref = b_[e:e + 1, :] - g_[e:e + 1, :]
                kt = f["k"] * jnp.exp(jnp.minimum(bref - b_, EXP_CLAMP))
                qt = f["q"][SUB * i:SUB * (i + 1), :] * jnp.exp(b_[SUB * i:SUB * (i + 1), :] - bref)
                qs = jnp.concatenate([qt * hmask[h] for h in range(GLA_HEADS)], axis=0)
                row_pieces.append(_dot_nt(qs, kt))
            pieces.append(row_pieces)
        scores = [jnp.where(causal[d], jnp.concatenate([pieces[i][j] for i in range(nblk)], axis=0), 0.0)
                  for j, (_, d) in enumerate(chains)]
        r = [_dot(s_, f["v"]) for (f, _), s_ in zip(chains, scores)]
        for j, (f, d) in enumerate(chains):
            intra = []
            for i in range(nblk):
                acc = None
                for h in range(GLA_HEADS):
                    lo = (i * GLA_HEADS + h) * SUB
                    term = r[j][lo:lo + SUB, :] * hmask[h]
                    acc = term if acc is None else acc + term
                intra.append(acc)
            row = f["row"]
            e = CHUNK - 1 if d == 0 else 0
            b_end = b[j][e:e + 1, :]
            o_refs[d][pl.ds(row, CHUNK), :] = jnp.concatenate(intra, axis=0)
            qh_ref[d, pl.ds(row, CHUNK), :] = (f["q"] * jnp.exp(b[j])).astype(BF16)
            kh_ref[d, pl.ds(row, CHUNK), :] = (f["k"] * jnp.exp(b_end - b[j])).astype(BF16)
            dg_ref[d, pl.ds(pl.multiple_of(row // 8, 8), 8), :] = jnp.broadcast_to(jnp.exp(b_end), (8, MIX_W))

    for seg_i, x_ref in enumerate(x_refs):
        def prep_body(c, carry, seg_i=seg_i):
            prep(seg_i, c)
            return carry

        lax.fori_loop(0, x_ref.shape[1] // (CHUNK * PREP_CHUNKS), prep_body, 0)

    st_ref[...] = jnp.zeros(st_ref.shape, F32)

    def scan_body(seg_i, i0, nch):
        x_ref = x_refs[seg_i]
        steps = [i0 * SCAN_UNROLL + k for k in range(SCAN_UNROLL)]
        r0s = [[pl.multiple_of((i if d == 0 else nch - 1 - i) * CHUNK, CHUNK) for d in range(2)] for i in steps]
        rows = [[row_off[seg_i] + r0 for r0 in r] for r in r0s]
        upd = [[jnp.dot(x_ref[0, pl.ds(r0s[k][d], CHUNK), 2 * MIX_W:3 * MIX_W].T.astype(BF16),
                        kh_ref[d, pl.ds(rows[k][d], CHUNK), :], preferred_element_type=F32) for d in range(2)]
               for k in range(SCAN_UNROLL)]
        st = [st_ref[d] for d in range(2)]
        for k in range(SCAN_UNROLL):
            inter = [_dot_nt(qh_ref[d, pl.ds(rows[k][d], CHUNK), :], st[d]) for d in range(2)]
            for d in range(2):
                o_refs[d][pl.ds(rows[k][d], CHUNK), :] += inter[d]
                dgr = dg_ref[d, pl.ds(pl.multiple_of(rows[k][d] // 8, 8), 8), :][0:1, :]
                st[d] = st[d] * dgr + jnp.where(bd, upd[k][d], 0.0)
        for d in range(2):
            st_ref[d] = st[d]

    for seg_i, x_ref in enumerate(x_refs):
        nch = x_ref.shape[1] // CHUNK

        def scan_iter(i, carry, seg_i=seg_i, nch=nch):
            scan_body(seg_i, i, nch)
            return carry

        lax.fori_loop(0, nch // SCAN_UNROLL, scan_iter, 0)

    gain = gain_ref[...]
    seg = seg_ref[...]
    blk = 256
    outs = ((oc_ref, xc_ref, 0), (ol_ref, xl_ref, lc))
    for o_ref, x_ref, off in outs:
        if o_ref is None:
            continue

        def fin(i, carry, o_ref=o_ref, x_ref=x_ref, off=off):
            r0 = pl.multiple_of(i * blk, blk)
            o = of_ref[pl.ds(off + r0, blk), :] + ob_ref[pl.ds(off + r0, blk), :]
            gate = x_ref[0, pl.ds(r0, blk), 3 * MIX_W:4 * MIX_W]
            o_ref[0, pl.ds(r0, blk), :] = _finish_rows(o, gate, gain, seg).astype(BF16)
            return carry

        lax.fori_loop(0, x_ref.shape[1] // blk, fin, 0)


def _gla(x_ctx, x_lat, s_ctx, s_lat, wg, bg, gain, seg, need_ctx):
    b, lc, _ = x_ctx.shape
    ll = x_lat.shape[1]
    in_specs = [pl.BlockSpec((1, lc, 1024), lambda i: (i, 0, 0)),
                pl.BlockSpec((1, ll, 1024), lambda i: (i, 0, 0)),
                pl.BlockSpec((1, lc, LANES), lambda i: (i, 0, 0)),
                pl.BlockSpec((1, ll, LANES), lambda i: (i, 0, 0)),
                pl.BlockSpec((2, LANES, MIX_W), lambda i: (0, 0, 0)),
                pl.BlockSpec((2, 1, MIX_W), lambda i: (0, 0, 0)),
                pl.BlockSpec((1, MIX_W), lambda i: (0, 0)),
                pl.BlockSpec((LANES, LANES), lambda i: (0, 0))]
    out_shape = [jax.ShapeDtypeStruct((b, ll, MIX_W), BF16)]
    out_specs = [pl.BlockSpec((1, ll, MIX_W), lambda i: (i, 0, 0))]
    if need_ctx:
        out_shape.insert(0, jax.ShapeDtypeStruct((b, lc, MIX_W), BF16))
        out_specs.insert(0, pl.BlockSpec((1, lc, MIX_W), lambda i: (i, 0, 0)))
    res = pl.pallas_call(
        functools.partial(_gla_kernel, need_ctx=need_ctx),
        grid=(b,), in_specs=in_specs, out_specs=out_specs, out_shape=out_shape,
        scratch_shapes=[pltpu.VMEM((lc + ll, MIX_W), F32), pltpu.VMEM((lc + ll, MIX_W), F32),
                        pltpu.VMEM((2, lc + ll, MIX_W), BF16), pltpu.VMEM((2, lc + ll, MIX_W), BF16),
                        pltpu.VMEM((2, (lc + ll) // 8, MIX_W), F32),
                        pltpu.VMEM((2, MIX_W, MIX_W), F32)],
        compiler_params=_cparams(("parallel",)),
        name="gla_scan",
    )(x_ctx, x_lat, s_ctx, s_lat, wg, bg, gain, seg)
    return (res[0], res[1]) if need_ctx else (None, res[0])


def _gdn2_kernel(*refs, need_ctx):
    if need_ctx:
        (xc_ref, xl_ref, sc_ref, sl_ref, cw_ref, alog_ref, dtb_ref, sel_ref, selb_ref, gain_ref, seg_ref,
         oc_ref, ol_ref, sk_ref, p_ref, qe_ref, sv_ref, kh_ref, dg_ref, of_ref, ob_ref, st_ref) = refs
    else:
        (xc_ref, xl_ref, sc_ref, sl_ref, cw_ref, alog_ref, dtb_ref, sel_ref, selb_ref, gain_ref, seg_ref,
         ol_ref, sk_ref, p_ref, qe_ref, sv_ref, kh_ref, dg_ref, of_ref, ob_ref, st_ref) = refs
        oc_ref = None
    lc = xc_ref.shape[1]
    x_refs = (xc_ref, xl_ref)
    s_refs = (sc_ref, sl_ref)
    row_off = (0, lc)
    npair = MIX_W // PAIR_W
    seg = seg_ref[...]
    ti = _iota((CHUNK, CHUNK), 0)
    si = _iota((CHUNK, CHUNK), 1)
    tri = [(si <= ti).astype(BF16), (si >= ti).astype(BF16)]
    ones = jnp.ones((CHUNK, CHUNK), BF16)
    tp = _iota((CHUNK, PAIR_W), 0)
    sp = _iota((CHUNK, PAIR_W), 1) % CHUNK
    le = sp <= tp
    ge = sp >= tp
    m_tri = [le, ge]
    m_strict = [sp < tp, sp > tp]
    m_sum = [ge.astype(F32).astype(BF16), le.astype(F32).astype(BF16)]
    blk16 = (tp // SUB) == (sp // SUB)
    eye = (tp == sp).astype(F32)
    bd2 = (_iota((PAIR_W, PAIR_W), 0) // HEAD_DIM) == (_iota((PAIR_W, PAIR_W), 1) // HEAD_DIM)
    lane_p = _iota((1, PAIR_W), 1)
    hm2 = [(lane_p // HEAD_DIM == h).astype(F32) for h in range(2)]
    cw = cw_ref[...]

    def pk2(y):
        yb = y.astype(BF16)
        return jnp.where(bd2, jnp.concatenate([yb, yb], axis=0), jnp.zeros((), BF16))

    def mm(x, y):
        return jnp.dot(x.astype(BF16), pk2(y), preferred_element_type=F32)

    def front(seg_i, c):
        x_ref, s_ref = x_refs[seg_i], s_refs[seg_i]
        ln = x_ref.shape[1]
        nch = ln // CHUNK
        r0 = pl.multiple_of(c * CHUNK, CHUNK)
        center = x_ref[0, pl.ds(r0, CHUNK), 0:3 * MIX_W]
        p0 = pl.multiple_of(jnp.maximum(r0 - 8, 0), 8)
        n0 = pl.multiple_of(jnp.minimum(r0 + CHUNK, ln - 8), 8)
        prev = x_ref[0, pl.ds(p0, 8), 0:3 * MIX_W] * jnp.where(c > 0, 1.0, 0.0)
        nxt = x_ref[0, pl.ds(n0, 8), 0:3 * MIX_W] * jnp.where(c < nch - 1, 1.0, 0.0)
        ext = jnp.concatenate([prev, center, nxt], axis=0)
        pad = SHORT_CONV // 2
        acc = None
        for j in range(SHORT_CONV):
            term = ext[8 - pad + j:8 - pad + j + CHUNK, :] * cw[j:j + 1, :]
            acc = term if acc is None else acc + term
        y = _silu(acc)
        qk = y[:, 0:2 * MIX_W]
        qk = qk * lax.rsqrt(_seg_sum64(qk * qk, seg, split=False) + EPS)
        f = dict(row=row_off[seg_i] + r0, qn=qk[:, 0:MIX_W] * (HEAD_DIM ** -0.5), kn=qk[:, MIX_W:2 * MIX_W],
                 v=y[:, 2 * MIX_W:3 * MIX_W])
        sm = s_ref[0, pl.ds(r0, CHUNK), :]
        beta = _sigmoid(sm)
        f["kk"], f["qk"] = [], []
        for p in range(npair):
            ls = slice(PAIR_W * p, PAIR_W * (p + 1))
            kstack = jnp.concatenate([f["kn"][:, ls] * hm2[0], f["kn"][:, ls] * hm2[1]], axis=0)
            f["kk"].append(_dot_nt(f["kn"][:, ls], kstack))
            f["qk"].append(_dot_nt(f["qn"][:, ls], kstack))
        gexp = [_dot_xhl(-jnp.exp(alog_ref[d]) * _softplus(sm + dtb_ref[d]), sel_ref[d]) for d in range(2)]
        f["ghl"] = [_split(g_) for g_ in gexp]
        f["bexp"] = [_dot(beta, selb_ref[d]) for d in range(2)]
        f["gam"] = [jnp.dot(tri[d], f["ghl"][d][0], preferred_element_type=F32)
                    + jnp.dot(tri[d], f["ghl"][d][1], preferred_element_type=F32) for d in range(2)]
        return f

    def prep(seg_i, c2):
        fs = [front(seg_i, c2 * PREP_CHUNKS + k) for k in range(PREP_CHUNKS)]
        chains = [(f, d, p) for f in fs for d in range(2) for p in range(npair)]
        lss = [slice(PAIR_W * p, PAIR_W * (p + 1)) for _, _, p in chains]
        gam_t = [f["gam"][d][:, ls] for (f, d, _), ls in zip(chains, lss)]
        gam_s = [jnp.dot(ones, f["ghl"][d][0][:, ls] * m_sum[d], preferred_element_type=F32)
                 + jnp.dot(ones, f["ghl"][d][1][:, ls] * m_sum[d], preferred_element_type=F32)
                 for (f, d, _), ls in zip(chains, lss)]
        bx = [f["bexp"][d][:, ls] for (f, d, _), ls in zip(chains, lss)]
        dec = [jnp.where(m_tri[d], jnp.exp(jnp.minimum(gt - gs, 0.0)), 0.0)
               for (_, d, _), gt, gs in zip(chains, gam_t, gam_s)]
        a = [jnp.where(m_strict[d], b_ * dc * f["kk"][p], 0.0) for (f, d, p), b_, dc in zip(chains, bx, dec)]
        dgn = [jnp.where(blk16, a_, 0.0) for a_ in a]
        lo = [a_ - g_ for a_, g_ in zip(a, dgn)]
        d2 = [mm(g_, g_) for g_ in dgn]
        t1 = [mm(eye - g_, eye + s_) for g_, s_ in zip(dgn, d2)]
        d4 = [mm(s_, s_) for s_ in d2]
        t2 = [mm(t_, eye + s_) for t_, s_ in zip(t1, d4)]
        d8 = [mm(s_, s_) for s_ in d4]
        t_inv = [mm(t_, eye + s_) for t_, s_ in zip(t2, d8)]
        m = [mm(t_, l_) for t_, l_ in zip(t_inv, lo)]
        m2 = [mm(m_, m_) for m_ in m]
        w1 = [mm(eye - m_, eye + s_) for m_, s_ in zip(m, m2)]
        w = [mm(w_, t_) for w_, t_ in zip(w1, t_inv)]
        egam = [jnp.exp(gt) for gt in gam_t]
        solv = [mm(w_, b_ * f["v"][:, ls]) for (f, _, _), w_, b_, ls in zip(chains, w, bx, lss)]
        solk = [mm(w_, b_ * eg * f["kn"][:, ls]) for (f, _, _), w_, b_, eg, ls in zip(chains, w, bx, egam, lss)]
        for i, (f, d, p) in enumerate(chains):
            ls, row = lss[i], f["row"]
            e = CHUNK - 1 if d == 0 else 0
            g_end = gam_t[i][e:e + 1, :]
            sv_ref[d, pl.ds(row, CHUNK), ls] = solv[i]
            sk_ref[d, pl.ds(row, CHUNK), ls] = solk[i].astype(BF16)
            p_ref[d, pl.ds(row, CHUNK), ls] = jnp.where(m_tri[d], f["qk"][p] * dec[i], 0.0).astype(BF16)
            qe_ref[d, pl.ds(row, CHUNK), ls] = (egam[i] * f["qn"][:, ls]).astype(BF16)
            kh_ref[d, pl.ds(row, CHUNK), ls] = f["kn"][:, ls] * jnp.exp(g_end - gam_t[i])
            dg_ref[d, pl.ds(pl.multiple_of(row // 8, 8), 8), ls] = jnp.broadcast_to(jnp.exp(g_end), (8, PAIR_W))

    for seg_i, x_ref in enumerate(x_refs):
        def prep_body(c, carry, seg_i=seg_i):
            prep(seg_i, c)
            return carry

        lax.fori_loop(0, x_ref.shape[1] // (CHUNK * PREP_CHUNKS), prep_body, 0)

    st_ref[...] = jnp.zeros(st_ref.shape, F32)

    def scan_body(seg_i, i0, nch):
        chains = [(d, p) for d in range(2) for p in range(npair)]
        lss = [slice(PAIR_W * p, PAIR_W * (p + 1)) for _, p in chains]
        st = [st_ref[d, p] for d, p in chains]
        pending = []
        for k in range(SCAN_UNROLL):
            i = i0 * SCAN_UNROLL + k
            rows = [row_off[seg_i] + pl.multiple_of((i if d == 0 else nch - 1 - i) * CHUNK, CHUNK)
                    for d, _ in chains]
            stb = [s_.astype(BF16) for s_ in st]
            u = [sv_ref[d, pl.ds(r, CHUNK), ls]
                 - jnp.dot(sk_ref[d, pl.ds(r, CHUNK), ls], sb, preferred_element_type=F32)
                 for (d, _), r, ls, sb in zip(chains, rows, lss, stb)]
            ku = [_dot(kh_ref[d, pl.ds(r, CHUNK), ls].T, u_) for (d, _), r, ls, u_ in zip(chains, rows, lss, u)]
            dgr = [dg_ref[d, pl.ds(pl.multiple_of(r // 8, 8), 8), ls][0:1, :]
                   for (d, _), r, ls in zip(chains, rows, lss)]
            st = [s_ * g_ + jnp.where(bd2, k_, 0.0) for s_, g_, k_ in zip(st, dgr, ku)]
            pending.append((rows, stb, u))
        for rows, stb, u in pending:
            oq = [jnp.dot(qe_ref[d, pl.ds(r, CHUNK), ls], sb, preferred_element_type=F32)
                  for (d, _), r, ls, sb in zip(chains, rows, lss, stb)]
            ou = [jnp.dot(p_ref[d, pl.ds(r, CHUNK), ls], pk2(u_), preferred_element_type=F32)
                  for (d, _), r, ls, u_ in zip(chains, rows, lss, u)]
            for j, (d, _) in enumerate(chains):
                dst = of_ref if d == 0 else ob_ref
                dst[pl.ds(rows[j], CHUNK), lss[j]] = oq[j] + ou[j]
        for j, (d, p) in enumerate(chains):
            st_ref[d, p] = st[j]

    for seg_i, x_ref in enumerate(x_refs):
        nch = x_ref.shape[1] // CHUNK

        def scan_iter(i, carry, seg_i=seg_i, nch=nch):
            scan_body(seg_i, i, nch)
            return carry

        lax.fori_loop(0, nch // SCAN_UNROLL, scan_iter, 0)

    gain = gain_ref[...]
    blk = 256
    outs = ((oc_ref, xc_ref, 0), (ol_ref, xl_ref, lc))
    for o_ref, x_ref, off in outs:
        if o_ref is None:
            continue

        def fin(i, carry, o_ref=o_ref, x_ref=x_ref, off=off):
            r0 = pl.multiple_of(i * blk, blk)
            o = of_ref[pl.ds(off + r0, blk), :] + ob_ref[pl.ds(off + r0, blk), :]
            gate = x_ref[0, pl.ds(r0, blk), 3 * MIX_W:4 * MIX_W]
            o_ref[0, pl.ds(r0, blk), :] = _finish_rows(o, gate, gain, seg).astype(BF16)
            return carry

        lax.fori_loop(0, x_ref.shape[1] // blk, fin, 0)


def _gdn2(x_ctx, x_lat, s_ctx, s_lat, cw, alog, dtb, sel, selb, gain, seg, need_ctx):
    b, lc, _ = x_ctx.shape
    ll = x_lat.shape[1]
    lt = lc + ll
    in_specs = [pl.BlockSpec((1, lc, 1024), lambda i: (i, 0, 0)),
                pl.BlockSpec((1, ll, 1024), lambda i: (i, 0, 0)),
                pl.BlockSpec((1, lc, LANES), lambda i: (i, 0, 0)),
                pl.BlockSpec((1, ll, LANES), lambda i: (i, 0, 0)),
                pl.BlockSpec((8, 3 * MIX_W), lambda i: (0, 0)),
                pl.BlockSpec((2, 1, LANES), lambda i: (0, 0, 0)),
                pl.BlockSpec((2, 1, LANES), lambda i: (0, 0, 0)),
                pl.BlockSpec((2, LANES, MIX_W), lambda i: (0, 0, 0)),
                pl.BlockSpec((2, LANES, MIX_W), lambda i: (0, 0, 0)),
                pl.BlockSpec((1, MIX_W), lambda i: (0, 0)),
                pl.BlockSpec((LANES, LANES), lambda i: (0, 0))]
    out_shape = [jax.ShapeDtypeStruct((b, ll, MIX_W), BF16)]
    out_specs = [pl.BlockSpec((1, ll, MIX_W), lambda i: (i, 0, 0))]
    if need_ctx:
        out_shape.insert(0, jax.ShapeDtypeStruct((b, lc, MIX_W), BF16))
        out_specs.insert(0, pl.BlockSpec((1, lc, MIX_W), lambda i: (i, 0, 0)))
    res = pl.pallas_call(
        functools.partial(_gdn2_kernel, need_ctx=need_ctx),
        grid=(b,), in_specs=in_specs, out_specs=out_specs, out_shape=out_shape,
        scratch_shapes=[pltpu.VMEM((2, lt, MIX_W), BF16), pltpu.VMEM((2, lt, MIX_W), BF16),
                        pltpu.VMEM((2, lt, MIX_W), BF16), pltpu.VMEM((2, lt, MIX_W), F32),
                        pltpu.VMEM((2, lt, MIX_W), F32), pltpu.VMEM((2, lt // 8, MIX_W), F32),
                        pltpu.VMEM((lt, MIX_W), F32), pltpu.VMEM((lt, MIX_W), F32),
                        pltpu.VMEM((2, MIX_W // PAIR_W, PAIR_W, PAIR_W), F32)],
        compiler_params=_cparams(("parallel",)),
        name="gdn_scan",
    )(x_ctx, x_lat, s_ctx, s_lat, cw, alog, dtb, sel, selb, gain, seg)
    return (res[0], res[1]) if need_ctx else (None, res[0])


def _mix_residual(gla_ref, gdn_ref, att_ref, h_ref, g1_ref, wo_ref):
    y = (jnp.dot(gla_ref[...], wo_ref[0:256, :], preferred_element_type=F32)
         + jnp.dot(gdn_ref[...], wo_ref[256:512, :], preferred_element_type=F32)
         + jnp.dot(att_ref[...], wo_ref[512:1024, :], preferred_element_type=F32))
    return h_ref[...] + g1_ref[0] * y


def _mix_specs(tm, d, index):
    return [pl.BlockSpec((tm, 256), lambda *g: (index(*g), 0)),
            pl.BlockSpec((tm, 256), lambda *g: (index(*g), 0)),
            pl.BlockSpec((tm, 512), lambda *g: (index(*g), 0)),
            pl.BlockSpec((d, d), lambda *g: (0, 0))]


def _norm_mod(x, sh_ref, sc_ref, gain_ref):
    ms = jnp.mean(x * x, axis=-1, keepdims=True)
    return x * lax.rsqrt(ms + EPS) * (gain_ref[...] * (1.0 + sc_ref[0])) + sh_ref[0]


def _swiglu_part(x, wg, wu, wd):
    gg = jnp.dot(x, wg[...], preferred_element_type=F32)
    uu = jnp.dot(x, wu[...], preferred_element_type=F32)
    hid = (_silu(gg) * uu).astype(BF16)
    return jnp.dot(hid, wd[...], preferred_element_type=F32)


def _ffn_kernel(gla_ref, gdn_ref, att_ref, wo_ref, h_ref, g1_ref, sh_ref, sc_ref, g2_ref, gain_ref,
                wg_ref, wu_ref, wd_ref, o_ref, h1_scr, b_scr, acc_scr, *, nf):
    f = pl.program_id(1)

    @pl.when(f == 0)
    def _():
        h1 = _mix_residual(gla_ref, gdn_ref, att_ref, h_ref, g1_ref, wo_ref)
        h1_scr[...] = h1
        b_scr[...] = _norm_mod(h1, sh_ref, sc_ref, gain_ref).astype(BF16)
        acc_scr[...] = jnp.zeros(acc_scr.shape, F32)

    b = b_scr[...]
    acc_scr[...] += _swiglu_part(b, wg_ref, wu_ref, wd_ref)

    @pl.when(f == nf - 1)
    def _():
        o_ref[...] = h1_scr[...] + g2_ref[0] * acc_scr[...]


def _ffn(mix, w_o, h2d, mod144, mod_row_fn, gain, w_gu, w_down, seq_len, tm, tf):
    rows, d = h2d.shape
    dff = w_down.shape[0]
    nf = dff // tf
    tiles_per_seq = seq_len // tm

    def mod_spec(k):
        return pl.BlockSpec((1, 1, d), lambda i, f: (mod_row_fn(i // tiles_per_seq) * 6 + k, 0, 0))

    return pl.pallas_call(
        functools.partial(_ffn_kernel, nf=nf),
        grid=(rows // tm, nf),
        in_specs=_mix_specs(tm, d, lambda i, f: i) + [
            pl.BlockSpec((tm, d), lambda i, f: (i, 0)),
            mod_spec(2), mod_spec(3), mod_spec(4), mod_spec(5),
            pl.BlockSpec((1, d), lambda i, f: (0, 0)),
            pl.BlockSpec((d, tf), lambda i, f: (0, f)),
            pl.BlockSpec((d, tf), lambda i, f: (0, nf + f)),
            pl.BlockSpec((tf, d), lambda i, f: (f, 0))],
        out_specs=pl.BlockSpec((tm, d), lambda i, f: (i, 0)),
        out_shape=jax.ShapeDtypeStruct((rows, d), F32),
        scratch_shapes=[pltpu.VMEM((tm, d), F32), pltpu.VMEM((tm, d), BF16), pltpu.VMEM((tm, d), F32)],
        compiler_params=_cparams(("parallel", "arbitrary")),
        name="ffn",
    )(*mix, w_o, h2d, mod144, mod144, mod144, mod144, gain, w_gu, w_gu, w_down)


MOE_TILE = 512
COMBINE_TOKENS = 512


def _router_kernel(gla_ref, gdn_ref, att_ref, wo_ref, h_ref, g1_ref, sh_ref, sc_ref, gain_ref, wr_ref, br_ref,
                   h1_ref, b_ref, route_ref):
    lane = _iota((1, LANES), 1)
    lane_f = lane.astype(F32)
    h1 = _mix_residual(gla_ref, gdn_ref, att_ref, h_ref, g1_ref, wo_ref)
    h1_ref[...] = h1
    b = _norm_mod(h1, sh_ref, sc_ref, gain_ref)
    b_ref[...] = b
    logits = _dot3(b, wr_ref[...]) + br_ref[...]
    logits = jnp.where(lane < N_EXPERTS, logits, -jnp.inf)
    m1 = jnp.max(logits, axis=-1, keepdims=True)
    i1 = jnp.min(jnp.where(logits == m1, lane_f, float(LANES)), axis=-1, keepdims=True)
    rest = jnp.where(lane_f == i1, -jnp.inf, logits)
    m2 = jnp.max(rest, axis=-1, keepdims=True)
    i2 = jnp.min(jnp.where(rest == m2, lane_f, float(LANES)), axis=-1, keepdims=True)
    t = jnp.exp(m2 - m1)
    w1 = 1.0 / (1.0 + t)
    route_ref[...] = (jnp.where(lane == 0, i1, 0.0) + jnp.where(lane == 1, i2, 0.0)
                      + jnp.where(lane == 2, w1, 0.0) + jnp.where(lane == 3, t * w1, 0.0))


def _router(mix, w_o, h2d, mod144, mod_row_fn, gain, w_router, b_router, seq_len, tm):
    rows, d = h2d.shape
    tiles_per_seq = seq_len // tm

    def mod_spec(k):
        return pl.BlockSpec((1, 1, d), lambda i: (mod_row_fn(i // tiles_per_seq) * 6 + k, 0, 0))

    row_spec = pl.BlockSpec((tm, d), lambda i: (i, 0))
    return pl.pallas_call(
        _router_kernel,
        grid=(rows // tm,),
        in_specs=_mix_specs(tm, d, lambda i: i) + [
            row_spec, mod_spec(2), mod_spec(3), mod_spec(4),
            pl.BlockSpec((1, d), lambda i: (0, 0)),
            pl.BlockSpec((d, LANES), lambda i: (0, 0)),
            pl.BlockSpec((1, LANES), lambda i: (0, 0))],
        out_specs=(row_spec, row_spec, pl.BlockSpec((tm, LANES), lambda i: (i, 0))),
        out_shape=(jax.ShapeDtypeStruct((rows, d), F32), jax.ShapeDtypeStruct((rows, d), F32),
                   jax.ShapeDtypeStruct((rows, LANES), F32)),
        compiler_params=_cparams(("parallel",)),
        name="moe_router",
    )(*mix, w_o, h2d, mod144, mod144, mod144, gain, w_router, b_router)


DISPATCH_TOKENS = 1024


def _moe_dispatch_kernel(pos_ref, b_ref, xs_in_ref, xs_ref, sem):
    del xs_in_ref
    n = b_ref.shape[0]
    for r in range(2 * n):
        pltpu.make_async_copy(b_ref.at[pl.ds(r % n, 1)], xs_ref.at[pl.ds(pos_ref[0, 0, r], 1)],
                              sem).start(priority=r % 2)
    for _ in range(2):
        pltpu.make_async_copy(b_ref, xs_ref.at[pl.ds(0, n)], sem).wait()


def _moe_dispatch(pos2, b, p_rows):
    rows, d = b.shape
    tm = DISPATCH_TOKENS
    return pl.pallas_call(
        _moe_dispatch_kernel,
        grid=(rows // tm,),
        in_specs=[pl.BlockSpec((1, 1, 2 * tm), lambda i: (i, 0, 0), memory_space=pltpu.SMEM),
                  pl.BlockSpec((tm, d), lambda i: (i, 0)),
                  pl.BlockSpec(memory_space=pl.ANY)],
        out_specs=pl.BlockSpec(memory_space=pl.ANY),
        out_shape=jax.ShapeDtypeStruct((p_rows, d), F32),
        scratch_shapes=[pltpu.SemaphoreType.DMA(())],
        input_output_aliases={2: 0},
        compiler_params=_cparams(("arbitrary",)),
        name="moe_dispatch",
    )(pos2, b, jnp.zeros((p_rows, d), F32))


def _moe_group_kernel(te_ref, nt_ref, xs_ref, wg_ref, wu_ref, wd_ref, ys_ref, xb_scr, acc_scr, *, nf):
    i = pl.program_id(0)
    f = pl.program_id(1)
    used = i < nt_ref[0]

    @pl.when(used & (f == 0))
    def _():
        xb_scr[...] = xs_ref[...].astype(BF16)

    @pl.when(used)
    def _():
        x = xb_scr[...]
        part = _swiglu_part(x, wg_ref.at[0], wu_ref.at[0], wd_ref.at[0])

        @pl.when(f == 0)
        def _():
            acc_scr[...] = part

        @pl.when(f > 0)
        def _():
            acc_scr[...] += part

    @pl.when(used & (f == nf - 1))
    def _():
        ys_ref[...] = acc_scr[...]

    @pl.when(jnp.logical_not(used) & (f == nf - 1))
    def _():
        ys_ref[...] = jnp.zeros(ys_ref.shape, F32)


def _moe_group(tile_expert, n_tiles_used, xs, w_gu, w_down, tf):
    p, d = xs.shape
    dff = w_down.shape[1]
    nf = dff // tf
    grid_spec = pltpu.PrefetchScalarGridSpec(
        num_scalar_prefetch=2,
        grid=(p // MOE_TILE, nf),
        in_specs=[pl.BlockSpec((MOE_TILE, d), lambda i, f, te, nt: (i, 0)),
                  pl.BlockSpec((1, d, tf), lambda i, f, te, nt: (te[i], 0, f)),
                  pl.BlockSpec((1, d, tf), lambda i, f, te, nt: (te[i], 0, nf + f)),
                  pl.BlockSpec((1, tf, d), lambda i, f, te, nt: (te[i], f, 0))],
        out_specs=pl.BlockSpec((MOE_TILE, d), lambda i, f, te, nt: (i, 0)),
        scratch_shapes=[pltpu.VMEM((MOE_TILE, d), BF16), pltpu.VMEM((MOE_TILE, d), F32)])
    return pl.pallas_call(
        functools.partial(_moe_group_kernel, nf=nf),
        grid_spec=grid_spec,
        out_shape=jax.ShapeDtypeStruct((p, d), F32),
        compiler_params=_cparams(("arbitrary", "arbitrary")),
        name="moe_experts",
    )(tile_expert, n_tiles_used, xs, w_gu, w_gu, w_down)


def _moe_combine_kernel(pos_ref, pos_next_ref, ys_ref, h_ref, route_ref, g_ref, o_ref, buf, sems):
    i = pl.program_id(0)
    steps = pl.num_programs(0)
    slot = i % 2
    n = buf.shape[1]
    half = n // 2

    def request(rows_ref, s):
        for r in range(n):
            pltpu.make_async_copy(ys_ref.at[pl.ds(rows_ref[0, 0, r], 1)], buf.at[s, pl.ds(r, 1)],
                                  sems.at[s]).start(priority=r % 2)

    @pl.when(i == 0)
    def _():
        request(pos_ref, 0)

    @pl.when(i + 1 < steps)
    def _():
        request(pos_next_ref, 1 - slot)

    pltpu.make_async_copy(ys_ref.at[pl.ds(0, n)], buf.at[slot], sems.at[slot]).wait()
    w1 = route_ref[:, 2:3]
    w2 = route_ref[:, 3:4]
    o_ref[...] = h_ref[...] + g_ref[0] * (w1 * buf[slot, 0:half, :] + w2 * buf[slot, half:n, :])


def _moe_combine(pos, ys, h2d, route, mod144, mod_row_fn, seq_len):
    rows, d = h2d.shape
    tm = pos.shape[2] // 2
    tiles_per_seq = seq_len // tm
    steps = rows // tm
    return pl.pallas_call(
        _moe_combine_kernel,
        grid=(steps,),
        in_specs=[pl.BlockSpec((1, 1, 2 * tm), lambda i: (i, 0, 0), memory_space=pltpu.SMEM),
                  pl.BlockSpec((1, 1, 2 * tm), lambda i: (jnp.minimum(i + 1, steps - 1), 0, 0),
                               memory_space=pltpu.SMEM),
                  pl.BlockSpec(memory_space=pl.ANY),
                  pl.BlockSpec((tm, d), lambda i: (i, 0)),
                  pl.BlockSpec((tm, LANES), lambda i: (i, 0)),
                  pl.BlockSpec((1, 1, d), lambda i: (mod_row_fn(i // tiles_per_seq) * 6 + 5, 0, 0))],
        out_specs=pl.BlockSpec((tm, d), lambda i: (i, 0)),
        out_shape=jax.ShapeDtypeStruct((rows, d), F32),
        scratch_shapes=[pltpu.VMEM((2, 2 * tm, d), F32), pltpu.SemaphoreType.DMA((2,))],
        compiler_params=_cparams(("arbitrary",)),
        name="moe_combine",
    )(pos, pos, ys, h2d, route, mod144)


def _moe_routed(mix, w_o, h2d, mod144, mod_row_fn, gain, w_router, b_router, w_gu, w_down, seq_len):
    rows, d = h2d.shape
    h1, b, route = _router(mix, w_o, h2d, mod144, mod_row_fn, gain, w_router, b_router, seq_len,
                           _rows(ROW_TILE, seq_len))
    ex = jnp.concatenate([route[:, 0], route[:, 1]]).astype(jnp.int32)
    onehot = (ex[:, None] == jnp.arange(N_EXPERTS, dtype=jnp.int32)[None, :]).astype(jnp.int32)
    rank = jnp.sum((jnp.cumsum(onehot, axis=0) - onehot) * onehot, axis=1)
    counts = jnp.sum(onehot, axis=0)
    padded = ((counts + MOE_TILE - 1) // MOE_TILE) * MOE_TILE
    ends = jnp.cumsum(padded)
    starts = ends - padded
    pos = jnp.sum(onehot * starts[None, :], axis=1) + rank
    pos = pos.astype(jnp.int32)
    p_rows = 2 * rows + N_EXPERTS * MOE_TILE
    tile_first = jnp.arange(p_rows // MOE_TILE, dtype=jnp.int32) * MOE_TILE
    tile_expert = jnp.minimum(jnp.sum((tile_first[:, None] >= ends[None, :]).astype(jnp.int32), axis=1),
                              N_EXPERTS - 1).astype(jnp.int32)
    n_tiles_used = (ends[-1:] // MOE_TILE).astype(jnp.int32)

    def per_tile(tm):
        return jnp.concatenate([pos[:rows].reshape(rows // tm, 1, tm), pos[rows:].reshape(rows // tm, 1, tm)], axis=2)

    xs = _moe_dispatch(per_tile(DISPATCH_TOKENS), b, p_rows)
    ys = _moe_group(tile_expert, n_tiles_used, xs, w_gu, w_down, FF_TILE)
    return _moe_combine(per_tile(_rows(COMBINE_TOKENS, seq_len)), ys, h1, route, mod144, mod_row_fn, seq_len)


ATT_HEAD_ORDER = (0, 4, 1, 5, 2, 6, 3, 7)


def _layout_w_in(w):
    gla = w[:, 0:1024]
    glow = w[:, 1024:1056]
    gdn = w[:, 1056:2080]
    ab = w[:, 2080:2096]
    q = jnp.concatenate([w[:, 2096 + HEAD_DIM * h:2096 + HEAD_DIM * (h + 1)] for h in ATT_HEAD_ORDER], axis=1)
    kv = w[:, 2608:2864]
    pad = jnp.zeros((w.shape[0], LANES - 48), w.dtype)
    return jnp.concatenate([gla, gdn, q, kv, glow, ab, pad], axis=1).astype(BF16)


def _layout_w_out(w):
    att = [w[512 + HEAD_DIM * h:512 + HEAD_DIM * (h + 1)] for h in ATT_HEAD_ORDER]
    return jnp.concatenate([w[0:512]] + att, axis=0).astype(BF16)


def _rope_tables(seq_len):
    rows = seq_len // GRID_W
    row = jnp.repeat(jnp.arange(rows), GRID_W).astype(F32)
    col = jnp.tile(jnp.arange(GRID_W), rows).astype(F32)
    inv_freq = ROPE_THETA ** (-jnp.arange(0, HEAD_DIM // 2, 2, dtype=F32) / (HEAD_DIM // 2))
    ar = row[:, None] * inv_freq
    ac = col[:, None] * inv_freq
    cos = jnp.concatenate([jnp.cos(ar), jnp.cos(ar), jnp.cos(ac), jnp.cos(ac)], axis=-1)
    sin = jnp.concatenate([-jnp.sin(ar), jnp.sin(ar), -jnp.sin(ac), jnp.sin(ac)], axis=-1)
    return jnp.tile(cos, (1, 2)), jnp.tile(sin, (1, 2))


def _seg_matrix():
    i = np.arange(LANES)
    return jnp.asarray((i[:, None] // HEAD_DIM) == (i[None, :] // HEAD_DIM), dtype=BF16)


def _gdn_select():
    sel = np.zeros((2, LANES, MIX_W), np.float32)
    selb = np.zeros((2, LANES, MIX_W), np.float32)
    for d in range(2):
        for h in range(GDN_HEADS):
            sel[d, 32 + GDN_HEADS * d + h, HEAD_DIM * h:HEAD_DIM * (h + 1)] = 1.0
            selb[d, 40 + GDN_HEADS * d + h, HEAD_DIM * h:HEAD_DIM * (h + 1)] = 1.0
    return jnp.asarray(sel, BF16), jnp.asarray(selb, BF16)


def _lane_rows(vals, base):
    out = jnp.zeros((2, 1, LANES), F32)
    for d in range(2):
        out = out.at[d, 0, base + GDN_HEADS * d:base + GDN_HEADS * (d + 1)].set(vals[d].astype(F32))
    return out


def kernel(x, c, ctx, c_ctx, w_mod, b_mod, norm_mix, norm_ffn, w_in, gla_gate_up, gla_gate_bias, gla_out_gain,
           gdn_conv, gdn_a_log, gdn_dt_bias, gdn_out_gain, att_q_gain, att_k_gain, w_out, ffn_gate_up, ffn_down,
           moe_router, moe_router_bias, moe_gate_up, moe_down):
    bsz, seq, d = x.shape
    lctx = ctx.shape[1]
    depth = w_mod.shape[0]
    ctx_row = bsz

    mod_rows = ((bsz + 1 + 7) // 8) * 8
    cvec = jnp.concatenate([c, c_ctx[None, :], jnp.zeros((mod_rows - bsz - 1, d), F32)], axis=0)
    mods = _modulation(cvec, w_mod, b_mod)

    seg = _seg_matrix()
    tables = _rope_tables(seq)
    sel, selb = _gdn_select()
    lat_row = lambda b: b
    ctx_row_fn = lambda b: ctx_row

    h_lat = x.reshape(bsz * seq, d)
    h_ctx = ctx.reshape(bsz * lctx, d)
    for layer in range(depth):
        need_ctx = layer < depth - 1
        mod144 = mods[layer].reshape(mod_rows * 6, 1, d)
        w_p = _layout_w_in(w_in[layer])
        w_o = _layout_w_out(w_out[layer])
        hg = jnp.concatenate([jnp.tile(att_q_gain[layer], ATT_Q_HEADS) * (HEAD_DIM ** -0.5 * LOG2E),
                              jnp.tile(att_k_gain[layer], ATT_KV_HEADS)])[None, :].astype(F32)
        gain_mix = norm_mix[layer][None, :]
        gain_ffn = norm_ffn[layer][None, :]

        gla_l, gdn_l, q_l, kv_l, sm_l = _inproj(h_lat, mod144, lat_row, gain_mix, w_p, hg, seg, tables, seq,
                                                _rows(INPROJ_ROWS, seq))
        gla_c, gdn_c, q_c, kv_c, sm_c = _inproj(h_ctx, mod144, ctx_row_fn, gain_mix, w_p, hg, seg, None, lctx,
                                                _rows(INPROJ_ROWS, lctx))

        r3 = lambda t, n: t.reshape(bsz, n, t.shape[-1])
        wg = jnp.zeros((2, LANES, MIX_W), F32)
        for dd in range(2):
            wg = wg.at[dd, GLA_GATE_RANK * dd:GLA_GATE_RANK * (dd + 1), :].set(gla_gate_up[layer, dd].astype(F32))
        bg = gla_gate_bias[layer].reshape(2, 1, MIX_W).astype(F32)
        gla_gain = jnp.tile(gla_out_gain[layer], GLA_HEADS)[None, :].astype(F32)
        o_gla_c, o_gla_l = _gla(r3(gla_c, lctx), r3(gla_l, seq), r3(sm_c, lctx), r3(sm_l, seq),
                                wg, bg, gla_gain, seg, need_ctx)

        cw = jnp.concatenate([gdn_conv[layer].astype(F32), jnp.zeros((8 - SHORT_CONV, 3 * MIX_W), F32)], axis=0)
        alog = _lane_rows(gdn_a_log[layer], 32)
        dtb = _lane_rows(gdn_dt_bias[layer], 32)
        gdn_gain = jnp.tile(gdn_out_gain[layer], GDN_HEADS)[None, :].astype(F32)
        o_gdn_c, o_gdn_l = _gdn2(r3(gdn_c, lctx), r3(gdn_l, seq), r3(sm_c, lctx), r3(sm_l, seq),
                                cw, alog, dtb, sel, selb, gdn_gain, seg, need_ctx)

        o_att_l = _attention(r3(q_l, seq), [r3(kv_l, seq), r3(kv_c, lctx)], _rows(ATT_Q_ROWS, seq))
        mix_l = (o_gla_l.reshape(-1, MIX_W), o_gdn_l.reshape(-1, MIX_W), o_att_l.reshape(-1, 512))
        if need_ctx:
            o_att_c = _attention(r3(q_c, lctx), [r3(kv_c, lctx)], _rows(ATT_Q_ROWS, lctx))
            mix_c = (o_gla_c.reshape(-1, MIX_W), o_gdn_c.reshape(-1, MIX_W), o_att_c.reshape(-1, 512))

        j = layer // 2
        if layer % 2 == 0:
            w_gu = ffn_gate_up[j].astype(BF16)
            w_dn = ffn_down[j].astype(BF16)
            h_lat = _ffn(mix_l, w_o, h_lat, mod144, lat_row, gain_ffn, w_gu, w_dn, seq, _rows(ROW_TILE, seq), FF_TILE)
            if need_ctx:
                h_ctx = _ffn(mix_c, w_o, h_ctx, mod144, ctx_row_fn, gain_ffn, w_gu, w_dn, lctx,
                             _rows(ROW_TILE, lctx), FF_TILE)
        else:
            w_gu = moe_gate_up[j].astype(BF16)
            w_dn = moe_down[j].astype(BF16)
            w_r = jnp.concatenate([moe_router[j].astype(F32), jnp.zeros((d, LANES - N_EXPERTS), F32)], axis=1)
            b_r = jnp.concatenate([moe_router_bias[j].astype(F32), jnp.zeros((LANES - N_EXPERTS,), F32)])[None, :]
            h_lat = _moe_routed(mix_l, w_o, h_lat, mod144, lat_row, gain_ffn, w_r, b_r, w_gu, w_dn, seq)
            if need_ctx:
                h_ctx = _moe_routed(mix_c, w_o, h_ctx, mod144, ctx_row_fn, gain_ffn, w_r, b_r, w_gu, w_dn, lctx)
    return h_lat.reshape(bsz, seq, d)
```
